```python
import math
import jax, jax.numpy as jnp
from jax import lax
import numpy as np

D_MODEL = 1024
BATCH = 1
SEQ = 16384
DEPTH = 2
DEC_BATCH = 128
DEC_SEQ = 1
PAST_LEN = 16384
PAGE_SIZE = 128

SSD_D_INNER = D_MODEL
SSD_HEAD_DIM = 64
SSD_HEADS = SSD_D_INNER // SSD_HEAD_DIM
SSD_GROUPS = 4
SSD_STATE = 128
SSD_CONV = 4
SSD_CONV_DIM = SSD_D_INNER + 2 * SSD_GROUPS * SSD_STATE
SSD_CHUNK = 128
ATT_HEAD_DIM = 64
ATT_Q_HEADS = D_MODEL // ATT_HEAD_DIM
ATT_KV_HEADS = 4
WINDOW = 128
ROPE_THETA = 500000.0
ROPE_DIM = ATT_HEAD_DIM // 4
RET_HEADS = 8
RET_DK = D_MODEL // RET_HEADS
RET_DV = D_MODEL // RET_HEADS
RET_THETA = 10000.0
RET_CHUNK = 128
D_FF = 2816
N_EXPERTS = 8
TOP_K = 2
N_DENSE_LAYERS = (DEPTH + 1) // 2
N_MOE_LAYERS = DEPTH // 2
PLE_DIM = 256
EPS = 1e-6

IN_WIDTHS = (SSD_D_INNER,
             SSD_CONV_DIM,
             SSD_HEADS,
             ATT_Q_HEADS * ATT_HEAD_DIM,
             ATT_KV_HEADS * ATT_HEAD_DIM,
             ATT_KV_HEADS * ATT_HEAD_DIM,
             RET_HEADS * RET_DK,
             RET_HEADS * RET_DK,
             RET_HEADS * RET_DV,
             RET_HEADS * RET_DV,
             3 * D_MODEL)
N_IN = sum(IN_WIDTHS)

kernel_name = 'hybrid_ssd_swa_retention_decoder_step'


def rms_norm(x, g):
    xf = x.astype(jnp.float32)
    y = xf * lax.rsqrt(jnp.mean(xf * xf, axis=-1, keepdims=True) + EPS)
    return (y * g.astype(jnp.float32)).astype(x.dtype)


def rope(x, pos, rot_dim, theta):
    half = rot_dim // 2
    inv = jnp.exp(-math.log(theta) * jnp.arange(half, dtype=jnp.float32) * (2.0 / rot_dim))
    ang = pos.astype(jnp.float32)[:, None] * inv[None, :]
    cos = jnp.cos(ang)[:, None, :]
    sin = jnp.sin(ang)[:, None, :]
    x1 = x[..., :half].astype(jnp.float32)
    x2 = x[..., half:rot_dim].astype(jnp.float32)
    rot = jnp.concatenate([x1 * cos - x2 * sin, x2 * cos + x1 * sin], axis=-1).astype(x.dtype)
    return jnp.concatenate([rot, x[..., rot_dim:]], axis=-1)


def causal_conv(xbc, buf, w, b):
    full = jnp.concatenate([buf.astype(xbc.dtype), xbc], axis=1)
    y = lax.conv_general_dilated(full, w[:, None, :].astype(xbc.dtype), window_strides=(1,),
                                 padding='VALID', dimension_numbers=('NWC', 'WIO', 'NWC'),
                                 feature_group_count=xbc.shape[-1])
    return jax.nn.silu(y + b.astype(y.dtype)), full[:, -(SSD_CONV - 1):]


def ssd_scan(x, dt, a, b_mat, c_mat, s0):
    f32 = jnp.float32
    bsz, seq, nh, hp = x.shape
    ng, ns = b_mat.shape[2], b_mat.shape[3]
    rep = nh // ng
    c = SSD_CHUNK if seq % SSD_CHUNK == 0 else seq
    nc = seq // c
    la = (dt * a).reshape(bsz, nc, c, ng, rep)
    xd = (x.astype(f32) * dt[..., None]).reshape(bsz, nc, c, ng, rep, hp)
    bm = b_mat.astype(f32).reshape(bsz, nc, c, ng, ns)
    cm = c_mat.astype(f32).reshape(bsz, nc, c, ng, ns)
    causal = jnp.tril(jnp.ones((c, c), dtype=bool))

    def step(s, inp):
        la_k, x_k, b_k, c_k = inp
        cum = jnp.cumsum(la_k, axis=1)
        seg = cum[:, :, None] - cum[:, None, :]
        decay = jnp.exp(jnp.where(causal[None, :, :, None, None], seg, -jnp.inf))
        cb = jnp.einsum('bign,bjgn->bijg', c_k, b_k)
        y = jnp.einsum('bijg,bijgr,bjgrp->bigrp', cb, decay, x_k)
        y = y + jnp.einsum('bign,bgrpn->bigrp', c_k, s) * jnp.exp(cum)[..., None]
        tail = jnp.exp(cum[:, -1:] - cum)
        s = (s * jnp.exp(cum[:, -1])[..., None, None]
             + jnp.einsum('bjgr,bjgrp,bjgn->bgrpn', tail, x_k, b_k))
        return s, y

    s_init = s0.astype(f32).reshape(bsz, ng, rep, hp, ns)
    s_fin, ys = lax.scan(step, s_init, (jnp.moveaxis(la, 1, 0), jnp.moveaxis(xd, 1, 0),
                                        jnp.moveaxis(bm, 1, 0), jnp.moveaxis(cm, 1, 0)))
    y = jnp.moveaxis(ys, 0, 1).reshape(bsz, seq, nh, hp)
    return y, s_fin.reshape(bsz, nh, hp, ns)


def sink_softmax(s, sink):
    m = jnp.maximum(jnp.max(s, axis=-1, keepdims=True), sink)
    e = jnp.exp(s - m)
    return e / (jnp.sum(e, axis=-1, keepdims=True) + jnp.exp(sink - m))


def swa_banded(q, k, v, sinks):
    bsz, seq, _, hd = q.shape
    rep = ATT_Q_HEADS // ATT_KV_HEADS
    blk = WINDOW
    nb = seq // blk
    qb = q.reshape(bsz, nb, blk, ATT_KV_HEADS, rep, hd)
    kb = k.reshape(bsz, nb, blk, ATT_KV_HEADS, hd)
    vb = v.reshape(bsz, nb, blk, ATT_KV_HEADS, hd)
    shift = lambda t: jnp.concatenate([jnp.zeros_like(t[:, :1]), t[:, :-1]], axis=1)
    kk = jnp.concatenate([shift(kb), kb], axis=2)
    vv = jnp.concatenate([shift(vb), vb], axis=2)
    s = jnp.einsum('bnigrd,bnjgd->bngrij', qb, kk).astype(jnp.float32) * (hd ** -0.5)
    kj = jnp.arange(2 * blk)
    rel = (jnp.arange(blk)[:, None] + blk) - kj[None, :]
    band = (rel >= 0) & (rel < WINDOW)
    has_prev = jnp.arange(nb)[:, None, None] > 0
    mask = band[None] & (has_prev | (kj >= blk)[None, None, :])
    s = jnp.where(mask[None, :, None, None], s, -jnp.inf)
    sink = sinks.astype(jnp.float32).reshape(ATT_KV_HEADS, rep)[None, None, :, :, None, None]
    p = sink_softmax(s, sink).astype(v.dtype)
    o = jnp.einsum('bngrij,bnjgd->bnigrd', p, vv)
    return o.reshape(bsz, seq, ATT_Q_HEADS * hd)


def swa_decode(q, k, v, kbuf, vbuf, sinks):
    bsz, t, _, hd = q.shape
    rep = ATT_Q_HEADS // ATT_KV_HEADS
    w = kbuf.shape[1]
    kk = jnp.concatenate([kbuf.astype(k.dtype), k], axis=1)
    vv = jnp.concatenate([vbuf.astype(v.dtype), v], axis=1)
    qpos = PAST_LEN + jnp.arange(t)
    kpos = PAST_LEN - w + jnp.arange(w + t)
    rel = qpos[:, None] - kpos[None, :]
    mask = (rel >= 0) & (rel < WINDOW)
    qg = q.reshape(bsz, t, ATT_KV_HEADS, rep, hd)
    s = jnp.einsum('btgrd,bsgd->bgrts', qg, kk).astype(jnp.float32) * (hd ** -0.5)
    s = jnp.where(mask[None, None, None], s, -jnp.inf)
    sink = sinks.astype(jnp.float32).reshape(ATT_KV_HEADS, rep)[None, :, :, None, None]
    p = sink_softmax(s, sink).astype(v.dtype)
    o = jnp.einsum('bgrts,bsgd->btgrd', p, vv).reshape(bsz, t, ATT_Q_HEADS * hd)
    return o, kk[:, -w:], vv[:, -w:]


def retention_scan(q, k, v, s0, log_gamma):
    f32 = jnp.float32
    bsz, seq, nh, _ = q.shape
    dv = v.shape[-1]
    c = RET_CHUNK if seq % RET_CHUNK == 0 else seq
    nc = seq // c
    idx = jnp.arange(c, dtype=f32)
    rel = idx[:, None] - idx[None, :]
    intra = jnp.exp(jnp.where(rel[None] >= 0, rel[None] * log_gamma[:, None, None], -jnp.inf))
    from_start = jnp.exp((idx[:, None] + 1.0) * log_gamma[None, :])
    to_end = jnp.exp((c - 1.0 - idx)[:, None] * log_gamma[None, :])
    chunk_decay = jnp.exp(c * log_gamma)
    chunked = lambda a: jnp.moveaxis(a.astype(f32).reshape(bsz, nc, c, nh, a.shape[-1]), 1, 0)

    def step(s, inp):
        q_k, k_k, v_k = inp
        att = jnp.einsum('bihd,bjhd->bhij', q_k, k_k) * intra[None]
        o = jnp.einsum('bhij,bjhe->bihe', att, v_k)
        o = o + jnp.einsum('bihd,bhde->bihe', q_k, s) * from_start[None, :, :, None]
        s = (s * chunk_decay[None, :, None, None]
             + jnp.einsum('bjhd,bjhe->bhde', k_k * to_end[None, :, :, None], v_k))
        return s, o

    s_fin, os_ = lax.scan(step, s0.astype(f32), (chunked(q), chunked(k), chunked(v)))
    return jnp.moveaxis(os_, 0, 1).reshape(bsz, seq, nh, dv), s_fin


def token_mixer(h, pos, ssm0, conv0, kbuf, vbuf, ret0,
                w_in, conv_w, conv_b, dt_bias, a_log, d_skip, ssd_norm, attn_sinks, ret_norm,
                w_o_ssd, w_o_att, w_o_ret, w_out):
    f32 = jnp.float32
    bsz, seq, _ = h.shape
    splits = np.cumsum(IN_WIDTHS)[:-1].tolist()
    z, xbc, dt_raw, qa, ka, va, qr, kr, vr, gr, gates = jnp.split(h @ w_in, splits, axis=-1)

    xbc, conv_new = causal_conv(xbc, conv0, conv_w, conv_b)
    xs, bm, cm = jnp.split(xbc, [SSD_D_INNER, SSD_D_INNER + SSD_GROUPS * SSD_STATE], axis=-1)
    xs = xs.reshape(bsz, seq, SSD_HEADS, SSD_HEAD_DIM)
    bm = bm.reshape(bsz, seq, SSD_GROUPS, SSD_STATE)
    cm = cm.reshape(bsz, seq, SSD_GROUPS, SSD_STATE)
    dt = jax.nn.softplus(dt_raw.astype(f32) + dt_bias.astype(f32))
    a = -jnp.exp(a_log.astype(f32))
    y, ssm_new = ssd_scan(xs, dt, a, bm, cm, ssm0)
    y = y + d_skip.astype(f32)[:, None] * xs.astype(f32)
    y = y.reshape(bsz, seq, SSD_D_INNER) * jax.nn.silu(z.astype(f32))
    y_ssd = rms_norm(y, ssd_norm).astype(h.dtype)

    q = rope(qa.reshape(bsz, seq, ATT_Q_HEADS, ATT_HEAD_DIM), pos, ROPE_DIM, ROPE_THETA)
    k = rope(ka.reshape(bsz, seq, ATT_KV_HEADS, ATT_HEAD_DIM), pos, ROPE_DIM, ROPE_THETA)
    v = va.reshape(bsz, seq, ATT_KV_HEADS, ATT_HEAD_DIM)
    if kbuf is None:
        o_att = swa_banded(q, k, v, attn_sinks)
        keep = min(WINDOW, seq)
        kbuf_new, vbuf_new = k[:, seq - keep:], v[:, seq - keep:]
    else:
        o_att, kbuf_new, vbuf_new = swa_decode(q, k, v, kbuf, vbuf, attn_sinks)

    log_gamma = jnp.log1p(-jnp.exp2(-5.0 - jnp.arange(RET_HEADS, dtype=f32)))
    qr = rope(qr.reshape(bsz, seq, RET_HEADS, RET_DK), pos, RET_DK, RET_THETA)
    kr = rope(kr.reshape(bsz, seq, RET_HEADS, RET_DK), pos, RET_DK, RET_THETA) * (RET_DK ** -0.5)
    vr = vr.reshape(bsz, seq, RET_HEADS, RET_DV)
    o_r, ret_new = retention_scan(qr, kr, vr, ret0, log_gamma)
    o_r = o_r * lax.rsqrt(jnp.mean(o_r * o_r, axis=-1, keepdims=True) + EPS)
    o_r = o_r.reshape(bsz, seq, RET_HEADS * RET_DV) * ret_norm.astype(f32) * jax.nn.silu(gr.astype(f32))
    o_ret = o_r.astype(h.dtype)

    g_ssd, g_att, g_ret = jnp.split(jax.nn.sigmoid(gates), 3, axis=-1)
    merged = g_ssd * (y_ssd @ w_o_ssd) + g_att * (o_att @ w_o_att) + g_ret * (o_ret @ w_o_ret)
    return merged @ w_out, (ssm_new, conv_new, kbuf_new, vbuf_new, ret_new)


def swiglu(h, w_gate, w_up, w_down):
    return (jax.nn.silu(h @ w_gate) * (h @ w_up)) @ w_down


def moe_swiglu(h, router_w, router_b, w_gate, w_up, w_down):
    logits = (h @ router_w).astype(jnp.float32) + router_b.astype(jnp.float32)
    top_val, top_idx = lax.top_k(logits, TOP_K)
    top_p = jax.nn.softmax(top_val, axis=-1)
    combine = jnp.einsum('blk,blke->ble', top_p, jax.nn.one_hot(top_idx, N_EXPERTS, dtype=jnp.float32))
    out = jnp.zeros_like(h)
    for e in range(N_EXPERTS):
        out = out + (combine[..., e:e + 1] * swiglu(h, w_gate[e], w_up[e], w_down[e])).astype(h.dtype)
    return out


def block(x, p_i, pos, state, mix_w, ffn_fn, g_mix, g_ffn, g_ple, w_ple_i, w_ple_gate_i):
    mix, new_state = token_mixer(rms_norm(x, g_mix), pos, *state, *mix_w)
    x = x + mix
    x = x + ffn_fn(rms_norm(x, g_ffn))
    x = x + (p_i.astype(x.dtype) @ w_ple_i) * jax.nn.sigmoid(rms_norm(x, g_ple) @ w_ple_gate_i)
    return x, new_state


def setup_inputs(seed: int = 0) -> dict:
    key = jax.random.key(seed)
    keys = jax.random.split(key, 64)
    counter = iter(range(64))
    f32 = jnp.float32
    nk = lambda: keys[next(counter)]
    nrm = lambda shape, scale: jax.random.normal(nk(), shape, f32) * scale
    gain = lambda shape: 1.0 + 0.01 * jax.random.normal(nk(), shape, f32)
    win = min(WINDOW, PAST_LEN)
    dt0 = jnp.exp(jax.random.uniform(nk(), (DEPTH, SSD_HEADS), f32, math.log(1e-3), math.log(1e-1)))
    dt_bias = dt0 + jnp.log(-jnp.expm1(-dt0))
    a_log = jnp.log(jax.random.uniform(nk(), (DEPTH, SSD_HEADS), f32, 1.0, 16.0))
    return {
        'x_prompt': nrm((BATCH, SEQ, D_MODEL), 1.0),
        'x_sample': nrm((DEC_BATCH, DEC_SEQ, D_MODEL), 1.0),
        'state_ssm': nrm((DEPTH, DEC_BATCH, SSD_HEADS, SSD_HEAD_DIM, SSD_STATE), 0.5),
        'state_conv': nrm((DEPTH, DEC_BATCH, SSD_CONV - 1, SSD_CONV_DIM), 1.0),
        'cache_win_k': nrm((DEPTH, DEC_BATCH, win, ATT_KV_HEADS, ATT_HEAD_DIM), 1.0),
        'cache_win_v': nrm((DEPTH, DEC_BATCH, win, ATT_KV_HEADS, ATT_HEAD_DIM), 1.0),
        'state_ret': nrm((DEPTH, DEC_BATCH, RET_HEADS, RET_DK, RET_DV), 1.0),
        'p_prompt': nrm((DEPTH, BATCH, SEQ, PLE_DIM), 1.0),
        'p_sample': nrm((DEPTH, DEC_BATCH, DEC_SEQ, PLE_DIM), 1.0),
        'w_in': nrm((DEPTH, D_MODEL, N_IN), D_MODEL ** -0.5),
        'conv_w': nrm((DEPTH, SSD_CONV, SSD_CONV_DIM), SSD_CONV ** -0.5),
        'conv_b': nrm((DEPTH, SSD_CONV_DIM), 0.02),
        'dt_bias': dt_bias,
        'a_log': a_log,
        'd_skip': gain((DEPTH, SSD_HEADS)),
        'ssd_norm': gain((DEPTH, SSD_D_INNER)),
        'attn_sinks': nrm((DEPTH, ATT_Q_HEADS), 0.5),
        'ret_norm': gain((DEPTH, RET_HEADS * RET_DV)),
        'w_o_ssd': nrm((DEPTH, SSD_D_INNER, D_MODEL), SSD_D_INNER ** -0.5),
        'w_o_att': nrm((DEPTH, ATT_Q_HEADS * ATT_HEAD_DIM, D_MODEL), (ATT_Q_HEADS * ATT_HEAD_DIM) ** -0.5),
        'w_o_ret': nrm((DEPTH, RET_HEADS * RET_DV, D_MODEL), (RET_HEADS * RET_DV) ** -0.5),
        'w_out': nrm((DEPTH, D_MODEL, D_MODEL), D_MODEL ** -0.5),
        'norm_mix': gain((DEPTH, D_MODEL)),
        'norm_ffn': gain((DEPTH, D_MODEL)),
        'norm_ple': gain((DEPTH, D_MODEL)),
        'ffn_w_gate': nrm((N_DENSE_LAYERS, D_MODEL, D_FF), D_MODEL ** -0.5),
        'ffn_w_up': nrm((N_DENSE_LAYERS, D_MODEL, D_FF), D_MODEL ** -0.5),
        'ffn_w_down': nrm((N_DENSE_LAYERS, D_FF, D_MODEL), D_FF ** -0.5),
        'router_w': nrm((N_MOE_LAYERS, D_MODEL, N_EXPERTS), D_MODEL ** -0.5),
        'router_b': nrm((N_MOE_LAYERS, N_EXPERTS), 0.01),
        'moe_w_gate': nrm((N_MOE_LAYERS, N_EXPERTS, D_MODEL, D_FF), D_MODEL ** -0.5),
        'moe_w_up': nrm((N_MOE_LAYERS, N_EXPERTS, D_MODEL, D_FF), D_MODEL ** -0.5),
        'moe_w_down': nrm((N_MOE_LAYERS, N_EXPERTS, D_FF, D_MODEL), D_FF ** -0.5),
        'w_ple': nrm((DEPTH, PLE_DIM, D_MODEL), PLE_DIM ** -0.5),
        'w_ple_gate': nrm((DEPTH, D_MODEL, D_MODEL), D_MODEL ** -0.5),
        'norm_final': gain((D_MODEL,)),
    }


def reference(x_prompt, x_sample, state_ssm, state_conv, cache_win_k, cache_win_v, state_ret,
              p_prompt, p_sample, w_in, conv_w, conv_b, dt_bias, a_log, d_skip, ssd_norm,
              attn_sinks, ret_norm, w_o_ssd, w_o_att, w_o_ret, w_out, norm_mix, norm_ffn, norm_ple,
              ffn_w_gate, ffn_w_up, ffn_w_down, router_w, router_b, moe_w_gate, moe_w_up, moe_w_down,
              w_ple, w_ple_gate, norm_final):
    bp = x_prompt.shape[0]
    pos_p = jnp.arange(x_prompt.shape[1])
    pos_s = PAST_LEN + jnp.arange(x_sample.shape[1])
    fresh = (jnp.zeros((bp, SSD_HEADS, SSD_HEAD_DIM, SSD_STATE), jnp.float32),
             jnp.zeros((bp, SSD_CONV - 1, SSD_CONV_DIM), x_prompt.dtype),
             None, None,
             jnp.zeros((bp, RET_HEADS, RET_DK, RET_DV), jnp.float32))
    xp, xs = x_prompt, x_sample
    new_p = [[], [], [], [], []]
    new_s = [[], [], [], [], []]
    for i in range(DEPTH):
        mix_w = (w_in[i], conv_w[i], conv_b[i], dt_bias[i], a_log[i], d_skip[i], ssd_norm[i],
                 attn_sinks[i], ret_norm[i], w_o_ssd[i], w_o_att[i], w_o_ret[i], w_out[i])
        j = i // 2
        if i % 2 == 0:
            ffn_fn = lambda t, j=j: swiglu(t, ffn_w_gate[j], ffn_w_up[j], ffn_w_down[j])
        else:
            ffn_fn = lambda t, j=j: moe_swiglu(t, router_w[j], router_b[j], moe_w_gate[j],
                                               moe_w_up[j], moe_w_down[j])
        ple_w = (norm_mix[i], norm_ffn[i], norm_ple[i], w_ple[i], w_ple_gate[i])
        xp, st_p = block(xp, p_prompt[i], pos_p, fresh, mix_w, ffn_fn, *ple_w)
        past = (state_ssm[i], state_conv[i], cache_win_k[i], cache_win_v[i], state_ret[i])
        xs, st_s = block(xs, p_sample[i], pos_s, past, mix_w, ffn_fn, *ple_w)
        for lst, val in zip(new_p, st_p):
            lst.append(val)
        for lst, val in zip(new_s, st_s):
            lst.append(val)
    y_prompt = rms_norm(xp, norm_final)
    y_sample = rms_norm(xs, norm_final)
    ssm_p, conv_p, wk_p, wv_p, ret_p = [jnp.stack(l) for l in new_p]
    ssm_s, conv_s, wk_s, wv_s, ret_s = [jnp.stack(l) for l in new_s]
    return (y_prompt, y_sample, ssm_p, conv_p, wk_p, wv_p, ret_p, ssm_s, conv_s, wk_s, wv_s, ret_s)
```

```python
import functools
import math

import jax
import jax.numpy as jnp
from jax import lax
from jax.experimental import pallas as pl
from jax.experimental.pallas import tpu as pltpu

F32 = jnp.float32
BF16 = jnp.bfloat16

D_MODEL = 1024
DEPTH = 2
PAST_LEN = 16384
SSD_HEADS = 16
SSD_HEAD_DIM = 64
SSD_GROUPS = 4
SSD_STATE = 128
SSD_CONV = 4
SSD_CONV_DIM = 2048
ATT_HEAD_DIM = 64
ATT_Q_HEADS = 16
ATT_KV_HEADS = 4
ATT_REP = ATT_Q_HEADS // ATT_KV_HEADS
WINDOW = 128
ROPE_THETA = 500000.0
ROPE_DIM = 16
RET_HEADS = 8
RET_DK = 128
RET_THETA = 10000.0
CHUNK = 128
D_FF = 2816
N_EXPERTS = 8
PLE_DIM = 256
EPS = 1e-6

IN_WIDTHS = (1024, 2048, 16, 1024, 256, 256, 1024, 1024, 1024, 1024, 3072)
N_PROJ = 12288
COL_XBC = 0
COL_Z = 2
COL_QA = 3
COL_QR = 4
COL_KR = 5
COL_VR = 6
COL_GR = 7
COL_GATE = 8
COL_KA = 44
COL_VA = 45
COL_DT = 92

LANES = 128
VMEM_LIMIT = 48 * 1024 * 1024


def _cparams(sem, vmem=VMEM_LIMIT):
    return pltpu.CompilerParams(dimension_semantics=sem, vmem_limit_bytes=vmem)


def _bdot(a, b):
    return jnp.dot(a.astype(BF16), b.astype(BF16), preferred_element_type=F32)


def _bdot_nt(a, b):
    return lax.dot_general(a.astype(BF16), b.astype(BF16), (((1,), (1,)), ((), ())),
                           preferred_element_type=F32)


def _split3(x):
    x0 = x.astype(BF16)
    r1 = x - x0.astype(F32)
    x1 = r1.astype(BF16)
    x2 = (r1 - x1.astype(F32)).astype(BF16)
    return x0, x1, x2


def _dot_exact_lhs01(m01, x):
    m = m01.astype(BF16)
    x0, x1, x2 = _split3(x)
    d = lambda b: jnp.dot(m, b, preferred_element_type=F32)
    return d(x0) + d(x1) + d(x2)


def _dot_exact_rhs01(x, m01):
    m = m01.astype(BF16)
    x0, x1, x2 = _split3(x)
    d = lambda a: jnp.dot(a, m, preferred_element_type=F32)
    return d(x0) + d(x1) + d(x2)


def _rms(x, g):
    return x * lax.rsqrt(jnp.mean(x * x, axis=-1, keepdims=True) + EPS) * g


def _sigmoid(x):
    return 1.0 / (1.0 + jnp.exp(-x))


def _silu(x):
    return x * _sigmoid(x)


def _softplus(x):
    return jnp.maximum(x, 0.0) + jnp.log1p(jnp.exp(-jnp.abs(x)))


def _rope_att(x, c, s1, s2):
    w = x.shape[1]
    return x * c + pltpu.roll(x, 8, axis=1) * s1 + pltpu.roll(x, w - 8, axis=1) * s2


def _tile_lanes(t, n):
    return jnp.concatenate([t] * n, axis=1) if n > 1 else t


def _inproj_kernel(x_ref, g_ref, w_ref, o_ref, h_ref):
    @pl.when(pl.program_id(1) == 0)
    def _():
        h_ref[...] = _rms(x_ref[...], g_ref[...]).astype(BF16)

    o_ref[...] = jnp.dot(h_ref[...], w_ref[...], preferred_element_type=F32)


def _inproj(x, g, w):
    rows = x.shape[0]
    tm = min(rows, 1024)
    tn = 1024
    return pl.pallas_call(
        _inproj_kernel,
        out_shape=jax.ShapeDtypeStruct((rows, N_PROJ), F32),
        grid=(rows // tm, N_PROJ // tn),
        in_specs=[pl.BlockSpec((tm, D_MODEL), lambda i, j: (i, 0)),
                  pl.BlockSpec((1, D_MODEL), lambda i, j: (0, 0)),
                  pl.BlockSpec((D_MODEL, tn), lambda i, j: (0, j))],
        out_specs=pl.BlockSpec((tm, tn), lambda i, j: (i, j)),
        scratch_shapes=[pltpu.VMEM((tm, D_MODEL), BF16)],
        compiler_params=_cparams(("parallel", "arbitrary")),
        name="inproj",
    )(x, g, w)


def _ssd_prompt_kernel(xbc_ref, z_ref, dt_ref, cw_ref, cb_ref, dtb_ref, alog_ref, dsk_ref, nrm_ref,
                       y_ref, sfin_ref, cfin_ref, xpad_ref, s_ref):
    t = pl.program_id(0)
    nt = pl.num_programs(0)

    @pl.when(t == 0)
    def _():
        xpad_ref[0:8, :] = jnp.zeros((8, SSD_CONV_DIM), F32)
        s_ref[...] = jnp.zeros_like(s_ref)

    xbc = xbc_ref[...]
    xpad_ref[8:8 + CHUNK, :] = xbc
    cw = cw_ref[...]
    acc = (xbc * cw[3:4, :] + xpad_ref[7:7 + CHUNK, :] * cw[2:3, :]
           + xpad_ref[6:6 + CHUNK, :] * cw[1:2, :] + xpad_ref[5:5 + CHUNK, :] * cw[0:1, :] + cb_ref[...])
    conv = _silu(acc)
    xpad_ref[0:8, :] = xbc[CHUNK - 8:CHUNK, :]

    xs = conv[:, :1024]
    dt = _softplus(dt_ref[...] + dtb_ref[...])
    la = dt * (-jnp.exp(alog_ref[...]))
    row = lax.broadcasted_iota(jnp.int32, (CHUNK, CHUNK), 0)
    col = lax.broadcasted_iota(jnp.int32, (CHUNK, CHUNK), 1)
    causal = row >= col
    cum = _dot_exact_lhs01(causal.astype(F32), la)
    cum_t = cum.T
    dt_t = dt.T
    cum_last = jnp.broadcast_to(cum_t[:, CHUNK - 1:CHUNK], (LANES, CHUNK))
    w_t = jnp.exp(cum_last - cum_t) * dt_t
    dec_end = jnp.exp(cum_last)

    xs_t = xs.T
    ys = []
    for g in range(SSD_GROUPS):
        bg = conv[:, 1024 + 128 * g:1024 + 128 * (g + 1)]
        cg = conv[:, 1536 + 128 * g:1536 + 128 * (g + 1)]
        cb = _bdot_nt(cg, bg)
        s_g = s_ref[256 * g:256 * (g + 1), :]
        cs = _bdot_nt(cg, s_g)
        xw_parts = []
        dec_parts = []
        for r in range(4):
            h = 4 * g + r
            colb = jnp.broadcast_to(cum[:, h:h + 1], (CHUNK, CHUNK))
            rowb = jnp.broadcast_to(cum_t[h:h + 1, :], (CHUNK, CHUNK))
            dec = jnp.exp(jnp.where(causal, colb - rowb, -jnp.inf))
            m = cb * dec * jnp.broadcast_to(dt_t[h:h + 1, :], (CHUNK, CHUNK))
            xh = xs[:, 64 * h:64 * (h + 1)]
            yh = _bdot(m, xh) + cs[:, 64 * r:64 * (r + 1)] * jnp.exp(colb)[:, :64]
            ys.append(yh)
            xw_parts.append(xs_t[64 * h:64 * (h + 1), :] * jnp.broadcast_to(w_t[h:h + 1, :], (64, CHUNK)))
            dec_parts.append(jnp.broadcast_to(dec_end[h:h + 1, :], (64, SSD_STATE)))
        xw = jnp.concatenate(xw_parts, axis=0)
        s_ref[256 * g:256 * (g + 1), :] = s_g * jnp.concatenate(dec_parts, axis=0) + _bdot(xw, bg)

    y = jnp.concatenate(ys, axis=1) + dsk_ref[...] * xs
    y = y * _silu(z_ref[...])
    y_ref[...] = _rms(y, nrm_ref[...])

    @pl.when(t == nt - 1)
    def _():
        sfin_ref[...] = s_ref[...]
        cfin_ref[...] = xbc[CHUNK - 3:CHUNK, :]


def _ssd_prompt(proj, cw, cb, dtb, alog, dsk, nrm):
    seq = proj.shape[0]
    const = lambda shape: pl.BlockSpec(shape, lambda t: (0,) * len(shape))
    return pl.pallas_call(
        _ssd_prompt_kernel,
        out_shape=(jax.ShapeDtypeStruct((seq, 1024), F32),
                   jax.ShapeDtypeStruct((1024, SSD_STATE), F32),
                   jax.ShapeDtypeStruct((SSD_CONV - 1, SSD_CONV_DIM), F32)),
        grid=(seq // CHUNK,),
        in_specs=[pl.BlockSpec((CHUNK, 2048), lambda t: (t, COL_XBC)),
                  pl.BlockSpec((CHUNK, 1024), lambda t: (t, COL_Z)),
                  pl.BlockSpec((CHUNK, 128), lambda t: (t, COL_DT)),
                  const((SSD_CONV, 2048)), const((1, 2048)), const((1, 128)), const((1, 128)),
                  const((1, 1024)), const((1, 1024))],
        out_specs=(pl.BlockSpec((CHUNK, 1024), lambda t: (t, 0)),
                   const((1024, SSD_STATE)), const((SSD_CONV - 1, SSD_CONV_DIM))),
        scratch_shapes=[pltpu.VMEM((8 + CHUNK, SSD_CONV_DIM), F32), pltpu.VMEM((1024, SSD_STATE), F32)],
        compiler_params=_cparams(("arbitrary",)),
        name="ssd_prompt",
    )(proj, proj, proj, cw, cb, dtb, alog, dsk, nrm)


def _ssd_dec_pre_kernel(xbc_ref, dt_ref, cst_ref, cw_ref, cb_ref, dtb_ref, alog_ref,
                        cnew_ref, xs_ref, bc_ref, dec_t_ref, xdt_t_ref):
    xbc = xbc_ref[...]
    cw = cw_ref[...]
    acc = (cst_ref[0] * cw[0:1, :] + cst_ref[1] * cw[1:2, :] + cst_ref[2] * cw[2:3, :]
           + xbc * cw[3:4, :] + cb_ref[...])
    conv = _silu(acc)
    cnew_ref[0] = cst_ref[1]
    cnew_ref[1] = cst_ref[2]
    cnew_ref[2] = xbc
    xs = conv[:, :1024]
    xs_ref[...] = xs
    bc_ref[...] = conv[:, 1024:]
    dt = _softplus(dt_ref[...] + dtb_ref[...])
    dec = jnp.exp(dt * (-jnp.exp(alog_ref[...])))
    hrow = lax.broadcasted_iota(jnp.int32, (LANES, 1024), 0)
    hcol = lax.broadcasted_iota(jnp.int32, (LANES, 1024), 1)
    expand = ((hcol >> 6) == hrow).astype(F32)
    dec_t_ref[...] = _dot_exact_rhs01(dec, expand).T
    xdt_t_ref[...] = (xs * _dot_exact_rhs01(dt, expand)).T


def _ssd_dec_pre(proj, cst_t, cw, cb, dtb, alog):
    nb = proj.shape[0]
    const = lambda shape: pl.BlockSpec(shape, lambda t: (0,) * len(shape))
    return pl.pallas_call(
        _ssd_dec_pre_kernel,
        out_shape=(jax.ShapeDtypeStruct((3, nb, 2048), F32), jax.ShapeDtypeStruct((nb, 1024), F32),
                   jax.ShapeDtypeStruct((nb, 1024), F32), jax.ShapeDtypeStruct((1024, nb), F32),
                   jax.ShapeDtypeStruct((1024, nb), F32)),
        grid=(1,),
        in_specs=[pl.BlockSpec((nb, 2048), lambda t: (0, COL_XBC)),
                  pl.BlockSpec((nb, 128), lambda t: (0, COL_DT)),
                  const((3, nb, 2048)), const((SSD_CONV, 2048)), const((1, 2048)), const((1, 128)), const((1, 128))],
        out_specs=(const((3, nb, 2048)), const((nb, 1024)), const((nb, 1024)), const((1024, nb)), const((1024, nb))),
        compiler_params=_cparams(("arbitrary",)),
        name="ssd_dec_pre",
    )(proj, proj, cst_t, cw, cb, dtb, alog)


DEC_BLOCK = 8


def _ssd_dec_state_kernel(dec_t_ref, xdt_t_ref, bc_ref, z_ref, xs_ref, st_ref, dsk_ref, nrm_ref,
                          y_ref, stn_ref):
    i = pl.program_id(0)
    shift = (LANES - i * DEC_BLOCK) % LANES
    decr = pltpu.roll(dec_t_ref[...], shift, axis=1)
    xr = pltpu.roll(xdt_t_ref[...], shift, axis=1)
    bc = bc_ref[...]
    rowid = lax.broadcasted_iota(jnp.int32, (DEC_BLOCK, SSD_STATE), 0)
    ys = [jnp.zeros((DEC_BLOCK, 256), F32) for _ in range(SSD_GROUPS)]
    for j in range(DEC_BLOCK):
        for g in range(SSD_GROUPS):
            lo, hi = 256 * g, 256 * (g + 1)
            s_old = st_ref[j, lo:hi, :]
            dcol = jnp.broadcast_to(decr[lo:hi, j:j + 1], (256, SSD_STATE))
            xcol = jnp.broadcast_to(xr[lo:hi, j:j + 1], (256, SSD_STATE))
            s_new = s_old * dcol + xcol * bc[j:j + 1, 128 * g:128 * (g + 1)]
            stn_ref[j, lo:hi, :] = s_new
            cm = jnp.where(rowid == j, bc[:, 512 + 128 * g:512 + 128 * (g + 1)], 0.0)
            ys[g] = ys[g] + _bdot_nt(cm, s_new)
    xs = xs_ref[...]
    y = jnp.concatenate(ys, axis=1) + dsk_ref[...] * xs
    y = y * _silu(z_ref[...])
    y_ref[...] = _rms(y, nrm_ref[...])


def _ssd_dec_state(dec_t, xdt_t, bc, proj, xs, st, dsk, nrm):
    nb = xs.shape[0]
    const = lambda shape: pl.BlockSpec(shape, lambda t: (0,) * len(shape))
    blk = lambda w: pl.BlockSpec((DEC_BLOCK, w), lambda t: (t, 0))
    return pl.pallas_call(
        _ssd_dec_state_kernel,
        out_shape=(jax.ShapeDtypeStruct((nb, 1024), F32), jax.ShapeDtypeStruct((nb, 1024, SSD_STATE), F32)),
        grid=(nb // DEC_BLOCK,),
        in_specs=[const((1024, nb)), const((1024, nb)), blk(1024),
                  pl.BlockSpec((DEC_BLOCK, 1024), lambda t: (t, COL_Z)), blk(1024),
                  pl.BlockSpec((DEC_BLOCK, 1024, SSD_STATE), lambda t: (t, 0, 0)),
                  const((1, 1024)), const((1, 1024))],
        out_specs=(blk(1024), pl.BlockSpec((DEC_BLOCK, 1024, SSD_STATE), lambda t: (t, 0, 0))),
        compiler_params=_cparams(("arbitrary",)),
        name="ssd_dec_state",
    )(dec_t, xdt_t, bc, proj, xs, st, dsk, nrm)


def _swa_prompt_kernel(sink_ref, q_ref, k_ref, v_ref, c_ref, s1_ref, s2_ref,
                       o_ref, wk_ref, wv_ref, kp_ref, vp_ref):
    n = pl.program_id(0)
    nb = pl.num_programs(0)

    @pl.when(n == 0)
    def _():
        kp_ref[...] = jnp.zeros_like(kp_ref)
        vp_ref[...] = jnp.zeros_like(vp_ref)

    c, s1, s2 = c_ref[...], s1_ref[...], s2_ref[...]
    q = _rope_att(q_ref[...], _tile_lanes(c, 8), _tile_lanes(s1, 8), _tile_lanes(s2, 8))
    k = _rope_att(k_ref[...], _tile_lanes(c, 2), _tile_lanes(s1, 2), _tile_lanes(s2, 2))
    v = v_ref[...]
    kp = kp_ref[...]
    vp = vp_ref[...]
    row = lax.broadcasted_iota(jnp.int32, (WINDOW, WINDOW), 0)
    col = lax.broadcasted_iota(jnp.int32, (WINDOW, WINDOW), 1)
    mask_prev = jnp.logical_and(col > row, n > 0)
    mask_cur = col <= row
    scale = ATT_HEAD_DIM ** -0.5
    for g in range(ATT_KV_HEADS):
        sl = slice(64 * g, 64 * (g + 1))
        kpg, kcg, vpg, vcg = kp[:, sl], k[:, sl], vp[:, sl], v[:, sl]
        for r in range(ATT_REP):
            lo = 256 * r + 64 * g
            qh = q[:, lo:lo + 64]
            sp = jnp.where(mask_prev, _bdot_nt(qh, kpg) * scale, -jnp.inf)
            sc = jnp.where(mask_cur, _bdot_nt(qh, kcg) * scale, -jnp.inf)
            sink = sink_ref[4 * g + r]
            m = jnp.maximum(jnp.maximum(jnp.max(sp, axis=1, keepdims=True), jnp.max(sc, axis=1, keepdims=True)), sink)
            ep = jnp.exp(sp - m)
            ec = jnp.exp(sc - m)
            den = jnp.sum(ep, axis=1, keepdims=True) + jnp.sum(ec, axis=1, keepdims=True) + jnp.exp(sink - m)
            inv = 1.0 / den
            o_ref[:, lo:lo + 64] = _bdot(ep * inv, vpg) + _bdot(ec * inv, vcg)
    kp_ref[...] = k
    vp_ref[...] = v

    @pl.when(n == nb - 1)
    def _():
        wk_ref[...] = k
        wv_ref[...] = v


def _swa_prompt(sinks, proj, tabs):
    seq = proj.shape[0]
    c, s1, s2 = tabs
    const = lambda shape: pl.BlockSpec(shape, lambda t: (0,) * len(shape))
    tab = pl.BlockSpec((WINDOW, LANES), lambda t: (t, 0))
    return pl.pallas_call(
        _swa_prompt_kernel,
        out_shape=(jax.ShapeDtypeStruct((seq, 1024), F32),
                   jax.ShapeDtypeStruct((WINDOW, 256), F32), jax.ShapeDtypeStruct((WINDOW, 256), F32)),
        grid=(seq // WINDOW,),
        in_specs=[pl.BlockSpec(memory_space=pltpu.SMEM),
                  pl.BlockSpec((WINDOW, 1024), lambda t: (t, COL_QA)),
                  pl.BlockSpec((WINDOW, 256), lambda t: (t, COL_KA)),
                  pl.BlockSpec((WINDOW, 256), lambda t: (t, COL_VA)),
                  tab, tab, tab],
        out_specs=(pl.BlockSpec((WINDOW, 1024), lambda t: (t, 0)), const((WINDOW, 256)), const((WINDOW, 256))),
        scratch_shapes=[pltpu.VMEM((WINDOW, 256), F32), pltpu.VMEM((WINDOW, 256), F32)],
        compiler_params=_cparams(("arbitrary",)),
        name="swa_prompt",
    )(sinks, proj, proj, proj, c, s1, s2)


def _swa_dec_kernel(q_ref, k_ref, v_ref, kc_ref, vc_ref, c_ref, s1_ref, s2_ref, sink_ref,
                    o_ref, kcn_ref, vcn_ref):
    c, s1, s2 = c_ref[...], s1_ref[...], s2_ref[...]
    q = _rope_att(q_ref[...], _tile_lanes(c, 8), _tile_lanes(s1, 8), _tile_lanes(s2, 8))
    k = _rope_att(k_ref[...], _tile_lanes(c, 2), _tile_lanes(s1, 2), _tile_lanes(s2, 2))
    v = v_ref[...]
    grow = lax.broadcasted_iota(jnp.int32, (8, 256), 0)
    gcol = lax.broadcasted_iota(jnp.int32, (8, 256), 1)
    gmask = ((gcol >> 6) == grow).astype(F32)
    gmask4 = jnp.concatenate([gmask] * ATT_REP, axis=0)
    lane = lax.broadcasted_iota(jnp.int32, (8 * ATT_REP, WINDOW), 1)
    sink = sink_ref[...][:, 0:1]
    scale = ATT_HEAD_DIM ** -0.5
    for j in range(DEC_BLOCK):
        a = jnp.concatenate([jnp.broadcast_to(q[j:j + 1, 256 * r:256 * (r + 1)], (8, 256)) * gmask
                             for r in range(ATT_REP)], axis=0)
        kb = kc_ref[j]
        vb = vc_ref[j]
        knew = k[j:j + 1, :]
        vnew = v[j:j + 1, :]
        s = jnp.where(lane == 0, -jnp.inf, _bdot_nt(a, kb) * scale)
        snew = jnp.sum(a * knew, axis=1, keepdims=True) * scale
        m = jnp.maximum(jnp.maximum(jnp.max(s, axis=1, keepdims=True), snew), sink)
        e = jnp.exp(s - m)
        enew = jnp.exp(snew - m)
        inv = 1.0 / (jnp.sum(e, axis=1, keepdims=True) + enew + jnp.exp(sink - m))
        o = (_bdot(e * inv, vb) + (enew * inv) * vnew) * gmask4
        for r in range(ATT_REP):
            o_ref[j:j + 1, 256 * r:256 * (r + 1)] = jnp.sum(o[8 * r:8 * (r + 1), :], axis=0, keepdims=True)
        kcn_ref[j, 0:WINDOW - 1, :] = kc_ref[j, 1:WINDOW, :]
        kcn_ref[j, WINDOW - 1:WINDOW, :] = knew
        vcn_ref[j, 0:WINDOW - 1, :] = vc_ref[j, 1:WINDOW, :]
        vcn_ref[j, WINDOW - 1:WINDOW, :] = vnew


def _swa_dec(proj, kc, vc, tabs, sink32):
    nb = proj.shape[0]
    c, s1, s2 = tabs
    const = lambda shape: pl.BlockSpec(shape, lambda t: (0,) * len(shape))
    cache = pl.BlockSpec((DEC_BLOCK, WINDOW, 256), lambda t: (t, 0, 0))
    return pl.pallas_call(
        _swa_dec_kernel,
        out_shape=(jax.ShapeDtypeStruct((nb, 1024), F32),
                   jax.ShapeDtypeStruct((nb, WINDOW, 256), F32), jax.ShapeDtypeStruct((nb, WINDOW, 256), F32)),
        grid=(nb // DEC_BLOCK,),
        in_specs=[pl.BlockSpec((DEC_BLOCK, 1024), lambda t: (t, COL_QA)),
                  pl.BlockSpec((DEC_BLOCK, 256), lambda t: (t, COL_KA)),
                  pl.BlockSpec((DEC_BLOCK, 256), lambda t: (t, COL_VA)),
                  cache, cache, const((1, LANES)), const((1, LANES)), const((1, LANES)), const((32, LANES))],
        out_specs=(pl.BlockSpec((DEC_BLOCK, 1024), lambda t: (t, 0)), cache, cache),
        compiler_params=_cparams(("arbitrary",)),
        name="swa_dec",
    )(proj, proj, proj, kc, vc, c, s1, s2, sink32)


def _ret_prompt_kernel(lg_ref, q_ref, k_ref, v_ref, gr_ref, c_ref, s_ref, nrm_ref,
                       o_ref, sfin_ref, st_ref, intra_ref, fs_ref, te_ref):
    t = pl.program_id(0)
    nt = pl.num_programs(0)

    @pl.when(t == 0)
    def _():
        st_ref[...] = jnp.zeros_like(st_ref)
        ri = lax.broadcasted_iota(jnp.int32, (CHUNK, CHUNK), 0).astype(F32)
        ci = lax.broadcasted_iota(jnp.int32, (CHUNK, CHUNK), 1).astype(F32)
        rel = ri - ci
        for h in range(RET_HEADS):
            lg = lg_ref[h]
            intra_ref[h] = jnp.exp(jnp.where(rel >= 0, rel * lg, -jnp.inf))
            fs_ref[h] = jnp.exp((ri + 1.0) * lg)
            te_ref[h] = jnp.exp((CHUNK - 1.0 - ri) * lg)

    c = c_ref[...]
    s = s_ref[...]
    q = q_ref[...]
    k = k_ref[...]
    v = v_ref[...]
    gr = gr_ref[...]
    nrm = nrm_ref[...]
    for h in range(RET_HEADS):
        sl = slice(128 * h, 128 * (h + 1))
        qh = q[:, sl]
        kh = k[:, sl]
        qh = qh * c + pltpu.roll(qh, 64, axis=1) * s
        kh = (kh * c + pltpu.roll(kh, 64, axis=1) * s) * (RET_DK ** -0.5)
        vh = v[:, sl]
        att = _bdot_nt(qh, kh) * intra_ref[h]
        s_old = st_ref[h]
        o = _bdot(att, vh) + _bdot(qh, s_old) * fs_ref[h]
        cd = jnp.exp(jnp.zeros((1, RET_DK), F32) + CHUNK * lg_ref[h])
        st_ref[h] = s_old * cd + _bdot((kh * te_ref[h]).T, vh)
        o = o * lax.rsqrt(jnp.mean(o * o, axis=-1, keepdims=True) + EPS)
        o_ref[:, sl] = o * nrm[:, sl] * _silu(gr[:, sl])

    @pl.when(t == nt - 1)
    def _():
        sfin_ref[...] = st_ref[...]


def _ret_prompt(log_gamma, proj, tabs, nrm):
    seq = proj.shape[0]
    c, s = tabs
    const = lambda shape: pl.BlockSpec(shape, lambda t: (0,) * len(shape))
    col = lambda cidx: pl.BlockSpec((CHUNK, 1024), lambda t: (t, cidx))
    tab = pl.BlockSpec((CHUNK, LANES), lambda t: (t, 0))
    tbl = pltpu.VMEM((RET_HEADS, CHUNK, CHUNK), F32)
    return pl.pallas_call(
        _ret_prompt_kernel,
        out_shape=(jax.ShapeDtypeStruct((seq, 1024), F32), jax.ShapeDtypeStruct((RET_HEADS, RET_DK, 128), F32)),
        grid=(seq // CHUNK,),
        in_specs=[pl.BlockSpec(memory_space=pltpu.SMEM), col(COL_QR), col(COL_KR), col(COL_VR), col(COL_GR),
                  tab, tab, const((1, 1024))],
        out_specs=(pl.BlockSpec((CHUNK, 1024), lambda t: (t, 0)), const((RET_HEADS, RET_DK, 128))),
        scratch_shapes=[tbl, tbl, tbl, tbl],
        compiler_params=_cparams(("arbitrary",)),
        name="ret_prompt",
    )(log_gamma, proj, proj, proj, proj, c, s, nrm)


def _ret_dec_pre_kernel(q_ref, k_ref, c_ref, s_ref, qrot_ref, kt_ref):
    c = c_ref[...]
    s = s_ref[...]
    q = q_ref[...]
    k = k_ref[...]
    ks = []
    for h in range(RET_HEADS):
        sl = slice(128 * h, 128 * (h + 1))
        qh = q[:, sl]
        kh = k[:, sl]
        qrot_ref[:, sl] = qh * c + pltpu.roll(qh, 64, axis=1) * s
        ks.append((kh * c + pltpu.roll(kh, 64, axis=1) * s) * (RET_DK ** -0.5))
    kt_ref[...] = jnp.concatenate(ks, axis=1).T


def _ret_dec_pre(proj, tabs):
    nb = proj.shape[0]
    c, s = tabs
    const = lambda shape: pl.BlockSpec(shape, lambda t: (0,) * len(shape))
    return pl.pallas_call(
        _ret_dec_pre_kernel,
        out_shape=(jax.ShapeDtypeStruct((nb, 1024), F32), jax.ShapeDtypeStruct((1024, nb), F32)),
        grid=(1,),
        in_specs=[pl.BlockSpec((nb, 1024), lambda t: (0, COL_QR)), pl.BlockSpec((nb, 1024), lambda t: (0, COL_KR)),
                  const((1, LANES)), const((1, LANES))],
        out_specs=(const((nb, 1024)), const((1024, nb))),
        compiler_params=_cparams(("arbitrary",)),
        name="ret_dec_pre",
    )(proj, proj, c, s)


def _ret_dec_state_kernel(gam_ref, kt_ref, q_ref, v_ref, gr_ref, st_ref, nrm_ref, o_ref, stn_ref):
    i = pl.program_id(0)
    shift = (LANES - i * DEC_BLOCK) % LANES
    kr = pltpu.roll(kt_ref[...], shift, axis=1)
    q = q_ref[...]
    v = v_ref[...]
    rowid = lax.broadcasted_iota(jnp.int32, (DEC_BLOCK, RET_DK), 0)
    os_ = [jnp.zeros((DEC_BLOCK, 128), F32) for _ in range(RET_HEADS)]
    for j in range(DEC_BLOCK):
        for h in range(RET_HEADS):
            lo, hi = 128 * h, 128 * (h + 1)
            kcol = jnp.broadcast_to(kr[lo:hi, j:j + 1], (RET_DK, 128))
            s_new = st_ref[j, lo:hi, :] * gam_ref[h] + kcol * v[j:j + 1, lo:hi]
            stn_ref[j, lo:hi, :] = s_new
            qm = jnp.where(rowid == j, q[:, lo:hi], 0.0)
            os_[h] = os_[h] + _bdot(qm, s_new)
    gr = gr_ref[...]
    nrm = nrm_ref[...]
    for h in range(RET_HEADS):
        sl = slice(128 * h, 128 * (h + 1))
        o = os_[h]
        o = o * lax.rsqrt(jnp.mean(o * o, axis=-1, keepdims=True) + EPS)
        o_ref[:, sl] = o * nrm[:, sl] * _silu(gr[:, sl])


def _ret_dec_state(gam, kt, qrot, proj, st, nrm):
    nb = qrot.shape[0]
    const = lambda shape: pl.BlockSpec(shape, lambda t: (0,) * len(shape))
    stb = pl.BlockSpec((DEC_BLOCK, 1024, 128), lambda t: (t, 0, 0))
    return pl.pallas_call(
        _ret_dec_state_kernel,
        out_shape=(jax.ShapeDtypeStruct((nb, 1024), F32), jax.ShapeDtypeStruct((nb, 1024, 128), F32)),
        grid=(nb // DEC_BLOCK,),
        in_specs=[pl.BlockSpec(memory_space=pltpu.SMEM), const((1024, nb)),
                  pl.BlockSpec((DEC_BLOCK, 1024), lambda t: (t, 0)),
                  pl.BlockSpec((DEC_BLOCK, 1024), lambda t: (t, COL_VR)),
                  pl.BlockSpec((DEC_BLOCK, 1024), lambda t: (t, COL_GR)),
                  stb, const((1, 1024))],
        out_specs=(pl.BlockSpec((DEC_BLOCK, 1024), lambda t: (t, 0)), stb),
        compiler_params=_cparams(("arbitrary",)),
        name="ret_dec_state",
    )(gam, kt, qrot, proj, proj, st, nrm)


def _merge_kernel(x_ref, a_ref, b_ref, c_ref, g1_ref, g2_ref, g3_ref, w1_ref, w2_ref, w3_ref, wo_ref, o_ref):
    m = (_sigmoid(g1_ref[...]) * jnp.dot(a_ref[...].astype(BF16), w1_ref[...], preferred_element_type=F32)
         + _sigmoid(g2_ref[...]) * jnp.dot(b_ref[...].astype(BF16), w2_ref[...], preferred_element_type=F32)
         + _sigmoid(g3_ref[...]) * jnp.dot(c_ref[...].astype(BF16), w3_ref[...], preferred_element_type=F32))
    o_ref[...] = x_ref[...] + jnp.dot(m.astype(BF16), wo_ref[...], preferred_element_type=F32)


def _merge(x, a, b, c, proj, w1, w2, w3, wo):
    rows = x.shape[0]
    tm = min(rows, 256)
    rowb = pl.BlockSpec((tm, 1024), lambda i: (i, 0))
    gate = lambda k: pl.BlockSpec((tm, 1024), lambda i: (i, COL_GATE + k))
    wsp = pl.BlockSpec((1024, 1024), lambda i: (0, 0))
    return pl.pallas_call(
        _merge_kernel,
        out_shape=jax.ShapeDtypeStruct((rows, 1024), F32),
        grid=(rows // tm,),
        in_specs=[rowb, rowb, rowb, rowb, gate(0), gate(1), gate(2), wsp, wsp, wsp, wsp],
        out_specs=rowb,
        compiler_params=_cparams(("parallel",)),
        name="merge",
    )(x, a, b, c, proj, proj, proj, w1, w2, w3, wo)


FF_TILE = 1408


def _ffn_kernel(x_ref, g_ref, wg_ref, wu_ref, wd_ref, o_ref, h_ref, acc_ref):
    j = pl.program_id(1)

    @pl.when(j == 0)
    def _():
        h_ref[...] = _rms(x_ref[...], g_ref[...]).astype(BF16)
        acc_ref[...] = jnp.zeros_like(acc_ref)

    h = h_ref[...]
    a = jnp.dot(h, wg_ref[...], preferred_element_type=F32)
    u = jnp.dot(h, wu_ref[...], preferred_element_type=F32)
    acc_ref[...] += jnp.dot((_silu(a) * u).astype(BF16), wd_ref[...], preferred_element_type=F32)

    @pl.when(j == pl.num_programs(1) - 1)
    def _():
        o_ref[...] = x_ref[...] + acc_ref[...]


def _ffn(x, g, wg, wu, wd):
    rows = x.shape[0]
    tm = min(rows, 512)
    return pl.pallas_call(
        _ffn_kernel,
        out_shape=jax.ShapeDtypeStruct((rows, 1024), F32),
        grid=(rows // tm, D_FF // FF_TILE),
        in_specs=[pl.BlockSpec((tm, 1024), lambda i, j: (i, 0)), pl.BlockSpec((1, 1024), lambda i, j: (0, 0)),
                  pl.BlockSpec((1024, FF_TILE), lambda i, j: (0, j)), pl.BlockSpec((1024, FF_TILE), lambda i, j: (0, j)),
                  pl.BlockSpec((FF_TILE, 1024), lambda i, j: (j, 0))],
        out_specs=pl.BlockSpec((tm, 1024), lambda i, j: (i, 0)),
        scratch_shapes=[pltpu.VMEM((tm, 1024), BF16), pltpu.VMEM((tm, 1024), F32)],
        compiler_params=_cparams(("parallel", "arbitrary")),
        name="ffn",
    )(x, g, wg, wu, wd)


MOE_FF_TILE = 256


def _moe_kernel(x_ref, g_ref, rw_ref, rb_ref, wg_ref, wu_ref, wd_ref, o_ref, h_ref, acc_ref, comb_ref):
    e = pl.program_id(1)
    j = pl.program_id(2)
    tm = x_ref.shape[0]
    lane = lax.broadcasted_iota(jnp.int32, (tm, LANES), 1).astype(F32)

    @pl.when(jnp.logical_and(e == 0, j == 0))
    def _():
        h = _rms(x_ref[...], g_ref[...]).astype(BF16)
        h_ref[...] = h
        logits = jnp.dot(h, rw_ref[...], preferred_element_type=F32) + rb_ref[...]
        logits = jnp.where(lane < N_EXPERTS, logits, -jnp.inf)
        m1 = jnp.max(logits, axis=1, keepdims=True)
        i1 = jnp.min(jnp.where(logits == m1, lane, float(LANES)), axis=1, keepdims=True)
        rest = jnp.where(lane == i1, -jnp.inf, logits)
        m2 = jnp.max(rest, axis=1, keepdims=True)
        i2 = jnp.min(jnp.where(rest == m2, lane, float(LANES)), axis=1, keepdims=True)
        e2 = jnp.exp(m2 - m1)
        p1 = 1.0 / (1.0 + e2)
        comb_ref[...] = jnp.where(lane == i1, p1, 0.0) + jnp.where(lane == i2, e2 * p1, 0.0)
        acc_ref[...] = jnp.zeros_like(acc_ref)

    ce = jnp.sum(jnp.where(lane == e.astype(F32), comb_ref[...], 0.0), axis=1, keepdims=True)
    h = h_ref[...]
    a = jnp.dot(h, wg_ref[0], preferred_element_type=F32)
    u = jnp.dot(h, wu_ref[0], preferred_element_type=F32)
    acc_ref[...] += ce * jnp.dot((_silu(a) * u).astype(BF16), wd_ref[0], preferred_element_type=F32)

    @pl.when(jnp.logical_and(e == pl.num_programs(1) - 1, j == pl.num_programs(2) - 1))
    def _():
        o_ref[...] = x_ref[...] + acc_ref[...]


def _moe(x, g, rw, rb, wg, wu, wd):
    rows = x.shape[0]
    tm = min(rows, 1024)
    tf = MOE_FF_TILE
    return pl.pallas_call(
        _moe_kernel,
        out_shape=jax.ShapeDtypeStruct((rows, 1024), F32),
        grid=(rows // tm, N_EXPERTS, D_FF // tf),
        in_specs=[pl.BlockSpec((tm, 1024), lambda i, e, j: (i, 0)), pl.BlockSpec((1, 1024), lambda i, e, j: (0, 0)),
                  pl.BlockSpec((1024, LANES), lambda i, e, j: (0, 0)), pl.BlockSpec((1, LANES), lambda i, e, j: (0, 0)),
                  pl.BlockSpec((1, 1024, tf), lambda i, e, j: (e, 0, j)),
                  pl.BlockSpec((1, 1024, tf), lambda i, e, j: (e, 0, j)),
                  pl.BlockSpec((1, tf, 1024), lambda i, e, j: (e, j, 0))],
        out_specs=pl.BlockSpec((tm, 1024), lambda i, e, j: (i, 0)),
        scratch_shapes=[pltpu.VMEM((tm, 1024), BF16), pltpu.VMEM((tm, 1024), F32), pltpu.VMEM((tm, LANES), F32)],
        compiler_params=_cparams(("parallel", "arbitrary", "arbitrary")),
        name="moe",
    )(x, g, rw, rb, wg, wu, wd)


def _ple_kernel(x_ref, p_ref, g_ref, wp_ref, wgt_ref, gf_ref, o_ref, *, final):
    x = x_ref[...]
    emb = jnp.dot(p_ref[...].astype(BF16), wp_ref[...], preferred_element_type=F32)
    gate = _sigmoid(jnp.dot(_rms(x, g_ref[...]).astype(BF16), wgt_ref[...], preferred_element_type=F32))
    y = x + emb * gate
    if final:
        y = _rms(y, gf_ref[...])
    o_ref[...] = y


def _ple(x, p, g, wp, wgt, gf, final):
    rows = x.shape[0]
    tm = min(rows, 512)
    vec = pl.BlockSpec((1, 1024), lambda i: (0, 0))
    return pl.pallas_call(
        functools.partial(_ple_kernel, final=final),
        out_shape=jax.ShapeDtypeStruct((rows, 1024), F32),
        grid=(rows // tm,),
        in_specs=[pl.BlockSpec((tm, 1024), lambda i: (i, 0)), pl.BlockSpec((tm, PLE_DIM), lambda i: (i, 0)), vec,
                  pl.BlockSpec((PLE_DIM, 1024), lambda i: (0, 0)), pl.BlockSpec((1024, 1024), lambda i: (0, 0)), vec],
        out_specs=pl.BlockSpec((tm, 1024), lambda i: (i, 0)),
        compiler_params=_cparams(("parallel",)),
        name="ple",
    )(x, p, g, wp, wgt, gf)


def _rmajor_cols(w):
    k = w.shape[0]
    return w.reshape(k, ATT_KV_HEADS, ATT_REP, ATT_HEAD_DIM).transpose(0, 2, 1, 3).reshape(k, ATT_Q_HEADS * ATT_HEAD_DIM)


def _prep_w_in(w):
    splits = list(jnp.cumsum(jnp.array(IN_WIDTHS))[:-1])
    offs = [0]
    for wd in IN_WIDTHS:
        offs.append(offs[-1] + wd)
    z, xbc, dt, qa, ka, va, qr, kr, vr, gr, gates = [w[:, offs[i]:offs[i + 1]] for i in range(len(IN_WIDTHS))]
    used = 2048 + 1024 * 6 + 3072 + 256 + 256 + 16
    pad = jnp.zeros((w.shape[0], N_PROJ - used), w.dtype)
    return jnp.concatenate([xbc, z, _rmajor_cols(qa), qr, kr, vr, gr, gates, ka, va, dt, pad], axis=1).astype(BF16)


def _att_tables(pos):
    half = ROPE_DIM // 2
    inv = jnp.exp(-math.log(ROPE_THETA) * jnp.arange(half, dtype=F32) * (2.0 / ROPE_DIM))
    ang = pos.astype(F32)[:, None] * inv[None, :]
    cos, sin = jnp.cos(ang), jnp.sin(ang)
    n = pos.shape[0]
    one = jnp.ones((n, ATT_HEAD_DIM - ROPE_DIM), F32)
    zero8 = jnp.zeros((n, half), F32)
    zero = jnp.zeros((n, ATT_HEAD_DIM - ROPE_DIM), F32)
    c = jnp.concatenate([cos, cos, one], axis=1)
    s1 = jnp.concatenate([zero8, sin, zero], axis=1)
    s2 = jnp.concatenate([-sin, zero8, zero], axis=1)
    return tuple(jnp.concatenate([t, t], axis=1) for t in (c, s1, s2))


def _ret_tables(pos):
    half = RET_DK // 2
    inv = jnp.exp(-math.log(RET_THETA) * jnp.arange(half, dtype=F32) * (2.0 / RET_DK))
    ang = pos.astype(F32)[:, None] * inv[None, :]
    cos, sin = jnp.cos(ang), jnp.sin(ang)
    return jnp.concatenate([cos, cos], axis=1), jnp.concatenate([-sin, sin], axis=1)


def _pad_lanes(v, fill=0.0):
    return jnp.concatenate([v.astype(F32), jnp.full((LANES - v.shape[0],), fill, F32)])[None, :]


def kernel(x_prompt, x_sample, state_ssm, state_conv, cache_win_k, cache_win_v, state_ret, p_prompt, p_sample,
           w_in, conv_w, conv_b, dt_bias, a_log, d_skip, ssd_norm, attn_sinks, ret_norm, w_o_ssd, w_o_att, w_o_ret,
           w_out, norm_mix, norm_ffn, norm_ple, ffn_w_gate, ffn_w_up, ffn_w_down, router_w, router_b, moe_w_gate,
           moe_w_up, moe_w_down, w_ple, w_ple_gate, norm_final):
    seq = x_prompt.shape[1]
    nb = x_sample.shape[0]
    xp = x_prompt.reshape(seq, D_MODEL)
    xs = x_sample.reshape(nb, D_MODEL)
    pos_p = jnp.arange(seq)
    pos_s = PAST_LEN + jnp.arange(1)
    att_tab_p, att_tab_s = _att_tables(pos_p), _att_tables(pos_s)
    ret_tab_p, ret_tab_s = _ret_tables(pos_p), _ret_tables(pos_s)
    log_gamma = jnp.log1p(-jnp.exp2(-5.0 - jnp.arange(RET_HEADS, dtype=F32)))
    gamma = jnp.exp(log_gamma)
    row = lambda v: v.astype(F32)[None, :]

    new_p = [[], [], [], [], []]
    new_s = [[], [], [], [], []]
    for i in range(DEPTH):
        w_in_i = _prep_w_in(w_in[i])
        cw, cb = conv_w[i], row(conv_b[i])
        dtb, alog = _pad_lanes(dt_bias[i]), _pad_lanes(a_log[i])
        dsk = row(jnp.repeat(d_skip[i], SSD_HEAD_DIM))
        nrm_ssd, nrm_ret = row(ssd_norm[i]), row(ret_norm[i])
        sinks = attn_sinks[i].astype(F32)
        sink32 = jnp.zeros((ATT_REP, 8), F32).at[:, :ATT_KV_HEADS].set(sinks.reshape(ATT_KV_HEADS, ATT_REP).T)
        sink32 = jnp.broadcast_to(sink32.reshape(32, 1), (32, LANES))
        w1 = w_o_ssd[i].astype(BF16)
        w2 = w_o_att[i].reshape(ATT_KV_HEADS, ATT_REP, ATT_HEAD_DIM, D_MODEL).transpose(1, 0, 2, 3) \
            .reshape(ATT_Q_HEADS * ATT_HEAD_DIM, D_MODEL).astype(BF16)
        w3 = w_o_ret[i].astype(BF16)
        wo = w_out[i].astype(BF16)
        g_mix, g_ffn, g_ple = row(norm_mix[i]), row(norm_ffn[i]), row(norm_ple[i])
        wp, wpg = w_ple[i].astype(BF16), w_ple_gate[i].astype(BF16)
        gf = row(norm_final)
        j = i // 2
        if i % 2 == 0:
            ffw = (ffn_w_gate[j].astype(BF16), ffn_w_up[j].astype(BF16), ffn_w_down[j].astype(BF16))
        else:
            rw = jnp.concatenate([router_w[j], jnp.zeros((D_MODEL, LANES - N_EXPERTS), F32)], axis=1).astype(BF16)
            ffw = (rw, _pad_lanes(router_b[j]), moe_w_gate[j].astype(BF16), moe_w_up[j].astype(BF16),
                   moe_w_down[j].astype(BF16))
        final = i == DEPTH - 1

        proj = _inproj(xp, g_mix, w_in_i)
        y_ssd, ssm_fin, conv_fin = _ssd_prompt(proj, cw, cb, dtb, alog, dsk, nrm_ssd)
        o_att, wk, wv = _swa_prompt(sinks, proj, att_tab_p)
        o_ret, ret_fin = _ret_prompt(log_gamma, proj, ret_tab_p, nrm_ret)
        xp = _merge(xp, y_ssd, o_att, o_ret, proj, w1, w2, w3, wo)
        xp = _ffn(xp, g_ffn, *ffw) if i % 2 == 0 else _moe(xp, g_ffn, *ffw)
        xp = _ple(xp, p_prompt[i].reshape(seq, PLE_DIM), g_ple, wp, wpg, gf, final)
        new_p[0].append(ssm_fin.reshape(1, SSD_HEADS, SSD_HEAD_DIM, SSD_STATE))
        new_p[1].append(conv_fin[None])
        new_p[2].append(wk.reshape(1, WINDOW, ATT_KV_HEADS, ATT_HEAD_DIM))
        new_p[3].append(wv.reshape(1, WINDOW, ATT_KV_HEADS, ATT_HEAD_DIM))
        new_p[4].append(ret_fin[None])

        proj = _inproj(xs, g_mix, w_in_i)
        cst_t = jnp.transpose(state_conv[i], (1, 0, 2))
        cnew_t, xs_conv, bc, dec_t, xdt_t = _ssd_dec_pre(proj, cst_t, cw, cb, dtb, alog)
        y_ssd, ssm_new = _ssd_dec_state(dec_t, xdt_t, bc, proj, xs_conv,
                                        state_ssm[i].reshape(nb, 1024, SSD_STATE), dsk, nrm_ssd)
        o_att, kc_new, vc_new = _swa_dec(proj, cache_win_k[i].reshape(nb, WINDOW, 256),
                                         cache_win_v[i].reshape(nb, WINDOW, 256), att_tab_s, sink32)
        qrot, kt = _ret_dec_pre(proj, ret_tab_s)
        o_ret, ret_new = _ret_dec_state(gamma, kt, qrot, proj, state_ret[i].reshape(nb, 1024, 128), nrm_ret)
        xs = _merge(xs, y_ssd, o_att, o_ret, proj, w1, w2, w3, wo)
        xs = _ffn(xs, g_ffn, *ffw) if i % 2 == 0 else _moe(xs, g_ffn, *ffw)
        xs = _ple(xs, p_sample[i].reshape(nb, PLE_DIM), g_ple, wp, wpg, gf, final)
        new_s[0].append(ssm_new.reshape(nb, SSD_HEADS, SSD_HEAD_DIM, SSD_STATE))
        new_s[1].append(jnp.transpose(cnew_t, (1, 0, 2)))
        new_s[2].append(kc_new.reshape(nb, WINDOW, ATT_KV_HEADS, ATT_HEAD_DIM))
        new_s[3].append(vc_new.reshape(nb, WINDOW, ATT_KV_HEADS, ATT_HEAD_DIM))
        new_s[4].append(ret_new.reshape(nb, RET_HEADS, RET_DK, 128))

    y_prompt = xp.reshape(1, seq, D_MODEL)
    y_sample = xs.reshape(nb, 1, D_MODEL)
    outs_p = [jnp.stack(l) for l in new_p]
    outs_s = [jnp.stack(l) for l in new_s]
    return (y_prompt, y_sample, *outs_p, *outs_s)
```

```python
import functools
import math

import jax
import jax.numpy as jnp
from jax import lax
from jax.experimental import pallas as pl
from jax.experimental.pallas import tpu as pltpu

F32 = jnp.float32
BF16 = jnp.bfloat16

D_MODEL = 1024
DEPTH = 2
PAST_LEN = 16384
SSD_HEADS = 16
SSD_HEAD_DIM = 64
SSD_GROUPS = 4
SSD_STATE = 128
SSD_CONV = 4
SSD_CONV_DIM = 2048
ATT_HEAD_DIM = 64
ATT_Q_HEADS = 16
ATT_KV_HEADS = 4
ATT_REP = ATT_Q_HEADS // ATT_KV_HEADS
WINDOW = 128
ROPE_THETA = 500000.0
ROPE_DIM = 16
RET_HEADS = 8
RET_DK = 128
RET_THETA = 10000.0
CHUNK = 128
D_FF = 2816
N_EXPERTS = 8
PLE_DIM = 256
EPS = 1e-6

IN_WIDTHS = (1024, 2048, 16, 1024, 256, 256, 1024, 1024, 1024, 1024, 3072)
N_PROJ = 12288
COL_XBC = 0
COL_Z = 2
COL_QA = 3
COL_QR = 4
COL_KR = 5
COL_VR = 6
COL_GR = 7
COL_GATE = 8
COL_KA = 44
COL_VA = 45
COL_DT = 92

LANES = 128
VMEM_LIMIT = 48 * 1024 * 1024


def _cparams(sem, vmem=VMEM_LIMIT):
    return pltpu.CompilerParams(dimension_semantics=sem, vmem_limit_bytes=vmem)


def _bdot(a, b):
    return jnp.dot(a.astype(BF16), b.astype(BF16), preferred_element_type=F32)


def _bdot_nt(a, b):
    return lax.dot_general(a.astype(BF16), b.astype(BF16), (((1,), (1,)), ((), ())),
                           preferred_element_type=F32)


def _split3(x):
    x0 = x.astype(BF16)
    r1 = x - x0.astype(F32)
    x1 = r1.astype(BF16)
    x2 = (r1 - x1.astype(F32)).astype(BF16)
    return x0, x1, x2


def _dot_exact_lhs01(m01, x):
    m = m01.astype(BF16)
    x0, x1, x2 = _split3(x)
    d = lambda b: jnp.dot(m, b, preferred_element_type=F32)
    return d(x0) + d(x1) + d(x2)


def _dot_exact_rhs01(x, m01):
    m = m01.astype(BF16)
    x0, x1, x2 = _split3(x)
    d = lambda a: jnp.dot(a, m, preferred_element_type=F32)
    return d(x0) + d(x1) + d(x2)


def _rms(x, g):
    return x * lax.rsqrt(jnp.mean(x * x, axis=-1, keepdims=True) + EPS) * g


def _sigmoid(x):
    return 1.0 / (1.0 + jnp.exp(-x))


def _silu(x):
    return x * _sigmoid(x)


def _softplus(x):
    return jnp.maximum(x, 0.0) + jnp.log1p(jnp.exp(-jnp.abs(x)))


def _rope_att(x, c, s1, s2):
    w = x.shape[1]
    return x * c + pltpu.roll(x, 8, axis=1) * s1 + pltpu.roll(x, w - 8, axis=1) * s2


def _tile_lanes(t, n):
    return jnp.concatenate([t] * n, axis=1) if n > 1 else t


def _inproj_kernel(x_ref, g_ref, w_ref, o_ref, h_ref):
    @pl.when(pl.program_id(1) == 0)
    def _():
        h_ref[...] = _rms(x_ref[...], g_ref[...]).astype(BF16)

    o_ref[...] = jnp.dot(h_ref[...], w_ref[...], preferred_element_type=F32)


def _inproj(x, g, w):
    rows = x.shape[0]
    tm = min(rows, 1024)
    tn = 1024
    return pl.pallas_call(
        _inproj_kernel,
        out_shape=jax.ShapeDtypeStruct((rows, N_PROJ), F32),
        grid=(rows // tm, N_PROJ // tn),
        in_specs=[pl.BlockSpec((tm, D_MODEL), lambda i, j: (i, 0)),
                  pl.BlockSpec((1, D_MODEL), lambda i, j: (0, 0)),
                  pl.BlockSpec((D_MODEL, tn), lambda i, j: (0, j))],
        out_specs=pl.BlockSpec((tm, tn), lambda i, j: (i, j)),
        scratch_shapes=[pltpu.VMEM((tm, D_MODEL), BF16)],
        compiler_params=_cparams(("parallel", "arbitrary")),
        name="inproj",
    )(x, g, w)


def _ssd_prompt_kernel(xbc_ref, z_ref, dt_ref, cw_ref, cb_ref, dtb_ref, alog_ref, dsk_ref, nrm_ref,
                       y_ref, sfin_ref, cfin_ref, xpad_ref, s_ref):
    t = pl.program_id(0)
    nt = pl.num_programs(0)

    @pl.when(t == 0)
    def _():
        xpad_ref[0:8, :] = jnp.zeros((8, SSD_CONV_DIM), F32)
        s_ref[...] = jnp.zeros_like(s_ref)

    xbc = xbc_ref[...]
    xpad_ref[8:8 + CHUNK, :] = xbc
    cw = cw_ref[...]
    acc = (xbc * cw[3:4, :] + xpad_ref[7:7 + CHUNK, :] * cw[2:3, :]
           + xpad_ref[6:6 + CHUNK, :] * cw[1:2, :] + xpad_ref[5:5 + CHUNK, :] * cw[0:1, :] + cb_ref[...])
    conv = _silu(acc)
    xpad_ref[0:8, :] = xbc[CHUNK - 8:CHUNK, :]

    xs = conv[:, :1024]
    dt = _softplus(dt_ref[...] + dtb_ref[...])
    la = dt * (-jnp.exp(alog_ref[...]))
    row = lax.broadcasted_iota(jnp.int32, (CHUNK, CHUNK), 0)
    col = lax.broadcasted_iota(jnp.int32, (CHUNK, CHUNK), 1)
    causal = row >= col
    cum = _dot_exact_lhs01(causal.astype(F32), la)
    cum_t = cum.T
    dt_t = dt.T
    cum_last = jnp.broadcast_to(cum_t[:, CHUNK - 1:CHUNK], (LANES, CHUNK))
    w_t = jnp.exp(cum_last - cum_t) * dt_t
    dec_end = jnp.exp(cum_last)

    xs_t = xs.T
    ys = []
    for g in range(SSD_GROUPS):
        bg = conv[:, 1024 + 128 * g:1024 + 128 * (g + 1)]
        cg = conv[:, 1536 + 128 * g:1536 + 128 * (g + 1)]
        cb = _bdot_nt(cg, bg)
        s_g = s_ref[256 * g:256 * (g + 1), :]
        cs = _bdot_nt(cg, s_g)
        xw_parts = []
        dec_parts = []
        for r in range(4):
            h = 4 * g + r
            colb = jnp.broadcast_to(cum[:, h:h + 1], (CHUNK, CHUNK))
            rowb = jnp.broadcast_to(cum_t[h:h + 1, :], (CHUNK, CHUNK))
            dec = jnp.exp(jnp.where(causal, colb - rowb, -jnp.inf))
            m = cb * dec * jnp.broadcast_to(dt_t[h:h + 1, :], (CHUNK, CHUNK))
            xh = xs[:, 64 * h:64 * (h + 1)]
            yh = _bdot(m, xh) + cs[:, 64 * r:64 * (r + 1)] * jnp.exp(colb)[:, :64]
            ys.append(yh)
            xw_parts.append(xs_t[64 * h:64 * (h + 1), :] * jnp.broadcast_to(w_t[h:h + 1, :], (64, CHUNK)))
            dec_parts.append(jnp.broadcast_to(dec_end[h:h + 1, :], (64, SSD_STATE)))
        xw = jnp.concatenate(xw_parts, axis=0)
        s_ref[256 * g:256 * (g + 1), :] = s_g * jnp.concatenate(dec_parts, axis=0) + _bdot(xw, bg)

    y = jnp.concatenate(ys, axis=1) + dsk_ref[...] * xs
    y = y * _silu(z_ref[...])
    y_ref[...] = _rms(y, nrm_ref[...])

    @pl.when(t == nt - 1)
    def _():
        sfin_ref[...] = s_ref[...]
        cfin_ref[...] = xbc[CHUNK - 3:CHUNK, :]


def _ssd_prompt(proj, cw, cb, dtb, alog, dsk, nrm):
    seq = proj.shape[0]
    const = lambda shape: pl.BlockSpec(shape, lambda t: (0,) * len(shape))
    return pl.pallas_call(
        _ssd_prompt_kernel,
        out_shape=(jax.ShapeDtypeStruct((seq, 1024), F32),
                   jax.ShapeDtypeStruct((1024, SSD_STATE), F32),
                   jax.ShapeDtypeStruct((SSD_CONV - 1, SSD_CONV_DIM), F32)),
        grid=(seq // CHUNK,),
        in_specs=[pl.BlockSpec((CHUNK, 2048), lambda t: (t, COL_XBC)),
                  pl.BlockSpec((CHUNK, 1024), lambda t: (t, COL_Z)),
                  pl.BlockSpec((CHUNK, 128), lambda t: (t, COL_DT)),
                  const((SSD_CONV, 2048)), const((1, 2048)), const((1, 128)), const((1, 128)),
                  const((1, 1024)), const((1, 1024))],
        out_specs=(pl.BlockSpec((CHUNK, 1024), lambda t: (t, 0)),
                   const((1024, SSD_STATE)), const((SSD_CONV - 1, SSD_CONV_DIM))),
        scratch_shapes=[pltpu.VMEM((8 + CHUNK, SSD_CONV_DIM), F32), pltpu.VMEM((1024, SSD_STATE), F32)],
        compiler_params=_cparams(("arbitrary",)),
        name="ssd_prompt",
    )(proj, proj, proj, cw, cb, dtb, alog, dsk, nrm)


def _ssd_dec_pre_kernel(xbc_ref, dt_ref, cst_ref, cw_ref, cb_ref, dtb_ref, alog_ref,
                        cnew_ref, xs_ref, bc_ref, dec_t_ref, xdt_t_ref):
    xbc = xbc_ref[...]
    cw = cw_ref[...]
    acc = (cst_ref[0] * cw[0:1, :] + cst_ref[1] * cw[1:2, :] + cst_ref[2] * cw[2:3, :]
           + xbc * cw[3:4, :] + cb_ref[...])
    conv = _silu(acc)
    cnew_ref[0] = cst_ref[1]
    cnew_ref[1] = cst_ref[2]
    cnew_ref[2] = xbc
    xs = conv[:, :1024]
    xs_ref[...] = xs
    bc_ref[...] = conv[:, 1024:]
    dt = _softplus(dt_ref[...] + dtb_ref[...])
    dec = jnp.exp(dt * (-jnp.exp(alog_ref[...])))
    hrow = lax.broadcasted_iota(jnp.int32, (LANES, 1024), 0)
    hcol = lax.broadcasted_iota(jnp.int32, (LANES, 1024), 1)
    expand = ((hcol >> 6) == hrow).astype(F32)
    dec_t_ref[...] = _dot_exact_rhs01(dec, expand).T
    xdt_t_ref[...] = (xs * _dot_exact_rhs01(dt, expand)).T


def _ssd_dec_pre(proj, cst_t, cw, cb, dtb, alog):
    nb = proj.shape[0]
    const = lambda shape: pl.BlockSpec(shape, lambda t: (0,) * len(shape))
    return pl.pallas_call(
        _ssd_dec_pre_kernel,
        out_shape=(jax.ShapeDtypeStruct((3, nb, 2048), F32), jax.ShapeDtypeStruct((nb, 1024), F32),
                   jax.ShapeDtypeStruct((nb, 1024), F32), jax.ShapeDtypeStruct((1024, nb), F32),
                   jax.ShapeDtypeStruct((1024, nb), F32)),
        grid=(1,),
        in_specs=[pl.BlockSpec((nb, 2048), lambda t: (0, COL_XBC)),
                  pl.BlockSpec((nb, 128), lambda t: (0, COL_DT)),
                  const((3, nb, 2048)), const((SSD_CONV, 2048)), const((1, 2048)), const((1, 128)), const((1, 128))],
        out_specs=(const((3, nb, 2048)), const((nb, 1024)), const((nb, 1024)), const((1024, nb)), const((1024, nb))),
        compiler_params=_cparams(("arbitrary",)),
        name="ssd_dec_pre",
    )(proj, proj, cst_t, cw, cb, dtb, alog)


DEC_BLOCK = 8


def _ssd_dec_state_kernel(dec_t_ref, xdt_t_ref, bc_ref, z_ref, xs_ref, st_ref, dsk_ref, nrm_ref,
                          y_ref, stn_ref):
    i = pl.program_id(0)
    shift = (LANES - i * DEC_BLOCK) % LANES
    decr = pltpu.roll(dec_t_ref[...], shift, axis=1)
    xr = pltpu.roll(xdt_t_ref[...], shift, axis=1)
    bc = bc_ref[...]
    rowid = lax.broadcasted_iota(jnp.int32, (DEC_BLOCK, SSD_STATE), 0)
    ys = [jnp.zeros((DEC_BLOCK, 256), F32) for _ in range(SSD_GROUPS)]
    for j in range(DEC_BLOCK):
        for g in range(SSD_GROUPS):
            lo, hi = 256 * g, 256 * (g + 1)
            s_old = st_ref[j, lo:hi, :]
            dcol = jnp.broadcast_to(decr[lo:hi, j:j + 1], (256, SSD_STATE))
            xcol = jnp.broadcast_to(xr[lo:hi, j:j + 1], (256, SSD_STATE))
            s_new = s_old * dcol + xcol * bc[j:j + 1, 128 * g:128 * (g + 1)]
            stn_ref[j, lo:hi, :] = s_new
            cm = jnp.where(rowid == j, bc[:, 512 + 128 * g:512 + 128 * (g + 1)], 0.0)
            ys[g] = ys[g] + _bdot_nt(cm, s_new)
    xs = xs_ref[...]
    y = jnp.concatenate(ys, axis=1) + dsk_ref[...] * xs
    y = y * _silu(z_ref[...])
    y_ref[...] = _rms(y, nrm_ref[...])


def _ssd_dec_state(dec_t, xdt_t, bc, proj, xs, st, dsk, nrm):
    nb = xs.shape[0]
    const = lambda shape: pl.BlockSpec(shape, lambda t: (0,) * len(shape))
    blk = lambda w: pl.BlockSpec((DEC_BLOCK, w), lambda t: (t, 0))
    return pl.pallas_call(
        _ssd_dec_state_kernel,
        out_shape=(jax.ShapeDtypeStruct((nb, 1024), F32), jax.ShapeDtypeStruct((nb, 1024, SSD_STATE), F32)),
        grid=(nb // DEC_BLOCK,),
        in_specs=[const((1024, nb)), const((1024, nb)), blk(1024),
                  pl.BlockSpec((DEC_BLOCK, 1024), lambda t: (t, COL_Z)), blk(1024),
                  pl.BlockSpec((DEC_BLOCK, 1024, SSD_STATE), lambda t: (t, 0, 0)),
                  const((1, 1024)), const((1, 1024))],
        out_specs=(blk(1024), pl.BlockSpec((DEC_BLOCK, 1024, SSD_STATE), lambda t: (t, 0, 0))),
        compiler_params=_cparams(("arbitrary",)),
        name="ssd_dec_state",
    )(dec_t, xdt_t, bc, proj, xs, st, dsk, nrm)


def _swa_prompt_kernel(sink_ref, q_ref, k_ref, v_ref, c_ref, s1_ref, s2_ref,
                       o_ref, wk_ref, wv_ref, kp_ref, vp_ref):
    n = pl.program_id(0)
    nb = pl.num_programs(0)

    @pl.when(n == 0)
    def _():
        kp_ref[...] = jnp.zeros_like(kp_ref)
        vp_ref[...] = jnp.zeros_like(vp_ref)

    c, s1, s2 = c_ref[...], s1_ref[...], s2_ref[...]
    q = _rope_att(q_ref[...], _tile_lanes(c, 8), _tile_lanes(s1, 8), _tile_lanes(s2, 8))
    k = _rope_att(k_ref[...], _tile_lanes(c, 2), _tile_lanes(s1, 2), _tile_lanes(s2, 2))
    v = v_ref[...]
    ghead = lax.broadcasted_iota(jnp.int32, (WINDOW, 256), 1) >> 6
    expand = lambda t: jnp.concatenate([jnp.where(ghead == g, t, 0.0) for g in range(ATT_KV_HEADS)],
                                       axis=0).astype(BF16)
    kbd, vbd = expand(k), expand(v)
    kbd_prev, vbd_prev = expand(kp_ref[...]), expand(vp_ref[...])
    qall = jnp.concatenate([q[:, 256 * r:256 * (r + 1)] for r in range(ATT_REP)], axis=0).astype(BF16)
    scale = ATT_HEAD_DIM ** -0.5
    nt_dims = (((1,), (1,)), ((), ()))
    sp_all = lax.dot_general(qall, kbd_prev, nt_dims, preferred_element_type=F32) * scale
    sc_all = lax.dot_general(qall, kbd, nt_dims, preferred_element_type=F32) * scale
    rows = ATT_REP * WINDOW
    qi = lax.broadcasted_iota(jnp.int32, (rows, WINDOW), 0) & (WINDOW - 1)
    kj = lax.broadcasted_iota(jnp.int32, (rows, WINDOW), 1)
    mask_prev = jnp.logical_and(kj > qi, n > 0)
    mask_cur = kj <= qi
    rep = lax.broadcasted_iota(jnp.int32, (rows, 1), 0) >> 7
    pp, pc = [], []
    for g in range(ATT_KV_HEADS):
        sp = jnp.where(mask_prev, sp_all[:, WINDOW * g:WINDOW * (g + 1)], -jnp.inf)
        sc = jnp.where(mask_cur, sc_all[:, WINDOW * g:WINDOW * (g + 1)], -jnp.inf)
        sink = jnp.where(rep == 0, sink_ref[4 * g],
                         jnp.where(rep == 1, sink_ref[4 * g + 1],
                                   jnp.where(rep == 2, sink_ref[4 * g + 2], sink_ref[4 * g + 3])))
        m = jnp.maximum(jnp.max(jnp.maximum(sp, sc), axis=1, keepdims=True), sink)
        ep = jnp.exp(sp - m)
        ec = jnp.exp(sc - m)
        inv = 1.0 / (jnp.sum(ep + ec, axis=1, keepdims=True) + jnp.exp(sink - m))
        pp.append((ep * inv).astype(BF16))
        pc.append((ec * inv).astype(BF16))
    o = (jnp.dot(jnp.concatenate(pp, axis=1), vbd_prev, preferred_element_type=F32)
         + jnp.dot(jnp.concatenate(pc, axis=1), vbd, preferred_element_type=F32))
    for r in range(ATT_REP):
        o_ref[:, 256 * r:256 * (r + 1)] = o[WINDOW * r:WINDOW * (r + 1), :]
    kp_ref[...] = k
    vp_ref[...] = v

    @pl.when(n == nb - 1)
    def _():
        wk_ref[...] = k
        wv_ref[...] = v


def _swa_prompt(sinks, proj, tabs):
    seq = proj.shape[0]
    c, s1, s2 = tabs
    const = lambda shape: pl.BlockSpec(shape, lambda t: (0,) * len(shape))
    tab = pl.BlockSpec((WINDOW, LANES), lambda t: (t, 0))
    return pl.pallas_call(
        _swa_prompt_kernel,
        out_shape=(jax.ShapeDtypeStruct((seq, 1024), F32),
                   jax.ShapeDtypeStruct((WINDOW, 256), F32), jax.ShapeDtypeStruct((WINDOW, 256), F32)),
        grid=(seq // WINDOW,),
        in_specs=[pl.BlockSpec(memory_space=pltpu.SMEM),
                  pl.BlockSpec((WINDOW, 1024), lambda t: (t, COL_QA)),
                  pl.BlockSpec((WINDOW, 256), lambda t: (t, COL_KA)),
                  pl.BlockSpec((WINDOW, 256), lambda t: (t, COL_VA)),
                  tab, tab, tab],
        out_specs=(pl.BlockSpec((WINDOW, 1024), lambda t: (t, 0)), const((WINDOW, 256)), const((WINDOW, 256))),
        scratch_shapes=[pltpu.VMEM((WINDOW, 256), F32), pltpu.VMEM((WINDOW, 256), F32)],
        compiler_params=_cparams(("arbitrary",)),
        name="swa_prompt",
    )(sinks, proj, proj, proj, c, s1, s2)


def _swa_dec_kernel(q_ref, k_ref, v_ref, kc_ref, vc_ref, c_ref, s1_ref, s2_ref, sink_ref,
                    o_ref, kcn_ref, vcn_ref):
    c, s1, s2 = c_ref[...], s1_ref[...], s2_ref[...]
    q = _rope_att(q_ref[...], _tile_lanes(c, 8), _tile_lanes(s1, 8), _tile_lanes(s2, 8))
    k = _rope_att(k_ref[...], _tile_lanes(c, 2), _tile_lanes(s1, 2), _tile_lanes(s2, 2))
    v = v_ref[...]
    grow = lax.broadcasted_iota(jnp.int32, (8, 256), 0)
    gcol = lax.broadcasted_iota(jnp.int32, (8, 256), 1)
    gmask = ((gcol >> 6) == grow).astype(F32)
    gmask4 = jnp.concatenate([gmask] * ATT_REP, axis=0)
    lane = lax.broadcasted_iota(jnp.int32, (8 * ATT_REP, WINDOW), 1)
    sink = sink_ref[...][:, 0:1]
    scale = ATT_HEAD_DIM ** -0.5
    for j in range(DEC_BLOCK):
        a = jnp.concatenate([jnp.broadcast_to(q[j:j + 1, 256 * r:256 * (r + 1)], (8, 256)) * gmask
                             for r in range(ATT_REP)], axis=0)
        kb = kc_ref[j]
        vb = vc_ref[j]
        knew = k[j:j + 1, :]
        vnew = v[j:j + 1, :]
        s = jnp.where(lane == 0, -jnp.inf, _bdot_nt(a, kb) * scale)
        snew = jnp.sum(a * knew, axis=1, keepdims=True) * scale
        m = jnp.maximum(jnp.maximum(jnp.max(s, axis=1, keepdims=True), snew), sink)
        e = jnp.exp(s - m)
        enew = jnp.exp(snew - m)
        inv = 1.0 / (jnp.sum(e, axis=1, keepdims=True) + enew + jnp.exp(sink - m))
        o = (_bdot(e * inv, vb) + (enew * inv) * vnew) * gmask4
        for r in range(ATT_REP):
            o_ref[j:j + 1, 256 * r:256 * (r + 1)] = jnp.sum(o[8 * r:8 * (r + 1), :], axis=0, keepdims=True)
        kcn_ref[j, 0:WINDOW - 1, :] = kc_ref[j, 1:WINDOW, :]
        kcn_ref[j, WINDOW - 1:WINDOW, :] = knew
        vcn_ref[j, 0:WINDOW - 1, :] = vc_ref[j, 1:WINDOW, :]
        vcn_ref[j, WINDOW - 1:WINDOW, :] = vnew


def _swa_dec(proj, kc, vc, tabs, sink32):
    nb = proj.shape[0]
    c, s1, s2 = tabs
    const = lambda shape: pl.BlockSpec(shape, lambda t: (0,) * len(shape))
    cache = pl.BlockSpec((DEC_BLOCK, WINDOW, 256), lambda t: (t, 0, 0))
    return pl.pallas_call(
        _swa_dec_kernel,
        out_shape=(jax.ShapeDtypeStruct((nb, 1024), F32),
                   jax.ShapeDtypeStruct((nb, WINDOW, 256), F32), jax.ShapeDtypeStruct((nb, WINDOW, 256), F32)),
        grid=(nb // DEC_BLOCK,),
        in_specs=[pl.BlockSpec((DEC_BLOCK, 1024), lambda t: (t, COL_QA)),
                  pl.BlockSpec((DEC_BLOCK, 256), lambda t: (t, COL_KA)),
                  pl.BlockSpec((DEC_BLOCK, 256), lambda t: (t, COL_VA)),
                  cache, cache, const((1, LANES)), const((1, LANES)), const((1, LANES)), const((32, LANES))],
        out_specs=(pl.BlockSpec((DEC_BLOCK, 1024), lambda t: (t, 0)), cache, cache),
        compiler_params=_cparams(("arbitrary",)),
        name="swa_dec",
    )(proj, proj, proj, kc, vc, c, s1, s2, sink32)


def _ret_prompt_kernel(lg_ref, q_ref, k_ref, v_ref, gr_ref, c_ref, s_ref, nrm_ref,
                       o_ref, sfin_ref, st_ref, intra_ref, fs_ref, te_ref):
    t = pl.program_id(0)
    nt = pl.num_programs(0)

    @pl.when(t == 0)
    def _():
        st_ref[...] = jnp.zeros_like(st_ref)
        ri = lax.broadcasted_iota(jnp.int32, (CHUNK, CHUNK), 0).astype(F32)
        ci = lax.broadcasted_iota(jnp.int32, (CHUNK, CHUNK), 1).astype(F32)
        rel = ri - ci
        for h in range(RET_HEADS):
            lg = lg_ref[h]
            intra_ref[h] = jnp.exp(jnp.where(rel >= 0, rel * lg, -jnp.inf))
            fs_ref[h] = jnp.exp((ri + 1.0) * lg)
            te_ref[h] = jnp.exp((CHUNK - 1.0 - ri) * lg)

    c = c_ref[...]
    s = s_ref[...]
    q = q_ref[...]
    k = k_ref[...]
    v = v_ref[...]
    gr = gr_ref[...]
    nrm = nrm_ref[...]
    for h in range(RET_HEADS):
        sl = slice(128 * h, 128 * (h + 1))
        qh = q[:, sl]
        kh = k[:, sl]
        qh = qh * c + pltpu.roll(qh, 64, axis=1) * s
        kh = (kh * c + pltpu.roll(kh, 64, axis=1) * s) * (RET_DK ** -0.5)
        vh = v[:, sl]
        att = _bdot_nt(qh, kh) * intra_ref[h]
        s_old = st_ref[h]
        o = _bdot(att, vh) + _bdot(qh, s_old) * fs_ref[h]
        cd = jnp.exp(jnp.zeros((1, RET_DK), F32) + CHUNK * lg_ref[h])
        st_ref[h] = s_old * cd + _bdot((kh * te_ref[h]).T, vh)
        o = o * lax.rsqrt(jnp.mean(o * o, axis=-1, keepdims=True) + EPS)
        o_ref[:, sl] = o * nrm[:, sl] * _silu(gr[:, sl])

    @pl.when(t == nt - 1)
    def _():
        sfin_ref[...] = st_ref[...]


def _ret_prompt(log_gamma, proj, tabs, nrm):
    seq = proj.shape[0]
    c, s = tabs
    const = lambda shape: pl.BlockSpec(shape, lambda t: (0,) * len(shape))
    col = lambda cidx: pl.BlockSpec((CHUNK, 1024), lambda t: (t, cidx))
    tab = pl.BlockSpec((CHUNK, LANES), lambda t: (t, 0))
    tbl = pltpu.VMEM((RET_HEADS, CHUNK, CHUNK), F32)
    return pl.pallas_call(
        _ret_prompt_kernel,
        out_shape=(jax.ShapeDtypeStruct((seq, 1024), F32), jax.ShapeDtypeStruct((RET_HEADS, RET_DK, 128), F32)),
        grid=(seq // CHUNK,),
        in_specs=[pl.BlockSpec(memory_space=pltpu.SMEM), col(COL_QR), col(COL_KR), col(COL_VR), col(COL_GR),
                  tab, tab, const((1, 1024))],
        out_specs=(pl.BlockSpec((CHUNK, 1024), lambda t: (t, 0)), const((RET_HEADS, RET_DK, 128))),
        scratch_shapes=[tbl, tbl, tbl, tbl],
        compiler_params=_cparams(("arbitrary",)),
        name="ret_prompt",
    )(log_gamma, proj, proj, proj, proj, c, s, nrm)


def _ret_dec_pre_kernel(q_ref, k_ref, c_ref, s_ref, qrot_ref, kt_ref):
    c = c_ref[...]
    s = s_ref[...]
    q = q_ref[...]
    k = k_ref[...]
    ks = []
    for h in range(RET_HEADS):
        sl = slice(128 * h, 128 * (h + 1))
        qh = q[:, sl]
        kh = k[:, sl]
        qrot_ref[:, sl] = qh * c + pltpu.roll(qh, 64, axis=1) * s
        ks.append((kh * c + pltpu.roll(kh, 64, axis=1) * s) * (RET_DK ** -0.5))
    kt_ref[...] = jnp.concatenate(ks, axis=1).T


def _ret_dec_pre(proj, tabs):
    nb = proj.shape[0]
    c, s = tabs
    const = lambda shape: pl.BlockSpec(shape, lambda t: (0,) * len(shape))
    return pl.pallas_call(
        _ret_dec_pre_kernel,
        out_shape=(jax.ShapeDtypeStruct((nb, 1024), F32), jax.ShapeDtypeStruct((1024, nb), F32)),
        grid=(1,),
        in_specs=[pl.BlockSpec((nb, 1024), lambda t: (0, COL_QR)), pl.BlockSpec((nb, 1024), lambda t: (0, COL_KR)),
                  const((1, LANES)), const((1, LANES))],
        out_specs=(const((nb, 1024)), const((1024, nb))),
        compiler_params=_cparams(("arbitrary",)),
        name="ret_dec_pre",
    )(proj, proj, c, s)


def _ret_dec_state_kernel(gam_ref, kt_ref, q_ref, v_ref, gr_ref, st_ref, nrm_ref, o_ref, stn_ref):
    i = pl.program_id(0)
    shift = (LANES - i * DEC_BLOCK) % LANES
    kr = pltpu.roll(kt_ref[...], shift, axis=1)
    q = q_ref[...]
    v = v_ref[...]
    rowid = lax.broadcasted_iota(jnp.int32, (DEC_BLOCK, RET_DK), 0)
    os_ = [jnp.zeros((DEC_BLOCK, 128), F32) for _ in range(RET_HEADS)]
    for j in range(DEC_BLOCK):
        for h in range(RET_HEADS):
            lo, hi = 128 * h, 128 * (h + 1)
            kcol = jnp.broadcast_to(kr[lo:hi, j:j + 1], (RET_DK, 128))
            s_new = st_ref[j, lo:hi, :] * gam_ref[h] + kcol * v[j:j + 1, lo:hi]
            stn_ref[j, lo:hi, :] = s_new
            qm = jnp.where(rowid == j, q[:, lo:hi], 0.0)
            os_[h] = os_[h] + _bdot(qm, s_new)
    gr = gr_ref[...]
    nrm = nrm_ref[...]
    for h in range(RET_HEADS):
        sl = slice(128 * h, 128 * (h + 1))
        o = os_[h]
        o = o * lax.rsqrt(jnp.mean(o * o, axis=-1, keepdims=True) + EPS)
        o_ref[:, sl] = o * nrm[:, sl] * _silu(gr[:, sl])


def _ret_dec_state(gam, kt, qrot, proj, st, nrm):
    nb = qrot.shape[0]
    const = lambda shape: pl.BlockSpec(shape, lambda t: (0,) * len(shape))
    stb = pl.BlockSpec((DEC_BLOCK, 1024, 128), lambda t: (t, 0, 0))
    return pl.pallas_call(
        _ret_dec_state_kernel,
        out_shape=(jax.ShapeDtypeStruct((nb, 1024), F32), jax.ShapeDtypeStruct((nb, 1024, 128), F32)),
        grid=(nb // DEC_BLOCK,),
        in_specs=[pl.BlockSpec(memory_space=pltpu.SMEM), const((1024, nb)),
                  pl.BlockSpec((DEC_BLOCK, 1024), lambda t: (t, 0)),
                  pl.BlockSpec((DEC_BLOCK, 1024), lambda t: (t, COL_VR)),
                  pl.BlockSpec((DEC_BLOCK, 1024), lambda t: (t, COL_GR)),
                  stb, const((1, 1024))],
        out_specs=(pl.BlockSpec((DEC_BLOCK, 1024), lambda t: (t, 0)), stb),
        compiler_params=_cparams(("arbitrary",)),
        name="ret_dec_state",
    )(gam, kt, qrot, proj, proj, st, nrm)


def _merge_kernel(x_ref, a_ref, b_ref, c_ref, g1_ref, g2_ref, g3_ref, w1_ref, w2_ref, w3_ref, wo_ref, o_ref):
    m = (_sigmoid(g1_ref[...]) * jnp.dot(a_ref[...].astype(BF16), w1_ref[...], preferred_element_type=F32)
         + _sigmoid(g2_ref[...]) * jnp.dot(b_ref[...].astype(BF16), w2_ref[...], preferred_element_type=F32)
         + _sigmoid(g3_ref[...]) * jnp.dot(c_ref[...].astype(BF16), w3_ref[...], preferred_element_type=F32))
    o_ref[...] = x_ref[...] + jnp.dot(m.astype(BF16), wo_ref[...], preferred_element_type=F32)


def _merge(x, a, b, c, proj, w1, w2, w3, wo):
    rows = x.shape[0]
    tm = min(rows, 256)
    rowb = pl.BlockSpec((tm, 1024), lambda i: (i, 0))
    gate = lambda k: pl.BlockSpec((tm, 1024), lambda i: (i, COL_GATE + k))
    wsp = pl.BlockSpec((1024, 1024), lambda i: (0, 0))
    return pl.pallas_call(
        _merge_kernel,
        out_shape=jax.ShapeDtypeStruct((rows, 1024), F32),
        grid=(rows // tm,),
        in_specs=[rowb, rowb, rowb, rowb, gate(0), gate(1), gate(2), wsp, wsp, wsp, wsp],
        out_specs=rowb,
        compiler_params=_cparams(("parallel",)),
        name="merge",
    )(x, a, b, c, proj, proj, proj, w1, w2, w3, wo)


FF_TILE = 1408


def _ffn_kernel(x_ref, g_ref, wg_ref, wu_ref, wd_ref, o_ref, h_ref, acc_ref):
    j = pl.program_id(1)

    @pl.when(j == 0)
    def _():
        h_ref[...] = _rms(x_ref[...], g_ref[...]).astype(BF16)
        acc_ref[...] = jnp.zeros_like(acc_ref)

    h = h_ref[...]
    a = jnp.dot(h, wg_ref[...], preferred_element_type=F32)
    u = jnp.dot(h, wu_ref[...], preferred_element_type=F32)
    acc_ref[...] += jnp.dot((_silu(a) * u).astype(BF16), wd_ref[...], preferred_element_type=F32)

    @pl.when(j == pl.num_programs(1) - 1)
    def _():
        o_ref[...] = x_ref[...] + acc_ref[...]


def _ffn(x, g, wg, wu, wd):
    rows = x.shape[0]
    tm = min(rows, 512)
    return pl.pallas_call(
        _ffn_kernel,
        out_shape=jax.ShapeDtypeStruct((rows, 1024), F32),
        grid=(rows // tm, D_FF // FF_TILE),
        in_specs=[pl.BlockSpec((tm, 1024), lambda i, j: (i, 0)), pl.BlockSpec((1, 1024), lambda i, j: (0, 0)),
                  pl.BlockSpec((1024, FF_TILE), lambda i, j: (0, j)), pl.BlockSpec((1024, FF_TILE), lambda i, j: (0, j)),
                  pl.BlockSpec((FF_TILE, 1024), lambda i, j: (j, 0))],
        out_specs=pl.BlockSpec((tm, 1024), lambda i, j: (i, 0)),
        scratch_shapes=[pltpu.VMEM((tm, 1024), BF16), pltpu.VMEM((tm, 1024), F32)],
        compiler_params=_cparams(("parallel", "arbitrary")),
        name="ffn",
    )(x, g, wg, wu, wd)


MOE_FF_TILE = 256


def _top2(h, rw, rb, lane):
    logits = jnp.dot(h, rw, preferred_element_type=F32) + rb
    logits = jnp.where(lane < N_EXPERTS, logits, -jnp.inf)
    m1 = jnp.max(logits, axis=1, keepdims=True)
    i1 = jnp.min(jnp.where(logits == m1, lane, float(LANES)), axis=1, keepdims=True)
    rest = jnp.where(lane == i1, -jnp.inf, logits)
    m2 = jnp.max(rest, axis=1, keepdims=True)
    i2 = jnp.min(jnp.where(rest == m2, lane, float(LANES)), axis=1, keepdims=True)
    e2 = jnp.exp(m2 - m1)
    p1 = 1.0 / (1.0 + e2)
    return i1, i2, p1, e2 * p1


def _moe_kernel(x_ref, g_ref, rw_ref, rb_ref, wg_ref, wu_ref, wd_ref, o_ref, h_ref, acc_ref, comb_ref):
    e = pl.program_id(1)
    j = pl.program_id(2)
    tm = x_ref.shape[0]
    lane = lax.broadcasted_iota(jnp.int32, (tm, LANES), 1).astype(F32)

    @pl.when(jnp.logical_and(e == 0, j == 0))
    def _():
        h = _rms(x_ref[...], g_ref[...]).astype(BF16)
        h_ref[...] = h
        i1, i2, p1, p2 = _top2(h, rw_ref[...], rb_ref[...], lane)
        comb_ref[...] = jnp.where(lane == i1, p1, 0.0) + jnp.where(lane == i2, p2, 0.0)
        acc_ref[...] = jnp.zeros_like(acc_ref)

    ce = jnp.sum(jnp.where(lane == e.astype(F32), comb_ref[...], 0.0), axis=1, keepdims=True)
    h = h_ref[...]
    a = jnp.dot(h, wg_ref[0], preferred_element_type=F32)
    u = jnp.dot(h, wu_ref[0], preferred_element_type=F32)
    acc_ref[...] += ce * jnp.dot((_silu(a) * u).astype(BF16), wd_ref[0], preferred_element_type=F32)

    @pl.when(jnp.logical_and(e == pl.num_programs(1) - 1, j == pl.num_programs(2) - 1))
    def _():
        o_ref[...] = x_ref[...] + acc_ref[...]


def _moe(x, g, rw, rb, wg, wu, wd):
    rows = x.shape[0]
    tm = min(rows, 1024)
    tf = MOE_FF_TILE
    return pl.pallas_call(
        _moe_kernel,
        out_shape=jax.ShapeDtypeStruct((rows, 1024), F32),
        grid=(rows // tm, N_EXPERTS, D_FF // tf),
        in_specs=[pl.BlockSpec((tm, 1024), lambda i, e, j: (i, 0)), pl.BlockSpec((1, 1024), lambda i, e, j: (0, 0)),
                  pl.BlockSpec((1024, LANES), lambda i, e, j: (0, 0)), pl.BlockSpec((1, LANES), lambda i, e, j: (0, 0)),
                  pl.BlockSpec((1, 1024, tf), lambda i, e, j: (e, 0, j)),
                  pl.BlockSpec((1, 1024, tf), lambda i, e, j: (e, 0, j)),
                  pl.BlockSpec((1, tf, 1024), lambda i, e, j: (e, j, 0))],
        out_specs=pl.BlockSpec((tm, 1024), lambda i, e, j: (i, 0)),
        scratch_shapes=[pltpu.VMEM((tm, 1024), BF16), pltpu.VMEM((tm, 1024), F32), pltpu.VMEM((tm, LANES), F32)],
        compiler_params=_cparams(("parallel", "arbitrary", "arbitrary")),
        name="moe",
    )(x, g, rw, rb, wg, wu, wd)


MOE_ROWS = 512
MOE_GROUP_FF = 1408


def _router_kernel(x_ref, g_ref, rw_ref, rb_ref, o_ref):
    tm = x_ref.shape[0]
    lane = lax.broadcasted_iota(jnp.int32, (tm, LANES), 1).astype(F32)
    h = _rms(x_ref[...], g_ref[...]).astype(BF16)
    i1, i2, p1, p2 = _top2(h, rw_ref[...], rb_ref[...], lane)
    o_ref[...] = jnp.where(lane == 0.0, i1, jnp.where(lane == 1.0, i2, jnp.where(lane == 2.0, p1,
                           jnp.where(lane == 3.0, p2, 0.0))))


def _router(x, g, rw, rb):
    rows = x.shape[0]
    tm = min(rows, 1024)
    return pl.pallas_call(
        _router_kernel,
        out_shape=jax.ShapeDtypeStruct((rows, LANES), F32),
        grid=(rows // tm,),
        in_specs=[pl.BlockSpec((tm, 1024), lambda i: (i, 0)), pl.BlockSpec((1, 1024), lambda i: (0, 0)),
                  pl.BlockSpec((1024, LANES), lambda i: (0, 0)), pl.BlockSpec((1, LANES), lambda i: (0, 0))],
        out_specs=pl.BlockSpec((tm, LANES), lambda i: (i, 0)),
        compiler_params=_cparams(("parallel",)),
        name="router",
    )(x, g, rw, rb)


def _route_plan(route, tm):
    n = route.shape[0]
    n_tiles = (2 * n) // tm + N_EXPERTS
    e_flat = route[:, :2].astype(jnp.int32).reshape(-1)
    onehot = (e_flat[:, None] == jnp.arange(N_EXPERTS, dtype=jnp.int32)[None, :]).astype(jnp.int32)
    csum = jnp.cumsum(onehot, axis=0)
    counts = csum[-1]
    tiles_e = (counts + tm - 1) // tm
    tile_end = jnp.cumsum(tiles_e)
    row_start = (tile_end - tiles_e) * tm
    pos = jnp.sum((csum - onehot + row_start[None, :]) * onehot, axis=1).astype(jnp.int32)
    tile_expert = jnp.minimum(jnp.sum(jnp.arange(n_tiles, dtype=jnp.int32)[:, None] >= tile_end[None, :], axis=1),
                              N_EXPERTS - 1).astype(jnp.int32)
    n_used = tile_end[-1:].astype(jnp.int32)
    src = jnp.zeros((n_tiles * tm,), jnp.int32).at[pos].set(jnp.arange(2 * n, dtype=jnp.int32) // 2)
    return pos, src.reshape(n_tiles, 1, tm), tile_expert, n_used


def _moe_group_kernel(te_ref, nu_ref, src_ref, srcn_ref, x_hbm, g_ref, wg_ref, wu_ref, wd_ref,
                      y_ref, buf, sem, h_ref, acc_ref):
    i = pl.program_id(0)
    j = pl.program_id(1)
    tm = buf.shape[1]
    slot = i % 2
    active = i < nu_ref[0]

    def row_copy(idx_ref, s, r):
        return pltpu.make_async_copy(x_hbm.at[pl.ds(idx_ref[0, 0, r], 1), :], buf.at[s, pl.ds(r, 1), :], sem.at[s])

    def gather(idx_ref, s):
        def body(r, c):
            row_copy(idx_ref, s, r).start()
            return c
        lax.fori_loop(0, tm, body, 0)

    @pl.when(jnp.logical_and(active, j == 0))
    def _():
        @pl.when(i == 0)
        def _():
            gather(src_ref, 0)

        pltpu.make_async_copy(x_hbm.at[pl.ds(0, tm), :], buf.at[slot], sem.at[slot]).wait()

        @pl.when(i + 1 < nu_ref[0])
        def _():
            gather(srcn_ref, 1 - slot)

        h_ref[...] = _rms(buf[slot], g_ref[...]).astype(BF16)
        acc_ref[...] = jnp.zeros_like(acc_ref)

    @pl.when(active)
    def _():
        h = h_ref[...]
        a = jnp.dot(h, wg_ref[0], preferred_element_type=F32)
        u = jnp.dot(h, wu_ref[0], preferred_element_type=F32)
        acc_ref[...] += jnp.dot((_silu(a) * u).astype(BF16), wd_ref[0], preferred_element_type=F32)

    @pl.when(j == pl.num_programs(1) - 1)
    def _():
        @pl.when(active)
        def _():
            y_ref[...] = acc_ref[...]

        @pl.when(jnp.logical_not(active))
        def _():
            y_ref[...] = jnp.zeros_like(y_ref)


def _moe_group(tile_expert, n_used, src, x, g, wg, wu, wd):
    n_tiles, _, tm = src.shape
    tf = MOE_GROUP_FF
    grid_spec = pltpu.PrefetchScalarGridSpec(
        num_scalar_prefetch=2,
        grid=(n_tiles, D_FF // tf),
        in_specs=[pl.BlockSpec((1, 1, tm), lambda i, j, te, nu: (i, 0, 0), memory_space=pltpu.SMEM),
                  pl.BlockSpec((1, 1, tm), lambda i, j, te, nu: (jnp.minimum(i + 1, n_tiles - 1), 0, 0),
                               memory_space=pltpu.SMEM),
                  pl.BlockSpec(memory_space=pl.ANY),
                  pl.BlockSpec((1, 1024), lambda i, j, te, nu: (0, 0)),
                  pl.BlockSpec((1, 1024, tf), lambda i, j, te, nu: (te[i], 0, j)),
                  pl.BlockSpec((1, 1024, tf), lambda i, j, te, nu: (te[i], 0, j)),
                  pl.BlockSpec((1, tf, 1024), lambda i, j, te, nu: (te[i], j, 0))],
        out_specs=pl.BlockSpec((tm, 1024), lambda i, j, te, nu: (i, 0)),
        scratch_shapes=[pltpu.VMEM((2, tm, 1024), F32), pltpu.SemaphoreType.DMA((2,)),
                        pltpu.VMEM((tm, 1024), BF16), pltpu.VMEM((tm, 1024), F32)])
    return pl.pallas_call(
        _moe_group_kernel,
        out_shape=jax.ShapeDtypeStruct((n_tiles * tm, 1024), F32),
        grid_spec=grid_spec,
        compiler_params=_cparams(("arbitrary", "arbitrary")),
        name="moe_group",
    )(tile_expert, n_used, src, src, x, g, wg, wu, wd)


def _moe_combine_kernel(pos_ref, posn_ref, x_ref, r_ref, y_hbm, o_ref, bufa, bufb, sem):
    i = pl.program_id(0)
    tm = x_ref.shape[0]
    slot = i % 2

    def gather(idx_ref, s):
        def body(t, c):
            pltpu.make_async_copy(y_hbm.at[pl.ds(idx_ref[0, 0, 2 * t], 1), :], bufa.at[s, pl.ds(t, 1), :],
                                  sem.at[0, s]).start()
            pltpu.make_async_copy(y_hbm.at[pl.ds(idx_ref[0, 0, 2 * t + 1], 1), :], bufb.at[s, pl.ds(t, 1), :],
                                  sem.at[1, s]).start()
            return c
        lax.fori_loop(0, tm, body, 0)

    @pl.when(i == 0)
    def _():
        gather(pos_ref, 0)

    pltpu.make_async_copy(y_hbm.at[pl.ds(0, tm), :], bufa.at[slot], sem.at[0, slot]).wait()
    pltpu.make_async_copy(y_hbm.at[pl.ds(0, tm), :], bufb.at[slot], sem.at[1, slot]).wait()

    @pl.when(i + 1 < pl.num_programs(0))
    def _():
        gather(posn_ref, 1 - slot)

    r = r_ref[...]
    o_ref[...] = x_ref[...] + r[:, 2:3] * bufa[slot] + r[:, 3:4] * bufb[slot]


def _moe_combine(pos, x, route, y):
    rows = x.shape[0]
    tm = MOE_ROWS
    n = rows // tm
    pos3 = pos.reshape(n, 1, 2 * tm)
    return pl.pallas_call(
        _moe_combine_kernel,
        out_shape=jax.ShapeDtypeStruct((rows, 1024), F32),
        grid=(n,),
        in_specs=[pl.BlockSpec((1, 1, 2 * tm), lambda i: (i, 0, 0), memory_space=pltpu.SMEM),
                  pl.BlockSpec((1, 1, 2 * tm), lambda i: (jnp.minimum(i + 1, n - 1), 0, 0), memory_space=pltpu.SMEM),
                  pl.BlockSpec((tm, 1024), lambda i: (i, 0)), pl.BlockSpec((tm, LANES), lambda i: (i, 0)),
                  pl.BlockSpec(memory_space=pl.ANY)],
        out_specs=pl.BlockSpec((tm, 1024), lambda i: (i, 0)),
        scratch_shapes=[pltpu.VMEM((2, tm, 1024), F32), pltpu.VMEM((2, tm, 1024), F32),
                        pltpu.SemaphoreType.DMA((2, 2))],
        compiler_params=_cparams(("arbitrary",)),
        name="moe_combine",
    )(pos3, pos3, x, route, y)


def _moe_routed(x, g, rw, rb, wg, wu, wd):
    route = _router(x, g, rw, rb)
    pos, src, tile_expert, n_used = _route_plan(route, MOE_ROWS)
    y = _moe_group(tile_expert, n_used, src, x, g, wg, wu, wd)
    return _moe_combine(pos, x, route, y)


def _ple_kernel(x_ref, p_ref, g_ref, wp_ref, wgt_ref, gf_ref, o_ref, *, final):
    x = x_ref[...]
    emb = jnp.dot(p_ref[...].astype(BF16), wp_ref[...], preferred_element_type=F32)
    gate = _sigmoid(jnp.dot(_rms(x, g_ref[...]).astype(BF16), wgt_ref[...], preferred_element_type=F32))
    y = x + emb * gate
    if final:
        y = _rms(y, gf_ref[...])
    o_ref[...] = y


def _ple(x, p, g, wp, wgt, gf, final):
    rows = x.shape[0]
    tm = min(rows, 512)
    vec = pl.BlockSpec((1, 1024), lambda i: (0, 0))
    return pl.pallas_call(
        functools.partial(_ple_kernel, final=final),
        out_shape=jax.ShapeDtypeStruct((rows, 1024), F32),
        grid=(rows // tm,),
        in_specs=[pl.BlockSpec((tm, 1024), lambda i: (i, 0)), pl.BlockSpec((tm, PLE_DIM), lambda i: (i, 0)), vec,
                  pl.BlockSpec((PLE_DIM, 1024), lambda i: (0, 0)), pl.BlockSpec((1024, 1024), lambda i: (0, 0)), vec],
        out_specs=pl.BlockSpec((tm, 1024), lambda i: (i, 0)),
        compiler_params=_cparams(("parallel",)),
        name="ple",
    )(x, p, g, wp, wgt, gf)


def _rmajor_cols(w):
    k = w.shape[0]
    return w.reshape(k, ATT_KV_HEADS, ATT_REP, ATT_HEAD_DIM).transpose(0, 2, 1, 3).reshape(k, ATT_Q_HEADS * ATT_HEAD_DIM)


def _prep_w_in(w):
    offs = [0]
    for wd in IN_WIDTHS:
        offs.append(offs[-1] + wd)
    z, xbc, dt, qa, ka, va, qr, kr, vr, gr, gates = [w[:, offs[i]:offs[i + 1]] for i in range(len(IN_WIDTHS))]
    used = 2048 + 1024 * 6 + 3072 + 256 + 256 + 16
    pad = jnp.zeros((w.shape[0], N_PROJ - used), w.dtype)
    return jnp.concatenate([xbc, z, _rmajor_cols(qa), qr, kr, vr, gr, gates, ka, va, dt, pad], axis=1).astype(BF16)


def _att_tables(pos):
    half = ROPE_DIM // 2
    inv = jnp.exp(-math.log(ROPE_THETA) * jnp.arange(half, dtype=F32) * (2.0 / ROPE_DIM))
    ang = pos.astype(F32)[:, None] * inv[None, :]
    cos, sin = jnp.cos(ang), jnp.sin(ang)
    n = pos.shape[0]
    one = jnp.ones((n, ATT_HEAD_DIM - ROPE_DIM), F32)
    zero8 = jnp.zeros((n, half), F32)
    zero = jnp.zeros((n, ATT_HEAD_DIM - ROPE_DIM), F32)
    c = jnp.concatenate([cos, cos, one], axis=1)
    s1 = jnp.concatenate([zero8, sin, zero], axis=1)
    s2 = jnp.concatenate([-sin, zero8, zero], axis=1)
    return tuple(jnp.concatenate([t, t], axis=1) for t in (c, s1, s2))


def _ret_tables(pos):
    half = RET_DK // 2
    inv = jnp.exp(-math.log(RET_THETA) * jnp.arange(half, dtype=F32) * (2.0 / RET_DK))
    ang = pos.astype(F32)[:, None] * inv[None, :]
    cos, sin = jnp.cos(ang), jnp.sin(ang)
    return jnp.concatenate([cos, cos], axis=1), jnp.concatenate([-sin, sin], axis=1)


def _pad_lanes(v, fill=0.0):
    return jnp.concatenate([v.astype(F32), jnp.full((LANES - v.shape[0],), fill, F32)])[None, :]


def kernel(x_prompt, x_sample, state_ssm, state_conv, cache_win_k, cache_win_v, state_ret, p_prompt, p_sample,
           w_in, conv_w, conv_b, dt_bias, a_log, d_skip, ssd_norm, attn_sinks, ret_norm, w_o_ssd, w_o_att, w_o_ret,
           w_out, norm_mix, norm_ffn, norm_ple, ffn_w_gate, ffn_w_up, ffn_w_down, router_w, router_b, moe_w_gate,
           moe_w_up, moe_w_down, w_ple, w_ple_gate, norm_final):
    seq = x_prompt.shape[1]
    nb = x_sample.shape[0]
    xp = x_prompt.reshape(seq, D_MODEL)
    xs = x_sample.reshape(nb, D_MODEL)
    pos_p = jnp.arange(seq)
    pos_s = PAST_LEN + jnp.arange(1)
    att_tab_p, att_tab_s = _att_tables(pos_p), _att_tables(pos_s)
    ret_tab_p, ret_tab_s = _ret_tables(pos_p), _ret_tables(pos_s)
    log_gamma = jnp.log1p(-jnp.exp2(-5.0 - jnp.arange(RET_HEADS, dtype=F32)))
    gamma = jnp.exp(log_gamma)
    row = lambda v: v.astype(F32)[None, :]

    new_p = [[], [], [], [], []]
    new_s = [[], [], [], [], []]
    for i in range(DEPTH):
        w_in_i = _prep_w_in(w_in[i])
        cw, cb = conv_w[i], row(conv_b[i])
        dtb, alog = _pad_lanes(dt_bias[i]), _pad_lanes(a_log[i])
        dsk = row(jnp.repeat(d_skip[i], SSD_HEAD_DIM))
        nrm_ssd, nrm_ret = row(ssd_norm[i]), row(ret_norm[i])
        sinks = attn_sinks[i].astype(F32)
        sink32 = jnp.zeros((ATT_REP, 8), F32).at[:, :ATT_KV_HEADS].set(sinks.reshape(ATT_KV_HEADS, ATT_REP).T)
        sink32 = jnp.broadcast_to(sink32.reshape(32, 1), (32, LANES))
        w1 = w_o_ssd[i].astype(BF16)
        w2 = w_o_att[i].reshape(ATT_KV_HEADS, ATT_REP, ATT_HEAD_DIM, D_MODEL).transpose(1, 0, 2, 3) \
            .reshape(ATT_Q_HEADS * ATT_HEAD_DIM, D_MODEL).astype(BF16)
        w3 = w_o_ret[i].astype(BF16)
        wo = w_out[i].astype(BF16)
        g_mix, g_ffn, g_ple = row(norm_mix[i]), row(norm_ffn[i]), row(norm_ple[i])
        wp, wpg = w_ple[i].astype(BF16), w_ple_gate[i].astype(BF16)
        gf = row(norm_final)
        j = i // 2
        if i % 2 == 0:
            ffw = (ffn_w_gate[j].astype(BF16), ffn_w_up[j].astype(BF16), ffn_w_down[j].astype(BF16))
        else:
            rw = jnp.concatenate([router_w[j], jnp.zeros((D_MODEL, LANES - N_EXPERTS), F32)], axis=1).astype(BF16)
            ffw = (rw, _pad_lanes(router_b[j]), moe_w_gate[j].astype(BF16), moe_w_up[j].astype(BF16),
                   moe_w_down[j].astype(BF16))
        final = i == DEPTH - 1

        proj = _inproj(xp, g_mix, w_in_i)
        y_ssd, ssm_fin, conv_fin = _ssd_prompt(proj, cw, cb, dtb, alog, dsk, nrm_ssd)
        o_att, wk, wv = _swa_prompt(sinks, proj, att_tab_p)
        o_ret, ret_fin = _ret_prompt(log_gamma, proj, ret_tab_p, nrm_ret)
        xp = _merge(xp, y_ssd, o_att, o_ret, proj, w1, w2, w3, wo)
        xp = _ffn(xp, g_ffn, *ffw) if i % 2 == 0 else _moe_routed(xp, g_ffn, *ffw)
        xp = _ple(xp, p_prompt[i].reshape(seq, PLE_DIM), g_ple, wp, wpg, gf, final)
        new_p[0].append(ssm_fin.reshape(1, SSD_HEADS, SSD_HEAD_DIM, SSD_STATE))
        new_p[1].append(conv_fin[None])
        new_p[2].append(wk.reshape(1, WINDOW, ATT_KV_HEADS, ATT_HEAD_DIM))
        new_p[3].append(wv.reshape(1, WINDOW, ATT_KV_HEADS, ATT_HEAD_DIM))
        new_p[4].append(ret_fin[None])

        proj = _inproj(xs, g_mix, w_in_i)
        cst_t = jnp.transpose(state_conv[i], (1, 0, 2))
        cnew_t, xs_conv, bc, dec_t, xdt_t = _ssd_dec_pre(proj, cst_t, cw, cb, dtb, alog)
        y_ssd, ssm_new = _ssd_dec_state(dec_t, xdt_t, bc, proj, xs_conv,
                                        state_ssm[i].reshape(nb, 1024, SSD_STATE), dsk, nrm_ssd)
        o_att, kc_new, vc_new = _swa_dec(proj, cache_win_k[i].reshape(nb, WINDOW, 256),
                                         cache_win_v[i].reshape(nb, WINDOW, 256), att_tab_s, sink32)
        qrot, kt = _ret_dec_pre(proj, ret_tab_s)
        o_ret, ret_new = _ret_dec_state(gamma, kt, qrot, proj, state_ret[i].reshape(nb, 1024, 128), nrm_ret)
        xs = _merge(xs, y_ssd, o_att, o_ret, proj, w1, w2, w3, wo)
        xs = _ffn(xs, g_ffn, *ffw) if i % 2 == 0 else _moe(xs, g_ffn, *ffw)
        xs = _ple(xs, p_sample[i].reshape(nb, PLE_DIM), g_ple, wp, wpg, gf, final)
        new_s[0].append(ssm_new.reshape(nb, SSD_HEADS, SSD_HEAD_DIM, SSD_STATE))
        new_s[1].append(jnp.transpose(cnew_t, (1, 0, 2)))
        new_s[2].append(kc_new.reshape(nb, WINDOW, ATT_KV_HEADS, ATT_HEAD_DIM))
        new_s[3].append(vc_new.reshape(nb, WINDOW, ATT_KV_HEADS, ATT_HEAD_DIM))
        new_s[4].append(ret_new.reshape(nb, RET_HEADS, RET_DK, 128))

    y_prompt = xp.reshape(1, seq, D_MODEL)
    y_sample = xs.reshape(nb, 1, D_MODEL)
    outs_p = [jnp.stack(l) for l in new_p]
    outs_s = [jnp.stack(l) for l in new_s]
    return (y_prompt, y_sample, *outs_p, *outs_s)
```

```python
import functools
import math

import jax
import jax.numpy as jnp
from jax import lax
from jax.experimental import pallas as pl
from jax.experimental.pallas import tpu as pltpu

F32 = jnp.float32
BF16 = jnp.bfloat16

D_MODEL = 1024
DEPTH = 2
PAST_LEN = 16384
SSD_HEADS = 16
SSD_HEAD_DIM = 64
SSD_GROUPS = 4
SSD_STATE = 128
SSD_CONV = 4
SSD_CONV_DIM = 2048
ATT_HEAD_DIM = 64
ATT_Q_HEADS = 16
ATT_KV_HEADS = 4
ATT_REP = ATT_Q_HEADS // ATT_KV_HEADS
WINDOW = 128
ROPE_THETA = 500000.0
ROPE_DIM = 16
RET_HEADS = 8
RET_DK = 128
RET_THETA = 10000.0
CHUNK = 128
D_FF = 2816
N_EXPERTS = 8
PLE_DIM = 256
EPS = 1e-6

IN_WIDTHS = (1024, 2048, 16, 1024, 256, 256, 1024, 1024, 1024, 1024, 3072)
N_PROJ = 12288
COL_XBC = 0
COL_Z = 2
COL_QA = 3
COL_QR = 4
COL_KR = 5
COL_VR = 6
COL_GR = 7
COL_GATE = 8
COL_KA = 44
COL_VA = 45
COL_DT = 92

LANES = 128
VMEM_LIMIT = 48 * 1024 * 1024


def _cparams(sem, vmem=VMEM_LIMIT):
    return pltpu.CompilerParams(dimension_semantics=sem, vmem_limit_bytes=vmem)


def _bdot(a, b):
    return jnp.dot(a.astype(BF16), b.astype(BF16), preferred_element_type=F32)


def _bdot_nt(a, b):
    return lax.dot_general(a.astype(BF16), b.astype(BF16), (((1,), (1,)), ((), ())),
                           preferred_element_type=F32)


def _split3(x):
    x0 = x.astype(BF16)
    r1 = x - x0.astype(F32)
    x1 = r1.astype(BF16)
    x2 = (r1 - x1.astype(F32)).astype(BF16)
    return x0, x1, x2


def _dot_exact_lhs01(m01, x):
    m = m01.astype(BF16)
    x0, x1, x2 = _split3(x)
    d = lambda b: jnp.dot(m, b, preferred_element_type=F32)
    return d(x0) + d(x1) + d(x2)


def _dot_exact_rhs01(x, m01):
    m = m01.astype(BF16)
    x0, x1, x2 = _split3(x)
    d = lambda a: jnp.dot(a, m, preferred_element_type=F32)
    return d(x0) + d(x1) + d(x2)


def _rms(x, g):
    return x * lax.rsqrt(jnp.mean(x * x, axis=-1, keepdims=True) + EPS) * g


def _sigmoid(x):
    return 1.0 / (1.0 + jnp.exp(-x))


def _silu(x):
    return x * _sigmoid(x)


def _softplus(x):
    return jnp.maximum(x, 0.0) + jnp.log1p(jnp.exp(-jnp.abs(x)))


def _rope_att(x, c, s1, s2):
    w = x.shape[1]
    return x * c + pltpu.roll(x, 8, axis=1) * s1 + pltpu.roll(x, w - 8, axis=1) * s2


def _tile_lanes(t, n):
    return jnp.concatenate([t] * n, axis=1) if n > 1 else t


INPROJ_TN = 1024
DT_TILE = (COL_DT * LANES) // INPROJ_TN
DT_OFF = COL_DT * LANES - DT_TILE * INPROJ_TN


def _inproj_kernel(x_ref, g_ref, w_ref, o_ref, dt_ref, h_ref):
    j = pl.program_id(1)

    @pl.when(j == 0)
    def _():
        h_ref[...] = _rms(x_ref[...], g_ref[...]).astype(BF16)

    acc = jnp.dot(h_ref[...], w_ref[...], preferred_element_type=F32)
    o_ref[...] = acc.astype(o_ref.dtype)

    @pl.when(j == DT_TILE)
    def _():
        dt_ref[...] = acc[:, DT_OFF:DT_OFF + LANES]


def _inproj(x, g, w, out_dtype):
    rows = x.shape[0]
    tm = min(rows, 1024)
    tn = INPROJ_TN
    return pl.pallas_call(
        _inproj_kernel,
        out_shape=(jax.ShapeDtypeStruct((rows, N_PROJ), out_dtype), jax.ShapeDtypeStruct((rows, LANES), F32)),
        grid=(rows // tm, N_PROJ // tn),
        in_specs=[pl.BlockSpec((tm, D_MODEL), lambda i, j: (i, 0)),
                  pl.BlockSpec((1, D_MODEL), lambda i, j: (0, 0)),
                  pl.BlockSpec((D_MODEL, tn), lambda i, j: (0, j))],
        out_specs=(pl.BlockSpec((tm, tn), lambda i, j: (i, j)), pl.BlockSpec((tm, LANES), lambda i, j: (i, 0))),
        scratch_shapes=[pltpu.VMEM((tm, D_MODEL), BF16)],
        compiler_params=_cparams(("parallel", "arbitrary")),
        name="inproj",
    )(x, g, w)


def _ssd_prompt_kernel(xbc_ref, z_ref, dt_ref, cw_ref, cb_ref, dtb_ref, alog_ref, dsk_ref, nrm_ref,
                       y_ref, sfin_ref, cfin_ref, xpad_ref, s_ref):
    t = pl.program_id(0)
    nt = pl.num_programs(0)

    @pl.when(t == 0)
    def _():
        xpad_ref[0:8, :] = jnp.zeros((8, SSD_CONV_DIM), F32)
        s_ref[...] = jnp.zeros_like(s_ref)

    xbc = xbc_ref[...].astype(F32)
    xpad_ref[8:8 + CHUNK, :] = xbc
    cw = cw_ref[...]
    acc = (xbc * cw[3:4, :] + xpad_ref[7:7 + CHUNK, :] * cw[2:3, :]
           + xpad_ref[6:6 + CHUNK, :] * cw[1:2, :] + xpad_ref[5:5 + CHUNK, :] * cw[0:1, :] + cb_ref[...])
    conv = _silu(acc)
    xpad_ref[0:8, :] = xbc[CHUNK - 8:CHUNK, :]

    xs = conv[:, :1024]
    dt = _softplus(dt_ref[...] + dtb_ref[...])
    la = dt * (-jnp.exp(alog_ref[...]))
    row = lax.broadcasted_iota(jnp.int32, (CHUNK, CHUNK), 0)
    col = lax.broadcasted_iota(jnp.int32, (CHUNK, CHUNK), 1)
    causal = row >= col
    cum = _dot_exact_lhs01(causal.astype(F32), la)
    cum_t = cum.T
    dt_t = dt.T
    cum_last = jnp.broadcast_to(cum_t[:, CHUNK - 1:CHUNK], (LANES, CHUNK))
    w_t = jnp.exp(cum_last - cum_t) * dt_t
    dec_end = jnp.exp(cum_last)

    xs_t = xs.T
    ys = []
    for g in range(SSD_GROUPS):
        bg = conv[:, 1024 + 128 * g:1024 + 128 * (g + 1)]
        cg = conv[:, 1536 + 128 * g:1536 + 128 * (g + 1)]
        cb = _bdot_nt(cg, bg)
        s_g = s_ref[256 * g:256 * (g + 1), :]
        cs = _bdot_nt(cg, s_g)
        xw_parts = []
        dec_parts = []
        for r in range(4):
            h = 4 * g + r
            colb = jnp.broadcast_to(cum[:, h:h + 1], (CHUNK, CHUNK))
            rowb = jnp.broadcast_to(cum_t[h:h + 1, :], (CHUNK, CHUNK))
            dec = jnp.exp(jnp.where(causal, colb - rowb, -jnp.inf))
            m = cb * dec * jnp.broadcast_to(dt_t[h:h + 1, :], (CHUNK, CHUNK))
            xh = xs[:, 64 * h:64 * (h + 1)]
            yh = _bdot(m, xh) + cs[:, 64 * r:64 * (r + 1)] * jnp.exp(colb)[:, :64]
            ys.append(yh)
            xw_parts.append(xs_t[64 * h:64 * (h + 1), :] * jnp.broadcast_to(w_t[h:h + 1, :], (64, CHUNK)))
            dec_parts.append(jnp.broadcast_to(dec_end[h:h + 1, :], (64, SSD_STATE)))
        xw = jnp.concatenate(xw_parts, axis=0)
        s_ref[256 * g:256 * (g + 1), :] = s_g * jnp.concatenate(dec_parts, axis=0) + _bdot(xw, bg)

    y = jnp.concatenate(ys, axis=1) + dsk_ref[...] * xs
    y = y * _silu(z_ref[...].astype(F32))
    y_ref[...] = _rms(y, nrm_ref[...])

    @pl.when(t == nt - 1)
    def _():
        sfin_ref[...] = s_ref[...]
        cfin_ref[...] = xbc[CHUNK - 3:CHUNK, :]


def _ssd_prompt(proj, dt, cw, cb, dtb, alog, dsk, nrm):
    seq = proj.shape[0]
    const = lambda shape: pl.BlockSpec(shape, lambda t: (0,) * len(shape))
    return pl.pallas_call(
        _ssd_prompt_kernel,
        out_shape=(jax.ShapeDtypeStruct((seq, 1024), F32),
                   jax.ShapeDtypeStruct((1024, SSD_STATE), F32),
                   jax.ShapeDtypeStruct((SSD_CONV - 1, SSD_CONV_DIM), F32)),
        grid=(seq // CHUNK,),
        in_specs=[pl.BlockSpec((CHUNK, 2048), lambda t: (t, COL_XBC)),
                  pl.BlockSpec((CHUNK, 1024), lambda t: (t, COL_Z)),
                  pl.BlockSpec((CHUNK, LANES), lambda t: (t, 0)),
                  const((SSD_CONV, 2048)), const((1, 2048)), const((1, 128)), const((1, 128)),
                  const((1, 1024)), const((1, 1024))],
        out_specs=(pl.BlockSpec((CHUNK, 1024), lambda t: (t, 0)),
                   const((1024, SSD_STATE)), const((SSD_CONV - 1, SSD_CONV_DIM))),
        scratch_shapes=[pltpu.VMEM((8 + CHUNK, SSD_CONV_DIM), F32), pltpu.VMEM((1024, SSD_STATE), F32)],
        compiler_params=_cparams(("arbitrary",)),
        name="ssd_prompt",
    )(proj, proj, dt, cw, cb, dtb, alog, dsk, nrm)


def _ssd_dec_pre_kernel(xbc_ref, dt_ref, cst_ref, cw_ref, cb_ref, dtb_ref, alog_ref,
                        cnew_ref, xs_ref, bc_ref, dec_t_ref, xdt_t_ref):
    xbc = xbc_ref[...]
    cw = cw_ref[...]
    acc = (cst_ref[0] * cw[0:1, :] + cst_ref[1] * cw[1:2, :] + cst_ref[2] * cw[2:3, :]
           + xbc * cw[3:4, :] + cb_ref[...])
    conv = _silu(acc)
    cnew_ref[0] = cst_ref[1]
    cnew_ref[1] = cst_ref[2]
    cnew_ref[2] = xbc
    xs = conv[:, :1024]
    xs_ref[...] = xs
    bc_ref[...] = conv[:, 1024:]
    dt = _softplus(dt_ref[...] + dtb_ref[...])
    dec = jnp.exp(dt * (-jnp.exp(alog_ref[...])))
    hrow = lax.broadcasted_iota(jnp.int32, (LANES, 1024), 0)
    hcol = lax.broadcasted_iota(jnp.int32, (LANES, 1024), 1)
    expand = ((hcol >> 6) == hrow).astype(F32)
    dec_t_ref[...] = _dot_exact_rhs01(dec, expand).T
    xdt_t_ref[...] = (xs * _dot_exact_rhs01(dt, expand)).T


def _ssd_dec_pre(proj, dt, cst_t, cw, cb, dtb, alog):
    nb = proj.shape[0]
    const = lambda shape: pl.BlockSpec(shape, lambda t: (0,) * len(shape))
    return pl.pallas_call(
        _ssd_dec_pre_kernel,
        out_shape=(jax.ShapeDtypeStruct((3, nb, 2048), F32), jax.ShapeDtypeStruct((nb, 1024), F32),
                   jax.ShapeDtypeStruct((nb, 1024), F32), jax.ShapeDtypeStruct((1024, nb), F32),
                   jax.ShapeDtypeStruct((1024, nb), F32)),
        grid=(1,),
        in_specs=[pl.BlockSpec((nb, 2048), lambda t: (0, COL_XBC)),
                  pl.BlockSpec((nb, LANES), lambda t: (0, 0)),
                  const((3, nb, 2048)), const((SSD_CONV, 2048)), const((1, 2048)), const((1, 128)), const((1, 128))],
        out_specs=(const((3, nb, 2048)), const((nb, 1024)), const((nb, 1024)), const((1024, nb)), const((1024, nb))),
        compiler_params=_cparams(("arbitrary",)),
        name="ssd_dec_pre",
    )(proj, dt, cst_t, cw, cb, dtb, alog)


DEC_BLOCK = 8


def _ssd_dec_state_kernel(dec_t_ref, xdt_t_ref, bc_ref, z_ref, xs_ref, st_ref, dsk_ref, nrm_ref,
                          y_ref, stn_ref):
    i = pl.program_id(0)
    shift = (LANES - i * DEC_BLOCK) % LANES
    decr = pltpu.roll(dec_t_ref[...], shift, axis=1)
    xr = pltpu.roll(xdt_t_ref[...], shift, axis=1)
    bc = bc_ref[...]
    rowid = lax.broadcasted_iota(jnp.int32, (DEC_BLOCK, SSD_STATE), 0)
    ys = [jnp.zeros((DEC_BLOCK, 256), F32) for _ in range(SSD_GROUPS)]
    for j in range(DEC_BLOCK):
        for g in range(SSD_GROUPS):
            lo, hi = 256 * g, 256 * (g + 1)
            s_old = st_ref[j, lo:hi, :]
            dcol = jnp.broadcast_to(decr[lo:hi, j:j + 1], (256, SSD_STATE))
            xcol = jnp.broadcast_to(xr[lo:hi, j:j + 1], (256, SSD_STATE))
            s_new = s_old * dcol + xcol * bc[j:j + 1, 128 * g:128 * (g + 1)]
            stn_ref[j, lo:hi, :] = s_new
            cm = jnp.where(rowid == j, bc[:, 512 + 128 * g:512 + 128 * (g + 1)], 0.0)
            ys[g] = ys[g] + _bdot_nt(cm, s_new)
    xs = xs_ref[...]
    y = jnp.concatenate(ys, axis=1) + dsk_ref[...] * xs
    y = y * _silu(z_ref[...])
    y_ref[...] = _rms(y, nrm_ref[...])


class _LayerCall:
    def __init__(self, layer, steps):
        self.layer, self.steps, self.first = layer, steps, layer == 0
        self.grid = (DEPTH * steps,) if self.first else (steps,)

    def _block(self, t):
        return jnp.minimum(t, self.steps - 1) if self.first else t

    def rows(self, width, col=0):
        return pl.BlockSpec((DEC_BLOCK, width), lambda t: (self._block(t), col))

    def state_in(self, tail):
        zeros = (0,) * len(tail)
        return pl.BlockSpec((None, DEC_BLOCK) + tail, lambda t: (self.layer, self._block(t)) + zeros)

    def state_out(self, tail):
        zeros = (0,) * len(tail)
        if self.first:
            return pl.BlockSpec((None, DEC_BLOCK) + tail, lambda t: (t // self.steps, t % self.steps) + zeros)
        return pl.BlockSpec((None, DEC_BLOCK) + tail, lambda t: (self.layer, t) + zeros)

    def kernel(self, body, n_in, n_carried, stacked_outs):
        def wrapped(*refs):
            refs = refs[:n_in] + refs[n_in + n_carried:]
            if not self.first:
                body(*refs)
                return
            t = pl.program_id(0)

            @pl.when(t < self.steps)
            def _():
                body(*refs)

            @pl.when(t >= self.steps)
            def _():
                for k in stacked_outs:
                    refs[n_in + k][...] = jnp.zeros_like(refs[n_in + k])
        return wrapped


def _ssd_dec_state(layer, dec_t, xdt_t, bc, proj, xs, st_all, dsk, nrm, carried):
    nb = xs.shape[0]
    lc = _LayerCall(layer, nb // DEC_BLOCK)
    const = lambda shape: pl.BlockSpec(shape, lambda t: (0,) * len(shape))
    tail = (1024, SSD_STATE)
    ins = [dec_t, xdt_t, bc, proj, xs, st_all, dsk, nrm]
    in_specs = [const((1024, nb)), const((1024, nb)), lc.rows(1024), lc.rows(1024, COL_Z), lc.rows(1024),
                lc.state_in(tail), const((1, 1024)), const((1, 1024))]
    n_in = len(ins)
    carried = [] if carried is None else [carried]
    return pl.pallas_call(
        lc.kernel(_ssd_dec_state_kernel, n_in, len(carried), (1,)),
        out_shape=(jax.ShapeDtypeStruct((nb, 1024), F32), jax.ShapeDtypeStruct(st_all.shape, F32)),
        grid=lc.grid,
        in_specs=in_specs + [pl.BlockSpec(memory_space=pl.ANY)] * len(carried),
        out_specs=(lc.rows(1024), lc.state_out(tail)),
        input_output_aliases={n_in + k: 1 + k for k in range(len(carried))},
        compiler_params=_cparams(("arbitrary",)),
        name="ssd_dec_state",
    )(*ins, *carried)


def _swa_prompt_kernel(sink_ref, q_ref, k_ref, v_ref, c_ref, s1_ref, s2_ref,
                       o_ref, wk_ref, wv_ref, kp_ref, vp_ref):
    n = pl.program_id(0)
    nb = pl.num_programs(0)

    @pl.when(n == 0)
    def _():
        kp_ref[...] = jnp.zeros_like(kp_ref)
        vp_ref[...] = jnp.zeros_like(vp_ref)

    c, s1, s2 = c_ref[...], s1_ref[...], s2_ref[...]
    q = _rope_att(q_ref[...].astype(F32), _tile_lanes(c, 8), _tile_lanes(s1, 8), _tile_lanes(s2, 8))
    k = _rope_att(k_ref[...].astype(F32), _tile_lanes(c, 2), _tile_lanes(s1, 2), _tile_lanes(s2, 2))
    v = v_ref[...].astype(F32)
    ghead = lax.broadcasted_iota(jnp.int32, (WINDOW, 256), 1) >> 6
    expand = lambda t: jnp.concatenate([jnp.where(ghead == g, t, 0.0) for g in range(ATT_KV_HEADS)],
                                       axis=0).astype(BF16)
    kbd, vbd = expand(k), expand(v)
    kbd_prev, vbd_prev = expand(kp_ref[...]), expand(vp_ref[...])
    qall = jnp.concatenate([q[:, 256 * r:256 * (r + 1)] for r in range(ATT_REP)], axis=0).astype(BF16)
    scale = ATT_HEAD_DIM ** -0.5
    nt_dims = (((1,), (1,)), ((), ()))
    sp_all = lax.dot_general(qall, kbd_prev, nt_dims, preferred_element_type=F32) * scale
    sc_all = lax.dot_general(qall, kbd, nt_dims, preferred_element_type=F32) * scale
    rows = ATT_REP * WINDOW
    qi = lax.broadcasted_iota(jnp.int32, (rows, WINDOW), 0) & (WINDOW - 1)
    kj = lax.broadcasted_iota(jnp.int32, (rows, WINDOW), 1)
    mask_prev = jnp.logical_and(kj > qi, n > 0)
    mask_cur = kj <= qi
    rep = lax.broadcasted_iota(jnp.int32, (rows, 1), 0) >> 7
    pp, pc = [], []
    for g in range(ATT_KV_HEADS):
        sp = jnp.where(mask_prev, sp_all[:, WINDOW * g:WINDOW * (g + 1)], -jnp.inf)
        sc = jnp.where(mask_cur, sc_all[:, WINDOW * g:WINDOW * (g + 1)], -jnp.inf)
        sink = jnp.where(rep == 0, sink_ref[4 * g],
                         jnp.where(rep == 1, sink_ref[4 * g + 1],
                                   jnp.where(rep == 2, sink_ref[4 * g + 2], sink_ref[4 * g + 3])))
        m = jnp.maximum(jnp.max(jnp.maximum(sp, sc), axis=1, keepdims=True), sink)
        ep = jnp.exp(sp - m)
        ec = jnp.exp(sc - m)
        inv = 1.0 / (jnp.sum(ep + ec, axis=1, keepdims=True) + jnp.exp(sink - m))
        pp.append((ep * inv).astype(BF16))
        pc.append((ec * inv).astype(BF16))
    o = (jnp.dot(jnp.concatenate(pp, axis=1), vbd_prev, preferred_element_type=F32)
         + jnp.dot(jnp.concatenate(pc, axis=1), vbd, preferred_element_type=F32))
    for r in range(ATT_REP):
        o_ref[:, 256 * r:256 * (r + 1)] = o[WINDOW * r:WINDOW * (r + 1), :]
    kp_ref[...] = k
    vp_ref[...] = v

    @pl.when(n == nb - 1)
    def _():
        wk_ref[...] = k
        wv_ref[...] = v


def _swa_prompt(sinks, proj, tabs):
    seq = proj.shape[0]
    c, s1, s2 = tabs
    const = lambda shape: pl.BlockSpec(shape, lambda t: (0,) * len(shape))
    tab = pl.BlockSpec((WINDOW, LANES), lambda t: (t, 0))
    return pl.pallas_call(
        _swa_prompt_kernel,
        out_shape=(jax.ShapeDtypeStruct((seq, 1024), F32),
                   jax.ShapeDtypeStruct((WINDOW, 256), F32), jax.ShapeDtypeStruct((WINDOW, 256), F32)),
        grid=(seq // WINDOW,),
        in_specs=[pl.BlockSpec(memory_space=pltpu.SMEM),
                  pl.BlockSpec((WINDOW, 1024), lambda t: (t, COL_QA)),
                  pl.BlockSpec((WINDOW, 256), lambda t: (t, COL_KA)),
                  pl.BlockSpec((WINDOW, 256), lambda t: (t, COL_VA)),
                  tab, tab, tab],
        out_specs=(pl.BlockSpec((WINDOW, 1024), lambda t: (t, 0)), const((WINDOW, 256)), const((WINDOW, 256))),
        scratch_shapes=[pltpu.VMEM((WINDOW, 256), F32), pltpu.VMEM((WINDOW, 256), F32)],
        compiler_params=_cparams(("arbitrary",)),
        name="swa_prompt",
    )(sinks, proj, proj, proj, c, s1, s2)


def _swa_dec_kernel(q_ref, k_ref, v_ref, kc_ref, vc_ref, c_ref, s1_ref, s2_ref, sink_ref,
                    o_ref, kcn_ref, vcn_ref):
    c, s1, s2 = c_ref[...], s1_ref[...], s2_ref[...]
    q = _rope_att(q_ref[...], _tile_lanes(c, 8), _tile_lanes(s1, 8), _tile_lanes(s2, 8))
    k = _rope_att(k_ref[...], _tile_lanes(c, 2), _tile_lanes(s1, 2), _tile_lanes(s2, 2))
    v = v_ref[...]
    grow = lax.broadcasted_iota(jnp.int32, (8, 256), 0)
    gcol = lax.broadcasted_iota(jnp.int32, (8, 256), 1)
    gmask = ((gcol >> 6) == grow).astype(F32)
    gmask4 = jnp.concatenate([gmask] * ATT_REP, axis=0)
    lane = lax.broadcasted_iota(jnp.int32, (8 * ATT_REP, WINDOW), 1)
    sink = sink_ref[...][:, 0:1]
    scale = ATT_HEAD_DIM ** -0.5
    for j in range(DEC_BLOCK):
        a = jnp.concatenate([jnp.broadcast_to(q[j:j + 1, 256 * r:256 * (r + 1)], (8, 256)) * gmask
                             for r in range(ATT_REP)], axis=0)
        kb = kc_ref[j]
        vb = vc_ref[j]
        knew = k[j:j + 1, :]
        vnew = v[j:j + 1, :]
        s = jnp.where(lane == 0, -jnp.inf, _bdot_nt(a, kb) * scale)
        snew = jnp.sum(a * knew, axis=1, keepdims=True) * scale
        m = jnp.maximum(jnp.maximum(jnp.max(s, axis=1, keepdims=True), snew), sink)
        e = jnp.exp(s - m)
        enew = jnp.exp(snew - m)
        inv = 1.0 / (jnp.sum(e, axis=1, keepdims=True) + enew + jnp.exp(sink - m))
        o = (_bdot(e * inv, vb) + (enew * inv) * vnew) * gmask4
        for r in range(ATT_REP):
            o_ref[j:j + 1, 256 * r:256 * (r + 1)] = jnp.sum(o[8 * r:8 * (r + 1), :], axis=0, keepdims=True)
        kcn_ref[j, 0:WINDOW - 1, :] = kc_ref[j, 1:WINDOW, :]
        kcn_ref[j, WINDOW - 1:WINDOW, :] = knew
        vcn_ref[j, 0:WINDOW - 1, :] = vc_ref[j, 1:WINDOW, :]
        vcn_ref[j, WINDOW - 1:WINDOW, :] = vnew


def _swa_dec(layer, proj, kc_all, vc_all, tabs, sink32, carried):
    nb = proj.shape[0]
    lc = _LayerCall(layer, nb // DEC_BLOCK)
    c, s1, s2 = tabs
    const = lambda shape: pl.BlockSpec(shape, lambda t: (0,) * len(shape))
    tail = (WINDOW, 256)
    ins = [proj, proj, proj, kc_all, vc_all, c, s1, s2, sink32]
    in_specs = [lc.rows(1024, COL_QA), lc.rows(256, COL_KA), lc.rows(256, COL_VA),
                lc.state_in(tail), lc.state_in(tail),
                const((1, LANES)), const((1, LANES)), const((1, LANES)), const((32, LANES))]
    n_in = len(ins)
    carried = [] if carried is None else list(carried)
    return pl.pallas_call(
        lc.kernel(_swa_dec_kernel, n_in, len(carried), (1, 2)),
        out_shape=(jax.ShapeDtypeStruct((nb, 1024), F32),
                   jax.ShapeDtypeStruct(kc_all.shape, F32), jax.ShapeDtypeStruct(vc_all.shape, F32)),
        grid=lc.grid,
        in_specs=in_specs + [pl.BlockSpec(memory_space=pl.ANY)] * len(carried),
        out_specs=(lc.rows(1024), lc.state_out(tail), lc.state_out(tail)),
        input_output_aliases={n_in + k: 1 + k for k in range(len(carried))},
        compiler_params=_cparams(("arbitrary",)),
        name="swa_dec",
    )(*ins, *carried)


def _ret_prompt_kernel(lg_ref, q_ref, k_ref, v_ref, gr_ref, c_ref, s_ref, nrm_ref,
                       o_ref, sfin_ref, st_ref, intra_ref, fs_ref, te_ref):
    t = pl.program_id(0)
    nt = pl.num_programs(0)

    @pl.when(t == 0)
    def _():
        st_ref[...] = jnp.zeros_like(st_ref)
        ri = lax.broadcasted_iota(jnp.int32, (CHUNK, CHUNK), 0).astype(F32)
        ci = lax.broadcasted_iota(jnp.int32, (CHUNK, CHUNK), 1).astype(F32)
        rel = ri - ci
        for h in range(RET_HEADS):
            lg = lg_ref[h]
            intra_ref[h] = jnp.exp(jnp.where(rel >= 0, rel * lg, -jnp.inf))
            fs_ref[h] = jnp.exp((ri + 1.0) * lg)
            te_ref[h] = jnp.exp((CHUNK - 1.0 - ri) * lg)

    c = c_ref[...]
    s = s_ref[...]
    q = q_ref[...].astype(F32)
    k = k_ref[...].astype(F32)
    v = v_ref[...].astype(F32)
    gr = gr_ref[...].astype(F32)
    nrm = nrm_ref[...]
    for h in range(RET_HEADS):
        sl = slice(128 * h, 128 * (h + 1))
        qh = q[:, sl]
        kh = k[:, sl]
        qh = qh * c + pltpu.roll(qh, 64, axis=1) * s
        kh = (kh * c + pltpu.roll(kh, 64, axis=1) * s) * (RET_DK ** -0.5)
        vh = v[:, sl]
        att = _bdot_nt(qh, kh) * intra_ref[h]
        s_old = st_ref[h]
        o = _bdot(att, vh) + _bdot(qh, s_old) * fs_ref[h]
        cd = jnp.exp(jnp.zeros((1, RET_DK), F32) + CHUNK * lg_ref[h])
        st_ref[h] = s_old * cd + _bdot((kh * te_ref[h]).T, vh)
        o = o * lax.rsqrt(jnp.mean(o * o, axis=-1, keepdims=True) + EPS)
        o_ref[:, sl] = o * nrm[:, sl] * _silu(gr[:, sl])

    @pl.when(t == nt - 1)
    def _():
        sfin_ref[...] = st_ref[...]


def _ret_prompt(log_gamma, proj, tabs, nrm):
    seq = proj.shape[0]
    c, s = tabs
    const = lambda shape: pl.BlockSpec(shape, lambda t: (0,) * len(shape))
    col = lambda cidx: pl.BlockSpec((CHUNK, 1024), lambda t: (t, cidx))
    tab = pl.BlockSpec((CHUNK, LANES), lambda t: (t, 0))
    tbl = pltpu.VMEM((RET_HEADS, CHUNK, CHUNK), F32)
    return pl.pallas_call(
        _ret_prompt_kernel,
        out_shape=(jax.ShapeDtypeStruct((seq, 1024), F32), jax.ShapeDtypeStruct((RET_HEADS, RET_DK, 128), F32)),
        grid=(seq // CHUNK,),
        in_specs=[pl.BlockSpec(memory_space=pltpu.SMEM), col(COL_QR), col(COL_KR), col(COL_VR), col(COL_GR),
                  tab, tab, const((1, 1024))],
        out_specs=(pl.BlockSpec((CHUNK, 1024), lambda t: (t, 0)), const((RET_HEADS, RET_DK, 128))),
        scratch_shapes=[tbl, tbl, tbl, tbl],
        compiler_params=_cparams(("arbitrary",)),
        name="ret_prompt",
    )(log_gamma, proj, proj, proj, proj, c, s, nrm)


def _ret_dec_pre_kernel(q_ref, k_ref, c_ref, s_ref, qrot_ref, kt_ref):
    c = c_ref[...]
    s = s_ref[...]
    q = q_ref[...]
    k = k_ref[...]
    ks = []
    for h in range(RET_HEADS):
        sl = slice(128 * h, 128 * (h + 1))
        qh = q[:, sl]
        kh = k[:, sl]
        qrot_ref[:, sl] = qh * c + pltpu.roll(qh, 64, axis=1) * s
        ks.append((kh * c + pltpu.roll(kh, 64, axis=1) * s) * (RET_DK ** -0.5))
    kt_ref[...] = jnp.concatenate(ks, axis=1).T


def _ret_dec_pre(proj, tabs):
    nb = proj.shape[0]
    c, s = tabs
    const = lambda shape: pl.BlockSpec(shape, lambda t: (0,) * len(shape))
    return pl.pallas_call(
        _ret_dec_pre_kernel,
        out_shape=(jax.ShapeDtypeStruct((nb, 1024), F32), jax.ShapeDtypeStruct((1024, nb), F32)),
        grid=(1,),
        in_specs=[pl.BlockSpec((nb, 1024), lambda t: (0, COL_QR)), pl.BlockSpec((nb, 1024), lambda t: (0, COL_KR)),
                  const((1, LANES)), const((1, LANES))],
        out_specs=(const((nb, 1024)), const((1024, nb))),
        compiler_params=_cparams(("arbitrary",)),
        name="ret_dec_pre",
    )(proj, proj, c, s)


def _ret_dec_state_kernel(gam_ref, kt_ref, q_ref, v_ref, gr_ref, st_ref, nrm_ref, o_ref, stn_ref):
    i = pl.program_id(0)
    shift = (LANES - i * DEC_BLOCK) % LANES
    kr = pltpu.roll(kt_ref[...], shift, axis=1)
    q = q_ref[...]
    v = v_ref[...]
    rowid = lax.broadcasted_iota(jnp.int32, (DEC_BLOCK, RET_DK), 0)
    os_ = [jnp.zeros((DEC_BLOCK, 128), F32) for _ in range(RET_HEADS)]
    for j in range(DEC_BLOCK):
        for h in range(RET_HEADS):
            lo, hi = 128 * h, 128 * (h + 1)
            kcol = jnp.broadcast_to(kr[lo:hi, j:j + 1], (RET_DK, 128))
            s_new = st_ref[j, lo:hi, :] * gam_ref[h] + kcol * v[j:j + 1, lo:hi]
            stn_ref[j, lo:hi, :] = s_new
            qm = jnp.where(rowid == j, q[:, lo:hi], 0.0)
            os_[h] = os_[h] + _bdot(qm, s_new)
    gr = gr_ref[...]
    nrm = nrm_ref[...]
    for h in range(RET_HEADS):
        sl = slice(128 * h, 128 * (h + 1))
        o = os_[h]
        o = o * lax.rsqrt(jnp.mean(o * o, axis=-1, keepdims=True) + EPS)
        o_ref[:, sl] = o * nrm[:, sl] * _silu(gr[:, sl])


def _ret_dec_state(layer, gam, kt, qrot, proj, st_all, nrm, carried):
    nb = qrot.shape[0]
    lc = _LayerCall(layer, nb // DEC_BLOCK)
    const = lambda shape: pl.BlockSpec(shape, lambda t: (0,) * len(shape))
    tail = (1024, 128)
    ins = [gam, kt, qrot, proj, proj, st_all, nrm]
    in_specs = [pl.BlockSpec(memory_space=pltpu.SMEM), const((1024, nb)), lc.rows(1024),
                lc.rows(1024, COL_VR), lc.rows(1024, COL_GR), lc.state_in(tail), const((1, 1024))]
    n_in = len(ins)
    carried = [] if carried is None else [carried]
    return pl.pallas_call(
        lc.kernel(_ret_dec_state_kernel, n_in, len(carried), (1,)),
        out_shape=(jax.ShapeDtypeStruct((nb, 1024), F32), jax.ShapeDtypeStruct(st_all.shape, F32)),
        grid=lc.grid,
        in_specs=in_specs + [pl.BlockSpec(memory_space=pl.ANY)] * len(carried),
        out_specs=(lc.rows(1024), lc.state_out(tail)),
        input_output_aliases={n_in + k: 1 + k for k in range(len(carried))},
        compiler_params=_cparams(("arbitrary",)),
        name="ret_dec_state",
    )(*ins, *carried)


def _merge_kernel(x_ref, a_ref, b_ref, c_ref, g1_ref, g2_ref, g3_ref, w1_ref, w2_ref, w3_ref, wo_ref, o_ref):
    gate = lambda ref: _sigmoid(ref[...].astype(F32))
    m = (gate(g1_ref) * jnp.dot(a_ref[...].astype(BF16), w1_ref[...], preferred_element_type=F32)
         + gate(g2_ref) * jnp.dot(b_ref[...].astype(BF16), w2_ref[...], preferred_element_type=F32)
         + gate(g3_ref) * jnp.dot(c_ref[...].astype(BF16), w3_ref[...], preferred_element_type=F32))
    o_ref[...] = x_ref[...] + jnp.dot(m.astype(BF16), wo_ref[...], preferred_element_type=F32)


def _merge(x, a, b, c, proj, w1, w2, w3, wo):
    rows = x.shape[0]
    tm = min(rows, 256)
    rowb = pl.BlockSpec((tm, 1024), lambda i: (i, 0))
    gate = lambda k: pl.BlockSpec((tm, 1024), lambda i: (i, COL_GATE + k))
    wsp = pl.BlockSpec((1024, 1024), lambda i: (0, 0))
    return pl.pallas_call(
        _merge_kernel,
        out_shape=jax.ShapeDtypeStruct((rows, 1024), F32),
        grid=(rows // tm,),
        in_specs=[rowb, rowb, rowb, rowb, gate(0), gate(1), gate(2), wsp, wsp, wsp, wsp],
        out_specs=rowb,
        compiler_params=_cparams(("parallel",)),
        name="merge",
    )(x, a, b, c, proj, proj, proj, w1, w2, w3, wo)


FF_TILE = 1408


def _ffn_kernel(x_ref, g_ref, wg_ref, wu_ref, wd_ref, o_ref, h_ref, acc_ref):
    j = pl.program_id(1)

    @pl.when(j == 0)
    def _():
        h_ref[...] = _rms(x_ref[...], g_ref[...]).astype(BF16)
        acc_ref[...] = jnp.zeros_like(acc_ref)

    h = h_ref[...]
    a = jnp.dot(h, wg_ref[...], preferred_element_type=F32)
    u = jnp.dot(h, wu_ref[...], preferred_element_type=F32)
    acc_ref[...] += jnp.dot((_silu(a) * u).astype(BF16), wd_ref[...], preferred_element_type=F32)

    @pl.when(j == pl.num_programs(1) - 1)
    def _():
        o_ref[...] = x_ref[...] + acc_ref[...]


def _ffn(x, g, wg, wu, wd):
    rows = x.shape[0]
    tm = min(rows, 512)
    return pl.pallas_call(
        _ffn_kernel,
        out_shape=jax.ShapeDtypeStruct((rows, 1024), F32),
        grid=(rows // tm, D_FF // FF_TILE),
        in_specs=[pl.BlockSpec((tm, 1024), lambda i, j: (i, 0)), pl.BlockSpec((1, 1024), lambda i, j: (0, 0)),
                  pl.BlockSpec((1024, FF_TILE), lambda i, j: (0, j)), pl.BlockSpec((1024, FF_TILE), lambda i, j: (0, j)),
                  pl.BlockSpec((FF_TILE, 1024), lambda i, j: (j, 0))],
        out_specs=pl.BlockSpec((tm, 1024), lambda i, j: (i, 0)),
        scratch_shapes=[pltpu.VMEM((tm, 1024), BF16), pltpu.VMEM((tm, 1024), F32)],
        compiler_params=_cparams(("parallel", "arbitrary")),
        name="ffn",
    )(x, g, wg, wu, wd)


MOE_FF_TILE = 256


def _top2(h, rw_hi, rw_lo, rb, lane):
    h_hi = h.astype(BF16)
    h_lo = (h - h_hi.astype(F32)).astype(BF16)
    d = lambda a, b: jnp.dot(a, b, preferred_element_type=F32)
    logits = d(h_hi, rw_hi) + d(h_hi, rw_lo) + d(h_lo, rw_hi) + rb
    logits = jnp.where(lane < N_EXPERTS, logits, -jnp.inf)
    m1 = jnp.max(logits, axis=1, keepdims=True)
    i1 = jnp.min(jnp.where(logits == m1, lane, float(LANES)), axis=1, keepdims=True)
    rest = jnp.where(lane == i1, -jnp.inf, logits)
    m2 = jnp.max(rest, axis=1, keepdims=True)
    i2 = jnp.min(jnp.where(rest == m2, lane, float(LANES)), axis=1, keepdims=True)
    e2 = jnp.exp(m2 - m1)
    p1 = 1.0 / (1.0 + e2)
    return i1, i2, p1, e2 * p1


def _moe_kernel(x_ref, g_ref, rw_ref, rb_ref, wg_ref, wu_ref, wd_ref, o_ref, h_ref, acc_ref, comb_ref):
    e = pl.program_id(1)
    j = pl.program_id(2)
    tm = x_ref.shape[0]
    lane = lax.broadcasted_iota(jnp.int32, (tm, LANES), 1).astype(F32)

    @pl.when(jnp.logical_and(e == 0, j == 0))
    def _():
        h = _rms(x_ref[...], g_ref[...])
        h_ref[...] = h.astype(BF16)
        i1, i2, p1, p2 = _top2(h, rw_ref[0], rw_ref[1], rb_ref[...], lane)
        comb_ref[...] = jnp.where(lane == i1, p1, 0.0) + jnp.where(lane == i2, p2, 0.0)
        acc_ref[...] = jnp.zeros_like(acc_ref)

    ce = jnp.sum(jnp.where(lane == e.astype(F32), comb_ref[...], 0.0), axis=1, keepdims=True)
    h = h_ref[...]
    a = jnp.dot(h, wg_ref[0], preferred_element_type=F32)
    u = jnp.dot(h, wu_ref[0], preferred_element_type=F32)
    acc_ref[...] += ce * jnp.dot((_silu(a) * u).astype(BF16), wd_ref[0], preferred_element_type=F32)

    @pl.when(jnp.logical_and(e == pl.num_programs(1) - 1, j == pl.num_programs(2) - 1))
    def _():
        o_ref[...] = x_ref[...] + acc_ref[...]


def _moe(x, g, rw, rb, wg, wu, wd):
    rows = x.shape[0]
    tm = min(rows, 1024)
    tf = MOE_FF_TILE
    return pl.pallas_call(
        _moe_kernel,
        out_shape=jax.ShapeDtypeStruct((rows, 1024), F32),
        grid=(rows // tm, N_EXPERTS, D_FF // tf),
        in_specs=[pl.BlockSpec((tm, 1024), lambda i, e, j: (i, 0)), pl.BlockSpec((1, 1024), lambda i, e, j: (0, 0)),
                  pl.BlockSpec((2, 1024, LANES), lambda i, e, j: (0, 0, 0)),
                  pl.BlockSpec((1, LANES), lambda i, e, j: (0, 0)),
                  pl.BlockSpec((1, 1024, tf), lambda i, e, j: (e, 0, j)),
                  pl.BlockSpec((1, 1024, tf), lambda i, e, j: (e, 0, j)),
                  pl.BlockSpec((1, tf, 1024), lambda i, e, j: (e, j, 0))],
        out_specs=pl.BlockSpec((tm, 1024), lambda i, e, j: (i, 0)),
        scratch_shapes=[pltpu.VMEM((tm, 1024), BF16), pltpu.VMEM((tm, 1024), F32), pltpu.VMEM((tm, LANES), F32)],
        compiler_params=_cparams(("parallel", "arbitrary", "arbitrary")),
        name="moe",
    )(x, g, rw, rb, wg, wu, wd)


MOE_ROWS = 512
MOE_GROUP_FF = 1408
GATHER_UNROLL = 8


def _router_kernel(x_ref, g_ref, rw_ref, rb_ref, o_ref):
    tm = x_ref.shape[0]
    lane = lax.broadcasted_iota(jnp.int32, (tm, LANES), 1).astype(F32)
    i1, i2, p1, p2 = _top2(_rms(x_ref[...], g_ref[...]), rw_ref[0], rw_ref[1], rb_ref[...], lane)
    o_ref[...] = jnp.where(lane == 0.0, i1, jnp.where(lane == 1.0, i2, jnp.where(lane == 2.0, p1,
                           jnp.where(lane == 3.0, p2, 0.0))))


def _router(x, g, rw, rb):
    rows = x.shape[0]
    tm = min(rows, 1024)
    return pl.pallas_call(
        _router_kernel,
        out_shape=jax.ShapeDtypeStruct((rows, LANES), F32),
        grid=(rows // tm,),
        in_specs=[pl.BlockSpec((tm, 1024), lambda i: (i, 0)), pl.BlockSpec((1, 1024), lambda i: (0, 0)),
                  pl.BlockSpec((2, 1024, LANES), lambda i: (0, 0, 0)), pl.BlockSpec((1, LANES), lambda i: (0, 0))],
        out_specs=pl.BlockSpec((tm, LANES), lambda i: (i, 0)),
        compiler_params=_cparams(("parallel",)),
        name="router",
    )(x, g, rw, rb)


def _route_plan(route, tm):
    n = route.shape[0]
    n_tiles = (2 * n) // tm + N_EXPERTS
    e_flat = route[:, :2].astype(jnp.int32).reshape(-1)
    onehot = (e_flat[:, None] == jnp.arange(N_EXPERTS, dtype=jnp.int32)[None, :]).astype(jnp.int32)
    csum = jnp.cumsum(onehot, axis=0)
    counts = csum[-1]
    tiles_e = (counts + tm - 1) // tm
    tile_end = jnp.cumsum(tiles_e)
    row_start = (tile_end - tiles_e) * tm
    pos = jnp.sum((csum - onehot + row_start[None, :]) * onehot, axis=1).astype(jnp.int32)
    tile_expert = jnp.minimum(jnp.sum(jnp.arange(n_tiles, dtype=jnp.int32)[:, None] >= tile_end[None, :], axis=1),
                              N_EXPERTS - 1).astype(jnp.int32)
    n_used = tile_end[-1:].astype(jnp.int32)
    src = jnp.zeros((n_tiles * tm,), jnp.int32).at[pos].set(jnp.arange(2 * n, dtype=jnp.int32) // 2)
    return pos, src.reshape(n_tiles, 1, tm), tile_expert, n_used


def _moe_group_kernel(te_ref, nu_ref, src_ref, srcn_ref, x_hbm, g_ref, wg_ref, wu_ref, wd_ref,
                      y_ref, buf, sem, h_ref, acc_ref):
    i = pl.program_id(0)
    j = pl.program_id(1)
    tm = buf.shape[1]
    slot = i % 2
    active = i < nu_ref[0]

    def row_copy(idx_ref, s, r):
        return pltpu.make_async_copy(x_hbm.at[pl.ds(idx_ref[0, 0, r], 1), :], buf.at[s, pl.ds(r, 1), :], sem.at[s])

    def gather(idx_ref, s):
        def body(r, c):
            row_copy(idx_ref, s, r).start()
            return c
        lax.fori_loop(0, tm, body, 0, unroll=GATHER_UNROLL)

    @pl.when(jnp.logical_and(active, j == 0))
    def _():
        @pl.when(i == 0)
        def _():
            gather(src_ref, 0)

        pltpu.make_async_copy(x_hbm.at[pl.ds(0, tm), :], buf.at[slot], sem.at[slot]).wait()

        @pl.when(i + 1 < nu_ref[0])
        def _():
            gather(srcn_ref, 1 - slot)

        h_ref[...] = _rms(buf[slot], g_ref[...]).astype(BF16)
        acc_ref[...] = jnp.zeros_like(acc_ref)

    @pl.when(active)
    def _():
        h = h_ref[...]
        a = jnp.dot(h, wg_ref[0], preferred_element_type=F32)
        u = jnp.dot(h, wu_ref[0], preferred_element_type=F32)
        acc_ref[...] += jnp.dot((_silu(a) * u).astype(BF16), wd_ref[0], preferred_element_type=F32)

    @pl.when(j == pl.num_programs(1) - 1)
    def _():
        @pl.when(active)
        def _():
            y_ref[...] = acc_ref[...]

        @pl.when(jnp.logical_not(active))
        def _():
            y_ref[...] = jnp.zeros_like(y_ref)


def _moe_group(tile_expert, n_used, src, x, g, wg, wu, wd):
    n_tiles, _, tm = src.shape
    tf = MOE_GROUP_FF
    grid_spec = pltpu.PrefetchScalarGridSpec(
        num_scalar_prefetch=2,
        grid=(n_tiles, D_FF // tf),
        in_specs=[pl.BlockSpec((1, 1, tm), lambda i, j, te, nu: (i, 0, 0), memory_space=pltpu.SMEM),
                  pl.BlockSpec((1, 1, tm), lambda i, j, te, nu: (jnp.minimum(i + 1, n_tiles - 1), 0, 0),
                               memory_space=pltpu.SMEM),
                  pl.BlockSpec(memory_space=pl.ANY),
                  pl.BlockSpec((1, 1024), lambda i, j, te, nu: (0, 0)),
                  pl.BlockSpec((1, 1024, tf), lambda i, j, te, nu: (te[i], 0, j)),
                  pl.BlockSpec((1, 1024, tf), lambda i, j, te, nu: (te[i], 0, j)),
                  pl.BlockSpec((1, tf, 1024), lambda i, j, te, nu: (te[i], j, 0))],
        out_specs=pl.BlockSpec((tm, 1024), lambda i, j, te, nu: (i, 0)),
        scratch_shapes=[pltpu.VMEM((2, tm, 1024), F32), pltpu.SemaphoreType.DMA((2,)),
                        pltpu.VMEM((tm, 1024), BF16), pltpu.VMEM((tm, 1024), F32)])
    return pl.pallas_call(
        _moe_group_kernel,
        out_shape=jax.ShapeDtypeStruct((n_tiles * tm, 1024), F32),
        grid_spec=grid_spec,
        compiler_params=_cparams(("arbitrary", "arbitrary")),
        name="moe_group",
    )(tile_expert, n_used, src, src, x, g, wg, wu, wd)


def _moe_combine_kernel(pos_ref, posn_ref, x_ref, r_ref, y_hbm, o_ref, bufa, bufb, sem):
    i = pl.program_id(0)
    tm = x_ref.shape[0]
    slot = i % 2

    def gather(idx_ref, s):
        def body(t, c):
            pltpu.make_async_copy(y_hbm.at[pl.ds(idx_ref[0, 0, 2 * t], 1), :], bufa.at[s, pl.ds(t, 1), :],
                                  sem.at[0, s]).start()
            pltpu.make_async_copy(y_hbm.at[pl.ds(idx_ref[0, 0, 2 * t + 1], 1), :], bufb.at[s, pl.ds(t, 1), :],
                                  sem.at[1, s]).start()
            return c
        lax.fori_loop(0, tm, body, 0, unroll=GATHER_UNROLL)

    @pl.when(i == 0)
    def _():
        gather(pos_ref, 0)

    pltpu.make_async_copy(y_hbm.at[pl.ds(0, tm), :], bufa.at[slot], sem.at[0, slot]).wait()
    pltpu.make_async_copy(y_hbm.at[pl.ds(0, tm), :], bufb.at[slot], sem.at[1, slot]).wait()

    @pl.when(i + 1 < pl.num_programs(0))
    def _():
        gather(posn_ref, 1 - slot)

    r = r_ref[...]
    o_ref[...] = x_ref[...] + r[:, 2:3] * bufa[slot] + r[:, 3:4] * bufb[slot]


def _moe_combine(pos, x, route, y):
    rows = x.shape[0]
    tm = MOE_ROWS
    n = rows // tm
    pos3 = pos.reshape(n, 1, 2 * tm)
    return pl.pallas_call(
        _moe_combine_kernel,
        out_shape=jax.ShapeDtypeStruct((rows, 1024), F32),
        grid=(n,),
        in_specs=[pl.BlockSpec((1, 1, 2 * tm), lambda i: (i, 0, 0), memory_space=pltpu.SMEM),
                  pl.BlockSpec((1, 1, 2 * tm), lambda i: (jnp.minimum(i + 1, n - 1), 0, 0), memory_space=pltpu.SMEM),
                  pl.BlockSpec((tm, 1024), lambda i: (i, 0)), pl.BlockSpec((tm, LANES), lambda i: (i, 0)),
                  pl.BlockSpec(memory_space=pl.ANY)],
        out_specs=pl.BlockSpec((tm, 1024), lambda i: (i, 0)),
        scratch_shapes=[pltpu.VMEM((2, tm, 1024), F32), pltpu.VMEM((2, tm, 1024), F32),
                        pltpu.SemaphoreType.DMA((2, 2))],
        compiler_params=_cparams(("arbitrary",)),
        name="moe_combine",
    )(pos3, pos3, x, route, y)


def _moe_routed(x, g, rw, rb, wg, wu, wd):
    route = _router(x, g, rw, rb)
    pos, src, tile_expert, n_used = _route_plan(route, MOE_ROWS)
    y = _moe_group(tile_expert, n_used, src, x, g, wg, wu, wd)
    return _moe_combine(pos, x, route, y)


def _ple_kernel(x_ref, p_ref, g_ref, wp_ref, wgt_ref, gf_ref, o_ref, *, final):
    x = x_ref[...]
    emb = jnp.dot(p_ref[...].astype(BF16), wp_ref[...], preferred_element_type=F32)
    gate = _sigmoid(jnp.dot(_rms(x, g_ref[...]).astype(BF16), wgt_ref[...], preferred_element_type=F32))
    y = x + emb * gate
    if final:
        y = _rms(y, gf_ref[...])
    o_ref[...] = y


def _ple(x, p, g, wp, wgt, gf, final):
    rows = x.shape[0]
    tm = min(rows, 512)
    vec = pl.BlockSpec((1, 1024), lambda i: (0, 0))
    return pl.pallas_call(
        functools.partial(_ple_kernel, final=final),
        out_shape=jax.ShapeDtypeStruct((rows, 1024), F32),
        grid=(rows // tm,),
        in_specs=[pl.BlockSpec((tm, 1024), lambda i: (i, 0)), pl.BlockSpec((tm, PLE_DIM), lambda i: (i, 0)), vec,
                  pl.BlockSpec((PLE_DIM, 1024), lambda i: (0, 0)), pl.BlockSpec((1024, 1024), lambda i: (0, 0)), vec],
        out_specs=pl.BlockSpec((tm, 1024), lambda i: (i, 0)),
        compiler_params=_cparams(("parallel",)),
        name="ple",
    )(x, p, g, wp, wgt, gf)


def _rmajor_cols(w):
    k = w.shape[0]
    return w.reshape(k, ATT_KV_HEADS, ATT_REP, ATT_HEAD_DIM).transpose(0, 2, 1, 3).reshape(k, ATT_Q_HEADS * ATT_HEAD_DIM)


def _prep_w_in(w):
    offs = [0]
    for wd in IN_WIDTHS:
        offs.append(offs[-1] + wd)
    z, xbc, dt, qa, ka, va, qr, kr, vr, gr, gates = [w[:, offs[i]:offs[i + 1]] for i in range(len(IN_WIDTHS))]
    used = 2048 + 1024 * 6 + 3072 + 256 + 256 + 16
    pad = jnp.zeros((w.shape[0], N_PROJ - used), w.dtype)
    return jnp.concatenate([xbc, z, _rmajor_cols(qa), qr, kr, vr, gr, gates, ka, va, dt, pad], axis=1).astype(BF16)


def _att_tables(pos):
    half = ROPE_DIM // 2
    inv = jnp.exp(-math.log(ROPE_THETA) * jnp.arange(half, dtype=F32) * (2.0 / ROPE_DIM))
    ang = pos.astype(F32)[:, None] * inv[None, :]
    cos, sin = jnp.cos(ang), jnp.sin(ang)
    n = pos.shape[0]
    one = jnp.ones((n, ATT_HEAD_DIM - ROPE_DIM), F32)
    zero8 = jnp.zeros((n, half), F32)
    zero = jnp.zeros((n, ATT_HEAD_DIM - ROPE_DIM), F32)
    c = jnp.concatenate([cos, cos, one], axis=1)
    s1 = jnp.concatenate([zero8, sin, zero], axis=1)
    s2 = jnp.concatenate([-sin, zero8, zero], axis=1)
    return tuple(jnp.concatenate([t, t], axis=1) for t in (c, s1, s2))


def _ret_tables(pos):
    half = RET_DK // 2
    inv = jnp.exp(-math.log(RET_THETA) * jnp.arange(half, dtype=F32) * (2.0 / RET_DK))
    ang = pos.astype(F32)[:, None] * inv[None, :]
    cos, sin = jnp.cos(ang), jnp.sin(ang)
    return jnp.concatenate([cos, cos], axis=1), jnp.concatenate([-sin, sin], axis=1)


def _pad_lanes(v, fill=0.0):
    return jnp.concatenate([v.astype(F32), jnp.full((LANES - v.shape[0],), fill, F32)])[None, :]


def kernel(x_prompt, x_sample, state_ssm, state_conv, cache_win_k, cache_win_v, state_ret, p_prompt, p_sample,
           w_in, conv_w, conv_b, dt_bias, a_log, d_skip, ssd_norm, attn_sinks, ret_norm, w_o_ssd, w_o_att, w_o_ret,
           w_out, norm_mix, norm_ffn, norm_ple, ffn_w_gate, ffn_w_up, ffn_w_down, router_w, router_b, moe_w_gate,
           moe_w_up, moe_w_down, w_ple, w_ple_gate, norm_final):
    seq = x_prompt.shape[1]
    nb = x_sample.shape[0]
    xp = x_prompt.reshape(seq, D_MODEL)
    xs = x_sample.reshape(nb, D_MODEL)
    pos_p = jnp.arange(seq)
    pos_s = PAST_LEN + jnp.arange(1)
    att_tab_p, att_tab_s = _att_tables(pos_p), _att_tables(pos_s)
    ret_tab_p, ret_tab_s = _ret_tables(pos_p), _ret_tables(pos_s)
    log_gamma = jnp.log1p(-jnp.exp2(-5.0 - jnp.arange(RET_HEADS, dtype=F32)))
    gamma = jnp.exp(log_gamma)
    row = lambda v: v.astype(F32)[None, :]

    ssm_all = state_ssm.reshape(DEPTH, nb, SSD_HEADS * SSD_HEAD_DIM, SSD_STATE)
    ret_all = state_ret.reshape(DEPTH, nb, RET_HEADS * RET_DK, 128)
    kc_all = cache_win_k.reshape(DEPTH, nb, WINDOW, ATT_KV_HEADS * ATT_HEAD_DIM)
    vc_all = cache_win_v.reshape(DEPTH, nb, WINDOW, ATT_KV_HEADS * ATT_HEAD_DIM)
    ssm_s = ret_s = kv_s = None
    conv_s = []

    new_p = [[], [], [], [], []]
    for i in range(DEPTH):
        w_in_i = _prep_w_in(w_in[i])
        cw, cb = conv_w[i], row(conv_b[i])
        dtb, alog = _pad_lanes(dt_bias[i]), _pad_lanes(a_log[i])
        dsk = row(jnp.repeat(d_skip[i], SSD_HEAD_DIM))
        nrm_ssd, nrm_ret = row(ssd_norm[i]), row(ret_norm[i])
        sinks = attn_sinks[i].astype(F32)
        sink32 = jnp.zeros((ATT_REP, 8), F32).at[:, :ATT_KV_HEADS].set(sinks.reshape(ATT_KV_HEADS, ATT_REP).T)
        sink32 = jnp.broadcast_to(sink32.reshape(32, 1), (32, LANES))
        w1 = w_o_ssd[i].astype(BF16)
        w2 = w_o_att[i].reshape(ATT_KV_HEADS, ATT_REP, ATT_HEAD_DIM, D_MODEL).transpose(1, 0, 2, 3) \
            .reshape(ATT_Q_HEADS * ATT_HEAD_DIM, D_MODEL).astype(BF16)
        w3 = w_o_ret[i].astype(BF16)
        wo = w_out[i].astype(BF16)
        g_mix, g_ffn, g_ple = row(norm_mix[i]), row(norm_ffn[i]), row(norm_ple[i])
        wp, wpg = w_ple[i].astype(BF16), w_ple_gate[i].astype(BF16)
        gf = row(norm_final)
        j = i // 2
        if i % 2 == 0:
            ffw = (ffn_w_gate[j].astype(BF16), ffn_w_up[j].astype(BF16), ffn_w_down[j].astype(BF16))
        else:
            rw = jnp.concatenate([router_w[j], jnp.zeros((D_MODEL, LANES - N_EXPERTS), F32)], axis=1)
            rw_hi = rw.astype(BF16)
            rw = jnp.stack([rw_hi, (rw - rw_hi.astype(F32)).astype(BF16)])
            ffw = (rw, _pad_lanes(router_b[j]), moe_w_gate[j].astype(BF16), moe_w_up[j].astype(BF16),
                   moe_w_down[j].astype(BF16))
        final = i == DEPTH - 1

        proj, dt = _inproj(xp, g_mix, w_in_i, BF16)
        y_ssd, ssm_fin, conv_fin = _ssd_prompt(proj, dt, cw, cb, dtb, alog, dsk, nrm_ssd)
        o_att, wk, wv = _swa_prompt(sinks, proj, att_tab_p)
        o_ret, ret_fin = _ret_prompt(log_gamma, proj, ret_tab_p, nrm_ret)
        xp = _merge(xp, y_ssd, o_att, o_ret, proj, w1, w2, w3, wo)
        xp = _ffn(xp, g_ffn, *ffw) if i % 2 == 0 else _moe_routed(xp, g_ffn, *ffw)
        xp = _ple(xp, p_prompt[i].reshape(seq, PLE_DIM), g_ple, wp, wpg, gf, final)
        new_p[0].append(ssm_fin.reshape(1, SSD_HEADS, SSD_HEAD_DIM, SSD_STATE))
        new_p[1].append(conv_fin[None])
        new_p[2].append(wk.reshape(1, WINDOW, ATT_KV_HEADS, ATT_HEAD_DIM))
        new_p[3].append(wv.reshape(1, WINDOW, ATT_KV_HEADS, ATT_HEAD_DIM))
        new_p[4].append(ret_fin[None])

        proj, dt = _inproj(xs, g_mix, w_in_i, F32)
        cst_t = jnp.transpose(state_conv[i], (1, 0, 2))
        cnew_t, xs_conv, bc, dec_t, xdt_t = _ssd_dec_pre(proj, dt, cst_t, cw, cb, dtb, alog)
        y_ssd, ssm_s = _ssd_dec_state(i, dec_t, xdt_t, bc, proj, xs_conv, ssm_all, dsk, nrm_ssd, ssm_s)
        o_att, *kv_s = _swa_dec(i, proj, kc_all, vc_all, att_tab_s, sink32, kv_s)
        qrot, kt = _ret_dec_pre(proj, ret_tab_s)
        o_ret, ret_s = _ret_dec_state(i, gamma, kt, qrot, proj, ret_all, nrm_ret, ret_s)
        xs = _merge(xs, y_ssd, o_att, o_ret, proj, w1, w2, w3, wo)
        xs = _ffn(xs, g_ffn, *ffw) if i % 2 == 0 else _moe(xs, g_ffn, *ffw)
        xs = _ple(xs, p_sample[i].reshape(nb, PLE_DIM), g_ple, wp, wpg, gf, final)
        conv_s.append(jnp.transpose(cnew_t, (1, 0, 2)))

    y_prompt = xp.reshape(1, seq, D_MODEL)
    y_sample = xs.reshape(nb, 1, D_MODEL)
    outs_p = [jnp.stack(l) for l in new_p]
    outs_s = [ssm_s.reshape(state_ssm.shape), jnp.stack(conv_s), kv_s[0].reshape(cache_win_k.shape),
              kv_s[1].reshape(cache_win_v.shape), ret_s.reshape(state_ret.shape)]
    return (y_prompt, y_sample, *outs_p, *outs_s)
```

```python
import functools
import math

import jax
import jax.numpy as jnp
from jax import lax
from jax.experimental import pallas as pl
from jax.experimental.pallas import tpu as pltpu

F32 = jnp.float32
BF16 = jnp.bfloat16

D_MODEL = 1024
DEPTH = 2
PAST_LEN = 16384
SSD_HEADS = 16
SSD_HEAD_DIM = 64
SSD_GROUPS = 4
SSD_STATE = 128
SSD_CONV = 4
SSD_CONV_DIM = 2048
ATT_HEAD_DIM = 64
ATT_Q_HEADS = 16
ATT_KV_HEADS = 4
ATT_REP = ATT_Q_HEADS // ATT_KV_HEADS
WINDOW = 128
ROPE_THETA = 500000.0
ROPE_DIM = 16
RET_HEADS = 8
RET_DK = 128
RET_THETA = 10000.0
CHUNK = 128
D_FF = 2816
N_EXPERTS = 8
PLE_DIM = 256
EPS = 1e-6

IN_WIDTHS = (1024, 2048, 16, 1024, 256, 256, 1024, 1024, 1024, 1024, 3072)
N_PROJ = 12288
COL_XBC = 0
COL_Z = 2
COL_QA = 3
COL_QR = 4
COL_KR = 5
COL_VR = 6
COL_GR = 7
COL_GATE = 8
COL_KA = 44
COL_VA = 45
COL_DT = 92

LANES = 128
VMEM_LIMIT = 48 * 1024 * 1024
VMEM_LIMIT_FFN = 56 * 1024 * 1024
MIX_ROWS = 4 * CHUNK


def _cparams(sem, vmem=VMEM_LIMIT):
    return pltpu.CompilerParams(dimension_semantics=sem, vmem_limit_bytes=vmem)


def _bdot(a, b):
    return jnp.dot(a.astype(BF16), b.astype(BF16), preferred_element_type=F32)


def _bdot_nt(a, b):
    return lax.dot_general(a.astype(BF16), b.astype(BF16), (((1,), (1,)), ((), ())),
                           preferred_element_type=F32)


def _split3(x):
    x0 = x.astype(BF16)
    r1 = x - x0.astype(F32)
    x1 = r1.astype(BF16)
    x2 = (r1 - x1.astype(F32)).astype(BF16)
    return x0, x1, x2


def _dot_exact_lhs01(m01, x):
    m = m01.astype(BF16)
    x0, x1, x2 = _split3(x)
    d = lambda b: jnp.dot(m, b, preferred_element_type=F32)
    return d(x0) + d(x1) + d(x2)


def _dot_exact_rhs01(x, m01):
    m = m01.astype(BF16)
    x0, x1, x2 = _split3(x)
    d = lambda a: jnp.dot(a, m, preferred_element_type=F32)
    return d(x0) + d(x1) + d(x2)


def _rms(x, g):
    return x * lax.rsqrt(jnp.mean(x * x, axis=-1, keepdims=True) + EPS) * g


def _sigmoid(x):
    return 1.0 / (1.0 + jnp.exp(-x))


def _silu(x):
    return x * _sigmoid(x)


def _softplus(x):
    return jnp.maximum(x, 0.0) + jnp.log1p(jnp.exp(-jnp.abs(x)))


def _rope_att(x, c, s1, s2):
    w = x.shape[1]
    return x * c + pltpu.roll(x, 8, axis=1) * s1 + pltpu.roll(x, w - 8, axis=1) * s2


def _tile_lanes(t, n):
    return jnp.concatenate([t] * n, axis=1) if n > 1 else t


INPROJ_TN = 1024
DT_TILE = (COL_DT * LANES) // INPROJ_TN
DT_OFF = COL_DT * LANES - DT_TILE * INPROJ_TN


def _inproj_kernel(x_ref, g_ref, w_ref, o_ref, dt_ref, h_ref):
    j = pl.program_id(1)

    @pl.when(j == 0)
    def _():
        h_ref[...] = _rms(x_ref[...], g_ref[...]).astype(BF16)

    acc = jnp.dot(h_ref[...], w_ref[...], preferred_element_type=F32)
    o_ref[...] = acc.astype(o_ref.dtype)

    @pl.when(j == DT_TILE)
    def _():
        dt_ref[...] = acc[:, DT_OFF:DT_OFF + LANES]


def _inproj(x, g, w, out_dtype):
    rows = x.shape[0]
    tm = min(rows, 2048)
    tn = INPROJ_TN
    return pl.pallas_call(
        _inproj_kernel,
        out_shape=(jax.ShapeDtypeStruct((rows, N_PROJ), out_dtype), jax.ShapeDtypeStruct((rows, LANES), F32)),
        grid=(rows // tm, N_PROJ // tn),
        in_specs=[pl.BlockSpec((tm, D_MODEL), lambda i, j: (i, 0)),
                  pl.BlockSpec((1, D_MODEL), lambda i, j: (0, 0)),
                  pl.BlockSpec((D_MODEL, tn), lambda i, j: (0, j))],
        out_specs=(pl.BlockSpec((tm, tn), lambda i, j: (i, j)), pl.BlockSpec((tm, LANES), lambda i, j: (i, 0))),
        scratch_shapes=[pltpu.VMEM((tm, D_MODEL), BF16)],
        compiler_params=_cparams(("parallel", "arbitrary")),
        name="inproj",
    )(x, g, w)


def _chunk_rows(ci):
    return pl.ds(pl.multiple_of(ci * CHUNK, CHUNK), CHUNK)


def _ssd_prompt_chunk(rows, xbc_ref, z_ref, dt_ref, cw_ref, cb_ref, dtb_ref, alog_ref, dsk_ref, nrm_ref,
                      y_ref, xpad_ref, s_ref):
    xbc = xbc_ref[rows, :].astype(F32)
    xpad_ref[8:8 + CHUNK, :] = xbc
    cw = cw_ref[...]
    acc = (xbc * cw[3:4, :] + xpad_ref[7:7 + CHUNK, :] * cw[2:3, :]
           + xpad_ref[6:6 + CHUNK, :] * cw[1:2, :] + xpad_ref[5:5 + CHUNK, :] * cw[0:1, :] + cb_ref[...])
    conv = _silu(acc)
    xpad_ref[0:8, :] = xbc[CHUNK - 8:CHUNK, :]

    xs = conv[:, :1024]
    dt = _softplus(dt_ref[rows, :] + dtb_ref[...])
    la = dt * (-jnp.exp(alog_ref[...]))
    row = lax.broadcasted_iota(jnp.int32, (CHUNK, CHUNK), 0)
    col = lax.broadcasted_iota(jnp.int32, (CHUNK, CHUNK), 1)
    causal = row >= col
    cum = _dot_exact_lhs01(causal.astype(F32), la)
    cum_t = cum.T
    dt_t = dt.T
    cum_last = jnp.broadcast_to(cum_t[:, CHUNK - 1:CHUNK], (LANES, CHUNK))
    w_t = jnp.exp(cum_last - cum_t) * dt_t
    dec_end = jnp.exp(cum_last)

    xs_t = xs.T
    ys = []
    for g in range(SSD_GROUPS):
        bg = conv[:, 1024 + 128 * g:1024 + 128 * (g + 1)]
        cg = conv[:, 1536 + 128 * g:1536 + 128 * (g + 1)]
        cb = _bdot_nt(cg, bg)
        s_g = s_ref[256 * g:256 * (g + 1), :]
        cs = _bdot_nt(cg, s_g)
        xw_parts = []
        dec_parts = []
        for r in range(4):
            h = 4 * g + r
            colb = jnp.broadcast_to(cum[:, h:h + 1], (CHUNK, CHUNK))
            rowb = jnp.broadcast_to(cum_t[h:h + 1, :], (CHUNK, CHUNK))
            dec = jnp.exp(jnp.where(causal, colb - rowb, -jnp.inf))
            m = cb * dec * jnp.broadcast_to(dt_t[h:h + 1, :], (CHUNK, CHUNK))
            xh = xs[:, 64 * h:64 * (h + 1)]
            yh = _bdot(m, xh) + cs[:, 64 * r:64 * (r + 1)] * jnp.exp(colb)[:, :64]
            ys.append(yh)
            xw_parts.append(xs_t[64 * h:64 * (h + 1), :] * jnp.broadcast_to(w_t[h:h + 1, :], (64, CHUNK)))
            dec_parts.append(jnp.broadcast_to(dec_end[h:h + 1, :], (64, SSD_STATE)))
        xw = jnp.concatenate(xw_parts, axis=0)
        s_ref[256 * g:256 * (g + 1), :] = s_g * jnp.concatenate(dec_parts, axis=0) + _bdot(xw, bg)

    y = jnp.concatenate(ys, axis=1) + dsk_ref[...] * xs
    y = y * _silu(z_ref[rows, :].astype(F32))
    y_ref[rows, :] = _rms(y, nrm_ref[...]).astype(y_ref.dtype)


def _ssd_prompt_kernel(xbc_ref, z_ref, dt_ref, cw_ref, cb_ref, dtb_ref, alog_ref, dsk_ref, nrm_ref,
                       y_ref, sfin_ref, cfin_ref, xpad_ref, s_ref):
    t = pl.program_id(0)

    @pl.when(t == 0)
    def _():
        xpad_ref[0:8, :] = jnp.zeros((8, SSD_CONV_DIM), F32)
        s_ref[...] = jnp.zeros_like(s_ref)

    def chunk(ci, carry):
        _ssd_prompt_chunk(_chunk_rows(ci), xbc_ref, z_ref, dt_ref, cw_ref, cb_ref, dtb_ref, alog_ref, dsk_ref,
                          nrm_ref, y_ref, xpad_ref, s_ref)
        return carry

    lax.fori_loop(0, xbc_ref.shape[0] // CHUNK, chunk, 0)

    @pl.when(t == pl.num_programs(0) - 1)
    def _():
        sfin_ref[...] = s_ref[...]
        cfin_ref[...] = xpad_ref[8 + CHUNK - (SSD_CONV - 1):8 + CHUNK, :]


def _ssd_prompt(proj, dt, cw, cb, dtb, alog, dsk, nrm):
    seq = proj.shape[0]
    const = lambda shape: pl.BlockSpec(shape, lambda t: (0,) * len(shape))
    return pl.pallas_call(
        _ssd_prompt_kernel,
        out_shape=(jax.ShapeDtypeStruct((seq, 1024), BF16),
                   jax.ShapeDtypeStruct((1024, SSD_STATE), F32),
                   jax.ShapeDtypeStruct((SSD_CONV - 1, SSD_CONV_DIM), F32)),
        grid=(seq // MIX_ROWS,),
        in_specs=[pl.BlockSpec((MIX_ROWS, 2048), lambda t: (t, COL_XBC)),
                  pl.BlockSpec((MIX_ROWS, 1024), lambda t: (t, COL_Z)),
                  pl.BlockSpec((MIX_ROWS, LANES), lambda t: (t, 0)),
                  const((SSD_CONV, 2048)), const((1, 2048)), const((1, 128)), const((1, 128)),
                  const((1, 1024)), const((1, 1024))],
        out_specs=(pl.BlockSpec((MIX_ROWS, 1024), lambda t: (t, 0)),
                   const((1024, SSD_STATE)), const((SSD_CONV - 1, SSD_CONV_DIM))),
        scratch_shapes=[pltpu.VMEM((8 + CHUNK, SSD_CONV_DIM), F32), pltpu.VMEM((1024, SSD_STATE), F32)],
        compiler_params=_cparams(("arbitrary",)),
        name="ssd_prompt",
    )(proj, proj, dt, cw, cb, dtb, alog, dsk, nrm)


def _ssd_dec_pre_kernel(xbc_ref, dt_ref, cst_ref, cw_ref, cb_ref, dtb_ref, alog_ref,
                        cnew_ref, xs_ref, bc_ref, dec_t_ref, xdt_t_ref):
    xbc = xbc_ref[...]
    cw = cw_ref[...]
    acc = (cst_ref[0] * cw[0:1, :] + cst_ref[1] * cw[1:2, :] + cst_ref[2] * cw[2:3, :]
           + xbc * cw[3:4, :] + cb_ref[...])
    conv = _silu(acc)
    cnew_ref[0] = cst_ref[1]
    cnew_ref[1] = cst_ref[2]
    cnew_ref[2] = xbc
    xs = conv[:, :1024]
    xs_ref[...] = xs
    bc_ref[...] = conv[:, 1024:]
    dt = _softplus(dt_ref[...] + dtb_ref[...])
    dec = jnp.exp(dt * (-jnp.exp(alog_ref[...])))
    hrow = lax.broadcasted_iota(jnp.int32, (LANES, 1024), 0)
    hcol = lax.broadcasted_iota(jnp.int32, (LANES, 1024), 1)
    expand = ((hcol >> 6) == hrow).astype(F32)
    dec_t_ref[...] = _dot_exact_rhs01(dec, expand).T
    xdt_t_ref[...] = (xs * _dot_exact_rhs01(dt, expand)).T


def _ssd_dec_pre(proj, dt, cst_t, cw, cb, dtb, alog):
    nb = proj.shape[0]
    const = lambda shape: pl.BlockSpec(shape, lambda t: (0,) * len(shape))
    return pl.pallas_call(
        _ssd_dec_pre_kernel,
        out_shape=(jax.ShapeDtypeStruct((3, nb, 2048), F32), jax.ShapeDtypeStruct((nb, 1024), F32),
                   jax.ShapeDtypeStruct((nb, 1024), F32), jax.ShapeDtypeStruct((1024, nb), F32),
                   jax.ShapeDtypeStruct((1024, nb), F32)),
        grid=(1,),
        in_specs=[pl.BlockSpec((nb, 2048), lambda t: (0, COL_XBC)),
                  pl.BlockSpec((nb, LANES), lambda t: (0, 0)),
                  const((3, nb, 2048)), const((SSD_CONV, 2048)), const((1, 2048)), const((1, 128)), const((1, 128))],
        out_specs=(const((3, nb, 2048)), const((nb, 1024)), const((nb, 1024)), const((1024, nb)), const((1024, nb))),
        compiler_params=_cparams(("arbitrary",)),
        name="ssd_dec_pre",
    )(proj, dt, cst_t, cw, cb, dtb, alog)


DEC_BLOCK = 8


def _ssd_dec_state_kernel(dec_t_ref, xdt_t_ref, bc_ref, z_ref, xs_ref, st_ref, dsk_ref, nrm_ref,
                          y_ref, stn_ref):
    i = pl.program_id(0)
    shift = (LANES - i * DEC_BLOCK) % LANES
    decr = pltpu.roll(dec_t_ref[...], shift, axis=1)
    xr = pltpu.roll(xdt_t_ref[...], shift, axis=1)
    bc = bc_ref[...]
    rowid = lax.broadcasted_iota(jnp.int32, (DEC_BLOCK, SSD_STATE), 0)
    ys = [jnp.zeros((DEC_BLOCK, 256), F32) for _ in range(SSD_GROUPS)]
    for j in range(DEC_BLOCK):
        for g in range(SSD_GROUPS):
            lo, hi = 256 * g, 256 * (g + 1)
            s_old = st_ref[j, lo:hi, :]
            dcol = jnp.broadcast_to(decr[lo:hi, j:j + 1], (256, SSD_STATE))
            xcol = jnp.broadcast_to(xr[lo:hi, j:j + 1], (256, SSD_STATE))
            s_new = s_old * dcol + xcol * bc[j:j + 1, 128 * g:128 * (g + 1)]
            stn_ref[j, lo:hi, :] = s_new
            cm = jnp.where(rowid == j, bc[:, 512 + 128 * g:512 + 128 * (g + 1)], 0.0)
            ys[g] = ys[g] + _bdot_nt(cm, s_new)
    xs = xs_ref[...]
    y = jnp.concatenate(ys, axis=1) + dsk_ref[...] * xs
    y = y * _silu(z_ref[...])
    y_ref[...] = _rms(y, nrm_ref[...])


class _LayerCall:
    def __init__(self, layer, steps):
        self.layer, self.steps, self.first = layer, steps, layer == 0
        self.grid = (DEPTH * steps,) if self.first else (steps,)

    def _block(self, t):
        return jnp.minimum(t, self.steps - 1) if self.first else t

    def rows(self, width, col=0):
        return pl.BlockSpec((DEC_BLOCK, width), lambda t: (self._block(t), col))

    def state_in(self, tail):
        zeros = (0,) * len(tail)
        return pl.BlockSpec((None, DEC_BLOCK) + tail, lambda t: (self.layer, self._block(t)) + zeros)

    def state_out(self, tail):
        zeros = (0,) * len(tail)
        if self.first:
            return pl.BlockSpec((None, DEC_BLOCK) + tail, lambda t: (t // self.steps, t % self.steps) + zeros)
        return pl.BlockSpec((None, DEC_BLOCK) + tail, lambda t: (self.layer, t) + zeros)

    def kernel(self, body, n_in, n_carried, stacked_outs):
        def wrapped(*refs):
            refs = refs[:n_in] + refs[n_in + n_carried:]
            if not self.first:
                body(*refs)
                return
            t = pl.program_id(0)

            @pl.when(t < self.steps)
            def _():
                body(*refs)

            @pl.when(t >= self.steps)
            def _():
                for k in stacked_outs:
                    refs[n_in + k][...] = jnp.zeros_like(refs[n_in + k])
        return wrapped


def _ssd_dec_state(layer, dec_t, xdt_t, bc, proj, xs, st_all, dsk, nrm, carried):
    nb = xs.shape[0]
    lc = _LayerCall(layer, nb // DEC_BLOCK)
    const = lambda shape: pl.BlockSpec(shape, lambda t: (0,) * len(shape))
    tail = (1024, SSD_STATE)
    ins = [dec_t, xdt_t, bc, proj, xs, st_all, dsk, nrm]
    in_specs = [const((1024, nb)), const((1024, nb)), lc.rows(1024), lc.rows(1024, COL_Z), lc.rows(1024),
                lc.state_in(tail), const((1, 1024)), const((1, 1024))]
    n_in = len(ins)
    carried = [] if carried is None else [carried]
    return pl.pallas_call(
        lc.kernel(_ssd_dec_state_kernel, n_in, len(carried), (1,)),
        out_shape=(jax.ShapeDtypeStruct((nb, 1024), F32), jax.ShapeDtypeStruct(st_all.shape, F32)),
        grid=lc.grid,
        in_specs=in_specs + [pl.BlockSpec(memory_space=pl.ANY)] * len(carried),
        out_specs=(lc.rows(1024), lc.state_out(tail)),
        input_output_aliases={n_in + k: 1 + k for k in range(len(carried))},
        compiler_params=_cparams(("arbitrary",)),
        name="ssd_dec_state",
    )(*ins, *carried)


def _swa_prompt_block(rows, has_prev, sink_ref, q_ref, k_ref, v_ref, c_ref, s1_ref, s2_ref, o_ref, kp_ref, vp_ref):
    c, s1, s2 = c_ref[rows, :], s1_ref[rows, :], s2_ref[rows, :]
    q = _rope_att(q_ref[rows, :].astype(F32), _tile_lanes(c, 8), _tile_lanes(s1, 8), _tile_lanes(s2, 8))
    k = _rope_att(k_ref[rows, :].astype(F32), _tile_lanes(c, 2), _tile_lanes(s1, 2), _tile_lanes(s2, 2))
    v = v_ref[rows, :].astype(F32)
    ghead = lax.broadcasted_iota(jnp.int32, (WINDOW, 256), 1) >> 6
    expand = lambda t: jnp.concatenate([jnp.where(ghead == g, t, 0.0) for g in range(ATT_KV_HEADS)],
                                       axis=0).astype(BF16)
    kbd, vbd = expand(k), expand(v)
    kbd_prev, vbd_prev = expand(kp_ref[...]), expand(vp_ref[...])
    qall = jnp.concatenate([q[:, 256 * r:256 * (r + 1)] for r in range(ATT_REP)], axis=0).astype(BF16)
    scale = ATT_HEAD_DIM ** -0.5
    nt_dims = (((1,), (1,)), ((), ()))
    sp_all = lax.dot_general(qall, kbd_prev, nt_dims, preferred_element_type=F32) * scale
    sc_all = lax.dot_general(qall, kbd, nt_dims, preferred_element_type=F32) * scale
    nq = ATT_REP * WINDOW
    qi = lax.broadcasted_iota(jnp.int32, (nq, WINDOW), 0) & (WINDOW - 1)
    kj = lax.broadcasted_iota(jnp.int32, (nq, WINDOW), 1)
    mask_prev = jnp.logical_and(kj > qi, has_prev)
    mask_cur = kj <= qi
    rep = lax.broadcasted_iota(jnp.int32, (nq, 1), 0) >> 7
    pp, pc = [], []
    for g in range(ATT_KV_HEADS):
        sp = jnp.where(mask_prev, sp_all[:, WINDOW * g:WINDOW * (g + 1)], -jnp.inf)
        sc = jnp.where(mask_cur, sc_all[:, WINDOW * g:WINDOW * (g + 1)], -jnp.inf)
        sink = jnp.where(rep == 0, sink_ref[4 * g],
                         jnp.where(rep == 1, sink_ref[4 * g + 1],
                                   jnp.where(rep == 2, sink_ref[4 * g + 2], sink_ref[4 * g + 3])))
        m = jnp.maximum(jnp.max(jnp.maximum(sp, sc), axis=1, keepdims=True), sink)
        ep = jnp.exp(sp - m)
        ec = jnp.exp(sc - m)
        inv = 1.0 / (jnp.sum(ep + ec, axis=1, keepdims=True) + jnp.exp(sink - m))
        pp.append((ep * inv).astype(BF16))
        pc.append((ec * inv).astype(BF16))
    o = (jnp.dot(jnp.concatenate(pp, axis=1), vbd_prev, preferred_element_type=F32)
         + jnp.dot(jnp.concatenate(pc, axis=1), vbd, preferred_element_type=F32))
    for r in range(ATT_REP):
        o_ref[rows, 256 * r:256 * (r + 1)] = o[WINDOW * r:WINDOW * (r + 1), :].astype(o_ref.dtype)
    kp_ref[...] = k
    vp_ref[...] = v


def _swa_prompt_kernel(sink_ref, q_ref, k_ref, v_ref, c_ref, s1_ref, s2_ref,
                       o_ref, wk_ref, wv_ref, kp_ref, vp_ref):
    n = pl.program_id(0)
    blocks = q_ref.shape[0] // WINDOW

    @pl.when(n == 0)
    def _():
        kp_ref[...] = jnp.zeros_like(kp_ref)
        vp_ref[...] = jnp.zeros_like(vp_ref)

    def block(bi, carry):
        _swa_prompt_block(_chunk_rows(bi), n * blocks + bi > 0, sink_ref, q_ref, k_ref, v_ref,
                          c_ref, s1_ref, s2_ref, o_ref, kp_ref, vp_ref)
        return carry

    lax.fori_loop(0, blocks, block, 0)

    @pl.when(n == pl.num_programs(0) - 1)
    def _():
        wk_ref[...] = kp_ref[...]
        wv_ref[...] = vp_ref[...]


def _swa_prompt(sinks, proj, tabs):
    seq = proj.shape[0]
    c, s1, s2 = tabs
    const = lambda shape: pl.BlockSpec(shape, lambda t: (0,) * len(shape))
    tab = pl.BlockSpec((MIX_ROWS, LANES), lambda t: (t, 0))
    return pl.pallas_call(
        _swa_prompt_kernel,
        out_shape=(jax.ShapeDtypeStruct((seq, 1024), BF16),
                   jax.ShapeDtypeStruct((WINDOW, 256), F32), jax.ShapeDtypeStruct((WINDOW, 256), F32)),
        grid=(seq // MIX_ROWS,),
        in_specs=[pl.BlockSpec(memory_space=pltpu.SMEM),
                  pl.BlockSpec((MIX_ROWS, 1024), lambda t: (t, COL_QA)),
                  pl.BlockSpec((MIX_ROWS, 256), lambda t: (t, COL_KA)),
                  pl.BlockSpec((MIX_ROWS, 256), lambda t: (t, COL_VA)),
                  tab, tab, tab],
        out_specs=(pl.BlockSpec((MIX_ROWS, 1024), lambda t: (t, 0)), const((WINDOW, 256)), const((WINDOW, 256))),
        scratch_shapes=[pltpu.VMEM((WINDOW, 256), F32), pltpu.VMEM((WINDOW, 256), F32)],
        compiler_params=_cparams(("arbitrary",)),
        name="swa_prompt",
    )(sinks, proj, proj, proj, c, s1, s2)


def _swa_dec_kernel(q_ref, k_ref, v_ref, kc_ref, vc_ref, c_ref, s1_ref, s2_ref, sink_ref,
                    o_ref, kcn_ref, vcn_ref):
    c, s1, s2 = c_ref[...], s1_ref[...], s2_ref[...]
    q = _rope_att(q_ref[...], _tile_lanes(c, 8), _tile_lanes(s1, 8), _tile_lanes(s2, 8))
    k = _rope_att(k_ref[...], _tile_lanes(c, 2), _tile_lanes(s1, 2), _tile_lanes(s2, 2))
    v = v_ref[...]
    grow = lax.broadcasted_iota(jnp.int32, (8, 256), 0)
    gcol = lax.broadcasted_iota(jnp.int32, (8, 256), 1)
    gmask = ((gcol >> 6) == grow).astype(F32)
    gmask4 = jnp.concatenate([gmask] * ATT_REP, axis=0)
    lane = lax.broadcasted_iota(jnp.int32, (8 * ATT_REP, WINDOW), 1)
    sink = sink_ref[...][:, 0:1]
    scale = ATT_HEAD_DIM ** -0.5
    for j in range(DEC_BLOCK):
        a = jnp.concatenate([jnp.broadcast_to(q[j:j + 1, 256 * r:256 * (r + 1)], (8, 256)) * gmask
                             for r in range(ATT_REP)], axis=0)
        kb = kc_ref[j]
        vb = vc_ref[j]
        knew = k[j:j + 1, :]
        vnew = v[j:j + 1, :]
        s = jnp.where(lane == 0, -jnp.inf, _bdot_nt(a, kb) * scale)
        snew = jnp.sum(a * knew, axis=1, keepdims=True) * scale
        m = jnp.maximum(jnp.maximum(jnp.max(s, axis=1, keepdims=True), snew), sink)
        e = jnp.exp(s - m)
        enew = jnp.exp(snew - m)
        inv = 1.0 / (jnp.sum(e, axis=1, keepdims=True) + enew + jnp.exp(sink - m))
        o = (_bdot(e * inv, vb) + (enew * inv) * vnew) * gmask4
        for r in range(ATT_REP):
            o_ref[j:j + 1, 256 * r:256 * (r + 1)] = jnp.sum(o[8 * r:8 * (r + 1), :], axis=0, keepdims=True)
        kcn_ref[j, 0:WINDOW - 1, :] = kc_ref[j, 1:WINDOW, :]
        kcn_ref[j, WINDOW - 1:WINDOW, :] = knew
        vcn_ref[j, 0:WINDOW - 1, :] = vc_ref[j, 1:WINDOW, :]
        vcn_ref[j, WINDOW - 1:WINDOW, :] = vnew


def _swa_dec(layer, proj, kc_all, vc_all, tabs, sink32, carried):
    nb = proj.shape[0]
    lc = _LayerCall(layer, nb // DEC_BLOCK)
    c, s1, s2 = tabs
    const = lambda shape: pl.BlockSpec(shape, lambda t: (0,) * len(shape))
    tail = (WINDOW, 256)
    ins = [proj, proj, proj, kc_all, vc_all, c, s1, s2, sink32]
    in_specs = [lc.rows(1024, COL_QA), lc.rows(256, COL_KA), lc.rows(256, COL_VA),
                lc.state_in(tail), lc.state_in(tail),
                const((1, LANES)), const((1, LANES)), const((1, LANES)), const((32, LANES))]
    n_in = len(ins)
    carried = [] if carried is None else list(carried)
    return pl.pallas_call(
        lc.kernel(_swa_dec_kernel, n_in, len(carried), (1, 2)),
        out_shape=(jax.ShapeDtypeStruct((nb, 1024), F32),
                   jax.ShapeDtypeStruct(kc_all.shape, F32), jax.ShapeDtypeStruct(vc_all.shape, F32)),
        grid=lc.grid,
        in_specs=in_specs + [pl.BlockSpec(memory_space=pl.ANY)] * len(carried),
        out_specs=(lc.rows(1024), lc.state_out(tail), lc.state_out(tail)),
        input_output_aliases={n_in + k: 1 + k for k in range(len(carried))},
        compiler_params=_cparams(("arbitrary",)),
        name="swa_dec",
    )(*ins, *carried)


def _ret_prompt_chunk(rows, lg_ref, q_ref, k_ref, v_ref, gr_ref, c_ref, s_ref, nrm_ref,
                      o_ref, st_ref, intra_ref, fs_ref, te_ref):
    c = c_ref[rows, :]
    s = s_ref[rows, :]
    q = q_ref[rows, :].astype(F32)
    k = k_ref[rows, :].astype(F32)
    v = v_ref[rows, :].astype(F32)
    gr = gr_ref[rows, :].astype(F32)
    nrm = nrm_ref[...]
    for h in range(RET_HEADS):
        sl = slice(128 * h, 128 * (h + 1))
        qh = q[:, sl]
        kh = k[:, sl]
        qh = qh * c + pltpu.roll(qh, 64, axis=1) * s
        kh = (kh * c + pltpu.roll(kh, 64, axis=1) * s) * (RET_DK ** -0.5)
        vh = v[:, sl]
        att = _bdot_nt(qh, kh) * intra_ref[h]
        s_old = st_ref[h]
        o = _bdot(att, vh) + _bdot(qh, s_old) * fs_ref[h]
        cd = jnp.exp(jnp.zeros((1, RET_DK), F32) + CHUNK * lg_ref[h])
        st_ref[h] = s_old * cd + _bdot((kh * te_ref[h]).T, vh)
        o = o * lax.rsqrt(jnp.mean(o * o, axis=-1, keepdims=True) + EPS)
        o_ref[rows, sl] = (o * nrm[:, sl] * _silu(gr[:, sl])).astype(o_ref.dtype)


def _ret_prompt_kernel(lg_ref, q_ref, k_ref, v_ref, gr_ref, c_ref, s_ref, nrm_ref,
                       o_ref, sfin_ref, st_ref, intra_ref, fs_ref, te_ref):
    t = pl.program_id(0)

    @pl.when(t == 0)
    def _():
        st_ref[...] = jnp.zeros_like(st_ref)
        ri = lax.broadcasted_iota(jnp.int32, (CHUNK, CHUNK), 0).astype(F32)
        ci = lax.broadcasted_iota(jnp.int32, (CHUNK, CHUNK), 1).astype(F32)
        rel = ri - ci
        for h in range(RET_HEADS):
            lg = lg_ref[h]
            intra_ref[h] = jnp.exp(jnp.where(rel >= 0, rel * lg, -jnp.inf))
            fs_ref[h] = jnp.exp((ri + 1.0) * lg)
            te_ref[h] = jnp.exp((CHUNK - 1.0 - ri) * lg)

    def chunk(ci, carry):
        _ret_prompt_chunk(_chunk_rows(ci), lg_ref, q_ref, k_ref, v_ref, gr_ref, c_ref, s_ref, nrm_ref,
                          o_ref, st_ref, intra_ref, fs_ref, te_ref)
        return carry

    lax.fori_loop(0, q_ref.shape[0] // CHUNK, chunk, 0)

    @pl.when(t == pl.num_programs(0) - 1)
    def _():
        sfin_ref[...] = st_ref[...]


def _ret_prompt(log_gamma, proj, tabs, nrm):
    seq = proj.shape[0]
    c, s = tabs
    const = lambda shape: pl.BlockSpec(shape, lambda t: (0,) * len(shape))
    col = lambda cidx: pl.BlockSpec((MIX_ROWS, 1024), lambda t: (t, cidx))
    tab = pl.BlockSpec((MIX_ROWS, LANES), lambda t: (t, 0))
    tbl = pltpu.VMEM((RET_HEADS, CHUNK, CHUNK), F32)
    return pl.pallas_call(
        _ret_prompt_kernel,
        out_shape=(jax.ShapeDtypeStruct((seq, 1024), BF16), jax.ShapeDtypeStruct((RET_HEADS, RET_DK, 128), F32)),
        grid=(seq // MIX_ROWS,),
        in_specs=[pl.BlockSpec(memory_space=pltpu.SMEM), col(COL_QR), col(COL_KR), col(COL_VR), col(COL_GR),
                  tab, tab, const((1, 1024))],
        out_specs=(pl.BlockSpec((MIX_ROWS, 1024), lambda t: (t, 0)), const((RET_HEADS, RET_DK, 128))),
        scratch_shapes=[tbl, tbl, tbl, tbl],
        compiler_params=_cparams(("arbitrary",)),
        name="ret_prompt",
    )(log_gamma, proj, proj, proj, proj, c, s, nrm)


def _ret_dec_pre_kernel(q_ref, k_ref, c_ref, s_ref, qrot_ref, kt_ref):
    c = c_ref[...]
    s = s_ref[...]
    q = q_ref[...]
    k = k_ref[...]
    ks = []
    for h in range(RET_HEADS):
        sl = slice(128 * h, 128 * (h + 1))
        qh = q[:, sl]
        kh = k[:, sl]
        qrot_ref[:, sl] = qh * c + pltpu.roll(qh, 64, axis=1) * s
        ks.append((kh * c + pltpu.roll(kh, 64, axis=1) * s) * (RET_DK ** -0.5))
    kt_ref[...] = jnp.concatenate(ks, axis=1).T


def _ret_dec_pre(proj, tabs):
    nb = proj.shape[0]
    c, s = tabs
    const = lambda shape: pl.BlockSpec(shape, lambda t: (0,) * len(shape))
    return pl.pallas_call(
        _ret_dec_pre_kernel,
        out_shape=(jax.ShapeDtypeStruct((nb, 1024), F32), jax.ShapeDtypeStruct((1024, nb), F32)),
        grid=(1,),
        in_specs=[pl.BlockSpec((nb, 1024), lambda t: (0, COL_QR)), pl.BlockSpec((nb, 1024), lambda t: (0, COL_KR)),
                  const((1, LANES)), const((1, LANES))],
        out_specs=(const((nb, 1024)), const((1024, nb))),
        compiler_params=_cparams(("arbitrary",)),
        name="ret_dec_pre",
    )(proj, proj, c, s)


def _ret_dec_state_kernel(gam_ref, kt_ref, q_ref, v_ref, gr_ref, st_ref, nrm_ref, o_ref, stn_ref):
    i = pl.program_id(0)
    shift = (LANES - i * DEC_BLOCK) % LANES
    kr = pltpu.roll(kt_ref[...], shift, axis=1)
    q = q_ref[...]
    v = v_ref[...]
    rowid = lax.broadcasted_iota(jnp.int32, (DEC_BLOCK, RET_DK), 0)
    os_ = [jnp.zeros((DEC_BLOCK, 128), F32) for _ in range(RET_HEADS)]
    for j in range(DEC_BLOCK):
        for h in range(RET_HEADS):
            lo, hi = 128 * h, 128 * (h + 1)
            kcol = jnp.broadcast_to(kr[lo:hi, j:j + 1], (RET_DK, 128))
            s_new = st_ref[j, lo:hi, :] * gam_ref[h] + kcol * v[j:j + 1, lo:hi]
            stn_ref[j, lo:hi, :] = s_new
            qm = jnp.where(rowid == j, q[:, lo:hi], 0.0)
            os_[h] = os_[h] + _bdot(qm, s_new)
    gr = gr_ref[...]
    nrm = nrm_ref[...]
    for h in range(RET_HEADS):
        sl = slice(128 * h, 128 * (h + 1))
        o = os_[h]
        o = o * lax.rsqrt(jnp.mean(o * o, axis=-1, keepdims=True) + EPS)
        o_ref[:, sl] = o * nrm[:, sl] * _silu(gr[:, sl])


def _ret_dec_state(layer, gam, kt, qrot, proj, st_all, nrm, carried):
    nb = qrot.shape[0]
    lc = _LayerCall(layer, nb // DEC_BLOCK)
    const = lambda shape: pl.BlockSpec(shape, lambda t: (0,) * len(shape))
    tail = (1024, 128)
    ins = [gam, kt, qrot, proj, proj, st_all, nrm]
    in_specs = [pl.BlockSpec(memory_space=pltpu.SMEM), const((1024, nb)), lc.rows(1024),
                lc.rows(1024, COL_VR), lc.rows(1024, COL_GR), lc.state_in(tail), const((1, 1024))]
    n_in = len(ins)
    carried = [] if carried is None else [carried]
    return pl.pallas_call(
        lc.kernel(_ret_dec_state_kernel, n_in, len(carried), (1,)),
        out_shape=(jax.ShapeDtypeStruct((nb, 1024), F32), jax.ShapeDtypeStruct(st_all.shape, F32)),
        grid=lc.grid,
        in_specs=in_specs + [pl.BlockSpec(memory_space=pl.ANY)] * len(carried),
        out_specs=(lc.rows(1024), lc.state_out(tail)),
        input_output_aliases={n_in + k: 1 + k for k in range(len(carried))},
        compiler_params=_cparams(("arbitrary",)),
        name="ret_dec_state",
    )(*ins, *carried)


def _merge_kernel(x_ref, a_ref, b_ref, c_ref, g1_ref, g2_ref, g3_ref, w1_ref, w2_ref, w3_ref, wo_ref, o_ref):
    gate = lambda ref: _sigmoid(ref[...].astype(F32))
    m = (gate(g1_ref) * jnp.dot(a_ref[...].astype(BF16), w1_ref[...], preferred_element_type=F32)
         + gate(g2_ref) * jnp.dot(b_ref[...].astype(BF16), w2_ref[...], preferred_element_type=F32)
         + gate(g3_ref) * jnp.dot(c_ref[...].astype(BF16), w3_ref[...], preferred_element_type=F32))
    o_ref[...] = x_ref[...] + jnp.dot(m.astype(BF16), wo_ref[...], preferred_element_type=F32)


def _merge(x, a, b, c, proj, w1, w2, w3, wo):
    rows = x.shape[0]
    tm = min(rows, 512)
    rowb = pl.BlockSpec((tm, 1024), lambda i: (i, 0))
    gate = lambda k: pl.BlockSpec((tm, 1024), lambda i: (i, COL_GATE + k))
    wsp = pl.BlockSpec((1024, 1024), lambda i: (0, 0))
    return pl.pallas_call(
        _merge_kernel,
        out_shape=jax.ShapeDtypeStruct((rows, 1024), F32),
        grid=(rows // tm,),
        in_specs=[rowb, rowb, rowb, rowb, gate(0), gate(1), gate(2), wsp, wsp, wsp, wsp],
        out_specs=rowb,
        compiler_params=_cparams(("parallel",)),
        name="merge",
    )(x, a, b, c, proj, proj, proj, w1, w2, w3, wo)


FF_TILE = 1408
FF_SPLIT = 768


def _ffn_kernel(x_ref, g_ref, wg_ref, wu_ref, wd_ref, o_ref, h_ref, acc_ref):
    j = pl.program_id(1)

    @pl.when(j == 0)
    def _():
        h_ref[...] = _rms(x_ref[...], g_ref[...]).astype(BF16)
        acc_ref[...] = jnp.zeros_like(acc_ref)

    h = h_ref[...]
    for lo, hi in ((0, FF_SPLIT), (FF_SPLIT, FF_TILE)):
        a = jnp.dot(h, wg_ref[:, lo:hi], preferred_element_type=F32)
        u = jnp.dot(h, wu_ref[:, lo:hi], preferred_element_type=F32)
        acc_ref[...] += jnp.dot((_silu(a) * u).astype(BF16), wd_ref[lo:hi, :], preferred_element_type=F32)

    @pl.when(j == pl.num_programs(1) - 1)
    def _():
        o_ref[...] = x_ref[...] + acc_ref[...]


def _ffn(x, g, wg, wu, wd):
    rows = x.shape[0]
    tm = min(rows, 1024)
    return pl.pallas_call(
        _ffn_kernel,
        out_shape=jax.ShapeDtypeStruct((rows, 1024), F32),
        grid=(rows // tm, D_FF // FF_TILE),
        in_specs=[pl.BlockSpec((tm, 1024), lambda i, j: (i, 0)), pl.BlockSpec((1, 1024), lambda i, j: (0, 0)),
                  pl.BlockSpec((1024, FF_TILE), lambda i, j: (0, j)), pl.BlockSpec((1024, FF_TILE), lambda i, j: (0, j)),
                  pl.BlockSpec((FF_TILE, 1024), lambda i, j: (j, 0))],
        out_specs=pl.BlockSpec((tm, 1024), lambda i, j: (i, 0)),
        scratch_shapes=[pltpu.VMEM((tm, 1024), BF16), pltpu.VMEM((tm, 1024), F32)],
        compiler_params=_cparams(("parallel", "arbitrary"), vmem=VMEM_LIMIT_FFN),
        name="ffn",
    )(x, g, wg, wu, wd)


MOE_FF_TILE = 256


def _top2(h, rw_hi, rw_lo, rb, lane):
    h_hi = h.astype(BF16)
    h_lo = (h - h_hi.astype(F32)).astype(BF16)
    d = lambda a, b: jnp.dot(a, b, preferred_element_type=F32)
    logits = d(h_hi, rw_hi) + d(h_hi, rw_lo) + d(h_lo, rw_hi) + rb
    logits = jnp.where(lane < N_EXPERTS, logits, -jnp.inf)
    m1 = jnp.max(logits, axis=1, keepdims=True)
    i1 = jnp.min(jnp.where(logits == m1, lane, float(LANES)), axis=1, keepdims=True)
    rest = jnp.where(lane == i1, -jnp.inf, logits)
    m2 = jnp.max(rest, axis=1, keepdims=True)
    i2 = jnp.min(jnp.where(rest == m2, lane, float(LANES)), axis=1, keepdims=True)
    e2 = jnp.exp(m2 - m1)
    p1 = 1.0 / (1.0 + e2)
    return i1, i2, p1, e2 * p1


def _moe_kernel(x_ref, g_ref, rw_ref, rb_ref, wg_ref, wu_ref, wd_ref, o_ref, h_ref, acc_ref, comb_ref):
    e = pl.program_id(1)
    j = pl.program_id(2)
    tm = x_ref.shape[0]
    lane = lax.broadcasted_iota(jnp.int32, (tm, LANES), 1).astype(F32)

    @pl.when(jnp.logical_and(e == 0, j == 0))
    def _():
        h = _rms(x_ref[...], g_ref[...])
        h_ref[...] = h.astype(BF16)
        i1, i2, p1, p2 = _top2(h, rw_ref[0], rw_ref[1], rb_ref[...], lane)
        comb_ref[...] = jnp.where(lane == i1, p1, 0.0) + jnp.where(lane == i2, p2, 0.0)
        acc_ref[...] = jnp.zeros_like(acc_ref)

    ce = jnp.sum(jnp.where(lane == e.astype(F32), comb_ref[...], 0.0), axis=1, keepdims=True)
    h = h_ref[...]
    a = jnp.dot(h, wg_ref[0], preferred_element_type=F32)
    u = jnp.dot(h, wu_ref[0], preferred_element_type=F32)
    acc_ref[...] += ce * jnp.dot((_silu(a) * u).astype(BF16), wd_ref[0], preferred_element_type=F32)

    @pl.when(jnp.logical_and(e == pl.num_programs(1) - 1, j == pl.num_programs(2) - 1))
    def _():
        o_ref[...] = x_ref[...] + acc_ref[...]


def _moe(x, g, rw, rb, wg, wu, wd):
    rows = x.shape[0]
    tm = min(rows, 1024)
    tf = MOE_FF_TILE
    return pl.pallas_call(
        _moe_kernel,
        out_shape=jax.ShapeDtypeStruct((rows, 1024), F32),
        grid=(rows // tm, N_EXPERTS, D_FF // tf),
        in_specs=[pl.BlockSpec((tm, 1024), lambda i, e, j: (i, 0)), pl.BlockSpec((1, 1024), lambda i, e, j: (0, 0)),
                  pl.BlockSpec((2, 1024, LANES), lambda i, e, j: (0, 0, 0)),
                  pl.BlockSpec((1, LANES), lambda i, e, j: (0, 0)),
                  pl.BlockSpec((1, 1024, tf), lambda i, e, j: (e, 0, j)),
                  pl.BlockSpec((1, 1024, tf), lambda i, e, j: (e, 0, j)),
                  pl.BlockSpec((1, tf, 1024), lambda i, e, j: (e, j, 0))],
        out_specs=pl.BlockSpec((tm, 1024), lambda i, e, j: (i, 0)),
        scratch_shapes=[pltpu.VMEM((tm, 1024), BF16), pltpu.VMEM((tm, 1024), F32), pltpu.VMEM((tm, LANES), F32)],
        compiler_params=_cparams(("parallel", "arbitrary", "arbitrary")),
        name="moe",
    )(x, g, rw, rb, wg, wu, wd)


MOE_ROWS = 512
MOE_GROUP_FF = 1408
GATHER_UNROLL = 8


def _router_kernel(x_ref, g_ref, rw_ref, rb_ref, o_ref):
    tm = x_ref.shape[0]
    lane = lax.broadcasted_iota(jnp.int32, (tm, LANES), 1).astype(F32)
    i1, i2, p1, p2 = _top2(_rms(x_ref[...], g_ref[...]), rw_ref[0], rw_ref[1], rb_ref[...], lane)
    o_ref[...] = jnp.where(lane == 0.0, i1, jnp.where(lane == 1.0, i2, jnp.where(lane == 2.0, p1,
                           jnp.where(lane == 3.0, p2, 0.0))))


def _router(x, g, rw, rb):
    rows = x.shape[0]
    tm = min(rows, 1024)
    return pl.pallas_call(
        _router_kernel,
        out_shape=jax.ShapeDtypeStruct((rows, LANES), F32),
        grid=(rows // tm,),
        in_specs=[pl.BlockSpec((tm, 1024), lambda i: (i, 0)), pl.BlockSpec((1, 1024), lambda i: (0, 0)),
                  pl.BlockSpec((2, 1024, LANES), lambda i: (0, 0, 0)), pl.BlockSpec((1, LANES), lambda i: (0, 0))],
        out_specs=pl.BlockSpec((tm, LANES), lambda i: (i, 0)),
        compiler_params=_cparams(("parallel",)),
        name="router",
    )(x, g, rw, rb)


def _route_plan(route, tm):
    n = route.shape[0]
    n_tiles = (2 * n) // tm + N_EXPERTS
    e_flat = route[:, :2].astype(jnp.int32).reshape(-1)
    onehot = (e_flat[:, None] == jnp.arange(N_EXPERTS, dtype=jnp.int32)[None, :]).astype(jnp.int32)
    csum = jnp.cumsum(onehot, axis=0)
    counts = csum[-1]
    tiles_e = (counts + tm - 1) // tm
    tile_end = jnp.cumsum(tiles_e)
    row_start = (tile_end - tiles_e) * tm
    pos = jnp.sum((csum - onehot + row_start[None, :]) * onehot, axis=1).astype(jnp.int32)
    tile_expert = jnp.minimum(jnp.sum(jnp.arange(n_tiles, dtype=jnp.int32)[:, None] >= tile_end[None, :], axis=1),
                              N_EXPERTS - 1).astype(jnp.int32)
    n_used = tile_end[-1:].astype(jnp.int32)
    src = jnp.zeros((n_tiles * tm,), jnp.int32).at[pos].set(jnp.arange(2 * n, dtype=jnp.int32) // 2)
    return pos, src.reshape(n_tiles, 1, tm), tile_expert, n_used


def _moe_group_kernel(te_ref, nu_ref, src_ref, srcn_ref, x_hbm, g_ref, wg_ref, wu_ref, wd_ref,
                      y_ref, buf, sem, h_ref, acc_ref):
    i = pl.program_id(0)
    j = pl.program_id(1)
    tm = buf.shape[1]
    slot = i % 2
    active = i < nu_ref[0]

    def row_copy(idx_ref, s, r):
        return pltpu.make_async_copy(x_hbm.at[pl.ds(idx_ref[0, 0, r], 1), :], buf.at[s, pl.ds(r, 1), :], sem.at[s])

    def gather(idx_ref, s):
        def body(r, c):
            row_copy(idx_ref, s, r).start()
            return c
        lax.fori_loop(0, tm, body, 0, unroll=GATHER_UNROLL)

    @pl.when(jnp.logical_and(active, j == 0))
    def _():
        @pl.when(i == 0)
        def _():
            gather(src_ref, 0)

        pltpu.make_async_copy(x_hbm.at[pl.ds(0, tm), :], buf.at[slot], sem.at[slot]).wait()

        @pl.when(i + 1 < nu_ref[0])
        def _():
            gather(srcn_ref, 1 - slot)

        h_ref[...] = _rms(buf[slot], g_ref[...]).astype(BF16)
        acc_ref[...] = jnp.zeros_like(acc_ref)

    @pl.when(active)
    def _():
        h = h_ref[...]
        a = jnp.dot(h, wg_ref[0], preferred_element_type=F32)
        u = jnp.dot(h, wu_ref[0], preferred_element_type=F32)
        acc_ref[...] += jnp.dot((_silu(a) * u).astype(BF16), wd_ref[0], preferred_element_type=F32)

    @pl.when(j == pl.num_programs(1) - 1)
    def _():
        @pl.when(active)
        def _():
            y_ref[...] = acc_ref[...]

        @pl.when(jnp.logical_not(active))
        def _():
            y_ref[...] = jnp.zeros_like(y_ref)


def _moe_group(tile_expert, n_used, src, x, g, wg, wu, wd):
    n_tiles, _, tm = src.shape
    tf = MOE_GROUP_FF
    grid_spec = pltpu.PrefetchScalarGridSpec(
        num_scalar_prefetch=2,
        grid=(n_tiles, D_FF // tf),
        in_specs=[pl.BlockSpec((1, 1, tm), lambda i, j, te, nu: (i, 0, 0), memory_space=pltpu.SMEM),
                  pl.BlockSpec((1, 1, tm), lambda i, j, te, nu: (jnp.minimum(i + 1, n_tiles - 1), 0, 0),
                               memory_space=pltpu.SMEM),
                  pl.BlockSpec(memory_space=pl.ANY),
                  pl.BlockSpec((1, 1024), lambda i, j, te, nu: (0, 0)),
                  pl.BlockSpec((1, 1024, tf), lambda i, j, te, nu: (te[i], 0, j)),
                  pl.BlockSpec((1, 1024, tf), lambda i, j, te, nu: (te[i], 0, j)),
                  pl.BlockSpec((1, tf, 1024), lambda i, j, te, nu: (te[i], j, 0))],
        out_specs=pl.BlockSpec((tm, 1024), lambda i, j, te, nu: (i, 0)),
        scratch_shapes=[pltpu.VMEM((2, tm, 1024), F32), pltpu.SemaphoreType.DMA((2,)),
                        pltpu.VMEM((tm, 1024), BF16), pltpu.VMEM((tm, 1024), F32)])
    return pl.pallas_call(
        _moe_group_kernel,
        out_shape=jax.ShapeDtypeStruct((n_tiles * tm, 1024), F32),
        grid_spec=grid_spec,
        compiler_params=_cparams(("arbitrary", "arbitrary")),
        name="moe_group",
    )(tile_expert, n_used, src, src, x, g, wg, wu, wd)


def _moe_combine_kernel(pos_ref, posn_ref, x_ref, r_ref, y_hbm, o_ref, bufa, bufb, sem):
    i = pl.program_id(0)
    tm = x_ref.shape[0]
    slot = i % 2

    def gather(idx_ref, s):
        def body(t, c):
            pltpu.make_async_copy(y_hbm.at[pl.ds(idx_ref[0, 0, 2 * t], 1), :], bufa.at[s, pl.ds(t, 1), :],
                                  sem.at[0, s]).start()
            pltpu.make_async_copy(y_hbm.at[pl.ds(idx_ref[0, 0, 2 * t + 1], 1), :], bufb.at[s, pl.ds(t, 1), :],
                                  sem.at[1, s]).start()
            return c
        lax.fori_loop(0, tm, body, 0, unroll=GATHER_UNROLL)

    @pl.when(i == 0)
    def _():
        gather(pos_ref, 0)

    pltpu.make_async_copy(y_hbm.at[pl.ds(0, tm), :], bufa.at[slot], sem.at[0, slot]).wait()
    pltpu.make_async_copy(y_hbm.at[pl.ds(0, tm), :], bufb.at[slot], sem.at[1, slot]).wait()

    @pl.when(i + 1 < pl.num_programs(0))
    def _():
        gather(posn_ref, 1 - slot)

    r = r_ref[...]
    o_ref[...] = x_ref[...] + r[:, 2:3] * bufa[slot] + r[:, 3:4] * bufb[slot]


def _moe_combine(pos, x, route, y):
    rows = x.shape[0]
    tm = MOE_ROWS
    n = rows // tm
    pos3 = pos.reshape(n, 1, 2 * tm)
    return pl.pallas_call(
        _moe_combine_kernel,
        out_shape=jax.ShapeDtypeStruct((rows, 1024), F32),
        grid=(n,),
        in_specs=[pl.BlockSpec((1, 1, 2 * tm), lambda i: (i, 0, 0), memory_space=pltpu.SMEM),
                  pl.BlockSpec((1, 1, 2 * tm), lambda i: (jnp.minimum(i + 1, n - 1), 0, 0), memory_space=pltpu.SMEM),
                  pl.BlockSpec((tm, 1024), lambda i: (i, 0)), pl.BlockSpec((tm, LANES), lambda i: (i, 0)),
                  pl.BlockSpec(memory_space=pl.ANY)],
        out_specs=pl.BlockSpec((tm, 1024), lambda i: (i, 0)),
        scratch_shapes=[pltpu.VMEM((2, tm, 1024), F32), pltpu.VMEM((2, tm, 1024), F32),
                        pltpu.SemaphoreType.DMA((2, 2))],
        compiler_params=_cparams(("arbitrary",)),
        name="moe_combine",
    )(pos3, pos3, x, route, y)


def _moe_routed(x, g, rw, rb, wg, wu, wd):
    route = _router(x, g, rw, rb)
    pos, src, tile_expert, n_used = _route_plan(route, MOE_ROWS)
    y = _moe_group(tile_expert, n_used, src, x, g, wg, wu, wd)
    return _moe_combine(pos, x, route, y)


def _ple_kernel(x_ref, p_ref, g_ref, wp_ref, wgt_ref, gf_ref, o_ref, *, final):
    x = x_ref[...]
    emb = jnp.dot(p_ref[...].astype(BF16), wp_ref[...], preferred_element_type=F32)
    gate = _sigmoid(jnp.dot(_rms(x, g_ref[...]).astype(BF16), wgt_ref[...], preferred_element_type=F32))
    y = x + emb * gate
    if final:
        y = _rms(y, gf_ref[...])
    o_ref[...] = y


def _ple(x, p, g, wp, wgt, gf, final):
    rows = x.shape[0]
    tm = min(rows, 512)
    vec = pl.BlockSpec((1, 1024), lambda i: (0, 0))
    return pl.pallas_call(
        functools.partial(_ple_kernel, final=final),
        out_shape=jax.ShapeDtypeStruct((rows, 1024), F32),
        grid=(rows // tm,),
        in_specs=[pl.BlockSpec((tm, 1024), lambda i: (i, 0)), pl.BlockSpec((tm, PLE_DIM), lambda i: (i, 0)), vec,
                  pl.BlockSpec((PLE_DIM, 1024), lambda i: (0, 0)), pl.BlockSpec((1024, 1024), lambda i: (0, 0)), vec],
        out_specs=pl.BlockSpec((tm, 1024), lambda i: (i, 0)),
        compiler_params=_cparams(("parallel",)),
        name="ple",
    )(x, p, g, wp, wgt, gf)


def _rmajor_cols(w):
    k = w.shape[0]
    return w.reshape(k, ATT_KV_HEADS, ATT_REP, ATT_HEAD_DIM).transpose(0, 2, 1, 3).reshape(k, ATT_Q_HEADS * ATT_HEAD_DIM)


def _prep_w_in(w):
    offs = [0]
    for wd in IN_WIDTHS:
        offs.append(offs[-1] + wd)
    z, xbc, dt, qa, ka, va, qr, kr, vr, gr, gates = [w[:, offs[i]:offs[i + 1]] for i in range(len(IN_WIDTHS))]
    used = 2048 + 1024 * 6 + 3072 + 256 + 256 + 16
    pad = jnp.zeros((w.shape[0], N_PROJ - used), w.dtype)
    parts = [xbc, z, _rmajor_cols(qa), qr, kr, vr, gr, gates, ka, va, dt, pad]
    return jnp.concatenate([p.astype(BF16) for p in parts], axis=1)


def _att_tables(pos):
    half = ROPE_DIM // 2
    inv = jnp.exp(-math.log(ROPE_THETA) * jnp.arange(half, dtype=F32) * (2.0 / ROPE_DIM))
    ang = pos.astype(F32)[:, None] * inv[None, :]
    cos, sin = jnp.cos(ang), jnp.sin(ang)
    n = pos.shape[0]
    one = jnp.ones((n, ATT_HEAD_DIM - ROPE_DIM), F32)
    zero8 = jnp.zeros((n, half), F32)
    zero = jnp.zeros((n, ATT_HEAD_DIM - ROPE_DIM), F32)
    c = jnp.concatenate([cos, cos, one], axis=1)
    s1 = jnp.concatenate([zero8, sin, zero], axis=1)
    s2 = jnp.concatenate([-sin, zero8, zero], axis=1)
    return tuple(jnp.concatenate([t, t], axis=1) for t in (c, s1, s2))


def _ret_tables(pos):
    half = RET_DK // 2
    inv = jnp.exp(-math.log(RET_THETA) * jnp.arange(half, dtype=F32) * (2.0 / RET_DK))
    ang = pos.astype(F32)[:, None] * inv[None, :]
    cos, sin = jnp.cos(ang), jnp.sin(ang)
    return jnp.concatenate([cos, cos], axis=1), jnp.concatenate([-sin, sin], axis=1)


def _pad_lanes(v, fill=0.0):
    return jnp.concatenate([v.astype(F32), jnp.full((LANES - v.shape[0],), fill, F32)])[None, :]


def kernel(x_prompt, x_sample, state_ssm, state_conv, cache_win_k, cache_win_v, state_ret, p_prompt, p_sample,
           w_in, conv_w, conv_b, dt_bias, a_log, d_skip, ssd_norm, attn_sinks, ret_norm, w_o_ssd, w_o_att, w_o_ret,
           w_out, norm_mix, norm_ffn, norm_ple, ffn_w_gate, ffn_w_up, ffn_w_down, router_w, router_b, moe_w_gate,
           moe_w_up, moe_w_down, w_ple, w_ple_gate, norm_final):
    seq = x_prompt.shape[1]
    nb = x_sample.shape[0]
    xp = x_prompt.reshape(seq, D_MODEL)
    xs = x_sample.reshape(nb, D_MODEL)
    pos_p = jnp.arange(seq)
    pos_s = PAST_LEN + jnp.arange(1)
    att_tab_p, att_tab_s = _att_tables(pos_p), _att_tables(pos_s)
    ret_tab_p, ret_tab_s = _ret_tables(pos_p), _ret_tables(pos_s)
    log_gamma = jnp.log1p(-jnp.exp2(-5.0 - jnp.arange(RET_HEADS, dtype=F32)))
    gamma = jnp.exp(log_gamma)
    row = lambda v: v.astype(F32)[None, :]

    ssm_all = state_ssm.reshape(DEPTH, nb, SSD_HEADS * SSD_HEAD_DIM, SSD_STATE)
    ret_all = state_ret.reshape(DEPTH, nb, RET_HEADS * RET_DK, 128)
    kc_all = cache_win_k.reshape(DEPTH, nb, WINDOW, ATT_KV_HEADS * ATT_HEAD_DIM)
    vc_all = cache_win_v.reshape(DEPTH, nb, WINDOW, ATT_KV_HEADS * ATT_HEAD_DIM)
    ssm_s = ret_s = kv_s = None
    conv_s = []

    new_p = [[], [], [], [], []]
    for i in range(DEPTH):
        w_in_i = _prep_w_in(w_in[i])
        cw, cb = conv_w[i], row(conv_b[i])
        dtb, alog = _pad_lanes(dt_bias[i]), _pad_lanes(a_log[i])
        dsk = row(jnp.repeat(d_skip[i], SSD_HEAD_DIM))
        nrm_ssd, nrm_ret = row(ssd_norm[i]), row(ret_norm[i])
        sinks = attn_sinks[i].astype(F32)
        sink32 = jnp.zeros((ATT_REP, 8), F32).at[:, :ATT_KV_HEADS].set(sinks.reshape(ATT_KV_HEADS, ATT_REP).T)
        sink32 = jnp.broadcast_to(sink32.reshape(32, 1), (32, LANES))
        w1 = w_o_ssd[i].astype(BF16)
        w2 = w_o_att[i].reshape(ATT_KV_HEADS, ATT_REP, ATT_HEAD_DIM, D_MODEL).transpose(1, 0, 2, 3) \
            .reshape(ATT_Q_HEADS * ATT_HEAD_DIM, D_MODEL).astype(BF16)
        w3 = w_o_ret[i].astype(BF16)
        wo = w_out[i].astype(BF16)
        g_mix, g_ffn, g_ple = row(norm_mix[i]), row(norm_ffn[i]), row(norm_ple[i])
        wp, wpg = w_ple[i].astype(BF16), w_ple_gate[i].astype(BF16)
        gf = row(norm_final)
        j = i // 2
        if i % 2 == 0:
            ffw = (ffn_w_gate[j].astype(BF16), ffn_w_up[j].astype(BF16), ffn_w_down[j].astype(BF16))
        else:
            rw = jnp.concatenate([router_w[j], jnp.zeros((D_MODEL, LANES - N_EXPERTS), F32)], axis=1)
            rw_hi = rw.astype(BF16)
            rw = jnp.stack([rw_hi, (rw - rw_hi.astype(F32)).astype(BF16)])
            ffw = (rw, _pad_lanes(router_b[j]), moe_w_gate[j].astype(BF16), moe_w_up[j].astype(BF16),
                   moe_w_down[j].astype(BF16))
        final = i == DEPTH - 1

        proj, dt = _inproj(xp, g_mix, w_in_i, BF16)
        y_ssd, ssm_fin, conv_fin = _ssd_prompt(proj, dt, cw, cb, dtb, alog, dsk, nrm_ssd)
        o_att, wk, wv = _swa_prompt(sinks, proj, att_tab_p)
        o_ret, ret_fin = _ret_prompt(log_gamma, proj, ret_tab_p, nrm_ret)
        xp = _merge(xp, y_ssd, o_att, o_ret, proj, w1, w2, w3, wo)
        xp = _ffn(xp, g_ffn, *ffw) if i % 2 == 0 else _moe_routed(xp, g_ffn, *ffw)
        xp = _ple(xp, p_prompt[i].reshape(seq, PLE_DIM), g_ple, wp, wpg, gf, final)
        new_p[0].append(ssm_fin.reshape(1, SSD_HEADS, SSD_HEAD_DIM, SSD_STATE))
        new_p[1].append(conv_fin[None])
        new_p[2].append(wk.reshape(1, WINDOW, ATT_KV_HEADS, ATT_HEAD_DIM))
        new_p[3].append(wv.reshape(1, WINDOW, ATT_KV_HEADS, ATT_HEAD_DIM))
        new_p[4].append(ret_fin[None])

        proj, dt = _inproj(xs, g_mix, w_in_i, F32)
        cst_t = jnp.transpose(state_conv[i], (1, 0, 2))
        cnew_t, xs_conv, bc, dec_t, xdt_t = _ssd_dec_pre(proj, dt, cst_t, cw, cb, dtb, alog)
        y_ssd, ssm_s = _ssd_dec_state(i, dec_t, xdt_t, bc, proj, xs_conv, ssm_all, dsk, nrm_ssd, ssm_s)
        o_att, *kv_s = _swa_dec(i, proj, kc_all, vc_all, att_tab_s, sink32, kv_s)
        qrot, kt = _ret_dec_pre(proj, ret_tab_s)
        o_ret, ret_s = _ret_dec_state(i, gamma, kt, qrot, proj, ret_all, nrm_ret, ret_s)
        xs = _merge(xs, y_ssd, o_att, o_ret, proj, w1, w2, w3, wo)
        xs = _ffn(xs, g_ffn, *ffw) if i % 2 == 0 else _moe(xs, g_ffn, *ffw)
        xs = _ple(xs, p_sample[i].reshape(nb, PLE_DIM), g_ple, wp, wpg, gf, final)
        conv_s.append(jnp.transpose(cnew_t, (1, 0, 2)))

    y_prompt = xp.reshape(1, seq, D_MODEL)
    y_sample = xs.reshape(nb, 1, D_MODEL)
    outs_p = [jnp.stack(l) for l in new_p]
    outs_s = [ssm_s.reshape(state_ssm.shape), jnp.stack(conv_s), kv_s[0].reshape(cache_win_k.shape),
              kv_s[1].reshape(cache_win_v.shape), ret_s.reshape(state_ret.shape)]
    return (y_prompt, y_sample, *outs_p, *outs_s)
```

```python
import functools
import math

import jax
import jax.numpy as jnp
from jax import lax
from jax.experimental import pallas as pl
from jax.experimental.pallas import tpu as pltpu

F32 = jnp.float32
BF16 = jnp.bfloat16

D_MODEL = 1024
DEPTH = 2
PAST_LEN = 16384
SSD_HEADS = 16
SSD_HEAD_DIM = 64
SSD_GROUPS = 4
SSD_STATE = 128
SSD_CONV = 4
SSD_CONV_DIM = 2048
ATT_HEAD_DIM = 64
ATT_Q_HEADS = 16
ATT_KV_HEADS = 4
ATT_REP = ATT_Q_HEADS // ATT_KV_HEADS
WINDOW = 128
ROPE_THETA = 500000.0
ROPE_DIM = 16
RET_HEADS = 8
RET_DK = 128
RET_THETA = 10000.0
CHUNK = 128
D_FF = 2816
N_EXPERTS = 8
PLE_DIM = 256
EPS = 1e-6

IN_WIDTHS = (1024, 2048, 16, 1024, 256, 256, 1024, 1024, 1024, 1024, 3072)
N_PROJ = 12288
COL_XBC = 0
COL_Z = 2
COL_QA = 3
COL_QR = 4
COL_KR = 5
COL_VR = 6
COL_GR = 7
COL_GATE = 8
COL_KA = 44
COL_VA = 45
COL_DT = 92

LANES = 128
VMEM_LIMIT = 48 * 1024 * 1024
VMEM_LIMIT_FFN = 56 * 1024 * 1024
MIX_ROWS = 4 * CHUNK


def _cparams(sem, vmem=VMEM_LIMIT):
    return pltpu.CompilerParams(dimension_semantics=sem, vmem_limit_bytes=vmem)


def _bdot(a, b):
    return jnp.dot(a.astype(BF16), b.astype(BF16), preferred_element_type=F32)


def _bdot_nt(a, b):
    return lax.dot_general(a.astype(BF16), b.astype(BF16), (((1,), (1,)), ((), ())),
                           preferred_element_type=F32)


def _split3(x):
    x0 = x.astype(BF16)
    r1 = x - x0.astype(F32)
    x1 = r1.astype(BF16)
    x2 = (r1 - x1.astype(F32)).astype(BF16)
    return x0, x1, x2


def _dot_exact_lhs01(m01, x):
    m = m01.astype(BF16)
    x0, x1, x2 = _split3(x)
    d = lambda b: jnp.dot(m, b, preferred_element_type=F32)
    return d(x0) + d(x1) + d(x2)


def _dot_exact_rhs01(x, m01):
    m = m01.astype(BF16)
    x0, x1, x2 = _split3(x)
    d = lambda a: jnp.dot(a, m, preferred_element_type=F32)
    return d(x0) + d(x1) + d(x2)


def _rms(x, g):
    return x * lax.rsqrt(jnp.mean(x * x, axis=-1, keepdims=True) + EPS) * g


def _sigmoid(x):
    return 1.0 / (1.0 + jnp.exp(-x))


def _silu(x):
    return x * _sigmoid(x)


def _softplus(x):
    return jnp.maximum(x, 0.0) + jnp.log1p(jnp.exp(-jnp.abs(x)))


def _rope_att(x, c, s1, s2):
    w = x.shape[1]
    return x * c + pltpu.roll(x, 8, axis=1) * s1 + pltpu.roll(x, w - 8, axis=1) * s2


def _tile_lanes(t, n):
    return jnp.concatenate([t] * n, axis=1) if n > 1 else t


INPROJ_TN = 1024
DT_TILE = (COL_DT * LANES) // INPROJ_TN
DT_OFF = COL_DT * LANES - DT_TILE * INPROJ_TN


def _inproj_kernel(x_ref, g_ref, w_ref, o_ref, dt_ref, h_ref):
    j = pl.program_id(1)

    @pl.when(j == 0)
    def _():
        h_ref[...] = _rms(x_ref[...], g_ref[...]).astype(BF16)

    acc = jnp.dot(h_ref[...], w_ref[...], preferred_element_type=F32)
    o_ref[...] = acc.astype(o_ref.dtype)

    @pl.when(j == DT_TILE)
    def _():
        dt_ref[...] = acc[:, DT_OFF:DT_OFF + LANES]


def _inproj(x, g, w, out_dtype):
    rows = x.shape[0]
    tm = min(rows, 2048)
    tn = INPROJ_TN
    return pl.pallas_call(
        _inproj_kernel,
        out_shape=(jax.ShapeDtypeStruct((rows, N_PROJ), out_dtype), jax.ShapeDtypeStruct((rows, LANES), F32)),
        grid=(rows // tm, N_PROJ // tn),
        in_specs=[pl.BlockSpec((tm, D_MODEL), lambda i, j: (i, 0)),
                  pl.BlockSpec((1, D_MODEL), lambda i, j: (0, 0)),
                  pl.BlockSpec((D_MODEL, tn), lambda i, j: (0, j))],
        out_specs=(pl.BlockSpec((tm, tn), lambda i, j: (i, j)), pl.BlockSpec((tm, LANES), lambda i, j: (i, 0))),
        scratch_shapes=[pltpu.VMEM((tm, D_MODEL), BF16)],
        compiler_params=_cparams(("parallel", "arbitrary")),
        name="inproj",
    )(x, g, w)


def _chunk_rows(ci):
    return pl.ds(pl.multiple_of(ci * CHUNK, CHUNK), CHUNK)


def _ssd_prompt_chunk(rows, xbc_ref, z_ref, dt_ref, cw_ref, cb_ref, dtb_ref, alog_ref, dsk_ref, nrm_ref,
                      y_ref, xpad_ref, s_ref):
    xbc = xbc_ref[rows, :].astype(F32)
    xpad_ref[8:8 + CHUNK, :] = xbc
    cw = cw_ref[...]
    acc = (xbc * cw[3:4, :] + xpad_ref[7:7 + CHUNK, :] * cw[2:3, :]
           + xpad_ref[6:6 + CHUNK, :] * cw[1:2, :] + xpad_ref[5:5 + CHUNK, :] * cw[0:1, :] + cb_ref[...])
    conv = _silu(acc)
    xpad_ref[0:8, :] = xbc[CHUNK - 8:CHUNK, :]

    xs = conv[:, :1024]
    dt = _softplus(dt_ref[rows, :] + dtb_ref[...])
    la = dt * (-jnp.exp(alog_ref[...]))
    row = lax.broadcasted_iota(jnp.int32, (CHUNK, CHUNK), 0)
    col = lax.broadcasted_iota(jnp.int32, (CHUNK, CHUNK), 1)
    causal = row >= col
    cum = _dot_exact_lhs01(causal.astype(F32), la)
    cum_t = cum.T
    dt_t = dt.T
    cum_last = jnp.broadcast_to(cum_t[:, CHUNK - 1:CHUNK], (LANES, CHUNK))
    w_t = jnp.exp(cum_last - cum_t) * dt_t
    dec_end = jnp.exp(cum_last)

    xs_t = xs.T
    ys = []
    for g in range(SSD_GROUPS):
        bg = conv[:, 1024 + 128 * g:1024 + 128 * (g + 1)]
        cg = conv[:, 1536 + 128 * g:1536 + 128 * (g + 1)]
        cb = _bdot_nt(cg, bg)
        s_g = s_ref[256 * g:256 * (g + 1), :]
        cs = _bdot_nt(cg, s_g)
        xw_parts = []
        dec_parts = []
        for r in range(4):
            h = 4 * g + r
            colb = jnp.broadcast_to(cum[:, h:h + 1], (CHUNK, CHUNK))
            rowb = jnp.broadcast_to(cum_t[h:h + 1, :], (CHUNK, CHUNK))
            dec = jnp.exp(jnp.where(causal, colb - rowb, -jnp.inf))
            m = cb * dec * jnp.broadcast_to(dt_t[h:h + 1, :], (CHUNK, CHUNK))
            xh = xs[:, 64 * h:64 * (h + 1)]
            yh = _bdot(m, xh) + cs[:, 64 * r:64 * (r + 1)] * jnp.exp(colb)[:, :64]
            ys.append(yh)
            xw_parts.append(xs_t[64 * h:64 * (h + 1), :] * jnp.broadcast_to(w_t[h:h + 1, :], (64, CHUNK)))
            dec_parts.append(jnp.broadcast_to(dec_end[h:h + 1, :], (64, SSD_STATE)))
        xw = jnp.concatenate(xw_parts, axis=0)
        s_ref[256 * g:256 * (g + 1), :] = s_g * jnp.concatenate(dec_parts, axis=0) + _bdot(xw, bg)

    y = jnp.concatenate(ys, axis=1) + dsk_ref[...] * xs
    y = y * _silu(z_ref[rows, :].astype(F32))
    y_ref[rows, :] = _rms(y, nrm_ref[...]).astype(y_ref.dtype)


def _ssd_prompt_kernel(xbc_ref, z_ref, dt_ref, cw_ref, cb_ref, dtb_ref, alog_ref, dsk_ref, nrm_ref,
                       y_ref, sfin_ref, cfin_ref, xpad_ref, s_ref):
    t = pl.program_id(0)

    @pl.when(t == 0)
    def _():
        xpad_ref[0:8, :] = jnp.zeros((8, SSD_CONV_DIM), F32)
        s_ref[...] = jnp.zeros_like(s_ref)

    def chunk(ci, carry):
        _ssd_prompt_chunk(_chunk_rows(ci), xbc_ref, z_ref, dt_ref, cw_ref, cb_ref, dtb_ref, alog_ref, dsk_ref,
                          nrm_ref, y_ref, xpad_ref, s_ref)
        return carry

    lax.fori_loop(0, xbc_ref.shape[0] // CHUNK, chunk, 0)

    @pl.when(t == pl.num_programs(0) - 1)
    def _():
        sfin_ref[...] = s_ref[...]
        cfin_ref[...] = xpad_ref[8 + CHUNK - (SSD_CONV - 1):8 + CHUNK, :]


def _ssd_prompt(proj, dt, cw, cb, dtb, alog, dsk, nrm):
    seq = proj.shape[0]
    const = lambda shape: pl.BlockSpec(shape, lambda t: (0,) * len(shape))
    return pl.pallas_call(
        _ssd_prompt_kernel,
        out_shape=(jax.ShapeDtypeStruct((seq, 1024), BF16),
                   jax.ShapeDtypeStruct((1024, SSD_STATE), F32),
                   jax.ShapeDtypeStruct((SSD_CONV - 1, SSD_CONV_DIM), F32)),
        grid=(seq // MIX_ROWS,),
        in_specs=[pl.BlockSpec((MIX_ROWS, 2048), lambda t: (t, COL_XBC)),
                  pl.BlockSpec((MIX_ROWS, 1024), lambda t: (t, COL_Z)),
                  pl.BlockSpec((MIX_ROWS, LANES), lambda t: (t, 0)),
                  const((SSD_CONV, 2048)), const((1, 2048)), const((1, 128)), const((1, 128)),
                  const((1, 1024)), const((1, 1024))],
        out_specs=(pl.BlockSpec((MIX_ROWS, 1024), lambda t: (t, 0)),
                   const((1024, SSD_STATE)), const((SSD_CONV - 1, SSD_CONV_DIM))),
        scratch_shapes=[pltpu.VMEM((8 + CHUNK, SSD_CONV_DIM), F32), pltpu.VMEM((1024, SSD_STATE), F32)],
        compiler_params=_cparams(("arbitrary",)),
        name="ssd_prompt",
    )(proj, proj, dt, cw, cb, dtb, alog, dsk, nrm)


def _ssd_dec_pre_kernel(xbc_ref, dt_ref, cst_ref, cw_ref, cb_ref, dtb_ref, alog_ref,
                        cnew_ref, xs_ref, bc_ref, dec_t_ref, xdt_t_ref):
    xbc = xbc_ref[...]
    cw = cw_ref[...]
    acc = (cst_ref[0] * cw[0:1, :] + cst_ref[1] * cw[1:2, :] + cst_ref[2] * cw[2:3, :]
           + xbc * cw[3:4, :] + cb_ref[...])
    conv = _silu(acc)
    cnew_ref[0] = cst_ref[1]
    cnew_ref[1] = cst_ref[2]
    cnew_ref[2] = xbc
    xs = conv[:, :1024]
    xs_ref[...] = xs
    bc_ref[...] = conv[:, 1024:]
    dt = _softplus(dt_ref[...] + dtb_ref[...])
    dec = jnp.exp(dt * (-jnp.exp(alog_ref[...])))
    hrow = lax.broadcasted_iota(jnp.int32, (LANES, 1024), 0)
    hcol = lax.broadcasted_iota(jnp.int32, (LANES, 1024), 1)
    expand = ((hcol >> 6) == hrow).astype(F32)
    dec_t_ref[...] = _dot_exact_rhs01(dec, expand).T
    xdt_t_ref[...] = (xs * _dot_exact_rhs01(dt, expand)).T


def _ssd_dec_pre(proj, dt, cst_t, cw, cb, dtb, alog):
    nb = proj.shape[0]
    const = lambda shape: pl.BlockSpec(shape, lambda t: (0,) * len(shape))
    return pl.pallas_call(
        _ssd_dec_pre_kernel,
        out_shape=(jax.ShapeDtypeStruct((3, nb, 2048), F32), jax.ShapeDtypeStruct((nb, 1024), F32),
                   jax.ShapeDtypeStruct((nb, 1024), F32), jax.ShapeDtypeStruct((1024, nb), F32),
                   jax.ShapeDtypeStruct((1024, nb), F32)),
        grid=(1,),
        in_specs=[pl.BlockSpec((nb, 2048), lambda t: (0, COL_XBC)),
                  pl.BlockSpec((nb, LANES), lambda t: (0, 0)),
                  const((3, nb, 2048)), const((SSD_CONV, 2048)), const((1, 2048)), const((1, 128)), const((1, 128))],
        out_specs=(const((3, nb, 2048)), const((nb, 1024)), const((nb, 1024)), const((1024, nb)), const((1024, nb))),
        compiler_params=_cparams(("arbitrary",)),
        name="ssd_dec_pre",
    )(proj, dt, cst_t, cw, cb, dtb, alog)


DEC_BLOCK = 8


def _ssd_dec_state_kernel(dec_t_ref, xdt_t_ref, bc_ref, z_ref, xs_ref, st_ref, dsk_ref, nrm_ref,
                          y_ref, stn_ref):
    i = pl.program_id(0)
    shift = (LANES - i * DEC_BLOCK) % LANES
    decr = pltpu.roll(dec_t_ref[...], shift, axis=1)
    xr = pltpu.roll(xdt_t_ref[...], shift, axis=1)
    bc = bc_ref[...]
    rowid = lax.broadcasted_iota(jnp.int32, (DEC_BLOCK, SSD_STATE), 0)
    ys = [jnp.zeros((DEC_BLOCK, 256), F32) for _ in range(SSD_GROUPS)]
    for j in range(DEC_BLOCK):
        for g in range(SSD_GROUPS):
            lo, hi = 256 * g, 256 * (g + 1)
            s_old = st_ref[j, lo:hi, :]
            dcol = jnp.broadcast_to(decr[lo:hi, j:j + 1], (256, SSD_STATE))
            xcol = jnp.broadcast_to(xr[lo:hi, j:j + 1], (256, SSD_STATE))
            s_new = s_old * dcol + xcol * bc[j:j + 1, 128 * g:128 * (g + 1)]
            stn_ref[j, lo:hi, :] = s_new
            cm = jnp.where(rowid == j, bc[:, 512 + 128 * g:512 + 128 * (g + 1)], 0.0)
            ys[g] = ys[g] + _bdot_nt(cm, s_new)
    xs = xs_ref[...]
    y = jnp.concatenate(ys, axis=1) + dsk_ref[...] * xs
    y = y * _silu(z_ref[...])
    y_ref[...] = _rms(y, nrm_ref[...])


class _LayerCall:
    def __init__(self, layer, steps):
        self.layer, self.steps, self.first = layer, steps, layer == 0
        self.grid = (DEPTH * steps,) if self.first else (steps,)

    def _block(self, t):
        return jnp.minimum(t, self.steps - 1) if self.first else t

    def rows(self, width, col=0):
        return pl.BlockSpec((DEC_BLOCK, width), lambda t: (self._block(t), col))

    def state_in(self, tail):
        zeros = (0,) * len(tail)
        return pl.BlockSpec((None, DEC_BLOCK) + tail, lambda t: (self.layer, self._block(t)) + zeros)

    def state_out(self, tail):
        zeros = (0,) * len(tail)
        if self.first:
            return pl.BlockSpec((None, DEC_BLOCK) + tail, lambda t: (t // self.steps, t % self.steps) + zeros)
        return pl.BlockSpec((None, DEC_BLOCK) + tail, lambda t: (self.layer, t) + zeros)

    def kernel(self, body, n_in, n_carried, stacked_outs):
        def wrapped(*refs):
            refs = refs[:n_in] + refs[n_in + n_carried:]
            if not self.first:
                body(*refs)
                return
            t = pl.program_id(0)

            @pl.when(t < self.steps)
            def _():
                body(*refs)

            @pl.when(t >= self.steps)
            def _():
                for k in stacked_outs:
                    refs[n_in + k][...] = jnp.zeros_like(refs[n_in + k])
        return wrapped


def _ssd_dec_state(layer, dec_t, xdt_t, bc, proj, xs, st_all, dsk, nrm, carried):
    nb = xs.shape[0]
    lc = _LayerCall(layer, nb // DEC_BLOCK)
    const = lambda shape: pl.BlockSpec(shape, lambda t: (0,) * len(shape))
    tail = (1024, SSD_STATE)
    ins = [dec_t, xdt_t, bc, proj, xs, st_all, dsk, nrm]
    in_specs = [const((1024, nb)), const((1024, nb)), lc.rows(1024), lc.rows(1024, COL_Z), lc.rows(1024),
                lc.state_in(tail), const((1, 1024)), const((1, 1024))]
    n_in = len(ins)
    carried = [] if carried is None else [carried]
    return pl.pallas_call(
        lc.kernel(_ssd_dec_state_kernel, n_in, len(carried), (1,)),
        out_shape=(jax.ShapeDtypeStruct((nb, 1024), F32), jax.ShapeDtypeStruct(st_all.shape, F32)),
        grid=lc.grid,
        in_specs=in_specs + [pl.BlockSpec(memory_space=pl.ANY)] * len(carried),
        out_specs=(lc.rows(1024), lc.state_out(tail)),
        input_output_aliases={n_in + k: 1 + k for k in range(len(carried))},
        compiler_params=_cparams(("arbitrary",)),
        name="ssd_dec_state",
    )(*ins, *carried)


def _block_cos_sin(within_ref, step_ref, bi):
    row = step_ref[pl.ds(bi, 1), :]
    cb, sb = row[:, :LANES], row[:, LANES:]
    ci, si = within_ref[:, :LANES], within_ref[:, LANES:]
    return cb * ci - sb * si, sb * ci + cb * si


def _swa_prompt_block(bi, has_prev, sink_ref, q_ref, k_ref, v_ref, within_ref, step_ref, o_ref, kp_ref, vp_ref):
    rows = _chunk_rows(bi)
    cosp, sinp = _block_cos_sin(within_ref, step_ref, bi)
    l64 = lax.broadcasted_iota(jnp.int32, (1, LANES), 1) & (ATT_HEAD_DIM - 1)
    c = jnp.where(l64 < ROPE_DIM, cosp, 1.0)
    s1 = jnp.where(jnp.logical_and(l64 >= ROPE_DIM // 2, l64 < ROPE_DIM), sinp, 0.0)
    s2 = jnp.where(l64 < ROPE_DIM // 2, -sinp, 0.0)
    q = _rope_att(q_ref[rows, :].astype(F32), _tile_lanes(c, 8), _tile_lanes(s1, 8), _tile_lanes(s2, 8))
    k = _rope_att(k_ref[rows, :].astype(F32), _tile_lanes(c, 2), _tile_lanes(s1, 2), _tile_lanes(s2, 2))
    v = v_ref[rows, :].astype(F32)
    ghead = lax.broadcasted_iota(jnp.int32, (WINDOW, 256), 1) >> 6
    expand = lambda t: jnp.concatenate([jnp.where(ghead == g, t, 0.0) for g in range(ATT_KV_HEADS)],
                                       axis=0).astype(BF16)
    kbd, vbd = expand(k), expand(v)
    kbd_prev, vbd_prev = expand(kp_ref[...]), expand(vp_ref[...])
    qall = jnp.concatenate([q[:, 256 * r:256 * (r + 1)] for r in range(ATT_REP)], axis=0).astype(BF16)
    scale = ATT_HEAD_DIM ** -0.5
    nt_dims = (((1,), (1,)), ((), ()))
    sp_all = lax.dot_general(qall, kbd_prev, nt_dims, preferred_element_type=F32) * scale
    sc_all = lax.dot_general(qall, kbd, nt_dims, preferred_element_type=F32) * scale
    nq = ATT_REP * WINDOW
    qi = lax.broadcasted_iota(jnp.int32, (nq, WINDOW), 0) & (WINDOW - 1)
    kj = lax.broadcasted_iota(jnp.int32, (nq, WINDOW), 1)
    mask_prev = jnp.logical_and(kj > qi, has_prev)
    mask_cur = kj <= qi
    rep = lax.broadcasted_iota(jnp.int32, (nq, 1), 0) >> 7
    pp, pc = [], []
    for g in range(ATT_KV_HEADS):
        sp = jnp.where(mask_prev, sp_all[:, WINDOW * g:WINDOW * (g + 1)], -jnp.inf)
        sc = jnp.where(mask_cur, sc_all[:, WINDOW * g:WINDOW * (g + 1)], -jnp.inf)
        sink = jnp.where(rep == 0, sink_ref[4 * g],
                         jnp.where(rep == 1, sink_ref[4 * g + 1],
                                   jnp.where(rep == 2, sink_ref[4 * g + 2], sink_ref[4 * g + 3])))
        m = jnp.maximum(jnp.max(jnp.maximum(sp, sc), axis=1, keepdims=True), sink)
        ep = jnp.exp(sp - m)
        ec = jnp.exp(sc - m)
        inv = 1.0 / (jnp.sum(ep + ec, axis=1, keepdims=True) + jnp.exp(sink - m))
        pp.append((ep * inv).astype(BF16))
        pc.append((ec * inv).astype(BF16))
    o = (jnp.dot(jnp.concatenate(pp, axis=1), vbd_prev, preferred_element_type=F32)
         + jnp.dot(jnp.concatenate(pc, axis=1), vbd, preferred_element_type=F32))
    for r in range(ATT_REP):
        o_ref[rows, 256 * r:256 * (r + 1)] = o[WINDOW * r:WINDOW * (r + 1), :].astype(o_ref.dtype)
    kp_ref[...] = k
    vp_ref[...] = v


def _swa_prompt_kernel(sink_ref, q_ref, k_ref, v_ref, within_ref, step_ref,
                       o_ref, wk_ref, wv_ref, kp_ref, vp_ref):
    n = pl.program_id(0)
    blocks = q_ref.shape[0] // WINDOW

    @pl.when(n == 0)
    def _():
        kp_ref[...] = jnp.zeros_like(kp_ref)
        vp_ref[...] = jnp.zeros_like(vp_ref)

    def block(bi, carry):
        _swa_prompt_block(bi, n * blocks + bi > 0, sink_ref, q_ref, k_ref, v_ref,
                          within_ref, step_ref, o_ref, kp_ref, vp_ref)
        return carry

    lax.fori_loop(0, blocks, block, 0)

    @pl.when(n == pl.num_programs(0) - 1)
    def _():
        wk_ref[...] = kp_ref[...]
        wv_ref[...] = vp_ref[...]


def _swa_prompt(sinks, proj, tabs):
    seq = proj.shape[0]
    within, steps = tabs
    const = lambda shape: pl.BlockSpec(shape, lambda t: (0,) * len(shape))
    return pl.pallas_call(
        _swa_prompt_kernel,
        out_shape=(jax.ShapeDtypeStruct((seq, 1024), BF16),
                   jax.ShapeDtypeStruct((WINDOW, 256), F32), jax.ShapeDtypeStruct((WINDOW, 256), F32)),
        grid=(seq // MIX_ROWS,),
        in_specs=[pl.BlockSpec(memory_space=pltpu.SMEM),
                  pl.BlockSpec((MIX_ROWS, 1024), lambda t: (t, COL_QA)),
                  pl.BlockSpec((MIX_ROWS, 256), lambda t: (t, COL_KA)),
                  pl.BlockSpec((MIX_ROWS, 256), lambda t: (t, COL_VA)),
                  const((CHUNK, 2 * LANES)), pl.BlockSpec((None, 8, 2 * LANES), lambda t: (t, 0, 0))],
        out_specs=(pl.BlockSpec((MIX_ROWS, 1024), lambda t: (t, 0)), const((WINDOW, 256)), const((WINDOW, 256))),
        scratch_shapes=[pltpu.VMEM((WINDOW, 256), F32), pltpu.VMEM((WINDOW, 256), F32)],
        compiler_params=_cparams(("arbitrary",)),
        name="swa_prompt",
    )(sinks, proj, proj, proj, within, steps)


def _swa_dec_kernel(q_ref, k_ref, v_ref, kc_ref, vc_ref, c_ref, s1_ref, s2_ref, sink_ref,
                    o_ref, kcn_ref, vcn_ref):
    c, s1, s2 = c_ref[...], s1_ref[...], s2_ref[...]
    q = _rope_att(q_ref[...], _tile_lanes(c, 8), _tile_lanes(s1, 8), _tile_lanes(s2, 8))
    k = _rope_att(k_ref[...], _tile_lanes(c, 2), _tile_lanes(s1, 2), _tile_lanes(s2, 2))
    v = v_ref[...]
    grow = lax.broadcasted_iota(jnp.int32, (8, 256), 0)
    gcol = lax.broadcasted_iota(jnp.int32, (8, 256), 1)
    gmask = ((gcol >> 6) == grow).astype(F32)
    gmask4 = jnp.concatenate([gmask] * ATT_REP, axis=0)
    lane = lax.broadcasted_iota(jnp.int32, (8 * ATT_REP, WINDOW), 1)
    sink = sink_ref[...][:, 0:1]
    scale = ATT_HEAD_DIM ** -0.5
    for j in range(DEC_BLOCK):
        a = jnp.concatenate([jnp.broadcast_to(q[j:j + 1, 256 * r:256 * (r + 1)], (8, 256)) * gmask
                             for r in range(ATT_REP)], axis=0)
        kb = kc_ref[j]
        vb = vc_ref[j]
        knew = k[j:j + 1, :]
        vnew = v[j:j + 1, :]
        s = jnp.where(lane == 0, -jnp.inf, _bdot_nt(a, kb) * scale)
        snew = jnp.sum(a * knew, axis=1, keepdims=True) * scale
        m = jnp.maximum(jnp.maximum(jnp.max(s, axis=1, keepdims=True), snew), sink)
        e = jnp.exp(s - m)
        enew = jnp.exp(snew - m)
        inv = 1.0 / (jnp.sum(e, axis=1, keepdims=True) + enew + jnp.exp(sink - m))
        o = (_bdot(e * inv, vb) + (enew * inv) * vnew) * gmask4
        for r in range(ATT_REP):
            o_ref[j:j + 1, 256 * r:256 * (r + 1)] = jnp.sum(o[8 * r:8 * (r + 1), :], axis=0, keepdims=True)
        kcn_ref[j, 0:WINDOW - 1, :] = kc_ref[j, 1:WINDOW, :]
        kcn_ref[j, WINDOW - 1:WINDOW, :] = knew
        vcn_ref[j, 0:WINDOW - 1, :] = vc_ref[j, 1:WINDOW, :]
        vcn_ref[j, WINDOW - 1:WINDOW, :] = vnew


def _swa_dec(layer, proj, kc_all, vc_all, tabs, sink32, carried):
    nb = proj.shape[0]
    lc = _LayerCall(layer, nb // DEC_BLOCK)
    c, s1, s2 = tabs
    const = lambda shape: pl.BlockSpec(shape, lambda t: (0,) * len(shape))
    tail = (WINDOW, 256)
    ins = [proj, proj, proj, kc_all, vc_all, c, s1, s2, sink32]
    in_specs = [lc.rows(1024, COL_QA), lc.rows(256, COL_KA), lc.rows(256, COL_VA),
                lc.state_in(tail), lc.state_in(tail),
                const((1, LANES)), const((1, LANES)), const((1, LANES)), const((32, LANES))]
    n_in = len(ins)
    carried = [] if carried is None else list(carried)
    return pl.pallas_call(
        lc.kernel(_swa_dec_kernel, n_in, len(carried), (1, 2)),
        out_shape=(jax.ShapeDtypeStruct((nb, 1024), F32),
                   jax.ShapeDtypeStruct(kc_all.shape, F32), jax.ShapeDtypeStruct(vc_all.shape, F32)),
        grid=lc.grid,
        in_specs=in_specs + [pl.BlockSpec(memory_space=pl.ANY)] * len(carried),
        out_specs=(lc.rows(1024), lc.state_out(tail), lc.state_out(tail)),
        input_output_aliases={n_in + k: 1 + k for k in range(len(carried))},
        compiler_params=_cparams(("arbitrary",)),
        name="swa_dec",
    )(*ins, *carried)


def _ret_prompt_chunk(ci, lg_ref, q_ref, k_ref, v_ref, gr_ref, within_ref, step_ref, nrm_ref,
                      o_ref, st_ref, intra_ref, fs_ref, te_ref):
    rows = _chunk_rows(ci)
    c, sinp = _block_cos_sin(within_ref, step_ref, ci)
    s = jnp.where(lax.broadcasted_iota(jnp.int32, (1, LANES), 1) < RET_DK // 2, -sinp, sinp)
    q = q_ref[rows, :].astype(F32)
    k = k_ref[rows, :].astype(F32)
    v = v_ref[rows, :].astype(F32)
    gr = gr_ref[rows, :].astype(F32)
    nrm = nrm_ref[...]
    for h in range(RET_HEADS):
        sl = slice(128 * h, 128 * (h + 1))
        qh = q[:, sl]
        kh = k[:, sl]
        qh = qh * c + pltpu.roll(qh, 64, axis=1) * s
        kh = (kh * c + pltpu.roll(kh, 64, axis=1) * s) * (RET_DK ** -0.5)
        vh = v[:, sl]
        att = _bdot_nt(qh, kh) * intra_ref[h]
        s_old = st_ref[h]
        o = _bdot(att, vh) + _bdot(qh, s_old) * fs_ref[h]
        cd = jnp.exp(jnp.zeros((1, RET_DK), F32) + CHUNK * lg_ref[h])
        st_ref[h] = s_old * cd + _bdot((kh * te_ref[h]).T, vh)
        o = o * lax.rsqrt(jnp.mean(o * o, axis=-1, keepdims=True) + EPS)
        o_ref[rows, sl] = (o * nrm[:, sl] * _silu(gr[:, sl])).astype(o_ref.dtype)


def _ret_prompt_kernel(lg_ref, q_ref, k_ref, v_ref, gr_ref, within_ref, step_ref, nrm_ref,
                       o_ref, sfin_ref, st_ref, intra_ref, fs_ref, te_ref):
    t = pl.program_id(0)

    @pl.when(t == 0)
    def _():
        st_ref[...] = jnp.zeros_like(st_ref)
        ri = lax.broadcasted_iota(jnp.int32, (CHUNK, CHUNK), 0).astype(F32)
        ci = lax.broadcasted_iota(jnp.int32, (CHUNK, CHUNK), 1).astype(F32)
        rel = ri - ci
        for h in range(RET_HEADS):
            lg = lg_ref[h]
            intra_ref[h] = jnp.exp(jnp.where(rel >= 0, rel * lg, -jnp.inf))
            fs_ref[h] = jnp.exp((ri + 1.0) * lg)
            te_ref[h] = jnp.exp((CHUNK - 1.0 - ri) * lg)

    def chunk(ci, carry):
        _ret_prompt_chunk(ci, lg_ref, q_ref, k_ref, v_ref, gr_ref, within_ref, step_ref, nrm_ref,
                          o_ref, st_ref, intra_ref, fs_ref, te_ref)
        return carry

    lax.fori_loop(0, q_ref.shape[0] // CHUNK, chunk, 0)

    @pl.when(t == pl.num_programs(0) - 1)
    def _():
        sfin_ref[...] = st_ref[...]


def _ret_prompt(log_gamma, proj, tabs, nrm):
    seq = proj.shape[0]
    within, steps = tabs
    const = lambda shape: pl.BlockSpec(shape, lambda t: (0,) * len(shape))
    col = lambda cidx: pl.BlockSpec((MIX_ROWS, 1024), lambda t: (t, cidx))
    tbl = pltpu.VMEM((RET_HEADS, CHUNK, CHUNK), F32)
    return pl.pallas_call(
        _ret_prompt_kernel,
        out_shape=(jax.ShapeDtypeStruct((seq, 1024), BF16), jax.ShapeDtypeStruct((RET_HEADS, RET_DK, 128), F32)),
        grid=(seq // MIX_ROWS,),
        in_specs=[pl.BlockSpec(memory_space=pltpu.SMEM), col(COL_QR), col(COL_KR), col(COL_VR), col(COL_GR),
                  const((CHUNK, 2 * LANES)), pl.BlockSpec((None, 8, 2 * LANES), lambda t: (t, 0, 0)), const((1, 1024))],
        out_specs=(pl.BlockSpec((MIX_ROWS, 1024), lambda t: (t, 0)), const((RET_HEADS, RET_DK, 128))),
        scratch_shapes=[tbl, tbl, tbl, tbl],
        compiler_params=_cparams(("arbitrary",)),
        name="ret_prompt",
    )(log_gamma, proj, proj, proj, proj, within, steps, nrm)


def _ret_dec_pre_kernel(q_ref, k_ref, c_ref, s_ref, qrot_ref, kt_ref):
    c = c_ref[...]
    s = s_ref[...]
    q = q_ref[...]
    k = k_ref[...]
    ks = []
    for h in range(RET_HEADS):
        sl = slice(128 * h, 128 * (h + 1))
        qh = q[:, sl]
        kh = k[:, sl]
        qrot_ref[:, sl] = qh * c + pltpu.roll(qh, 64, axis=1) * s
        ks.append((kh * c + pltpu.roll(kh, 64, axis=1) * s) * (RET_DK ** -0.5))
    kt_ref[...] = jnp.concatenate(ks, axis=1).T


def _ret_dec_pre(proj, tabs):
    nb = proj.shape[0]
    c, s = tabs
    const = lambda shape: pl.BlockSpec(shape, lambda t: (0,) * len(shape))
    return pl.pallas_call(
        _ret_dec_pre_kernel,
        out_shape=(jax.ShapeDtypeStruct((nb, 1024), F32), jax.ShapeDtypeStruct((1024, nb), F32)),
        grid=(1,),
        in_specs=[pl.BlockSpec((nb, 1024), lambda t: (0, COL_QR)), pl.BlockSpec((nb, 1024), lambda t: (0, COL_KR)),
                  const((1, LANES)), const((1, LANES))],
        out_specs=(const((nb, 1024)), const((1024, nb))),
        compiler_params=_cparams(("arbitrary",)),
        name="ret_dec_pre",
    )(proj, proj, c, s)


def _ret_dec_state_kernel(gam_ref, kt_ref, q_ref, v_ref, gr_ref, st_ref, nrm_ref, o_ref, stn_ref):
    i = pl.program_id(0)
    shift = (LANES - i * DEC_BLOCK) % LANES
    kr = pltpu.roll(kt_ref[...], shift, axis=1)
    q = q_ref[...]
    v = v_ref[...]
    rowid = lax.broadcasted_iota(jnp.int32, (DEC_BLOCK, RET_DK), 0)
    os_ = [jnp.zeros((DEC_BLOCK, 128), F32) for _ in range(RET_HEADS)]
    for j in range(DEC_BLOCK):
        for h in range(RET_HEADS):
            lo, hi = 128 * h, 128 * (h + 1)
            kcol = jnp.broadcast_to(kr[lo:hi, j:j + 1], (RET_DK, 128))
            s_new = st_ref[j, lo:hi, :] * gam_ref[h] + kcol * v[j:j + 1, lo:hi]
            stn_ref[j, lo:hi, :] = s_new
            qm = jnp.where(rowid == j, q[:, lo:hi], 0.0)
            os_[h] = os_[h] + _bdot(qm, s_new)
    gr = gr_ref[...]
    nrm = nrm_ref[...]
    for h in range(RET_HEADS):
        sl = slice(128 * h, 128 * (h + 1))
        o = os_[h]
        o = o * lax.rsqrt(jnp.mean(o * o, axis=-1, keepdims=True) + EPS)
        o_ref[:, sl] = o * nrm[:, sl] * _silu(gr[:, sl])


def _ret_dec_state(layer, gam, kt, qrot, proj, st_all, nrm, carried):
    nb = qrot.shape[0]
    lc = _LayerCall(layer, nb // DEC_BLOCK)
    const = lambda shape: pl.BlockSpec(shape, lambda t: (0,) * len(shape))
    tail = (1024, 128)
    ins = [gam, kt, qrot, proj, proj, st_all, nrm]
    in_specs = [pl.BlockSpec(memory_space=pltpu.SMEM), const((1024, nb)), lc.rows(1024),
                lc.rows(1024, COL_VR), lc.rows(1024, COL_GR), lc.state_in(tail), const((1, 1024))]
    n_in = len(ins)
    carried = [] if carried is None else [carried]
    return pl.pallas_call(
        lc.kernel(_ret_dec_state_kernel, n_in, len(carried), (1,)),
        out_shape=(jax.ShapeDtypeStruct((nb, 1024), F32), jax.ShapeDtypeStruct(st_all.shape, F32)),
        grid=lc.grid,
        in_specs=in_specs + [pl.BlockSpec(memory_space=pl.ANY)] * len(carried),
        out_specs=(lc.rows(1024), lc.state_out(tail)),
        input_output_aliases={n_in + k: 1 + k for k in range(len(carried))},
        compiler_params=_cparams(("arbitrary",)),
        name="ret_dec_state",
    )(*ins, *carried)


def _merge_kernel(x_ref, a_ref, b_ref, c_ref, g1_ref, g2_ref, g3_ref, w1_ref, w2_ref, w3_ref, wo_ref, o_ref):
    gate = lambda ref: _sigmoid(ref[...].astype(F32))
    m = (gate(g1_ref) * jnp.dot(a_ref[...].astype(BF16), w1_ref[...], preferred_element_type=F32)
         + gate(g2_ref) * jnp.dot(b_ref[...].astype(BF16), w2_ref[...], preferred_element_type=F32)
         + gate(g3_ref) * jnp.dot(c_ref[...].astype(BF16), w3_ref[...], preferred_element_type=F32))
    o_ref[...] = x_ref[...] + jnp.dot(m.astype(BF16), wo_ref[...], preferred_element_type=F32)


def _merge(x, a, b, c, proj, w1, w2, w3, wo):
    rows = x.shape[0]
    tm = min(rows, 512)
    rowb = pl.BlockSpec((tm, 1024), lambda i: (i, 0))
    gate = lambda k: pl.BlockSpec((tm, 1024), lambda i: (i, COL_GATE + k))
    wsp = pl.BlockSpec((1024, 1024), lambda i: (0, 0))
    return pl.pallas_call(
        _merge_kernel,
        out_shape=jax.ShapeDtypeStruct((rows, 1024), F32),
        grid=(rows // tm,),
        in_specs=[rowb, rowb, rowb, rowb, gate(0), gate(1), gate(2), wsp, wsp, wsp, wsp],
        out_specs=rowb,
        compiler_params=_cparams(("parallel",)),
        name="merge",
    )(x, a, b, c, proj, proj, proj, w1, w2, w3, wo)


FF_TILE = 1408
FF_SPLIT = 768


def _ffn_kernel(x_ref, g_ref, wg_ref, wu_ref, wd_ref, o_ref, h_ref, acc_ref):
    j = pl.program_id(1)

    @pl.when(j == 0)
    def _():
        h_ref[...] = _rms(x_ref[...], g_ref[...]).astype(BF16)
        acc_ref[...] = jnp.zeros_like(acc_ref)

    h = h_ref[...]
    for lo, hi in ((0, FF_SPLIT), (FF_SPLIT, FF_TILE)):
        a = jnp.dot(h, wg_ref[:, lo:hi], preferred_element_type=F32)
        u = jnp.dot(h, wu_ref[:, lo:hi], preferred_element_type=F32)
        acc_ref[...] += jnp.dot((_silu(a) * u).astype(BF16), wd_ref[lo:hi, :], preferred_element_type=F32)

    @pl.when(j == pl.num_programs(1) - 1)
    def _():
        o_ref[...] = x_ref[...] + acc_ref[...]


def _ffn(x, g, wg, wu, wd):
    rows = x.shape[0]
    tm = min(rows, 1024)
    return pl.pallas_call(
        _ffn_kernel,
        out_shape=jax.ShapeDtypeStruct((rows, 1024), F32),
        grid=(rows // tm, D_FF // FF_TILE),
        in_specs=[pl.BlockSpec((tm, 1024), lambda i, j: (i, 0)), pl.BlockSpec((1, 1024), lambda i, j: (0, 0)),
                  pl.BlockSpec((1024, FF_TILE), lambda i, j: (0, j)), pl.BlockSpec((1024, FF_TILE), lambda i, j: (0, j)),
                  pl.BlockSpec((FF_TILE, 1024), lambda i, j: (j, 0))],
        out_specs=pl.BlockSpec((tm, 1024), lambda i, j: (i, 0)),
        scratch_shapes=[pltpu.VMEM((tm, 1024), BF16), pltpu.VMEM((tm, 1024), F32)],
        compiler_params=_cparams(("parallel", "arbitrary"), vmem=VMEM_LIMIT_FFN),
        name="ffn",
    )(x, g, wg, wu, wd)


MOE_FF_TILE = 256


def _top2(h, rw_hi, rw_lo, rb, lane):
    h_hi = h.astype(BF16)
    h_lo = (h - h_hi.astype(F32)).astype(BF16)
    d = lambda a, b: jnp.dot(a, b, preferred_element_type=F32)
    logits = d(h_hi, rw_hi) + d(h_hi, rw_lo) + d(h_lo, rw_hi) + rb
    logits = jnp.where(lane < N_EXPERTS, logits, -jnp.inf)
    m1 = jnp.max(logits, axis=1, keepdims=True)
    i1 = jnp.min(jnp.where(logits == m1, lane, float(LANES)), axis=1, keepdims=True)
    rest = jnp.where(lane == i1, -jnp.inf, logits)
    m2 = jnp.max(rest, axis=1, keepdims=True)
    i2 = jnp.min(jnp.where(rest == m2, lane, float(LANES)), axis=1, keepdims=True)
    e2 = jnp.exp(m2 - m1)
    p1 = 1.0 / (1.0 + e2)
    return i1, i2, p1, e2 * p1


def _moe_kernel(x_ref, g_ref, rw_ref, rb_ref, wg_ref, wu_ref, wd_ref, o_ref, h_ref, acc_ref, comb_ref):
    e = pl.program_id(1)
    j = pl.program_id(2)
    tm = x_ref.shape[0]
    lane = lax.broadcasted_iota(jnp.int32, (tm, LANES), 1).astype(F32)

    @pl.when(jnp.logical_and(e == 0, j == 0))
    def _():
        h = _rms(x_ref[...], g_ref[...])
        h_ref[...] = h.astype(BF16)
        i1, i2, p1, p2 = _top2(h, rw_ref[0], rw_ref[1], rb_ref[...], lane)
        comb_ref[...] = jnp.where(lane == i1, p1, 0.0) + jnp.where(lane == i2, p2, 0.0)
        acc_ref[...] = jnp.zeros_like(acc_ref)

    ce = jnp.sum(jnp.where(lane == e.astype(F32), comb_ref[...], 0.0), axis=1, keepdims=True)
    h = h_ref[...]
    a = jnp.dot(h, wg_ref[0], preferred_element_type=F32)
    u = jnp.dot(h, wu_ref[0], preferred_element_type=F32)
    acc_ref[...] += ce * jnp.dot((_silu(a) * u).astype(BF16), wd_ref[0], preferred_element_type=F32)

    @pl.when(jnp.logical_and(e == pl.num_programs(1) - 1, j == pl.num_programs(2) - 1))
    def _():
        o_ref[...] = x_ref[...] + acc_ref[...]


def _moe(x, g, rw, rb, wg, wu, wd):
    rows = x.shape[0]
    tm = min(rows, 1024)
    tf = MOE_FF_TILE
    return pl.pallas_call(
        _moe_kernel,
        out_shape=jax.ShapeDtypeStruct((rows, 1024), F32),
        grid=(rows // tm, N_EXPERTS, D_FF // tf),
        in_specs=[pl.BlockSpec((tm, 1024), lambda i, e, j: (i, 0)), pl.BlockSpec((1, 1024), lambda i, e, j: (0, 0)),
                  pl.BlockSpec((2, 1024, LANES), lambda i, e, j: (0, 0, 0)),
                  pl.BlockSpec((1, LANES), lambda i, e, j: (0, 0)),
                  pl.BlockSpec((1, 1024, tf), lambda i, e, j: (e, 0, j)),
                  pl.BlockSpec((1, 1024, tf), lambda i, e, j: (e, 0, j)),
                  pl.BlockSpec((1, tf, 1024), lambda i, e, j: (e, j, 0))],
        out_specs=pl.BlockSpec((tm, 1024), lambda i, e, j: (i, 0)),
        scratch_shapes=[pltpu.VMEM((tm, 1024), BF16), pltpu.VMEM((tm, 1024), F32), pltpu.VMEM((tm, LANES), F32)],
        compiler_params=_cparams(("parallel", "arbitrary", "arbitrary")),
        name="moe",
    )(x, g, rw, rb, wg, wu, wd)


MOE_ROWS = 512
MOE_GROUP_FF = 1408
GATHER_UNROLL = 8


def _router_kernel(x_ref, g_ref, rw_ref, rb_ref, o_ref):
    tm = x_ref.shape[0]
    lane = lax.broadcasted_iota(jnp.int32, (tm, LANES), 1).astype(F32)
    i1, i2, p1, p2 = _top2(_rms(x_ref[...], g_ref[...]), rw_ref[0], rw_ref[1], rb_ref[...], lane)
    o_ref[...] = jnp.where(lane == 0.0, i1, jnp.where(lane == 1.0, i2, jnp.where(lane == 2.0, p1,
                           jnp.where(lane == 3.0, p2, 0.0))))


def _router(x, g, rw, rb):
    rows = x.shape[0]
    tm = min(rows, 1024)
    return pl.pallas_call(
        _router_kernel,
        out_shape=jax.ShapeDtypeStruct((rows, LANES), F32),
        grid=(rows // tm,),
        in_specs=[pl.BlockSpec((tm, 1024), lambda i: (i, 0)), pl.BlockSpec((1, 1024), lambda i: (0, 0)),
                  pl.BlockSpec((2, 1024, LANES), lambda i: (0, 0, 0)), pl.BlockSpec((1, LANES), lambda i: (0, 0))],
        out_specs=pl.BlockSpec((tm, LANES), lambda i: (i, 0)),
        compiler_params=_cparams(("parallel",)),
        name="router",
    )(x, g, rw, rb)


def _route_plan(route, tm):
    n = route.shape[0]
    n_tiles = (2 * n) // tm + N_EXPERTS
    e_flat = route[:, :2].astype(jnp.int32).reshape(-1)
    onehot = (e_flat[:, None] == jnp.arange(N_EXPERTS, dtype=jnp.int32)[None, :]).astype(jnp.int32)
    csum = jnp.cumsum(onehot, axis=0)
    counts = csum[-1]
    tiles_e = (counts + tm - 1) // tm
    tile_end = jnp.cumsum(tiles_e)
    row_start = (tile_end - tiles_e) * tm
    pos = jnp.sum((csum - onehot + row_start[None, :]) * onehot, axis=1).astype(jnp.int32)
    tile_expert = jnp.minimum(jnp.sum(jnp.arange(n_tiles, dtype=jnp.int32)[:, None] >= tile_end[None, :], axis=1),
                              N_EXPERTS - 1).astype(jnp.int32)
    n_used = tile_end[-1:].astype(jnp.int32)
    src = jnp.zeros((n_tiles * tm,), jnp.int32).at[pos].set(jnp.arange(2 * n, dtype=jnp.int32) // 2)
    return pos, src.reshape(n_tiles, 1, tm), tile_expert, n_used


def _moe_group_kernel(te_ref, nu_ref, src_ref, srcn_ref, x_hbm, g_ref, wg_ref, wu_ref, wd_ref,
                      y_ref, buf, sem, h_ref, acc_ref):
    i = pl.program_id(0)
    j = pl.program_id(1)
    tm = buf.shape[1]
    slot = i % 2
    active = i < nu_ref[0]

    def row_copy(idx_ref, s, r):
        return pltpu.make_async_copy(x_hbm.at[pl.ds(idx_ref[0, 0, r], 1), :], buf.at[s, pl.ds(r, 1), :], sem.at[s])

    def gather(idx_ref, s):
        def body(r, c):
            row_copy(idx_ref, s, r).start()
            return c
        lax.fori_loop(0, tm, body, 0, unroll=GATHER_UNROLL)

    @pl.when(jnp.logical_and(active, j == 0))
    def _():
        @pl.when(i == 0)
        def _():
            gather(src_ref, 0)

        pltpu.make_async_copy(x_hbm.at[pl.ds(0, tm), :], buf.at[slot], sem.at[slot]).wait()

        @pl.when(i + 1 < nu_ref[0])
        def _():
            gather(srcn_ref, 1 - slot)

        h_ref[...] = _rms(buf[slot], g_ref[...]).astype(BF16)
        acc_ref[...] = jnp.zeros_like(acc_ref)

    @pl.when(active)
    def _():
        h = h_ref[...]
        a = jnp.dot(h, wg_ref[0], preferred_element_type=F32)
        u = jnp.dot(h, wu_ref[0], preferred_element_type=F32)
        acc_ref[...] += jnp.dot((_silu(a) * u).astype(BF16), wd_ref[0], preferred_element_type=F32)

    @pl.when(j == pl.num_programs(1) - 1)
    def _():
        @pl.when(active)
        def _():
            y_ref[...] = acc_ref[...]

        @pl.when(jnp.logical_not(active))
        def _():
            y_ref[...] = jnp.zeros_like(y_ref)


def _moe_group(tile_expert, n_used, src, x, g, wg, wu, wd):
    n_tiles, _, tm = src.shape
    tf = MOE_GROUP_FF
    grid_spec = pltpu.PrefetchScalarGridSpec(
        num_scalar_prefetch=2,
        grid=(n_tiles, D_FF // tf),
        in_specs=[pl.BlockSpec((1, 1, tm), lambda i, j, te, nu: (i, 0, 0), memory_space=pltpu.SMEM),
                  pl.BlockSpec((1, 1, tm), lambda i, j, te, nu: (jnp.minimum(i + 1, n_tiles - 1), 0, 0),
                               memory_space=pltpu.SMEM),
                  pl.BlockSpec(memory_space=pl.ANY),
                  pl.BlockSpec((1, 1024), lambda i, j, te, nu: (0, 0)),
                  pl.BlockSpec((1, 1024, tf), lambda i, j, te, nu: (te[i], 0, j)),
                  pl.BlockSpec((1, 1024, tf), lambda i, j, te, nu: (te[i], 0, j)),
                  pl.BlockSpec((1, tf, 1024), lambda i, j, te, nu: (te[i], j, 0))],
        out_specs=pl.BlockSpec((tm, 1024), lambda i, j, te, nu: (i, 0)),
        scratch_shapes=[pltpu.VMEM((2, tm, 1024), F32), pltpu.SemaphoreType.DMA((2,)),
                        pltpu.VMEM((tm, 1024), BF16), pltpu.VMEM((tm, 1024), F32)])
    return pl.pallas_call(
        _moe_group_kernel,
        out_shape=jax.ShapeDtypeStruct((n_tiles * tm, 1024), F32),
        grid_spec=grid_spec,
        compiler_params=_cparams(("arbitrary", "arbitrary")),
        name="moe_group",
    )(tile_expert, n_used, src, src, x, g, wg, wu, wd)


def _moe_combine_kernel(pos_ref, posn_ref, x_ref, r_ref, y_hbm, o_ref, bufa, bufb, sem):
    i = pl.program_id(0)
    tm = x_ref.shape[0]
    slot = i % 2

    def gather(idx_ref, s):
        def body(t, c):
            pltpu.make_async_copy(y_hbm.at[pl.ds(idx_ref[0, 0, 2 * t], 1), :], bufa.at[s, pl.ds(t, 1), :],
                                  sem.at[0, s]).start()
            pltpu.make_async_copy(y_hbm.at[pl.ds(idx_ref[0, 0, 2 * t + 1], 1), :], bufb.at[s, pl.ds(t, 1), :],
                                  sem.at[1, s]).start()
            return c
        lax.fori_loop(0, tm, body, 0, unroll=GATHER_UNROLL)

    @pl.when(i == 0)
    def _():
        gather(pos_ref, 0)

    pltpu.make_async_copy(y_hbm.at[pl.ds(0, tm), :], bufa.at[slot], sem.at[0, slot]).wait()
    pltpu.make_async_copy(y_hbm.at[pl.ds(0, tm), :], bufb.at[slot], sem.at[1, slot]).wait()

    @pl.when(i + 1 < pl.num_programs(0))
    def _():
        gather(posn_ref, 1 - slot)

    r = r_ref[...]
    o_ref[...] = x_ref[...] + r[:, 2:3] * bufa[slot] + r[:, 3:4] * bufb[slot]


def _moe_combine(pos, x, route, y):
    rows = x.shape[0]
    tm = MOE_ROWS
    n = rows // tm
    pos3 = pos.reshape(n, 1, 2 * tm)
    return pl.pallas_call(
        _moe_combine_kernel,
        out_shape=jax.ShapeDtypeStruct((rows, 1024), F32),
        grid=(n,),
        in_specs=[pl.BlockSpec((1, 1, 2 * tm), lambda i: (i, 0, 0), memory_space=pltpu.SMEM),
                  pl.BlockSpec((1, 1, 2 * tm), lambda i: (jnp.minimum(i + 1, n - 1), 0, 0), memory_space=pltpu.SMEM),
                  pl.BlockSpec((tm, 1024), lambda i: (i, 0)), pl.BlockSpec((tm, LANES), lambda i: (i, 0)),
                  pl.BlockSpec(memory_space=pl.ANY)],
        out_specs=pl.BlockSpec((tm, 1024), lambda i: (i, 0)),
        scratch_shapes=[pltpu.VMEM((2, tm, 1024), F32), pltpu.VMEM((2, tm, 1024), F32),
                        pltpu.SemaphoreType.DMA((2, 2))],
        compiler_params=_cparams(("arbitrary",)),
        name="moe_combine",
    )(pos3, pos3, x, route, y)


def _moe_routed(x, g, rw, rb, wg, wu, wd):
    route = _router(x, g, rw, rb)
    pos, src, tile_expert, n_used = _route_plan(route, MOE_ROWS)
    y = _moe_group(tile_expert, n_used, src, x, g, wg, wu, wd)
    return _moe_combine(pos, x, route, y)


def _ple_kernel(x_ref, p_ref, g_ref, wp_ref, wgt_ref, gf_ref, o_ref, *, final):
    x = x_ref[...]
    emb = jnp.dot(p_ref[...].astype(BF16), wp_ref[...], preferred_element_type=F32)
    gate = _sigmoid(jnp.dot(_rms(x, g_ref[...]).astype(BF16), wgt_ref[...], preferred_element_type=F32))
    y = x + emb * gate
    if final:
        y = _rms(y, gf_ref[...])
    o_ref[...] = y


def _ple(x, p, g, wp, wgt, gf, final):
    rows = x.shape[0]
    tm = min(rows, 512)
    vec = pl.BlockSpec((1, 1024), lambda i: (0, 0))
    return pl.pallas_call(
        functools.partial(_ple_kernel, final=final),
        out_shape=jax.ShapeDtypeStruct((rows, 1024), F32),
        grid=(rows // tm,),
        in_specs=[pl.BlockSpec((tm, 1024), lambda i: (i, 0)), pl.BlockSpec((tm, PLE_DIM), lambda i: (i, 0)), vec,
                  pl.BlockSpec((PLE_DIM, 1024), lambda i: (0, 0)), pl.BlockSpec((1024, 1024), lambda i: (0, 0)), vec],
        out_specs=pl.BlockSpec((tm, 1024), lambda i: (i, 0)),
        compiler_params=_cparams(("parallel",)),
        name="ple",
    )(x, p, g, wp, wgt, gf)


REPACK_ROWS = 128


def _repack_w_in_kernel(w_ref, o_ref):
    offs = [0]
    for wd in IN_WIDTHS:
        offs.append(offs[-1] + wd)
    z, xbc, dt, qa, ka, va, qr, kr, vr, gr, gates = [(offs[i], offs[i + 1]) for i in range(len(IN_WIDTHS))]
    take = lambda lo, hi: w_ref[:, lo:hi].astype(BF16)
    q = take(*qa)
    heads = lambda h: q[:, h * ATT_HEAD_DIM:(h + 1) * ATT_HEAD_DIM]
    q_rmajor = jnp.concatenate([heads(ATT_REP * g + r) for r in range(ATT_REP) for g in range(ATT_KV_HEADS)], axis=1)
    parts = [take(*xbc), take(*z), q_rmajor] + [take(*p) for p in (qr, kr, vr, gr, gates, ka, va)]
    used = sum(p.shape[1] for p in parts) + (dt[1] - dt[0])
    parts.append(jnp.concatenate([take(*dt), jnp.zeros((o_ref.shape[0], N_PROJ - used), BF16)], axis=1))
    dst = 0
    for part in parts:
        o_ref[:, dst:dst + part.shape[1]] = part
        dst += part.shape[1]


def _repack_w_in(w_all, layer):
    rows = w_all.shape[1]
    return pl.pallas_call(
        _repack_w_in_kernel,
        out_shape=jax.ShapeDtypeStruct((rows, N_PROJ), BF16),
        grid=(rows // REPACK_ROWS,),
        in_specs=[pl.BlockSpec((None, REPACK_ROWS, w_all.shape[2]), lambda i: (layer, i, 0))],
        out_specs=pl.BlockSpec((REPACK_ROWS, N_PROJ), lambda i: (i, 0)),
        compiler_params=_cparams(("parallel",)),
        name="repack_w_in",
    )(w_all)


def _att_tables(pos):
    half = ROPE_DIM // 2
    inv = jnp.exp(-math.log(ROPE_THETA) * jnp.arange(half, dtype=F32) * (2.0 / ROPE_DIM))
    ang = pos.astype(F32)[:, None] * inv[None, :]
    cos, sin = jnp.cos(ang), jnp.sin(ang)
    n = pos.shape[0]
    one = jnp.ones((n, ATT_HEAD_DIM - ROPE_DIM), F32)
    zero8 = jnp.zeros((n, half), F32)
    zero = jnp.zeros((n, ATT_HEAD_DIM - ROPE_DIM), F32)
    c = jnp.concatenate([cos, cos, one], axis=1)
    s1 = jnp.concatenate([zero8, sin, zero], axis=1)
    s2 = jnp.concatenate([-sin, zero8, zero], axis=1)
    return tuple(jnp.concatenate([t, t], axis=1) for t in (c, s1, s2))


def _ret_tables(pos):
    half = RET_DK // 2
    inv = jnp.exp(-math.log(RET_THETA) * jnp.arange(half, dtype=F32) * (2.0 / RET_DK))
    ang = pos.astype(F32)[:, None] * inv[None, :]
    cos, sin = jnp.cos(ang), jnp.sin(ang)
    return jnp.concatenate([cos, cos], axis=1), jnp.concatenate([-sin, sin], axis=1)


def _rope_step_tables(inv_lane, seq):
    per = MIX_ROWS // CHUNK
    ang_i = jnp.arange(CHUNK, dtype=F32)[:, None] * inv_lane[None, :]
    within = jnp.concatenate([jnp.cos(ang_i), jnp.sin(ang_i)], axis=1)
    ang_b = (jnp.arange(seq // CHUNK, dtype=F32) * CHUNK)[:, None] * inv_lane[None, :]
    blk = jnp.concatenate([jnp.cos(ang_b), jnp.sin(ang_b)], axis=1).reshape(seq // MIX_ROWS, per, 2 * LANES)
    steps = jnp.concatenate([blk, jnp.zeros((seq // MIX_ROWS, 8 - per, 2 * LANES), F32)], axis=1)
    return within, steps


def _att_inv_lanes():
    inv = jnp.exp(-math.log(ROPE_THETA) * jnp.arange(ROPE_DIM // 2, dtype=F32) * (2.0 / ROPE_DIM))
    return jnp.tile(inv, LANES // (ROPE_DIM // 2))


def _ret_inv_lanes():
    inv = jnp.exp(-math.log(RET_THETA) * jnp.arange(RET_DK // 2, dtype=F32) * (2.0 / RET_DK))
    return jnp.tile(inv, 2)


def _pad_lanes(v, fill=0.0):
    return jnp.concatenate([v.astype(F32), jnp.full((LANES - v.shape[0],), fill, F32)])[None, :]


def kernel(x_prompt, x_sample, state_ssm, state_conv, cache_win_k, cache_win_v, state_ret, p_prompt, p_sample,
           w_in, conv_w, conv_b, dt_bias, a_log, d_skip, ssd_norm, attn_sinks, ret_norm, w_o_ssd, w_o_att, w_o_ret,
           w_out, norm_mix, norm_ffn, norm_ple, ffn_w_gate, ffn_w_up, ffn_w_down, router_w, router_b, moe_w_gate,
           moe_w_up, moe_w_down, w_ple, w_ple_gate, norm_final):
    seq = x_prompt.shape[1]
    nb = x_sample.shape[0]
    xp = x_prompt.reshape(seq, D_MODEL)
    xs = x_sample.reshape(nb, D_MODEL)
    pos_s = PAST_LEN + jnp.arange(1)
    att_tab_p, att_tab_s = _rope_step_tables(_att_inv_lanes(), seq), _att_tables(pos_s)
    ret_tab_p, ret_tab_s = _rope_step_tables(_ret_inv_lanes(), seq), _ret_tables(pos_s)
    log_gamma = jnp.log1p(-jnp.exp2(-5.0 - jnp.arange(RET_HEADS, dtype=F32)))
    gamma = jnp.exp(log_gamma)
    row = lambda v: v.astype(F32)[None, :]

    ssm_all = state_ssm.reshape(DEPTH, nb, SSD_HEADS * SSD_HEAD_DIM, SSD_STATE)
    ret_all = state_ret.reshape(DEPTH, nb, RET_HEADS * RET_DK, 128)
    kc_all = cache_win_k.reshape(DEPTH, nb, WINDOW, ATT_KV_HEADS * ATT_HEAD_DIM)
    vc_all = cache_win_v.reshape(DEPTH, nb, WINDOW, ATT_KV_HEADS * ATT_HEAD_DIM)
    ssm_s = ret_s = kv_s = None
    conv_s = []

    new_p = [[], [], [], [], []]
    for i in range(DEPTH):
        w_in_i = _repack_w_in(w_in, i)
        cw, cb = conv_w[i], row(conv_b[i])
        dtb, alog = _pad_lanes(dt_bias[i]), _pad_lanes(a_log[i])
        dsk = row(jnp.repeat(d_skip[i], SSD_HEAD_DIM))
        nrm_ssd, nrm_ret = row(ssd_norm[i]), row(ret_norm[i])
        sinks = attn_sinks[i].astype(F32)
        sink32 = jnp.zeros((ATT_REP, 8), F32).at[:, :ATT_KV_HEADS].set(sinks.reshape(ATT_KV_HEADS, ATT_REP).T)
        sink32 = jnp.broadcast_to(sink32.reshape(32, 1), (32, LANES))
        w1 = w_o_ssd[i].astype(BF16)
        w2 = w_o_att[i].reshape(ATT_KV_HEADS, ATT_REP, ATT_HEAD_DIM, D_MODEL).transpose(1, 0, 2, 3) \
            .reshape(ATT_Q_HEADS * ATT_HEAD_DIM, D_MODEL).astype(BF16)
        w3 = w_o_ret[i].astype(BF16)
        wo = w_out[i].astype(BF16)
        g_mix, g_ffn, g_ple = row(norm_mix[i]), row(norm_ffn[i]), row(norm_ple[i])
        wp, wpg = w_ple[i].astype(BF16), w_ple_gate[i].astype(BF16)
        gf = row(norm_final)
        j = i // 2
        if i % 2 == 0:
            ffw = (ffn_w_gate[j].astype(BF16), ffn_w_up[j].astype(BF16), ffn_w_down[j].astype(BF16))
        else:
            rw = jnp.concatenate([router_w[j], jnp.zeros((D_MODEL, LANES - N_EXPERTS), F32)], axis=1)
            rw_hi = rw.astype(BF16)
            rw = jnp.stack([rw_hi, (rw - rw_hi.astype(F32)).astype(BF16)])
            ffw = (rw, _pad_lanes(router_b[j]), moe_w_gate[j].astype(BF16), moe_w_up[j].astype(BF16),
                   moe_w_down[j].astype(BF16))
        final = i == DEPTH - 1

        proj, dt = _inproj(xp, g_mix, w_in_i, BF16)
        y_ssd, ssm_fin, conv_fin = _ssd_prompt(proj, dt, cw, cb, dtb, alog, dsk, nrm_ssd)
        o_att, wk, wv = _swa_prompt(sinks, proj, att_tab_p)
        o_ret, ret_fin = _ret_prompt(log_gamma, proj, ret_tab_p, nrm_ret)
        xp = _merge(xp, y_ssd, o_att, o_ret, proj, w1, w2, w3, wo)
        xp = _ffn(xp, g_ffn, *ffw) if i % 2 == 0 else _moe_routed(xp, g_ffn, *ffw)
        xp = _ple(xp, p_prompt[i].reshape(seq, PLE_DIM), g_ple, wp, wpg, gf, final)
        new_p[0].append(ssm_fin.reshape(1, SSD_HEADS, SSD_HEAD_DIM, SSD_STATE))
        new_p[1].append(conv_fin[None])
        new_p[2].append(wk.reshape(1, WINDOW, ATT_KV_HEADS, ATT_HEAD_DIM))
        new_p[3].append(wv.reshape(1, WINDOW, ATT_KV_HEADS, ATT_HEAD_DIM))
        new_p[4].append(ret_fin[None])

        proj, dt = _inproj(xs, g_mix, w_in_i, F32)
        cst_t = jnp.transpose(state_conv[i], (1, 0, 2))
        cnew_t, xs_conv, bc, dec_t, xdt_t = _ssd_dec_pre(proj, dt, cst_t, cw, cb, dtb, alog)
        y_ssd, ssm_s = _ssd_dec_state(i, dec_t, xdt_t, bc, proj, xs_conv, ssm_all, dsk, nrm_ssd, ssm_s)
        o_att, *kv_s = _swa_dec(i, proj, kc_all, vc_all, att_tab_s, sink32, kv_s)
        qrot, kt = _ret_dec_pre(proj, ret_tab_s)
        o_ret, ret_s = _ret_dec_state(i, gamma, kt, qrot, proj, ret_all, nrm_ret, ret_s)
        xs = _merge(xs, y_ssd, o_att, o_ret, proj, w1, w2, w3, wo)
        xs = _ffn(xs, g_ffn, *ffw) if i % 2 == 0 else _moe(xs, g_ffn, *ffw)
        xs = _ple(xs, p_sample[i].reshape(nb, PLE_DIM), g_ple, wp, wpg, gf, final)
        conv_s.append(jnp.transpose(cnew_t, (1, 0, 2)))

    y_prompt = xp.reshape(1, seq, D_MODEL)
    y_sample = xs.reshape(nb, 1, D_MODEL)
    outs_p = [jnp.stack(l) for l in new_p]
    outs_s = [ssm_s.reshape(state_ssm.shape), jnp.stack(conv_s), kv_s[0].reshape(cache_win_k.shape),
              kv_s[1].reshape(cache_win_v.shape), ret_s.reshape(state_ret.shape)]
    return (y_prompt, y_sample, *outs_p, *outs_s)
```

```python
import functools
import math

import jax
import jax.numpy as jnp
from jax import lax
from jax.experimental import pallas as pl
from jax.experimental.pallas import tpu as pltpu

F32 = jnp.float32
BF16 = jnp.bfloat16

D_MODEL = 1024
DEPTH = 2
PAST_LEN = 16384
SSD_HEADS = 16
SSD_HEAD_DIM = 64
SSD_GROUPS = 4
SSD_STATE = 128
SSD_CONV = 4
SSD_CONV_DIM = 2048
ATT_HEAD_DIM = 64
ATT_Q_HEADS = 16
ATT_KV_HEADS = 4
ATT_REP = ATT_Q_HEADS // ATT_KV_HEADS
WINDOW = 128
ROPE_THETA = 500000.0
ROPE_DIM = 16
RET_HEADS = 8
RET_DK = 128
RET_THETA = 10000.0
CHUNK = 128
D_FF = 2816
N_EXPERTS = 8
PLE_DIM = 256
EPS = 1e-6

IN_WIDTHS = (1024, 2048, 16, 1024, 256, 256, 1024, 1024, 1024, 1024, 3072)
N_PROJ = 12288
COL_XBC = 0
COL_Z = 2
COL_QA = 3
COL_QR = 4
COL_KR = 5
COL_VR = 6
COL_GR = 7
COL_GATE = 8
COL_KA = 44
COL_VA = 45
COL_DT = 92

LANES = 128
VMEM_LIMIT = 48 * 1024 * 1024
VMEM_LIMIT_FFN = 56 * 1024 * 1024
MIX_ROWS = 4 * CHUNK


def _cparams(sem, vmem=VMEM_LIMIT):
    return pltpu.CompilerParams(dimension_semantics=sem, vmem_limit_bytes=vmem)


def _bdot(a, b):
    return jnp.dot(a.astype(BF16), b.astype(BF16), preferred_element_type=F32)


def _bdot_nt(a, b):
    return lax.dot_general(a.astype(BF16), b.astype(BF16), (((1,), (1,)), ((), ())),
                           preferred_element_type=F32)


def _split3(x):
    x0 = x.astype(BF16)
    r1 = x - x0.astype(F32)
    x1 = r1.astype(BF16)
    x2 = (r1 - x1.astype(F32)).astype(BF16)
    return x0, x1, x2


def _dot_exact_lhs01(m01, x):
    m = m01.astype(BF16)
    x0, x1, x2 = _split3(x)
    d = lambda b: jnp.dot(m, b, preferred_element_type=F32)
    return d(x0) + d(x1) + d(x2)


def _dot_exact_rhs01(x, m01):
    m = m01.astype(BF16)
    x0, x1, x2 = _split3(x)
    d = lambda a: jnp.dot(a, m, preferred_element_type=F32)
    return d(x0) + d(x1) + d(x2)


def _rms(x, g):
    return x * lax.rsqrt(jnp.mean(x * x, axis=-1, keepdims=True) + EPS) * g


def _sigmoid(x):
    return 1.0 / (1.0 + jnp.exp(-x))


def _silu(x):
    return x * _sigmoid(x)


def _softplus(x):
    return jnp.maximum(x, 0.0) + jnp.log1p(jnp.exp(-jnp.abs(x)))


def _rope_att(x, c, s1, s2):
    w = x.shape[1]
    return x * c + pltpu.roll(x, 8, axis=1) * s1 + pltpu.roll(x, w - 8, axis=1) * s2


def _tile_lanes(t, n):
    return jnp.concatenate([t] * n, axis=1) if n > 1 else t


INPROJ_TN = 1024
DT_TILE = (COL_DT * LANES) // INPROJ_TN
DT_OFF = COL_DT * LANES - DT_TILE * INPROJ_TN


def _inproj_kernel(x_ref, g_ref, w_ref, o_ref, dt_ref, h_ref):
    j = pl.program_id(1)

    @pl.when(j == 0)
    def _():
        h_ref[...] = _rms(x_ref[...], g_ref[...]).astype(BF16)

    acc = jnp.dot(h_ref[...], w_ref[...], preferred_element_type=F32)
    o_ref[...] = acc.astype(o_ref.dtype)

    @pl.when(j == DT_TILE)
    def _():
        dt_ref[...] = acc[:, DT_OFF:DT_OFF + LANES]


def _inproj(x, g, w, out_dtype):
    rows = x.shape[0]
    tm = min(rows, 2048)
    tn = INPROJ_TN
    return pl.pallas_call(
        _inproj_kernel,
        out_shape=(jax.ShapeDtypeStruct((rows, N_PROJ), out_dtype), jax.ShapeDtypeStruct((rows, LANES), F32)),
        grid=(rows // tm, N_PROJ // tn),
        in_specs=[pl.BlockSpec((tm, D_MODEL), lambda i, j: (i, 0)),
                  pl.BlockSpec((1, D_MODEL), lambda i, j: (0, 0)),
                  pl.BlockSpec((D_MODEL, tn), lambda i, j: (0, j))],
        out_specs=(pl.BlockSpec((tm, tn), lambda i, j: (i, j)), pl.BlockSpec((tm, LANES), lambda i, j: (i, 0))),
        scratch_shapes=[pltpu.VMEM((tm, D_MODEL), BF16)],
        compiler_params=_cparams(("parallel", "arbitrary")),
        name="inproj",
    )(x, g, w)


def _chunk_rows(ci):
    return pl.ds(pl.multiple_of(ci * CHUNK, CHUNK), CHUNK)


def _ssd_prompt_chunk(rows, xbc_ref, z_ref, dt_ref, cw_ref, cb_ref, dtb_ref, alog_ref, dsk_ref, nrm_ref,
                      y_ref, xpad_ref, s_ref):
    xbc = xbc_ref[rows, :].astype(F32)
    xpad_ref[8:8 + CHUNK, :] = xbc
    cw = cw_ref[...]
    acc = (xbc * cw[3:4, :] + xpad_ref[7:7 + CHUNK, :] * cw[2:3, :]
           + xpad_ref[6:6 + CHUNK, :] * cw[1:2, :] + xpad_ref[5:5 + CHUNK, :] * cw[0:1, :] + cb_ref[...])
    conv = _silu(acc)
    xpad_ref[0:8, :] = xbc[CHUNK - 8:CHUNK, :]

    xs = conv[:, :1024]
    dt = _softplus(dt_ref[rows, :] + dtb_ref[...])
    la = dt * (-jnp.exp(alog_ref[...]))
    row = lax.broadcasted_iota(jnp.int32, (CHUNK, CHUNK), 0)
    col = lax.broadcasted_iota(jnp.int32, (CHUNK, CHUNK), 1)
    causal = row >= col
    cum = _dot_exact_lhs01(causal.astype(F32), la)
    cum_t = cum.T
    dt_t = dt.T
    cum_last = jnp.broadcast_to(cum_t[:, CHUNK - 1:CHUNK], (LANES, CHUNK))
    w_t = jnp.exp(cum_last - cum_t) * dt_t
    dec_end = jnp.exp(cum_last)

    xs_t = xs.T
    ys = []
    for g in range(SSD_GROUPS):
        bg = conv[:, 1024 + 128 * g:1024 + 128 * (g + 1)]
        cg = conv[:, 1536 + 128 * g:1536 + 128 * (g + 1)]
        cb = _bdot_nt(cg, bg)
        s_g = s_ref[256 * g:256 * (g + 1), :]
        cs = _bdot_nt(cg, s_g)
        xw_parts = []
        dec_parts = []
        for r in range(4):
            h = 4 * g + r
            colb = jnp.broadcast_to(cum[:, h:h + 1], (CHUNK, CHUNK))
            rowb = jnp.broadcast_to(cum_t[h:h + 1, :], (CHUNK, CHUNK))
            dec = jnp.exp(jnp.where(causal, colb - rowb, -jnp.inf))
            m = cb * dec * jnp.broadcast_to(dt_t[h:h + 1, :], (CHUNK, CHUNK))
            xh = xs[:, 64 * h:64 * (h + 1)]
            yh = _bdot(m, xh) + cs[:, 64 * r:64 * (r + 1)] * jnp.exp(colb)[:, :64]
            ys.append(yh)
            xw_parts.append(xs_t[64 * h:64 * (h + 1), :] * jnp.broadcast_to(w_t[h:h + 1, :], (64, CHUNK)))
            dec_parts.append(jnp.broadcast_to(dec_end[h:h + 1, :], (64, SSD_STATE)))
        xw = jnp.concatenate(xw_parts, axis=0)
        s_ref[256 * g:256 * (g + 1), :] = s_g * jnp.concatenate(dec_parts, axis=0) + _bdot(xw, bg)

    y = jnp.concatenate(ys, axis=1) + dsk_ref[...] * xs
    y = y * _silu(z_ref[rows, :].astype(F32))
    y_ref[rows, :] = _rms(y, nrm_ref[...]).astype(y_ref.dtype)


def _ssd_prompt_kernel(xbc_ref, z_ref, dt_ref, cw_ref, cb_ref, dtb_ref, alog_ref, dsk_ref, nrm_ref,
                       y_ref, sfin_ref, cfin_ref, xpad_ref, s_ref):
    t = pl.program_id(0)

    @pl.when(t == 0)
    def _():
        xpad_ref[0:8, :] = jnp.zeros((8, SSD_CONV_DIM), F32)
        s_ref[...] = jnp.zeros_like(s_ref)

    def chunk(ci, carry):
        _ssd_prompt_chunk(_chunk_rows(ci), xbc_ref, z_ref, dt_ref, cw_ref, cb_ref, dtb_ref, alog_ref, dsk_ref,
                          nrm_ref, y_ref, xpad_ref, s_ref)
        return carry

    lax.fori_loop(0, xbc_ref.shape[0] // CHUNK, chunk, 0)

    @pl.when(t == pl.num_programs(0) - 1)
    def _():
        sfin_ref[...] = s_ref[...]
        cfin_ref[...] = xpad_ref[8 + CHUNK - (SSD_CONV - 1):8 + CHUNK, :]


def _ssd_prompt(proj, dt, cw, cb, dtb, alog, dsk, nrm):
    seq = proj.shape[0]
    const = lambda shape: pl.BlockSpec(shape, lambda t: (0,) * len(shape))
    return pl.pallas_call(
        _ssd_prompt_kernel,
        out_shape=(jax.ShapeDtypeStruct((seq, 1024), BF16),
                   jax.ShapeDtypeStruct((1024, SSD_STATE), F32),
                   jax.ShapeDtypeStruct((SSD_CONV - 1, SSD_CONV_DIM), F32)),
        grid=(seq // MIX_ROWS,),
        in_specs=[pl.BlockSpec((MIX_ROWS, 2048), lambda t: (t, COL_XBC)),
                  pl.BlockSpec((MIX_ROWS, 1024), lambda t: (t, COL_Z)),
                  pl.BlockSpec((MIX_ROWS, LANES), lambda t: (t, 0)),
                  const((SSD_CONV, 2048)), const((1, 2048)), const((1, 128)), const((1, 128)),
                  const((1, 1024)), const((1, 1024))],
        out_specs=(pl.BlockSpec((MIX_ROWS, 1024), lambda t: (t, 0)),
                   const((1024, SSD_STATE)), const((SSD_CONV - 1, SSD_CONV_DIM))),
        scratch_shapes=[pltpu.VMEM((8 + CHUNK, SSD_CONV_DIM), F32), pltpu.VMEM((1024, SSD_STATE), F32)],
        compiler_params=_cparams(("arbitrary",)),
        name="ssd_prompt",
    )(proj, proj, dt, cw, cb, dtb, alog, dsk, nrm)


def _ssd_dec_pre_kernel(xbc_ref, dt_ref, cst_ref, cw_ref, cb_ref, dtb_ref, alog_ref,
                        cnew_ref, xs_ref, bc_ref, dec_t_ref, xdt_t_ref):
    xbc = xbc_ref[...]
    cw = cw_ref[...]
    acc = (cst_ref[0] * cw[0:1, :] + cst_ref[1] * cw[1:2, :] + cst_ref[2] * cw[2:3, :]
           + xbc * cw[3:4, :] + cb_ref[...])
    conv = _silu(acc)
    cnew_ref[0] = cst_ref[1]
    cnew_ref[1] = cst_ref[2]
    cnew_ref[2] = xbc
    xs = conv[:, :1024]
    xs_ref[...] = xs
    bc_ref[...] = conv[:, 1024:]
    dt = _softplus(dt_ref[...] + dtb_ref[...])
    dec = jnp.exp(dt * (-jnp.exp(alog_ref[...])))
    hrow = lax.broadcasted_iota(jnp.int32, (LANES, 1024), 0)
    hcol = lax.broadcasted_iota(jnp.int32, (LANES, 1024), 1)
    expand = ((hcol >> 6) == hrow).astype(F32)
    dec_t_ref[...] = _dot_exact_rhs01(dec, expand).T
    xdt_t_ref[...] = (xs * _dot_exact_rhs01(dt, expand)).T


def _ssd_dec_pre(proj, dt, cst_t, cw, cb, dtb, alog):
    nb = proj.shape[0]
    const = lambda shape: pl.BlockSpec(shape, lambda t: (0,) * len(shape))
    return pl.pallas_call(
        _ssd_dec_pre_kernel,
        out_shape=(jax.ShapeDtypeStruct((3, nb, 2048), F32), jax.ShapeDtypeStruct((nb, 1024), F32),
                   jax.ShapeDtypeStruct((nb, 1024), F32), jax.ShapeDtypeStruct((1024, nb), F32),
                   jax.ShapeDtypeStruct((1024, nb), F32)),
        grid=(1,),
        in_specs=[pl.BlockSpec((nb, 2048), lambda t: (0, COL_XBC)),
                  pl.BlockSpec((nb, LANES), lambda t: (0, 0)),
                  const((3, nb, 2048)), const((SSD_CONV, 2048)), const((1, 2048)), const((1, 128)), const((1, 128))],
        out_specs=(const((3, nb, 2048)), const((nb, 1024)), const((nb, 1024)), const((1024, nb)), const((1024, nb))),
        compiler_params=_cparams(("arbitrary",)),
        name="ssd_dec_pre",
    )(proj, dt, cst_t, cw, cb, dtb, alog)


DEC_BLOCK = 8


def _ssd_dec_state_kernel(dec_t_ref, xdt_t_ref, bc_ref, z_ref, xs_ref, st_ref, dsk_ref, nrm_ref,
                          y_ref, stn_ref):
    i = pl.program_id(0)
    shift = (LANES - i * DEC_BLOCK) % LANES
    decr = pltpu.roll(dec_t_ref[...], shift, axis=1)
    xr = pltpu.roll(xdt_t_ref[...], shift, axis=1)
    bc = bc_ref[...]
    for j in range(DEC_BLOCK):
        for g in range(SSD_GROUPS):
            lo, hi = 256 * g, 256 * (g + 1)
            s_old = st_ref[j, lo:hi, :]
            dcol = jnp.broadcast_to(decr[lo:hi, j:j + 1], (256, SSD_STATE))
            xcol = jnp.broadcast_to(xr[lo:hi, j:j + 1], (256, SSD_STATE))
            s_new = s_old * dcol + xcol * bc[j:j + 1, 128 * g:128 * (g + 1)]
            stn_ref[j, lo:hi, :] = s_new
            cs = _bdot_nt(bc[:, 512 + 128 * g:512 + 128 * (g + 1)], s_new)
            y_ref[j:j + 1, lo:hi] = cs[j:j + 1, :]
    xs = xs_ref[...]
    y = y_ref[...] + dsk_ref[...] * xs
    y = y * _silu(z_ref[...])
    y_ref[...] = _rms(y, nrm_ref[...])


class _LayerCall:
    def __init__(self, layer, steps):
        self.layer, self.steps, self.first = layer, steps, layer == 0
        self.grid = (DEPTH * steps,) if self.first else (steps,)

    def _block(self, t):
        return jnp.minimum(t, self.steps - 1) if self.first else t

    def rows(self, width, col=0):
        return pl.BlockSpec((DEC_BLOCK, width), lambda t: (self._block(t), col))

    def state_in(self, tail):
        zeros = (0,) * len(tail)
        return pl.BlockSpec((None, DEC_BLOCK) + tail, lambda t: (self.layer, self._block(t)) + zeros)

    def state_out(self, tail):
        zeros = (0,) * len(tail)
        if self.first:
            return pl.BlockSpec((None, DEC_BLOCK) + tail, lambda t: (t // self.steps, t % self.steps) + zeros)
        return pl.BlockSpec((None, DEC_BLOCK) + tail, lambda t: (self.layer, t) + zeros)

    def kernel(self, body, n_in, n_carried, stacked_outs):
        def wrapped(*refs):
            refs = refs[:n_in] + refs[n_in + n_carried:]
            if not self.first:
                body(*refs)
                return
            t = pl.program_id(0)

            @pl.when(t < self.steps)
            def _():
                body(*refs)

            @pl.when(t >= self.steps)
            def _():
                for k in stacked_outs:
                    refs[n_in + k][...] = jnp.zeros_like(refs[n_in + k])
        return wrapped


def _ssd_dec_state(layer, dec_t, xdt_t, bc, proj, xs, st_all, dsk, nrm, carried):
    nb = xs.shape[0]
    lc = _LayerCall(layer, nb // DEC_BLOCK)
    const = lambda shape: pl.BlockSpec(shape, lambda t: (0,) * len(shape))
    tail = (1024, SSD_STATE)
    ins = [dec_t, xdt_t, bc, proj, xs, st_all, dsk, nrm]
    in_specs = [const((1024, nb)), const((1024, nb)), lc.rows(1024), lc.rows(1024, COL_Z), lc.rows(1024),
                lc.state_in(tail), const((1, 1024)), const((1, 1024))]
    n_in = len(ins)
    carried = [] if carried is None else [carried]
    return pl.pallas_call(
        lc.kernel(_ssd_dec_state_kernel, n_in, len(carried), (1,)),
        out_shape=(jax.ShapeDtypeStruct((nb, 1024), F32), jax.ShapeDtypeStruct(st_all.shape, F32)),
        grid=lc.grid,
        in_specs=in_specs + [pl.BlockSpec(memory_space=pl.ANY)] * len(carried),
        out_specs=(lc.rows(1024), lc.state_out(tail)),
        input_output_aliases={n_in + k: 1 + k for k in range(len(carried))},
        compiler_params=_cparams(("arbitrary",)),
        name="ssd_dec_state",
    )(*ins, *carried)


def _block_cos_sin(within_ref, step_ref, bi):
    row = step_ref[pl.ds(bi, 1), :]
    cb, sb = row[:, :LANES], row[:, LANES:]
    ci, si = within_ref[:, :LANES], within_ref[:, LANES:]
    return cb * ci - sb * si, sb * ci + cb * si


def _swa_prompt_block(bi, has_prev, sink_ref, q_ref, k_ref, v_ref, within_ref, step_ref, o_ref, kp_ref, vp_ref):
    rows = _chunk_rows(bi)
    cosp, sinp = _block_cos_sin(within_ref, step_ref, bi)
    l64 = lax.broadcasted_iota(jnp.int32, (1, LANES), 1) & (ATT_HEAD_DIM - 1)
    c = jnp.where(l64 < ROPE_DIM, cosp, 1.0)
    s1 = jnp.where(jnp.logical_and(l64 >= ROPE_DIM // 2, l64 < ROPE_DIM), sinp, 0.0)
    s2 = jnp.where(l64 < ROPE_DIM // 2, -sinp, 0.0)
    q = _rope_att(q_ref[rows, :].astype(F32), _tile_lanes(c, 8), _tile_lanes(s1, 8), _tile_lanes(s2, 8))
    k = _rope_att(k_ref[rows, :].astype(F32), _tile_lanes(c, 2), _tile_lanes(s1, 2), _tile_lanes(s2, 2))
    v = v_ref[rows, :].astype(F32)
    ghead = lax.broadcasted_iota(jnp.int32, (WINDOW, 256), 1) >> 6
    expand = lambda t: jnp.concatenate([jnp.where(ghead == g, t, 0.0) for g in range(ATT_KV_HEADS)],
                                       axis=0).astype(BF16)
    kbd, vbd = expand(k), expand(v)
    kbd_prev, vbd_prev = expand(kp_ref[...]), expand(vp_ref[...])
    qall = jnp.concatenate([q[:, 256 * r:256 * (r + 1)] for r in range(ATT_REP)], axis=0).astype(BF16)
    scale = ATT_HEAD_DIM ** -0.5
    nt_dims = (((1,), (1,)), ((), ()))
    sp_all = lax.dot_general(qall, kbd_prev, nt_dims, preferred_element_type=F32) * scale
    sc_all = lax.dot_general(qall, kbd, nt_dims, preferred_element_type=F32) * scale
    nq = ATT_REP * WINDOW
    qi = lax.broadcasted_iota(jnp.int32, (nq, WINDOW), 0) & (WINDOW - 1)
    kj = lax.broadcasted_iota(jnp.int32, (nq, WINDOW), 1)
    mask_prev = jnp.logical_and(kj > qi, has_prev)
    mask_cur = kj <= qi
    rep = lax.broadcasted_iota(jnp.int32, (nq, 1), 0) >> 7
    pp, pc = [], []
    for g in range(ATT_KV_HEADS):
        sp = jnp.where(mask_prev, sp_all[:, WINDOW * g:WINDOW * (g + 1)], -jnp.inf)
        sc = jnp.where(mask_cur, sc_all[:, WINDOW * g:WINDOW * (g + 1)], -jnp.inf)
        sink = jnp.where(rep == 0, sink_ref[4 * g],
                         jnp.where(rep == 1, sink_ref[4 * g + 1],
                                   jnp.where(rep == 2, sink_ref[4 * g + 2], sink_ref[4 * g + 3])))
        m = jnp.maximum(jnp.max(jnp.maximum(sp, sc), axis=1, keepdims=True), sink)
        ep = jnp.exp(sp - m)
        ec = jnp.exp(sc - m)
        inv = 1.0 / (jnp.sum(ep + ec, axis=1, keepdims=True) + jnp.exp(sink - m))
        pp.append((ep * inv).astype(BF16))
        pc.append((ec * inv).astype(BF16))
    o = (jnp.dot(jnp.concatenate(pp, axis=1), vbd_prev, preferred_element_type=F32)
         + jnp.dot(jnp.concatenate(pc, axis=1), vbd, preferred_element_type=F32))
    for r in range(ATT_REP):
        o_ref[rows, 256 * r:256 * (r + 1)] = o[WINDOW * r:WINDOW * (r + 1), :].astype(o_ref.dtype)
    kp_ref[...] = k
    vp_ref[...] = v


def _swa_prompt_kernel(sink_ref, q_ref, k_ref, v_ref, within_ref, step_ref,
                       o_ref, wk_ref, wv_ref, kp_ref, vp_ref):
    n = pl.program_id(0)
    blocks = q_ref.shape[0] // WINDOW

    @pl.when(n == 0)
    def _():
        kp_ref[...] = jnp.zeros_like(kp_ref)
        vp_ref[...] = jnp.zeros_like(vp_ref)

    def block(bi, carry):
        _swa_prompt_block(bi, n * blocks + bi > 0, sink_ref, q_ref, k_ref, v_ref,
                          within_ref, step_ref, o_ref, kp_ref, vp_ref)
        return carry

    lax.fori_loop(0, blocks, block, 0)

    @pl.when(n == pl.num_programs(0) - 1)
    def _():
        wk_ref[...] = kp_ref[...]
        wv_ref[...] = vp_ref[...]


def _swa_prompt(sinks, proj, tabs):
    seq = proj.shape[0]
    within, steps = tabs
    const = lambda shape: pl.BlockSpec(shape, lambda t: (0,) * len(shape))
    return pl.pallas_call(
        _swa_prompt_kernel,
        out_shape=(jax.ShapeDtypeStruct((seq, 1024), BF16),
                   jax.ShapeDtypeStruct((WINDOW, 256), F32), jax.ShapeDtypeStruct((WINDOW, 256), F32)),
        grid=(seq // MIX_ROWS,),
        in_specs=[pl.BlockSpec(memory_space=pltpu.SMEM),
                  pl.BlockSpec((MIX_ROWS, 1024), lambda t: (t, COL_QA)),
                  pl.BlockSpec((MIX_ROWS, 256), lambda t: (t, COL_KA)),
                  pl.BlockSpec((MIX_ROWS, 256), lambda t: (t, COL_VA)),
                  const((CHUNK, 2 * LANES)), pl.BlockSpec((None, 8, 2 * LANES), lambda t: (t, 0, 0))],
        out_specs=(pl.BlockSpec((MIX_ROWS, 1024), lambda t: (t, 0)), const((WINDOW, 256)), const((WINDOW, 256))),
        scratch_shapes=[pltpu.VMEM((WINDOW, 256), F32), pltpu.VMEM((WINDOW, 256), F32)],
        compiler_params=_cparams(("arbitrary",)),
        name="swa_prompt",
    )(sinks, proj, proj, proj, within, steps)


def _swa_dec_pre_kernel(q_ref, k_ref, v_ref, c_ref, s1_ref, s2_ref, qrot_ref, kt_ref, vt_ref):
    c, s1, s2 = c_ref[...], s1_ref[...], s2_ref[...]
    qrot_ref[...] = _rope_att(q_ref[...], _tile_lanes(c, 8), _tile_lanes(s1, 8), _tile_lanes(s2, 8))
    kt_ref[...] = _rope_att(k_ref[...], _tile_lanes(c, 2), _tile_lanes(s1, 2), _tile_lanes(s2, 2)).T
    vt_ref[...] = v_ref[...].T


def _swa_dec_pre(proj, tabs):
    nb = proj.shape[0]
    c, s1, s2 = tabs
    const = lambda shape: pl.BlockSpec(shape, lambda t: (0,) * len(shape))
    return pl.pallas_call(
        _swa_dec_pre_kernel,
        out_shape=(jax.ShapeDtypeStruct((nb, 1024), F32), jax.ShapeDtypeStruct((256, nb), F32),
                   jax.ShapeDtypeStruct((256, nb), F32)),
        grid=(1,),
        in_specs=[pl.BlockSpec((nb, 1024), lambda t: (0, COL_QA)), pl.BlockSpec((nb, 256), lambda t: (0, COL_KA)),
                  pl.BlockSpec((nb, 256), lambda t: (0, COL_VA)), const((1, LANES)), const((1, LANES)), const((1, LANES))],
        out_specs=(const((nb, 1024)), const((256, nb)), const((256, nb))),
        compiler_params=_cparams(("arbitrary",)),
        name="swa_dec_pre",
    )(proj, proj, proj, c, s1, s2)


def _swa_dec_kernel(q_ref, kt_ref, vt_ref, kc_ref, vc_ref, sink_ref, o_ref, kcn_ref, vcn_ref):
    i = pl.program_id(0)
    shift = (LANES - i * DEC_BLOCK) % LANES
    ktr = pltpu.roll(kt_ref[...], shift, axis=1)
    vtr = pltpu.roll(vt_ref[...], shift, axis=1)
    q = q_ref[...]
    lane = lax.broadcasted_iota(jnp.int32, (ATT_HEAD_DIM, WINDOW), 1)
    rowid = lax.broadcasted_iota(jnp.int32, (8, ATT_HEAD_DIM), 0)
    scale = ATT_HEAD_DIM ** -0.5
    append = lambda old, col: jnp.where(lane == WINDOW - 1, jnp.broadcast_to(col, (ATT_HEAD_DIM, WINDOW)),
                                        pltpu.roll(old, WINDOW - 1, axis=1))
    pairs = [(j, g) for j in range(DEC_BLOCK) for g in range(ATT_KV_HEADS)]
    scores, values = [], []
    for j, g in pairs:
        lo, hi = ATT_HEAD_DIM * g, ATT_HEAD_DIM * (g + 1)
        kt = append(kc_ref[j, g], ktr[lo:hi, j:j + 1])
        vt = append(vc_ref[j, g], vtr[lo:hi, j:j + 1])
        kcn_ref[j, g] = kt
        vcn_ref[j, g] = vt
        qg = jnp.zeros((8, ATT_HEAD_DIM), F32)
        for r in range(ATT_REP):
            src = 256 * r + lo
            qg = jnp.where(rowid == r, jnp.broadcast_to(q[j:j + 1, src:src + ATT_HEAD_DIM], (8, ATT_HEAD_DIM)), qg)
        scores.append(_bdot(qg, kt))
        values.append(vt.astype(BF16))
    s = jnp.concatenate(scores, axis=0) * scale
    sink = jnp.concatenate([sink_ref[...]] * DEC_BLOCK, axis=0)[:, 0:1]
    m = jnp.maximum(jnp.max(s, axis=1, keepdims=True), sink)
    e = jnp.exp(s - m)
    p = e * (1.0 / (jnp.sum(e, axis=1, keepdims=True) + jnp.exp(sink - m)))
    for idx, (j, g) in enumerate(pairs):
        o = lax.dot_general(p[8 * idx:8 * (idx + 1), :].astype(BF16), values[idx], (((1,), (1,)), ((), ())),
                            preferred_element_type=F32)
        for r in range(ATT_REP):
            dst = 256 * r + ATT_HEAD_DIM * g
            o_ref[j:j + 1, dst:dst + ATT_HEAD_DIM] = o[r:r + 1, :]


def _swa_dec(layer, qrot, kt, vt, kc_all, vc_all, sink32, carried):
    nb = qrot.shape[0]
    lc = _LayerCall(layer, nb // DEC_BLOCK)
    const = lambda shape: pl.BlockSpec(shape, lambda t: (0,) * len(shape))
    tail = (ATT_KV_HEADS, ATT_HEAD_DIM, WINDOW)
    ins = [qrot, kt, vt, kc_all, vc_all, sink32]
    in_specs = [lc.rows(1024), const((256, nb)), const((256, nb)), lc.state_in(tail), lc.state_in(tail),
                const((32, LANES))]
    n_in = len(ins)
    carried = [] if carried is None else list(carried)
    return pl.pallas_call(
        lc.kernel(_swa_dec_kernel, n_in, len(carried), (1, 2)),
        out_shape=(jax.ShapeDtypeStruct((nb, 1024), F32),
                   jax.ShapeDtypeStruct(kc_all.shape, F32), jax.ShapeDtypeStruct(vc_all.shape, F32)),
        grid=lc.grid,
        in_specs=in_specs + [pl.BlockSpec(memory_space=pl.ANY)] * len(carried),
        out_specs=(lc.rows(1024), lc.state_out(tail), lc.state_out(tail)),
        input_output_aliases={n_in + k: 1 + k for k in range(len(carried))},
        compiler_params=_cparams(("arbitrary",)),
        name="swa_dec",
    )(*ins, *carried)


def _ret_prompt_chunk(ci, lg_ref, q_ref, k_ref, v_ref, gr_ref, within_ref, step_ref, nrm_ref,
                      o_ref, st_ref, intra_ref, fs_ref, te_ref):
    rows = _chunk_rows(ci)
    c, sinp = _block_cos_sin(within_ref, step_ref, ci)
    s = jnp.where(lax.broadcasted_iota(jnp.int32, (1, LANES), 1) < RET_DK // 2, -sinp, sinp)
    q = q_ref[rows, :].astype(F32)
    k = k_ref[rows, :].astype(F32)
    v = v_ref[rows, :].astype(F32)
    gr = gr_ref[rows, :].astype(F32)
    nrm = nrm_ref[...]
    for h in range(RET_HEADS):
        sl = slice(128 * h, 128 * (h + 1))
        qh = q[:, sl]
        kh = k[:, sl]
        qh = qh * c + pltpu.roll(qh, 64, axis=1) * s
        kh = (kh * c + pltpu.roll(kh, 64, axis=1) * s) * (RET_DK ** -0.5)
        vh = v[:, sl]
        att = _bdot_nt(qh, kh) * intra_ref[h]
        s_old = st_ref[h]
        o = _bdot(att, vh) + _bdot(qh, s_old) * fs_ref[h]
        cd = jnp.exp(jnp.zeros((1, RET_DK), F32) + CHUNK * lg_ref[h])
        st_ref[h] = s_old * cd + _bdot((kh * te_ref[h]).T, vh)
        o = o * lax.rsqrt(jnp.mean(o * o, axis=-1, keepdims=True) + EPS)
        o_ref[rows, sl] = (o * nrm[:, sl] * _silu(gr[:, sl])).astype(o_ref.dtype)


def _ret_prompt_kernel(lg_ref, q_ref, k_ref, v_ref, gr_ref, within_ref, step_ref, nrm_ref,
                       o_ref, sfin_ref, st_ref, intra_ref, fs_ref, te_ref):
    t = pl.program_id(0)

    @pl.when(t == 0)
    def _():
        st_ref[...] = jnp.zeros_like(st_ref)
        ri = lax.broadcasted_iota(jnp.int32, (CHUNK, CHUNK), 0).astype(F32)
        ci = lax.broadcasted_iota(jnp.int32, (CHUNK, CHUNK), 1).astype(F32)
        rel = ri - ci
        for h in range(RET_HEADS):
            lg = lg_ref[h]
            intra_ref[h] = jnp.exp(jnp.where(rel >= 0, rel * lg, -jnp.inf))
            fs_ref[h] = jnp.exp((ri + 1.0) * lg)
            te_ref[h] = jnp.exp((CHUNK - 1.0 - ri) * lg)

    def chunk(ci, carry):
        _ret_prompt_chunk(ci, lg_ref, q_ref, k_ref, v_ref, gr_ref, within_ref, step_ref, nrm_ref,
                          o_ref, st_ref, intra_ref, fs_ref, te_ref)
        return carry

    lax.fori_loop(0, q_ref.shape[0] // CHUNK, chunk, 0)

    @pl.when(t == pl.num_programs(0) - 1)
    def _():
        sfin_ref[...] = st_ref[...]


def _ret_prompt(log_gamma, proj, tabs, nrm):
    seq = proj.shape[0]
    within, steps = tabs
    const = lambda shape: pl.BlockSpec(shape, lambda t: (0,) * len(shape))
    col = lambda cidx: pl.BlockSpec((MIX_ROWS, 1024), lambda t: (t, cidx))
    tbl = pltpu.VMEM((RET_HEADS, CHUNK, CHUNK), F32)
    return pl.pallas_call(
        _ret_prompt_kernel,
        out_shape=(jax.ShapeDtypeStruct((seq, 1024), BF16), jax.ShapeDtypeStruct((RET_HEADS, RET_DK, 128), F32)),
        grid=(seq // MIX_ROWS,),
        in_specs=[pl.BlockSpec(memory_space=pltpu.SMEM), col(COL_QR), col(COL_KR), col(COL_VR), col(COL_GR),
                  const((CHUNK, 2 * LANES)), pl.BlockSpec((None, 8, 2 * LANES), lambda t: (t, 0, 0)), const((1, 1024))],
        out_specs=(pl.BlockSpec((MIX_ROWS, 1024), lambda t: (t, 0)), const((RET_HEADS, RET_DK, 128))),
        scratch_shapes=[tbl, tbl, tbl, tbl],
        compiler_params=_cparams(("arbitrary",)),
        name="ret_prompt",
    )(log_gamma, proj, proj, proj, proj, within, steps, nrm)


def _ret_dec_pre_kernel(q_ref, k_ref, c_ref, s_ref, qrot_ref, kt_ref):
    c = c_ref[...]
    s = s_ref[...]
    q = q_ref[...]
    k = k_ref[...]
    ks = []
    for h in range(RET_HEADS):
        sl = slice(128 * h, 128 * (h + 1))
        qh = q[:, sl]
        kh = k[:, sl]
        qrot_ref[:, sl] = qh * c + pltpu.roll(qh, 64, axis=1) * s
        ks.append((kh * c + pltpu.roll(kh, 64, axis=1) * s) * (RET_DK ** -0.5))
    kt_ref[...] = jnp.concatenate(ks, axis=1).T


def _ret_dec_pre(proj, tabs):
    nb = proj.shape[0]
    c, s = tabs
    const = lambda shape: pl.BlockSpec(shape, lambda t: (0,) * len(shape))
    return pl.pallas_call(
        _ret_dec_pre_kernel,
        out_shape=(jax.ShapeDtypeStruct((nb, 1024), F32), jax.ShapeDtypeStruct((1024, nb), F32)),
        grid=(1,),
        in_specs=[pl.BlockSpec((nb, 1024), lambda t: (0, COL_QR)), pl.BlockSpec((nb, 1024), lambda t: (0, COL_KR)),
                  const((1, LANES)), const((1, LANES))],
        out_specs=(const((nb, 1024)), const((1024, nb))),
        compiler_params=_cparams(("arbitrary",)),
        name="ret_dec_pre",
    )(proj, proj, c, s)


def _ret_dec_state_kernel(gam_ref, kt_ref, q_ref, v_ref, gr_ref, st_ref, nrm_ref, o_ref, stn_ref):
    i = pl.program_id(0)
    shift = (LANES - i * DEC_BLOCK) % LANES
    kr = pltpu.roll(kt_ref[...], shift, axis=1)
    q = q_ref[...]
    v = v_ref[...]
    for j in range(DEC_BLOCK):
        for h in range(RET_HEADS):
            lo, hi = 128 * h, 128 * (h + 1)
            kcol = jnp.broadcast_to(kr[lo:hi, j:j + 1], (RET_DK, 128))
            s_new = st_ref[j, lo:hi, :] * gam_ref[h] + kcol * v[j:j + 1, lo:hi]
            stn_ref[j, lo:hi, :] = s_new
            qs = _bdot(q[:, lo:hi], s_new)
            o_ref[j:j + 1, lo:hi] = qs[j:j + 1, :]
    gr = gr_ref[...]
    nrm = nrm_ref[...]
    for h in range(RET_HEADS):
        sl = slice(128 * h, 128 * (h + 1))
        o = o_ref[:, sl]
        o = o * lax.rsqrt(jnp.mean(o * o, axis=-1, keepdims=True) + EPS)
        o_ref[:, sl] = o * nrm[:, sl] * _silu(gr[:, sl])


def _ret_dec_state(layer, gam, kt, qrot, proj, st_all, nrm, carried):
    nb = qrot.shape[0]
    lc = _LayerCall(layer, nb // DEC_BLOCK)
    const = lambda shape: pl.BlockSpec(shape, lambda t: (0,) * len(shape))
    tail = (1024, 128)
    ins = [gam, kt, qrot, proj, proj, st_all, nrm]
    in_specs = [pl.BlockSpec(memory_space=pltpu.SMEM), const((1024, nb)), lc.rows(1024),
                lc.rows(1024, COL_VR), lc.rows(1024, COL_GR), lc.state_in(tail), const((1, 1024))]
    n_in = len(ins)
    carried = [] if carried is None else [carried]
    return pl.pallas_call(
        lc.kernel(_ret_dec_state_kernel, n_in, len(carried), (1,)),
        out_shape=(jax.ShapeDtypeStruct((nb, 1024), F32), jax.ShapeDtypeStruct(st_all.shape, F32)),
        grid=lc.grid,
        in_specs=in_specs + [pl.BlockSpec(memory_space=pl.ANY)] * len(carried),
        out_specs=(lc.rows(1024), lc.state_out(tail)),
        input_output_aliases={n_in + k: 1 + k for k in range(len(carried))},
        compiler_params=_cparams(("arbitrary",)),
        name="ret_dec_state",
    )(*ins, *carried)


def _merge_kernel(x_ref, a_ref, b_ref, c_ref, g1_ref, g2_ref, g3_ref, w1_ref, w2_ref, w3_ref, wo_ref, o_ref):
    gate = lambda ref: _sigmoid(ref[...].astype(F32))
    m = (gate(g1_ref) * jnp.dot(a_ref[...].astype(BF16), w1_ref[...], preferred_element_type=F32)
         + gate(g2_ref) * jnp.dot(b_ref[...].astype(BF16), w2_ref[...], preferred_element_type=F32)
         + gate(g3_ref) * jnp.dot(c_ref[...].astype(BF16), w3_ref[...], preferred_element_type=F32))
    o_ref[...] = x_ref[...] + jnp.dot(m.astype(BF16), wo_ref[...], preferred_element_type=F32)


def _merge(x, a, b, c, proj, w1, w2, w3, wo):
    rows = x.shape[0]
    tm = min(rows, 512)
    rowb = pl.BlockSpec((tm, 1024), lambda i: (i, 0))
    gate = lambda k: pl.BlockSpec((tm, 1024), lambda i: (i, COL_GATE + k))
    wsp = pl.BlockSpec((1024, 1024), lambda i: (0, 0))
    return pl.pallas_call(
        _merge_kernel,
        out_shape=jax.ShapeDtypeStruct((rows, 1024), F32),
        grid=(rows // tm,),
        in_specs=[rowb, rowb, rowb, rowb, gate(0), gate(1), gate(2), wsp, wsp, wsp, wsp],
        out_specs=rowb,
        compiler_params=_cparams(("parallel",)),
        name="merge",
    )(x, a, b, c, proj, proj, proj, w1, w2, w3, wo)


FF_TILE = 1408
FF_SPLIT = 768


def _ffn_kernel(x_ref, g_ref, wg_ref, wu_ref, wd_ref, o_ref, h_ref, acc_ref):
    j = pl.program_id(1)

    @pl.when(j == 0)
    def _():
        h_ref[...] = _rms(x_ref[...], g_ref[...]).astype(BF16)
        acc_ref[...] = jnp.zeros_like(acc_ref)

    h = h_ref[...]
    for lo, hi in ((0, FF_SPLIT), (FF_SPLIT, FF_TILE)):
        a = jnp.dot(h, wg_ref[:, lo:hi], preferred_element_type=F32)
        u = jnp.dot(h, wu_ref[:, lo:hi], preferred_element_type=F32)
        acc_ref[...] += jnp.dot((_silu(a) * u).astype(BF16), wd_ref[lo:hi, :], preferred_element_type=F32)

    @pl.when(j == pl.num_programs(1) - 1)
    def _():
        o_ref[...] = x_ref[...] + acc_ref[...]


def _ffn(x, g, wg, wu, wd):
    rows = x.shape[0]
    tm = min(rows, 1024)
    return pl.pallas_call(
        _ffn_kernel,
        out_shape=jax.ShapeDtypeStruct((rows, 1024), F32),
        grid=(rows // tm, D_FF // FF_TILE),
        in_specs=[pl.BlockSpec((tm, 1024), lambda i, j: (i, 0)), pl.BlockSpec((1, 1024), lambda i, j: (0, 0)),
                  pl.BlockSpec((1024, FF_TILE), lambda i, j: (0, j)), pl.BlockSpec((1024, FF_TILE), lambda i, j: (0, j)),
                  pl.BlockSpec((FF_TILE, 1024), lambda i, j: (j, 0))],
        out_specs=pl.BlockSpec((tm, 1024), lambda i, j: (i, 0)),
        scratch_shapes=[pltpu.VMEM((tm, 1024), BF16), pltpu.VMEM((tm, 1024), F32)],
        compiler_params=_cparams(("parallel", "arbitrary"), vmem=VMEM_LIMIT_FFN),
        name="ffn",
    )(x, g, wg, wu, wd)


MOE_FF_TILE = 256


def _top2(h, rw_hi, rw_lo, rb, lane):
    h_hi = h.astype(BF16)
    h_lo = (h - h_hi.astype(F32)).astype(BF16)
    d = lambda a, b: jnp.dot(a, b, preferred_element_type=F32)
    logits = d(h_hi, rw_hi) + d(h_hi, rw_lo) + d(h_lo, rw_hi) + rb
    logits = jnp.where(lane < N_EXPERTS, logits, -jnp.inf)
    m1 = jnp.max(logits, axis=1, keepdims=True)
    i1 = jnp.min(jnp.where(logits == m1, lane, float(LANES)), axis=1, keepdims=True)
    rest = jnp.where(lane == i1, -jnp.inf, logits)
    m2 = jnp.max(rest, axis=1, keepdims=True)
    i2 = jnp.min(jnp.where(rest == m2, lane, float(LANES)), axis=1, keepdims=True)
    e2 = jnp.exp(m2 - m1)
    p1 = 1.0 / (1.0 + e2)
    return i1, i2, p1, e2 * p1


def _moe_kernel(x_ref, g_ref, rw_ref, rb_ref, wg_ref, wu_ref, wd_ref, o_ref, h_ref, acc_ref, comb_ref):
    e = pl.program_id(1)
    j = pl.program_id(2)
    tm = x_ref.shape[0]
    lane = lax.broadcasted_iota(jnp.int32, (tm, LANES), 1).astype(F32)

    @pl.when(jnp.logical_and(e == 0, j == 0))
    def _():
        h = _rms(x_ref[...], g_ref[...])
        h_ref[...] = h.astype(BF16)
        i1, i2, p1, p2 = _top2(h, rw_ref[0], rw_ref[1], rb_ref[...], lane)
        comb_ref[...] = jnp.where(lane == i1, p1, 0.0) + jnp.where(lane == i2, p2, 0.0)
        acc_ref[...] = jnp.zeros_like(acc_ref)

    ce = jnp.sum(jnp.where(lane == e.astype(F32), comb_ref[...], 0.0), axis=1, keepdims=True)
    h = h_ref[...]
    a = jnp.dot(h, wg_ref[0], preferred_element_type=F32)
    u = jnp.dot(h, wu_ref[0], preferred_element_type=F32)
    acc_ref[...] += ce * jnp.dot((_silu(a) * u).astype(BF16), wd_ref[0], preferred_element_type=F32)

    @pl.when(jnp.logical_and(e == pl.num_programs(1) - 1, j == pl.num_programs(2) - 1))
    def _():
        o_ref[...] = x_ref[...] + acc_ref[...]


def _moe(x, g, rw, rb, wg, wu, wd):
    rows = x.shape[0]
    tm = min(rows, 1024)
    tf = MOE_FF_TILE
    return pl.pallas_call(
        _moe_kernel,
        out_shape=jax.ShapeDtypeStruct((rows, 1024), F32),
        grid=(rows // tm, N_EXPERTS, D_FF // tf),
        in_specs=[pl.BlockSpec((tm, 1024), lambda i, e, j: (i, 0)), pl.BlockSpec((1, 1024), lambda i, e, j: (0, 0)),
                  pl.BlockSpec((2, 1024, LANES), lambda i, e, j: (0, 0, 0)),
                  pl.BlockSpec((1, LANES), lambda i, e, j: (0, 0)),
                  pl.BlockSpec((1, 1024, tf), lambda i, e, j: (e, 0, j)),
                  pl.BlockSpec((1, 1024, tf), lambda i, e, j: (e, 0, j)),
                  pl.BlockSpec((1, tf, 1024), lambda i, e, j: (e, j, 0))],
        out_specs=pl.BlockSpec((tm, 1024), lambda i, e, j: (i, 0)),
        scratch_shapes=[pltpu.VMEM((tm, 1024), BF16), pltpu.VMEM((tm, 1024), F32), pltpu.VMEM((tm, LANES), F32)],
        compiler_params=_cparams(("parallel", "arbitrary", "arbitrary")),
        name="moe",
    )(x, g, rw, rb, wg, wu, wd)


MOE_ROWS = 512
MOE_GROUP_FF = 1408
GATHER_UNROLL = 8


def _router_kernel(x_ref, g_ref, rw_ref, rb_ref, o_ref):
    tm = x_ref.shape[0]
    lane = lax.broadcasted_iota(jnp.int32, (tm, LANES), 1).astype(F32)
    i1, i2, p1, p2 = _top2(_rms(x_ref[...], g_ref[...]), rw_ref[0], rw_ref[1], rb_ref[...], lane)
    o_ref[...] = jnp.where(lane == 0.0, i1, jnp.where(lane == 1.0, i2, jnp.where(lane == 2.0, p1,
                           jnp.where(lane == 3.0, p2, 0.0))))


def _router(x, g, rw, rb):
    rows = x.shape[0]
    tm = min(rows, 1024)
    return pl.pallas_call(
        _router_kernel,
        out_shape=jax.ShapeDtypeStruct((rows, LANES), F32),
        grid=(rows // tm,),
        in_specs=[pl.BlockSpec((tm, 1024), lambda i: (i, 0)), pl.BlockSpec((1, 1024), lambda i: (0, 0)),
                  pl.BlockSpec((2, 1024, LANES), lambda i: (0, 0, 0)), pl.BlockSpec((1, LANES), lambda i: (0, 0))],
        out_specs=pl.BlockSpec((tm, LANES), lambda i: (i, 0)),
        compiler_params=_cparams(("parallel",)),
        name="router",
    )(x, g, rw, rb)


def _route_plan(route, tm):
    n = route.shape[0]
    n_tiles = (2 * n) // tm + N_EXPERTS
    e_flat = route[:, :2].astype(jnp.int32).reshape(-1)
    onehot = (e_flat[:, None] == jnp.arange(N_EXPERTS, dtype=jnp.int32)[None, :]).astype(jnp.int32)
    csum = jnp.cumsum(onehot, axis=0)
    counts = csum[-1]
    tiles_e = (counts + tm - 1) // tm
    tile_end = jnp.cumsum(tiles_e)
    row_start = (tile_end - tiles_e) * tm
    pos = jnp.sum((csum - onehot + row_start[None, :]) * onehot, axis=1).astype(jnp.int32)
    tile_expert = jnp.minimum(jnp.sum(jnp.arange(n_tiles, dtype=jnp.int32)[:, None] >= tile_end[None, :], axis=1),
                              N_EXPERTS - 1).astype(jnp.int32)
    n_used = tile_end[-1:].astype(jnp.int32)
    src = jnp.zeros((n_tiles * tm,), jnp.int32).at[pos].set(jnp.arange(2 * n, dtype=jnp.int32) // 2)
    return pos, src.reshape(n_tiles, 1, tm), tile_expert, n_used


def _moe_group_kernel(te_ref, nu_ref, src_ref, srcn_ref, x_hbm, g_ref, wg_ref, wu_ref, wd_ref,
                      y_ref, buf, sem, h_ref, acc_ref):
    i = pl.program_id(0)
    j = pl.program_id(1)
    tm = buf.shape[1]
    slot = i % 2
    active = i < nu_ref[0]

    def row_copy(idx_ref, s, r):
        return pltpu.make_async_copy(x_hbm.at[pl.ds(idx_ref[0, 0, r], 1), :], buf.at[s, pl.ds(r, 1), :], sem.at[s])

    def gather(idx_ref, s):
        def body(r, c):
            row_copy(idx_ref, s, r).start()
            return c
        lax.fori_loop(0, tm, body, 0, unroll=GATHER_UNROLL)

    @pl.when(jnp.logical_and(active, j == 0))
    def _():
        @pl.when(i == 0)
        def _():
            gather(src_ref, 0)

        pltpu.make_async_copy(x_hbm.at[pl.ds(0, tm), :], buf.at[slot], sem.at[slot]).wait()

        @pl.when(i + 1 < nu_ref[0])
        def _():
            gather(srcn_ref, 1 - slot)

        h_ref[...] = _rms(buf[slot], g_ref[...]).astype(BF16)
        acc_ref[...] = jnp.zeros_like(acc_ref)

    @pl.when(active)
    def _():
        h = h_ref[...]
        a = jnp.dot(h, wg_ref[0], preferred_element_type=F32)
        u = jnp.dot(h, wu_ref[0], preferred_element_type=F32)
        acc_ref[...] += jnp.dot((_silu(a) * u).astype(BF16), wd_ref[0], preferred_element_type=F32)

    @pl.when(j == pl.num_programs(1) - 1)
    def _():
        @pl.when(active)
        def _():
            y_ref[...] = acc_ref[...]

        @pl.when(jnp.logical_not(active))
        def _():
            y_ref[...] = jnp.zeros_like(y_ref)


def _moe_group(tile_expert, n_used, src, x, g, wg, wu, wd):
    n_tiles, _, tm = src.shape
    tf = MOE_GROUP_FF
    grid_spec = pltpu.PrefetchScalarGridSpec(
        num_scalar_prefetch=2,
        grid=(n_tiles, D_FF // tf),
        in_specs=[pl.BlockSpec((1, 1, tm), lambda i, j, te, nu: (i, 0, 0), memory_space=pltpu.SMEM),
                  pl.BlockSpec((1, 1, tm), lambda i, j, te, nu: (jnp.minimum(i + 1, n_tiles - 1), 0, 0),
                               memory_space=pltpu.SMEM),
                  pl.BlockSpec(memory_space=pl.ANY),
                  pl.BlockSpec((1, 1024), lambda i, j, te, nu: (0, 0)),
                  pl.BlockSpec((1, 1024, tf), lambda i, j, te, nu: (te[i], 0, j)),
                  pl.BlockSpec((1, 1024, tf), lambda i, j, te, nu: (te[i], 0, j)),
                  pl.BlockSpec((1, tf, 1024), lambda i, j, te, nu: (te[i], j, 0))],
        out_specs=pl.BlockSpec((tm, 1024), lambda i, j, te, nu: (i, 0)),
        scratch_shapes=[pltpu.VMEM((2, tm, 1024), F32), pltpu.SemaphoreType.DMA((2,)),
                        pltpu.VMEM((tm, 1024), BF16), pltpu.VMEM((tm, 1024), F32)])
    return pl.pallas_call(
        _moe_group_kernel,
        out_shape=jax.ShapeDtypeStruct((n_tiles * tm, 1024), F32),
        grid_spec=grid_spec,
        compiler_params=_cparams(("arbitrary", "arbitrary")),
        name="moe_group",
    )(tile_expert, n_used, src, src, x, g, wg, wu, wd)


def _moe_combine_kernel(pos_ref, posn_ref, x_ref, r_ref, y_hbm, o_ref, bufa, bufb, sem):
    i = pl.program_id(0)
    tm = x_ref.shape[0]
    slot = i % 2

    def gather(idx_ref, s):
        def body(t, c):
            pltpu.make_async_copy(y_hbm.at[pl.ds(idx_ref[0, 0, 2 * t], 1), :], bufa.at[s, pl.ds(t, 1), :],
                                  sem.at[0, s]).start()
            pltpu.make_async_copy(y_hbm.at[pl.ds(idx_ref[0, 0, 2 * t + 1], 1), :], bufb.at[s, pl.ds(t, 1), :],
                                  sem.at[1, s]).start()
            return c
        lax.fori_loop(0, tm, body, 0, unroll=GATHER_UNROLL)

    @pl.when(i == 0)
    def _():
        gather(pos_ref, 0)

    pltpu.make_async_copy(y_hbm.at[pl.ds(0, tm), :], bufa.at[slot], sem.at[0, slot]).wait()
    pltpu.make_async_copy(y_hbm.at[pl.ds(0, tm), :], bufb.at[slot], sem.at[1, slot]).wait()

    @pl.when(i + 1 < pl.num_programs(0))
    def _():
        gather(posn_ref, 1 - slot)

    r = r_ref[...]
    o_ref[...] = x_ref[...] + r[:, 2:3] * bufa[slot] + r[:, 3:4] * bufb[slot]


def _moe_combine(pos, x, route, y):
    rows = x.shape[0]
    tm = MOE_ROWS
    n = rows // tm
    pos3 = pos.reshape(n, 1, 2 * tm)
    return pl.pallas_call(
        _moe_combine_kernel,
        out_shape=jax.ShapeDtypeStruct((rows, 1024), F32),
        grid=(n,),
        in_specs=[pl.BlockSpec((1, 1, 2 * tm), lambda i: (i, 0, 0), memory_space=pltpu.SMEM),
                  pl.BlockSpec((1, 1, 2 * tm), lambda i: (jnp.minimum(i + 1, n - 1), 0, 0), memory_space=pltpu.SMEM),
                  pl.BlockSpec((tm, 1024), lambda i: (i, 0)), pl.BlockSpec((tm, LANES), lambda i: (i, 0)),
                  pl.BlockSpec(memory_space=pl.ANY)],
        out_specs=pl.BlockSpec((tm, 1024), lambda i: (i, 0)),
        scratch_shapes=[pltpu.VMEM((2, tm, 1024), F32), pltpu.VMEM((2, tm, 1024), F32),
                        pltpu.SemaphoreType.DMA((2, 2))],
        compiler_params=_cparams(("arbitrary",)),
        name="moe_combine",
    )(pos3, pos3, x, route, y)


def _moe_routed(x, g, rw, rb, wg, wu, wd):
    route = _router(x, g, rw, rb)
    pos, src, tile_expert, n_used = _route_plan(route, MOE_ROWS)
    y = _moe_group(tile_expert, n_used, src, x, g, wg, wu, wd)
    return _moe_combine(pos, x, route, y)


def _ple_kernel(x_ref, p_ref, g_ref, wp_ref, wgt_ref, gf_ref, o_ref, *, final):
    x = x_ref[...]
    emb = jnp.dot(p_ref[...].astype(BF16), wp_ref[...], preferred_element_type=F32)
    gate = _sigmoid(jnp.dot(_rms(x, g_ref[...]).astype(BF16), wgt_ref[...], preferred_element_type=F32))
    y = x + emb * gate
    if final:
        y = _rms(y, gf_ref[...])
    o_ref[...] = y


def _ple(x, p, g, wp, wgt, gf, final):
    rows = x.shape[0]
    tm = min(rows, 512)
    vec = pl.BlockSpec((1, 1024), lambda i: (0, 0))
    return pl.pallas_call(
        functools.partial(_ple_kernel, final=final),
        out_shape=jax.ShapeDtypeStruct((rows, 1024), F32),
        grid=(rows // tm,),
        in_specs=[pl.BlockSpec((tm, 1024), lambda i: (i, 0)), pl.BlockSpec((tm, PLE_DIM), lambda i: (i, 0)), vec,
                  pl.BlockSpec((PLE_DIM, 1024), lambda i: (0, 0)), pl.BlockSpec((1024, 1024), lambda i: (0, 0)), vec],
        out_specs=pl.BlockSpec((tm, 1024), lambda i: (i, 0)),
        compiler_params=_cparams(("parallel",)),
        name="ple",
    )(x, p, g, wp, wgt, gf)


REPACK_ROWS = 128


def _repack_w_in_kernel(w_ref, o_ref):
    offs = [0]
    for wd in IN_WIDTHS:
        offs.append(offs[-1] + wd)
    z, xbc, dt, qa, ka, va, qr, kr, vr, gr, gates = [(offs[i], offs[i + 1]) for i in range(len(IN_WIDTHS))]
    take = lambda lo, hi: w_ref[:, lo:hi].astype(BF16)
    q = take(*qa)
    heads = lambda h: q[:, h * ATT_HEAD_DIM:(h + 1) * ATT_HEAD_DIM]
    q_rmajor = jnp.concatenate([heads(ATT_REP * g + r) for r in range(ATT_REP) for g in range(ATT_KV_HEADS)], axis=1)
    parts = [take(*xbc), take(*z), q_rmajor] + [take(*p) for p in (qr, kr, vr, gr, gates, ka, va)]
    used = sum(p.shape[1] for p in parts) + (dt[1] - dt[0])
    parts.append(jnp.concatenate([take(*dt), jnp.zeros((o_ref.shape[0], N_PROJ - used), BF16)], axis=1))
    dst = 0
    for part in parts:
        o_ref[:, dst:dst + part.shape[1]] = part
        dst += part.shape[1]


def _repack_w_in(w_all, layer):
    rows = w_all.shape[1]
    return pl.pallas_call(
        _repack_w_in_kernel,
        out_shape=jax.ShapeDtypeStruct((rows, N_PROJ), BF16),
        grid=(rows // REPACK_ROWS,),
        in_specs=[pl.BlockSpec((None, REPACK_ROWS, w_all.shape[2]), lambda i: (layer, i, 0))],
        out_specs=pl.BlockSpec((REPACK_ROWS, N_PROJ), lambda i: (i, 0)),
        compiler_params=_cparams(("parallel",)),
        name="repack_w_in",
    )(w_all)


def _att_tables(pos):
    half = ROPE_DIM // 2
    inv = jnp.exp(-math.log(ROPE_THETA) * jnp.arange(half, dtype=F32) * (2.0 / ROPE_DIM))
    ang = pos.astype(F32)[:, None] * inv[None, :]
    cos, sin = jnp.cos(ang), jnp.sin(ang)
    n = pos.shape[0]
    one = jnp.ones((n, ATT_HEAD_DIM - ROPE_DIM), F32)
    zero8 = jnp.zeros((n, half), F32)
    zero = jnp.zeros((n, ATT_HEAD_DIM - ROPE_DIM), F32)
    c = jnp.concatenate([cos, cos, one], axis=1)
    s1 = jnp.concatenate([zero8, sin, zero], axis=1)
    s2 = jnp.concatenate([-sin, zero8, zero], axis=1)
    return tuple(jnp.concatenate([t, t], axis=1) for t in (c, s1, s2))


def _ret_tables(pos):
    half = RET_DK // 2
    inv = jnp.exp(-math.log(RET_THETA) * jnp.arange(half, dtype=F32) * (2.0 / RET_DK))
    ang = pos.astype(F32)[:, None] * inv[None, :]
    cos, sin = jnp.cos(ang), jnp.sin(ang)
    return jnp.concatenate([cos, cos], axis=1), jnp.concatenate([-sin, sin], axis=1)


def _rope_step_tables(inv_lane, seq):
    per = MIX_ROWS // CHUNK
    ang_i = jnp.arange(CHUNK, dtype=F32)[:, None] * inv_lane[None, :]
    within = jnp.concatenate([jnp.cos(ang_i), jnp.sin(ang_i)], axis=1)
    ang_b = (jnp.arange(seq // CHUNK, dtype=F32) * CHUNK)[:, None] * inv_lane[None, :]
    blk = jnp.concatenate([jnp.cos(ang_b), jnp.sin(ang_b)], axis=1).reshape(seq // MIX_ROWS, per, 2 * LANES)
    steps = jnp.concatenate([blk, jnp.zeros((seq // MIX_ROWS, 8 - per, 2 * LANES), F32)], axis=1)
    return within, steps


def _att_inv_lanes():
    inv = jnp.exp(-math.log(ROPE_THETA) * jnp.arange(ROPE_DIM // 2, dtype=F32) * (2.0 / ROPE_DIM))
    return jnp.tile(inv, LANES // (ROPE_DIM // 2))


def _ret_inv_lanes():
    inv = jnp.exp(-math.log(RET_THETA) * jnp.arange(RET_DK // 2, dtype=F32) * (2.0 / RET_DK))
    return jnp.tile(inv, 2)


def _pad_lanes(v, fill=0.0):
    return jnp.concatenate([v.astype(F32), jnp.full((LANES - v.shape[0],), fill, F32)])[None, :]


def kernel(x_prompt, x_sample, state_ssm, state_conv, cache_win_k, cache_win_v, state_ret, p_prompt, p_sample,
           w_in, conv_w, conv_b, dt_bias, a_log, d_skip, ssd_norm, attn_sinks, ret_norm, w_o_ssd, w_o_att, w_o_ret,
           w_out, norm_mix, norm_ffn, norm_ple, ffn_w_gate, ffn_w_up, ffn_w_down, router_w, router_b, moe_w_gate,
           moe_w_up, moe_w_down, w_ple, w_ple_gate, norm_final):
    seq = x_prompt.shape[1]
    nb = x_sample.shape[0]
    xp = x_prompt.reshape(seq, D_MODEL)
    xs = x_sample.reshape(nb, D_MODEL)
    pos_s = PAST_LEN + jnp.arange(1)
    att_tab_p, att_tab_s = _rope_step_tables(_att_inv_lanes(), seq), _att_tables(pos_s)
    ret_tab_p, ret_tab_s = _rope_step_tables(_ret_inv_lanes(), seq), _ret_tables(pos_s)
    log_gamma = jnp.log1p(-jnp.exp2(-5.0 - jnp.arange(RET_HEADS, dtype=F32)))
    gamma = jnp.exp(log_gamma)
    row = lambda v: v.astype(F32)[None, :]

    ssm_all = state_ssm.reshape(DEPTH, nb, SSD_HEADS * SSD_HEAD_DIM, SSD_STATE)
    ret_all = state_ret.reshape(DEPTH, nb, RET_HEADS * RET_DK, 128)
    kc_all = jnp.transpose(cache_win_k, (0, 1, 3, 4, 2))
    vc_all = jnp.transpose(cache_win_v, (0, 1, 3, 4, 2))
    ssm_s = ret_s = kv_s = None
    conv_s = []

    new_p = [[], [], [], [], []]
    for i in range(DEPTH):
        w_in_i = _repack_w_in(w_in, i)
        cw, cb = conv_w[i], row(conv_b[i])
        dtb, alog = _pad_lanes(dt_bias[i]), _pad_lanes(a_log[i])
        dsk = row(jnp.repeat(d_skip[i], SSD_HEAD_DIM))
        nrm_ssd, nrm_ret = row(ssd_norm[i]), row(ret_norm[i])
        sinks = attn_sinks[i].astype(F32)
        sink32 = jnp.zeros((ATT_KV_HEADS, 8), F32).at[:, :ATT_REP].set(sinks.reshape(ATT_KV_HEADS, ATT_REP))
        sink32 = jnp.broadcast_to(sink32.reshape(32, 1), (32, LANES))
        w1 = w_o_ssd[i].astype(BF16)
        w2 = w_o_att[i].reshape(ATT_KV_HEADS, ATT_REP, ATT_HEAD_DIM, D_MODEL).transpose(1, 0, 2, 3) \
            .reshape(ATT_Q_HEADS * ATT_HEAD_DIM, D_MODEL).astype(BF16)
        w3 = w_o_ret[i].astype(BF16)
        wo = w_out[i].astype(BF16)
        g_mix, g_ffn, g_ple = row(norm_mix[i]), row(norm_ffn[i]), row(norm_ple[i])
        wp, wpg = w_ple[i].astype(BF16), w_ple_gate[i].astype(BF16)
        gf = row(norm_final)
        j = i // 2
        if i % 2 == 0:
            ffw = (ffn_w_gate[j].astype(BF16), ffn_w_up[j].astype(BF16), ffn_w_down[j].astype(BF16))
        else:
            rw = jnp.concatenate([router_w[j], jnp.zeros((D_MODEL, LANES - N_EXPERTS), F32)], axis=1)
            rw_hi = rw.astype(BF16)
            rw = jnp.stack([rw_hi, (rw - rw_hi.astype(F32)).astype(BF16)])
            ffw = (rw, _pad_lanes(router_b[j]), moe_w_gate[j].astype(BF16), moe_w_up[j].astype(BF16),
                   moe_w_down[j].astype(BF16))
        final = i == DEPTH - 1

        proj, dt = _inproj(xp, g_mix, w_in_i, BF16)
        y_ssd, ssm_fin, conv_fin = _ssd_prompt(proj, dt, cw, cb, dtb, alog, dsk, nrm_ssd)
        o_att, wk, wv = _swa_prompt(sinks, proj, att_tab_p)
        o_ret, ret_fin = _ret_prompt(log_gamma, proj, ret_tab_p, nrm_ret)
        xp = _merge(xp, y_ssd, o_att, o_ret, proj, w1, w2, w3, wo)
        xp = _ffn(xp, g_ffn, *ffw) if i % 2 == 0 else _moe_routed(xp, g_ffn, *ffw)
        xp = _ple(xp, p_prompt[i].reshape(seq, PLE_DIM), g_ple, wp, wpg, gf, final)
        new_p[0].append(ssm_fin.reshape(1, SSD_HEADS, SSD_HEAD_DIM, SSD_STATE))
        new_p[1].append(conv_fin[None])
        new_p[2].append(wk.reshape(1, WINDOW, ATT_KV_HEADS, ATT_HEAD_DIM))
        new_p[3].append(wv.reshape(1, WINDOW, ATT_KV_HEADS, ATT_HEAD_DIM))
        new_p[4].append(ret_fin[None])

        proj, dt = _inproj(xs, g_mix, w_in_i, F32)
        cst_t = jnp.transpose(state_conv[i], (1, 0, 2))
        cnew_t, xs_conv, bc, dec_t, xdt_t = _ssd_dec_pre(proj, dt, cst_t, cw, cb, dtb, alog)
        y_ssd, ssm_s = _ssd_dec_state(i, dec_t, xdt_t, bc, proj, xs_conv, ssm_all, dsk, nrm_ssd, ssm_s)
        qa_rot, ka_t, va_t = _swa_dec_pre(proj, att_tab_s)
        o_att, *kv_s = _swa_dec(i, qa_rot, ka_t, va_t, kc_all, vc_all, sink32, kv_s)
        qrot, kt = _ret_dec_pre(proj, ret_tab_s)
        o_ret, ret_s = _ret_dec_state(i, gamma, kt, qrot, proj, ret_all, nrm_ret, ret_s)
        xs = _merge(xs, y_ssd, o_att, o_ret, proj, w1, w2, w3, wo)
        xs = _ffn(xs, g_ffn, *ffw) if i % 2 == 0 else _moe(xs, g_ffn, *ffw)
        xs = _ple(xs, p_sample[i].reshape(nb, PLE_DIM), g_ple, wp, wpg, gf, final)
        conv_s.append(jnp.transpose(cnew_t, (1, 0, 2)))

    y_prompt = xp.reshape(1, seq, D_MODEL)
    y_sample = xs.reshape(nb, 1, D_MODEL)
    outs_p = [jnp.stack(l) for l in new_p]
    outs_s = [ssm_s.reshape(state_ssm.shape), jnp.stack(conv_s), jnp.transpose(kv_s[0], (0, 1, 4, 2, 3)),
              jnp.transpose(kv_s[1], (0, 1, 4, 2, 3)), ret_s.reshape(state_ret.shape)]
    return (y_prompt, y_sample, *outs_p, *outs_s)
```

```python
import functools
import math

import jax
import jax.numpy as jnp
from jax import lax
from jax.experimental import pallas as pl
from jax.experimental.pallas import tpu as pltpu

F32 = jnp.float32
BF16 = jnp.bfloat16

D_MODEL = 1024
DEPTH = 2
PAST_LEN = 16384
SSD_HEADS = 16
SSD_HEAD_DIM = 64
SSD_GROUPS = 4
SSD_STATE = 128
SSD_CONV = 4
SSD_CONV_DIM = 2048
ATT_HEAD_DIM = 64
ATT_Q_HEADS = 16
ATT_KV_HEADS = 4
ATT_REP = ATT_Q_HEADS // ATT_KV_HEADS
WINDOW = 128
ROPE_THETA = 500000.0
ROPE_DIM = 16
RET_HEADS = 8
RET_DK = 128
RET_THETA = 10000.0
CHUNK = 128
D_FF = 2816
N_EXPERTS = 8
PLE_DIM = 256
EPS = 1e-6

IN_WIDTHS = (1024, 2048, 16, 1024, 256, 256, 1024, 1024, 1024, 1024, 3072)
N_PROJ = 12288
COL_XBC = 0
COL_Z = 2
COL_QA = 3
COL_QR = 4
COL_KR = 5
COL_VR = 6
COL_GR = 7
COL_GATE = 8
COL_KA = 44
COL_VA = 45
COL_DT = 92

LANES = 128
VMEM_LIMIT = 48 * 1024 * 1024
VMEM_LIMIT_FFN = 56 * 1024 * 1024
MIX_ROWS = 4 * CHUNK


def _cparams(sem, vmem=VMEM_LIMIT):
    return pltpu.CompilerParams(dimension_semantics=sem, vmem_limit_bytes=vmem)


def _bdot(a, b):
    return jnp.dot(a.astype(BF16), b.astype(BF16), preferred_element_type=F32)


def _bdot_nt(a, b):
    return lax.dot_general(a.astype(BF16), b.astype(BF16), (((1,), (1,)), ((), ())),
                           preferred_element_type=F32)


def _split3(x):
    x0 = x.astype(BF16)
    r1 = x - x0.astype(F32)
    x1 = r1.astype(BF16)
    x2 = (r1 - x1.astype(F32)).astype(BF16)
    return x0, x1, x2


def _dot_exact_lhs01(m01, x):
    m = m01.astype(BF16)
    x0, x1, x2 = _split3(x)
    d = lambda b: jnp.dot(m, b, preferred_element_type=F32)
    return d(x0) + d(x1) + d(x2)


def _dot_exact_rhs01(x, m01):
    m = m01.astype(BF16)
    x0, x1, x2 = _split3(x)
    d = lambda a: jnp.dot(a, m, preferred_element_type=F32)
    return d(x0) + d(x1) + d(x2)


def _rms(x, g):
    return x * lax.rsqrt(jnp.mean(x * x, axis=-1, keepdims=True) + EPS) * g


def _sigmoid(x):
    return 1.0 / (1.0 + jnp.exp(-x))


def _silu(x):
    return x * _sigmoid(x)


def _softplus(x):
    return jnp.maximum(x, 0.0) + jnp.log1p(jnp.exp(-jnp.abs(x)))


def _rope_att(x, c, s1, s2):
    w = x.shape[1]
    return x * c + pltpu.roll(x, 8, axis=1) * s1 + pltpu.roll(x, w - 8, axis=1) * s2


def _tile_lanes(t, n):
    return jnp.concatenate([t] * n, axis=1) if n > 1 else t


INPROJ_TN = 1024
DT_TILE = (COL_DT * LANES) // INPROJ_TN
DT_OFF = COL_DT * LANES - DT_TILE * INPROJ_TN


def _inproj_kernel(x_ref, g_ref, w_ref, o_ref, dt_ref, h_ref):
    j = pl.program_id(1)

    @pl.when(j == 0)
    def _():
        h_ref[...] = _rms(x_ref[...], g_ref[...]).astype(BF16)

    acc = jnp.dot(h_ref[...], w_ref[...], preferred_element_type=F32)
    o_ref[...] = acc.astype(o_ref.dtype)

    @pl.when(j == DT_TILE)
    def _():
        dt_ref[...] = acc[:, DT_OFF:DT_OFF + LANES]


def _inproj(x, g, w, out_dtype):
    rows = x.shape[0]
    tm = min(rows, 2048)
    tn = INPROJ_TN
    return pl.pallas_call(
        _inproj_kernel,
        out_shape=(jax.ShapeDtypeStruct((rows, N_PROJ), out_dtype), jax.ShapeDtypeStruct((rows, LANES), F32)),
        grid=(rows // tm, N_PROJ // tn),
        in_specs=[pl.BlockSpec((tm, D_MODEL), lambda i, j: (i, 0)),
                  pl.BlockSpec((1, D_MODEL), lambda i, j: (0, 0)),
                  pl.BlockSpec((D_MODEL, tn), lambda i, j: (0, j))],
        out_specs=(pl.BlockSpec((tm, tn), lambda i, j: (i, j)), pl.BlockSpec((tm, LANES), lambda i, j: (i, 0))),
        scratch_shapes=[pltpu.VMEM((tm, D_MODEL), BF16)],
        compiler_params=_cparams(("parallel", "arbitrary")),
        name="inproj",
    )(x, g, w)


def _chunk_rows(ci):
    return pl.ds(pl.multiple_of(ci * CHUNK, CHUNK), CHUNK)


def _ssd_prompt_chunk(rows, xbc_ref, z_ref, dt_ref, cw_ref, cb_ref, dtb_ref, alog_ref, dsk_ref, nrm_ref,
                      y_ref, xpad_ref, s_ref):
    xbc = xbc_ref[rows, :].astype(F32)
    xpad_ref[8:8 + CHUNK, :] = xbc
    cw = cw_ref[...]
    acc = (xbc * cw[3:4, :] + xpad_ref[7:7 + CHUNK, :] * cw[2:3, :]
           + xpad_ref[6:6 + CHUNK, :] * cw[1:2, :] + xpad_ref[5:5 + CHUNK, :] * cw[0:1, :] + cb_ref[...])
    conv = _silu(acc)
    xpad_ref[0:8, :] = xbc[CHUNK - 8:CHUNK, :]

    xs = conv[:, :1024]
    dt_t = _softplus(dt_ref[rows, :].T[:SSD_HEADS, :] + dtb_ref[...])
    la_t = dt_t * (-jnp.exp(alog_ref[...]))
    row = lax.broadcasted_iota(jnp.int32, (CHUNK, CHUNK), 0)
    col = lax.broadcasted_iota(jnp.int32, (CHUNK, CHUNK), 1)
    causal = row >= col
    cum_t = _dot_exact_rhs01(la_t, (row <= col).astype(F32))
    cum = jnp.concatenate([cum_t, jnp.zeros((LANES - SSD_HEADS, CHUNK), F32)], axis=0).T
    cum_last = jnp.broadcast_to(cum_t[:, CHUNK - 1:CHUNK], (SSD_HEADS, CHUNK))
    w_t = jnp.exp(cum_last - cum_t) * dt_t
    dec_end = jnp.exp(cum_last)

    xs_t = xs.T
    ys = []
    for g in range(SSD_GROUPS):
        bg = conv[:, 1024 + 128 * g:1024 + 128 * (g + 1)]
        cg = conv[:, 1536 + 128 * g:1536 + 128 * (g + 1)]
        cb = _bdot_nt(cg, bg)
        s_g = s_ref[256 * g:256 * (g + 1), :]
        cs = _bdot_nt(cg, s_g)
        xw_parts = []
        dec_parts = []
        for r in range(4):
            h = 4 * g + r
            colb = jnp.broadcast_to(cum[:, h:h + 1], (CHUNK, CHUNK))
            rowb = jnp.broadcast_to(cum_t[h:h + 1, :], (CHUNK, CHUNK))
            dec = jnp.exp(jnp.where(causal, colb - rowb, -jnp.inf))
            m = cb * dec * jnp.broadcast_to(dt_t[h:h + 1, :], (CHUNK, CHUNK))
            xh = xs[:, 64 * h:64 * (h + 1)]
            yh = _bdot(m, xh) + cs[:, 64 * r:64 * (r + 1)] * jnp.exp(colb)[:, :64]
            ys.append(yh)
            xw_parts.append(xs_t[64 * h:64 * (h + 1), :] * jnp.broadcast_to(w_t[h:h + 1, :], (64, CHUNK)))
            dec_parts.append(jnp.broadcast_to(dec_end[h:h + 1, :], (64, SSD_STATE)))
        xw = jnp.concatenate(xw_parts, axis=0)
        s_ref[256 * g:256 * (g + 1), :] = s_g * jnp.concatenate(dec_parts, axis=0) + _bdot(xw, bg)

    y = jnp.concatenate(ys, axis=1) + dsk_ref[...] * xs
    y = y * _silu(z_ref[rows, :].astype(F32))
    y_ref[rows, :] = _rms(y, nrm_ref[...]).astype(y_ref.dtype)


def _ssd_prompt_kernel(xbc_ref, z_ref, dt_ref, cw_ref, cb_ref, dtb_ref, alog_ref, dsk_ref, nrm_ref,
                       y_ref, sfin_ref, cfin_ref, xpad_ref, s_ref):
    t = pl.program_id(0)

    @pl.when(t == 0)
    def _():
        xpad_ref[0:8, :] = jnp.zeros((8, SSD_CONV_DIM), F32)
        s_ref[...] = jnp.zeros_like(s_ref)

    def chunk(ci, carry):
        _ssd_prompt_chunk(_chunk_rows(ci), xbc_ref, z_ref, dt_ref, cw_ref, cb_ref, dtb_ref, alog_ref, dsk_ref,
                          nrm_ref, y_ref, xpad_ref, s_ref)
        return carry

    lax.fori_loop(0, xbc_ref.shape[0] // CHUNK, chunk, 0)

    @pl.when(t == pl.num_programs(0) - 1)
    def _():
        sfin_ref[...] = s_ref[...]
        cfin_ref[...] = xpad_ref[8 + CHUNK - (SSD_CONV - 1):8 + CHUNK, :]


def _ssd_prompt(proj, dt, cw, cb, dtb, alog, dsk, nrm):
    seq = proj.shape[0]
    const = lambda shape: pl.BlockSpec(shape, lambda t: (0,) * len(shape))
    return pl.pallas_call(
        _ssd_prompt_kernel,
        out_shape=(jax.ShapeDtypeStruct((seq, 1024), BF16),
                   jax.ShapeDtypeStruct((1024, SSD_STATE), F32),
                   jax.ShapeDtypeStruct((SSD_CONV - 1, SSD_CONV_DIM), F32)),
        grid=(seq // MIX_ROWS,),
        in_specs=[pl.BlockSpec((MIX_ROWS, 2048), lambda t: (t, COL_XBC)),
                  pl.BlockSpec((MIX_ROWS, 1024), lambda t: (t, COL_Z)),
                  pl.BlockSpec((MIX_ROWS, LANES), lambda t: (t, 0)),
                  const((SSD_CONV, 2048)), const((1, 2048)), const((SSD_HEADS, LANES)), const((SSD_HEADS, LANES)),
                  const((1, 1024)), const((1, 1024))],
        out_specs=(pl.BlockSpec((MIX_ROWS, 1024), lambda t: (t, 0)),
                   const((1024, SSD_STATE)), const((SSD_CONV - 1, SSD_CONV_DIM))),
        scratch_shapes=[pltpu.VMEM((8 + CHUNK, SSD_CONV_DIM), F32), pltpu.VMEM((1024, SSD_STATE), F32)],
        compiler_params=_cparams(("arbitrary",)),
        name="ssd_prompt",
    )(proj, proj, dt, cw, cb, dtb, alog, dsk, nrm)


def _ssd_dec_pre_kernel(xbc_ref, dt_ref, cst_ref, cw_ref, cb_ref, dtb_ref, alog_ref,
                        cnew_ref, xs_ref, bc_ref, dec_t_ref, xdt_t_ref):
    xbc = xbc_ref[...]
    cw = cw_ref[...]
    acc = (cst_ref[0] * cw[0:1, :] + cst_ref[1] * cw[1:2, :] + cst_ref[2] * cw[2:3, :]
           + xbc * cw[3:4, :] + cb_ref[...])
    conv = _silu(acc)
    cnew_ref[0] = cst_ref[1]
    cnew_ref[1] = cst_ref[2]
    cnew_ref[2] = xbc
    xs = conv[:, :1024]
    xs_ref[...] = xs
    bc_ref[...] = conv[:, 1024:]
    dt = _softplus(dt_ref[...] + dtb_ref[...])
    dec = jnp.exp(dt * (-jnp.exp(alog_ref[...])))
    hrow = lax.broadcasted_iota(jnp.int32, (LANES, 1024), 0)
    hcol = lax.broadcasted_iota(jnp.int32, (LANES, 1024), 1)
    expand = ((hcol >> 6) == hrow).astype(F32)
    dec_t_ref[...] = _dot_exact_rhs01(dec, expand).T
    xdt_t_ref[...] = (xs * _dot_exact_rhs01(dt, expand)).T


def _ssd_dec_pre(proj, dt, cst_t, cw, cb, dtb, alog):
    nb = proj.shape[0]
    const = lambda shape: pl.BlockSpec(shape, lambda t: (0,) * len(shape))
    return pl.pallas_call(
        _ssd_dec_pre_kernel,
        out_shape=(jax.ShapeDtypeStruct((3, nb, 2048), F32), jax.ShapeDtypeStruct((nb, 1024), F32),
                   jax.ShapeDtypeStruct((nb, 1024), F32), jax.ShapeDtypeStruct((1024, nb), F32),
                   jax.ShapeDtypeStruct((1024, nb), F32)),
        grid=(1,),
        in_specs=[pl.BlockSpec((nb, 2048), lambda t: (0, COL_XBC)),
                  pl.BlockSpec((nb, LANES), lambda t: (0, 0)),
                  const((3, nb, 2048)), const((SSD_CONV, 2048)), const((1, 2048)), const((1, 128)), const((1, 128))],
        out_specs=(const((3, nb, 2048)), const((nb, 1024)), const((nb, 1024)), const((1024, nb)), const((1024, nb))),
        compiler_params=_cparams(("arbitrary",)),
        name="ssd_dec_pre",
    )(proj, dt, cst_t, cw, cb, dtb, alog)


DEC_BLOCK = 8


def _ssd_dec_state_kernel(dec_t_ref, xdt_t_ref, bc_ref, z_ref, xs_ref, st_ref, dsk_ref, nrm_ref,
                          y_ref, stn_ref):
    i = pl.program_id(0)
    shift = (LANES - i * DEC_BLOCK) % LANES
    decr = pltpu.roll(dec_t_ref[...], shift, axis=1)
    xr = pltpu.roll(xdt_t_ref[...], shift, axis=1)
    bc = bc_ref[...]
    for j in range(DEC_BLOCK):
        for g in range(SSD_GROUPS):
            lo, hi = 256 * g, 256 * (g + 1)
            s_old = st_ref[j, lo:hi, :]
            dcol = jnp.broadcast_to(decr[lo:hi, j:j + 1], (256, SSD_STATE))
            xcol = jnp.broadcast_to(xr[lo:hi, j:j + 1], (256, SSD_STATE))
            s_new = s_old * dcol + xcol * bc[j:j + 1, 128 * g:128 * (g + 1)]
            stn_ref[j, lo:hi, :] = s_new
            cs = _bdot_nt(bc[:, 512 + 128 * g:512 + 128 * (g + 1)], s_new)
            y_ref[j:j + 1, lo:hi] = cs[j:j + 1, :]
    xs = xs_ref[...]
    y = y_ref[...] + dsk_ref[...] * xs
    y = y * _silu(z_ref[...])
    y_ref[...] = _rms(y, nrm_ref[...])


class _LayerCall:
    def __init__(self, layer, steps):
        self.layer, self.steps, self.first = layer, steps, layer == 0
        self.grid = (DEPTH * steps,) if self.first else (steps,)

    def _block(self, t):
        return jnp.minimum(t, self.steps - 1) if self.first else t

    def rows(self, width, col=0):
        return pl.BlockSpec((DEC_BLOCK, width), lambda t: (self._block(t), col))

    def state_in(self, tail):
        zeros = (0,) * len(tail)
        return pl.BlockSpec((None, DEC_BLOCK) + tail, lambda t: (self.layer, self._block(t)) + zeros)

    def state_out(self, tail):
        zeros = (0,) * len(tail)
        if self.first:
            return pl.BlockSpec((None, DEC_BLOCK) + tail, lambda t: (t // self.steps, t % self.steps) + zeros)
        return pl.BlockSpec((None, DEC_BLOCK) + tail, lambda t: (self.layer, t) + zeros)

    def kernel(self, body, n_in, n_carried, stacked_outs):
        def wrapped(*refs):
            refs = refs[:n_in] + refs[n_in + n_carried:]
            if not self.first:
                body(*refs)
                return
            t = pl.program_id(0)

            @pl.when(t < self.steps)
            def _():
                body(*refs)

            @pl.when(t >= self.steps)
            def _():
                for k in stacked_outs:
                    refs[n_in + k][...] = jnp.zeros_like(refs[n_in + k])
        return wrapped


def _ssd_dec_state(layer, dec_t, xdt_t, bc, proj, xs, st_all, dsk, nrm, carried):
    nb = xs.shape[0]
    lc = _LayerCall(layer, nb // DEC_BLOCK)
    const = lambda shape: pl.BlockSpec(shape, lambda t: (0,) * len(shape))
    tail = (1024, SSD_STATE)
    ins = [dec_t, xdt_t, bc, proj, xs, st_all, dsk, nrm]
    in_specs = [const((1024, nb)), const((1024, nb)), lc.rows(1024), lc.rows(1024, COL_Z), lc.rows(1024),
                lc.state_in(tail), const((1, 1024)), const((1, 1024))]
    n_in = len(ins)
    carried = [] if carried is None else [carried]
    return pl.pallas_call(
        lc.kernel(_ssd_dec_state_kernel, n_in, len(carried), (1,)),
        out_shape=(jax.ShapeDtypeStruct((nb, 1024), F32), jax.ShapeDtypeStruct(st_all.shape, F32)),
        grid=lc.grid,
        in_specs=in_specs + [pl.BlockSpec(memory_space=pl.ANY)] * len(carried),
        out_specs=(lc.rows(1024), lc.state_out(tail)),
        input_output_aliases={n_in + k: 1 + k for k in range(len(carried))},
        compiler_params=_cparams(("arbitrary",)),
        name="ssd_dec_state",
    )(*ins, *carried)


def _block_cos_sin(within_ref, step_ref, bi):
    row = step_ref[pl.ds(bi, 1), :]
    cb, sb = row[:, :LANES], row[:, LANES:]
    ci, si = within_ref[:, :LANES], within_ref[:, LANES:]
    return cb * ci - sb * si, sb * ci + cb * si


def _swa_prompt_block(bi, has_prev, sink_ref, q_ref, k_ref, v_ref, within_ref, step_ref, o_ref, kp_ref, vp_ref):
    rows = _chunk_rows(bi)
    cosp, sinp = _block_cos_sin(within_ref, step_ref, bi)
    l64 = lax.broadcasted_iota(jnp.int32, (1, LANES), 1) & (ATT_HEAD_DIM - 1)
    c = jnp.where(l64 < ROPE_DIM, cosp, 1.0)
    s1 = jnp.where(jnp.logical_and(l64 >= ROPE_DIM // 2, l64 < ROPE_DIM), sinp, 0.0)
    s2 = jnp.where(l64 < ROPE_DIM // 2, -sinp, 0.0)
    q = _rope_att(q_ref[rows, :].astype(F32), _tile_lanes(c, 8), _tile_lanes(s1, 8), _tile_lanes(s2, 8))
    k = _rope_att(k_ref[rows, :].astype(F32), _tile_lanes(c, 2), _tile_lanes(s1, 2), _tile_lanes(s2, 2))
    v = v_ref[rows, :].astype(F32)
    ghead = lax.broadcasted_iota(jnp.int32, (WINDOW, 256), 1) >> 6
    expand = lambda t: jnp.concatenate([jnp.where(ghead == g, t, 0.0) for g in range(ATT_KV_HEADS)],
                                       axis=0).astype(BF16)
    kbd, vbd = expand(k), expand(v)
    kbd_prev, vbd_prev = expand(kp_ref[...]), expand(vp_ref[...])
    qall = jnp.concatenate([q[:, 256 * r:256 * (r + 1)] for r in range(ATT_REP)], axis=0).astype(BF16)
    scale = ATT_HEAD_DIM ** -0.5
    nt_dims = (((1,), (1,)), ((), ()))
    sp_all = lax.dot_general(qall, kbd_prev, nt_dims, preferred_element_type=F32) * scale
    sc_all = lax.dot_general(qall, kbd, nt_dims, preferred_element_type=F32) * scale
    nq = ATT_REP * WINDOW
    qi = lax.broadcasted_iota(jnp.int32, (nq, WINDOW), 0) & (WINDOW - 1)
    kj = lax.broadcasted_iota(jnp.int32, (nq, WINDOW), 1)
    mask_prev = jnp.logical_and(kj > qi, has_prev)
    mask_cur = kj <= qi
    rep = lax.broadcasted_iota(jnp.int32, (nq, 1), 0) >> 7
    pp, pc = [], []
    for g in range(ATT_KV_HEADS):
        sp = jnp.where(mask_prev, sp_all[:, WINDOW * g:WINDOW * (g + 1)], -jnp.inf)
        sc = jnp.where(mask_cur, sc_all[:, WINDOW * g:WINDOW * (g + 1)], -jnp.inf)
        sink = jnp.where(rep == 0, sink_ref[4 * g],
                         jnp.where(rep == 1, sink_ref[4 * g + 1],
                                   jnp.where(rep == 2, sink_ref[4 * g + 2], sink_ref[4 * g + 3])))
        m = jnp.maximum(jnp.max(jnp.maximum(sp, sc), axis=1, keepdims=True), sink)
        ep = jnp.exp(sp - m)
        ec = jnp.exp(sc - m)
        inv = 1.0 / (jnp.sum(ep + ec, axis=1, keepdims=True) + jnp.exp(sink - m))
        pp.append((ep * inv).astype(BF16))
        pc.append((ec * inv).astype(BF16))
    o = (jnp.dot(jnp.concatenate(pp, axis=1), vbd_prev, preferred_element_type=F32)
         + jnp.dot(jnp.concatenate(pc, axis=1), vbd, preferred_element_type=F32))
    for r in range(ATT_REP):
        o_ref[rows, 256 * r:256 * (r + 1)] = o[WINDOW * r:WINDOW * (r + 1), :].astype(o_ref.dtype)
    kp_ref[...] = k
    vp_ref[...] = v


def _swa_prompt_kernel(sink_ref, q_ref, k_ref, v_ref, within_ref, step_ref,
                       o_ref, wk_ref, wv_ref, kp_ref, vp_ref):
    n = pl.program_id(0)
    blocks = q_ref.shape[0] // WINDOW

    @pl.when(n == 0)
    def _():
        kp_ref[...] = jnp.zeros_like(kp_ref)
        vp_ref[...] = jnp.zeros_like(vp_ref)

    def block(bi, carry):
        _swa_prompt_block(bi, n * blocks + bi > 0, sink_ref, q_ref, k_ref, v_ref,
                          within_ref, step_ref, o_ref, kp_ref, vp_ref)
        return carry

    lax.fori_loop(0, blocks, block, 0)

    @pl.when(n == pl.num_programs(0) - 1)
    def _():
        wk_ref[...] = kp_ref[...]
        wv_ref[...] = vp_ref[...]


def _swa_prompt(sinks, proj, tabs):
    seq = proj.shape[0]
    within, steps = tabs
    const = lambda shape: pl.BlockSpec(shape, lambda t: (0,) * len(shape))
    return pl.pallas_call(
        _swa_prompt_kernel,
        out_shape=(jax.ShapeDtypeStruct((seq, 1024), BF16),
                   jax.ShapeDtypeStruct((WINDOW, 256), F32), jax.ShapeDtypeStruct((WINDOW, 256), F32)),
        grid=(seq // MIX_ROWS,),
        in_specs=[pl.BlockSpec(memory_space=pltpu.SMEM),
                  pl.BlockSpec((MIX_ROWS, 1024), lambda t: (t, COL_QA)),
                  pl.BlockSpec((MIX_ROWS, 256), lambda t: (t, COL_KA)),
                  pl.BlockSpec((MIX_ROWS, 256), lambda t: (t, COL_VA)),
                  const((CHUNK, 2 * LANES)), pl.BlockSpec((None, 8, 2 * LANES), lambda t: (t, 0, 0))],
        out_specs=(pl.BlockSpec((MIX_ROWS, 1024), lambda t: (t, 0)), const((WINDOW, 256)), const((WINDOW, 256))),
        scratch_shapes=[pltpu.VMEM((WINDOW, 256), F32), pltpu.VMEM((WINDOW, 256), F32)],
        compiler_params=_cparams(("arbitrary",)),
        name="swa_prompt",
    )(sinks, proj, proj, proj, within, steps)


def _swa_dec_pre_kernel(q_ref, k_ref, v_ref, c_ref, s1_ref, s2_ref, qrot_ref, kt_ref, vt_ref):
    c, s1, s2 = c_ref[...], s1_ref[...], s2_ref[...]
    qrot_ref[...] = _rope_att(q_ref[...], _tile_lanes(c, 8), _tile_lanes(s1, 8), _tile_lanes(s2, 8))
    kt_ref[...] = _rope_att(k_ref[...], _tile_lanes(c, 2), _tile_lanes(s1, 2), _tile_lanes(s2, 2)).T
    vt_ref[...] = v_ref[...].T


def _swa_dec_pre(proj, tabs):
    nb = proj.shape[0]
    c, s1, s2 = tabs
    const = lambda shape: pl.BlockSpec(shape, lambda t: (0,) * len(shape))
    return pl.pallas_call(
        _swa_dec_pre_kernel,
        out_shape=(jax.ShapeDtypeStruct((nb, 1024), F32), jax.ShapeDtypeStruct((256, nb), F32),
                   jax.ShapeDtypeStruct((256, nb), F32)),
        grid=(1,),
        in_specs=[pl.BlockSpec((nb, 1024), lambda t: (0, COL_QA)), pl.BlockSpec((nb, 256), lambda t: (0, COL_KA)),
                  pl.BlockSpec((nb, 256), lambda t: (0, COL_VA)), const((1, LANES)), const((1, LANES)), const((1, LANES))],
        out_specs=(const((nb, 1024)), const((256, nb)), const((256, nb))),
        compiler_params=_cparams(("arbitrary",)),
        name="swa_dec_pre",
    )(proj, proj, proj, c, s1, s2)


def _swa_dec_kernel(q_ref, kt_ref, vt_ref, kc_ref, vc_ref, sink_ref, o_ref, kcn_ref, vcn_ref):
    i = pl.program_id(0)
    shift = (LANES - i * DEC_BLOCK) % LANES
    ktr = pltpu.roll(kt_ref[...], shift, axis=1)
    vtr = pltpu.roll(vt_ref[...], shift, axis=1)
    q = q_ref[...]
    lane = lax.broadcasted_iota(jnp.int32, (ATT_HEAD_DIM, WINDOW), 1)
    rowid = lax.broadcasted_iota(jnp.int32, (8, ATT_HEAD_DIM), 0)
    scale = ATT_HEAD_DIM ** -0.5
    append = lambda old, col: jnp.where(lane == WINDOW - 1, jnp.broadcast_to(col, (ATT_HEAD_DIM, WINDOW)),
                                        pltpu.roll(old, WINDOW - 1, axis=1))
    pairs = [(j, g) for j in range(DEC_BLOCK) for g in range(ATT_KV_HEADS)]
    scores, values = [], []
    for j, g in pairs:
        lo, hi = ATT_HEAD_DIM * g, ATT_HEAD_DIM * (g + 1)
        kt = append(kc_ref[j, g], ktr[lo:hi, j:j + 1])
        vt = append(vc_ref[j, g], vtr[lo:hi, j:j + 1])
        kcn_ref[j, g] = kt
        vcn_ref[j, g] = vt
        qg = jnp.zeros((8, ATT_HEAD_DIM), F32)
        for r in range(ATT_REP):
            src = 256 * r + lo
            qg = jnp.where(rowid == r, jnp.broadcast_to(q[j:j + 1, src:src + ATT_HEAD_DIM], (8, ATT_HEAD_DIM)), qg)
        scores.append(_bdot(qg, kt))
        values.append(vt.astype(BF16))
    s = jnp.concatenate(scores, axis=0) * scale
    sink = jnp.concatenate([sink_ref[...]] * DEC_BLOCK, axis=0)[:, 0:1]
    m = jnp.maximum(jnp.max(s, axis=1, keepdims=True), sink)
    e = jnp.exp(s - m)
    p = e * (1.0 / (jnp.sum(e, axis=1, keepdims=True) + jnp.exp(sink - m)))
    for idx, (j, g) in enumerate(pairs):
        o = lax.dot_general(p[8 * idx:8 * (idx + 1), :].astype(BF16), values[idx], (((1,), (1,)), ((), ())),
                            preferred_element_type=F32)
        for r in range(ATT_REP):
            dst = 256 * r + ATT_HEAD_DIM * g
            o_ref[j:j + 1, dst:dst + ATT_HEAD_DIM] = o[r:r + 1, :]


def _swa_dec(layer, qrot, kt, vt, kc_all, vc_all, sink32, carried):
    nb = qrot.shape[0]
    lc = _LayerCall(layer, nb // DEC_BLOCK)
    const = lambda shape: pl.BlockSpec(shape, lambda t: (0,) * len(shape))
    tail = (ATT_KV_HEADS, ATT_HEAD_DIM, WINDOW)
    ins = [qrot, kt, vt, kc_all, vc_all, sink32]
    in_specs = [lc.rows(1024), const((256, nb)), const((256, nb)), lc.state_in(tail), lc.state_in(tail),
                const((32, LANES))]
    n_in = len(ins)
    carried = [] if carried is None else list(carried)
    return pl.pallas_call(
        lc.kernel(_swa_dec_kernel, n_in, len(carried), (1, 2)),
        out_shape=(jax.ShapeDtypeStruct((nb, 1024), F32),
                   jax.ShapeDtypeStruct(kc_all.shape, F32), jax.ShapeDtypeStruct(vc_all.shape, F32)),
        grid=lc.grid,
        in_specs=in_specs + [pl.BlockSpec(memory_space=pl.ANY)] * len(carried),
        out_specs=(lc.rows(1024), lc.state_out(tail), lc.state_out(tail)),
        input_output_aliases={n_in + k: 1 + k for k in range(len(carried))},
        compiler_params=_cparams(("arbitrary",)),
        name="swa_dec",
    )(*ins, *carried)


def _ret_prompt_chunk(ci, lg_ref, q_ref, k_ref, v_ref, gr_ref, within_ref, step_ref, nrm_ref,
                      o_ref, st_ref, intra_ref, fs_ref, te_ref):
    rows = _chunk_rows(ci)
    c, sinp = _block_cos_sin(within_ref, step_ref, ci)
    s = jnp.where(lax.broadcasted_iota(jnp.int32, (1, LANES), 1) < RET_DK // 2, -sinp, sinp)
    q = q_ref[rows, :].astype(F32)
    k = k_ref[rows, :].astype(F32)
    v = v_ref[rows, :].astype(F32)
    gr = gr_ref[rows, :].astype(F32)
    nrm = nrm_ref[...]
    for h in range(RET_HEADS):
        sl = slice(128 * h, 128 * (h + 1))
        qh = q[:, sl]
        kh = k[:, sl]
        qh = qh * c + pltpu.roll(qh, 64, axis=1) * s
        kh = (kh * c + pltpu.roll(kh, 64, axis=1) * s) * (RET_DK ** -0.5)
        vh = v[:, sl]
        att = _bdot_nt(qh, kh) * intra_ref[h]
        s_old = st_ref[h]
        o = _bdot(att, vh) + _bdot(qh, s_old) * fs_ref[h]
        cd = jnp.exp(jnp.zeros((1, RET_DK), F32) + CHUNK * lg_ref[h])
        st_ref[h] = s_old * cd + _bdot((kh * te_ref[h]).T, vh)
        o = o * lax.rsqrt(jnp.mean(o * o, axis=-1, keepdims=True) + EPS)
        o_ref[rows, sl] = (o * nrm[:, sl] * _silu(gr[:, sl])).astype(o_ref.dtype)


def _ret_prompt_kernel(lg_ref, q_ref, k_ref, v_ref, gr_ref, within_ref, step_ref, nrm_ref,
                       o_ref, sfin_ref, st_ref, intra_ref, fs_ref, te_ref):
    t = pl.program_id(0)

    @pl.when(t == 0)
    def _():
        st_ref[...] = jnp.zeros_like(st_ref)
        ri = lax.broadcasted_iota(jnp.int32, (CHUNK, CHUNK), 0).astype(F32)
        ci = lax.broadcasted_iota(jnp.int32, (CHUNK, CHUNK), 1).astype(F32)
        rel = ri - ci
        for h in range(RET_HEADS):
            lg = lg_ref[h]
            intra_ref[h] = jnp.exp(jnp.where(rel >= 0, rel * lg, -jnp.inf))
            fs_ref[h] = jnp.exp((ri + 1.0) * lg)
            te_ref[h] = jnp.exp((CHUNK - 1.0 - ri) * lg)

    def chunk(ci, carry):
        _ret_prompt_chunk(ci, lg_ref, q_ref, k_ref, v_ref, gr_ref, within_ref, step_ref, nrm_ref,
                          o_ref, st_ref, intra_ref, fs_ref, te_ref)
        return carry

    lax.fori_loop(0, q_ref.shape[0] // CHUNK, chunk, 0)

    @pl.when(t == pl.num_programs(0) - 1)
    def _():
        sfin_ref[...] = st_ref[...]


def _ret_prompt(log_gamma, proj, tabs, nrm):
    seq = proj.shape[0]
    within, steps = tabs
    const = lambda shape: pl.BlockSpec(shape, lambda t: (0,) * len(shape))
    col = lambda cidx: pl.BlockSpec((MIX_ROWS, 1024), lambda t: (t, cidx))
    tbl = pltpu.VMEM((RET_HEADS, CHUNK, CHUNK), F32)
    return pl.pallas_call(
        _ret_prompt_kernel,
        out_shape=(jax.ShapeDtypeStruct((seq, 1024), BF16), jax.ShapeDtypeStruct((RET_HEADS, RET_DK, 128), F32)),
        grid=(seq // MIX_ROWS,),
        in_specs=[pl.BlockSpec(memory_space=pltpu.SMEM), col(COL_QR), col(COL_KR), col(COL_VR), col(COL_GR),
                  const((CHUNK, 2 * LANES)), pl.BlockSpec((None, 8, 2 * LANES), lambda t: (t, 0, 0)), const((1, 1024))],
        out_specs=(pl.BlockSpec((MIX_ROWS, 1024), lambda t: (t, 0)), const((RET_HEADS, RET_DK, 128))),
        scratch_shapes=[tbl, tbl, tbl, tbl],
        compiler_params=_cparams(("arbitrary",)),
        name="ret_prompt",
    )(log_gamma, proj, proj, proj, proj, within, steps, nrm)


def _ret_dec_pre_kernel(q_ref, k_ref, c_ref, s_ref, qrot_ref, kt_ref):
    c = c_ref[...]
    s = s_ref[...]
    q = q_ref[...]
    k = k_ref[...]
    ks = []
    for h in range(RET_HEADS):
        sl = slice(128 * h, 128 * (h + 1))
        qh = q[:, sl]
        kh = k[:, sl]
        qrot_ref[:, sl] = qh * c + pltpu.roll(qh, 64, axis=1) * s
        ks.append((kh * c + pltpu.roll(kh, 64, axis=1) * s) * (RET_DK ** -0.5))
    kt_ref[...] = jnp.concatenate(ks, axis=1).T


def _ret_dec_pre(proj, tabs):
    nb = proj.shape[0]
    c, s = tabs
    const = lambda shape: pl.BlockSpec(shape, lambda t: (0,) * len(shape))
    return pl.pallas_call(
        _ret_dec_pre_kernel,
        out_shape=(jax.ShapeDtypeStruct((nb, 1024), F32), jax.ShapeDtypeStruct((1024, nb), F32)),
        grid=(1,),
        in_specs=[pl.BlockSpec((nb, 1024), lambda t: (0, COL_QR)), pl.BlockSpec((nb, 1024), lambda t: (0, COL_KR)),
                  const((1, LANES)), const((1, LANES))],
        out_specs=(const((nb, 1024)), const((1024, nb))),
        compiler_params=_cparams(("arbitrary",)),
        name="ret_dec_pre",
    )(proj, proj, c, s)


def _ret_dec_state_kernel(gam_ref, kt_ref, q_ref, v_ref, gr_ref, st_ref, nrm_ref, o_ref, stn_ref):
    i = pl.program_id(0)
    shift = (LANES - i * DEC_BLOCK) % LANES
    kr = pltpu.roll(kt_ref[...], shift, axis=1)
    q = q_ref[...]
    v = v_ref[...]
    for j in range(DEC_BLOCK):
        for h in range(RET_HEADS):
            lo, hi = 128 * h, 128 * (h + 1)
            kcol = jnp.broadcast_to(kr[lo:hi, j:j + 1], (RET_DK, 128))
            s_new = st_ref[j, lo:hi, :] * gam_ref[h] + kcol * v[j:j + 1, lo:hi]
            stn_ref[j, lo:hi, :] = s_new
            qs = _bdot(q[:, lo:hi], s_new)
            o_ref[j:j + 1, lo:hi] = qs[j:j + 1, :]
    gr = gr_ref[...]
    nrm = nrm_ref[...]
    for h in range(RET_HEADS):
        sl = slice(128 * h, 128 * (h + 1))
        o = o_ref[:, sl]
        o = o * lax.rsqrt(jnp.mean(o * o, axis=-1, keepdims=True) + EPS)
        o_ref[:, sl] = o * nrm[:, sl] * _silu(gr[:, sl])


def _ret_dec_state(layer, gam, kt, qrot, proj, st_all, nrm, carried):
    nb = qrot.shape[0]
    lc = _LayerCall(layer, nb // DEC_BLOCK)
    const = lambda shape: pl.BlockSpec(shape, lambda t: (0,) * len(shape))
    tail = (1024, 128)
    ins = [gam, kt, qrot, proj, proj, st_all, nrm]
    in_specs = [pl.BlockSpec(memory_space=pltpu.SMEM), const((1024, nb)), lc.rows(1024),
                lc.rows(1024, COL_VR), lc.rows(1024, COL_GR), lc.state_in(tail), const((1, 1024))]
    n_in = len(ins)
    carried = [] if carried is None else [carried]
    return pl.pallas_call(
        lc.kernel(_ret_dec_state_kernel, n_in, len(carried), (1,)),
        out_shape=(jax.ShapeDtypeStruct((nb, 1024), F32), jax.ShapeDtypeStruct(st_all.shape, F32)),
        grid=lc.grid,
        in_specs=in_specs + [pl.BlockSpec(memory_space=pl.ANY)] * len(carried),
        out_specs=(lc.rows(1024), lc.state_out(tail)),
        input_output_aliases={n_in + k: 1 + k for k in range(len(carried))},
        compiler_params=_cparams(("arbitrary",)),
        name="ret_dec_state",
    )(*ins, *carried)


def _merge_math(x_ref, a_ref, b_ref, c_ref, g1_ref, g2_ref, g3_ref, w1_ref, w2_ref, w3_ref, wo_ref):
    gate = lambda ref: _sigmoid(ref[...].astype(F32))
    m = (gate(g1_ref) * jnp.dot(a_ref[...].astype(BF16), w1_ref[...], preferred_element_type=F32)
         + gate(g2_ref) * jnp.dot(b_ref[...].astype(BF16), w2_ref[...], preferred_element_type=F32)
         + gate(g3_ref) * jnp.dot(c_ref[...].astype(BF16), w3_ref[...], preferred_element_type=F32))
    return x_ref[...] + jnp.dot(m.astype(BF16), wo_ref[...], preferred_element_type=F32)


def _merge_kernel(*refs):
    *ins, o_ref = refs
    o_ref[...] = _merge_math(*ins)


def _merge_route_kernel(*refs):
    *ins, gn_ref, rw_ref, rb_ref, o_ref, route_ref = refs
    x = _merge_math(*ins)
    o_ref[...] = x
    route_ref[...] = _route_row(_rms(x, gn_ref[...]), rw_ref[0], rw_ref[1], rb_ref[...])


def _merge(x, a, b, c, proj, w1, w2, w3, wo, router=None):
    rows = x.shape[0]
    tm = min(rows, 512)
    rowb = pl.BlockSpec((tm, 1024), lambda i: (i, 0))
    gate = lambda k: pl.BlockSpec((tm, 1024), lambda i: (i, COL_GATE + k))
    wsp = pl.BlockSpec((1024, 1024), lambda i: (0, 0))
    ins = [x, a, b, c, proj, proj, proj, w1, w2, w3, wo]
    in_specs = [rowb, rowb, rowb, rowb, gate(0), gate(1), gate(2), wsp, wsp, wsp, wsp]
    out_shape = jax.ShapeDtypeStruct((rows, 1024), F32)
    if router is None:
        kern, out_specs = _merge_kernel, rowb
    else:
        kern = _merge_route_kernel
        ins += list(router)
        in_specs += [pl.BlockSpec((1, 1024), lambda i: (0, 0)), pl.BlockSpec((2, 1024, LANES), lambda i: (0, 0, 0)),
                     pl.BlockSpec((1, LANES), lambda i: (0, 0))]
        out_shape = (out_shape, jax.ShapeDtypeStruct((rows, LANES), F32))
        out_specs = (rowb, pl.BlockSpec((tm, LANES), lambda i: (i, 0)))
    return pl.pallas_call(
        kern,
        out_shape=out_shape,
        grid=(rows // tm,),
        in_specs=in_specs,
        out_specs=out_specs,
        compiler_params=_cparams(("parallel",)),
        name="merge",
    )(*ins)


FF_TILE = 1408
FF_SPLIT = 768


def _ffn_kernel(x_ref, g_ref, wg_ref, wu_ref, wd_ref, o_ref, h_ref, acc_ref):
    j = pl.program_id(1)

    @pl.when(j == 0)
    def _():
        h_ref[...] = _rms(x_ref[...], g_ref[...]).astype(BF16)
        acc_ref[...] = jnp.zeros_like(acc_ref)

    h = h_ref[...]
    for lo, hi in ((0, FF_SPLIT), (FF_SPLIT, FF_TILE)):
        a = jnp.dot(h, wg_ref[:, lo:hi], preferred_element_type=F32)
        u = jnp.dot(h, wu_ref[:, lo:hi], preferred_element_type=F32)
        acc_ref[...] += jnp.dot((_silu(a) * u).astype(BF16), wd_ref[lo:hi, :], preferred_element_type=F32)

    @pl.when(j == pl.num_programs(1) - 1)
    def _():
        o_ref[...] = x_ref[...] + acc_ref[...]


def _ffn(x, g, wg, wu, wd):
    rows = x.shape[0]
    tm = min(rows, 1024)
    return pl.pallas_call(
        _ffn_kernel,
        out_shape=jax.ShapeDtypeStruct((rows, 1024), F32),
        grid=(rows // tm, D_FF // FF_TILE),
        in_specs=[pl.BlockSpec((tm, 1024), lambda i, j: (i, 0)), pl.BlockSpec((1, 1024), lambda i, j: (0, 0)),
                  pl.BlockSpec((1024, FF_TILE), lambda i, j: (0, j)), pl.BlockSpec((1024, FF_TILE), lambda i, j: (0, j)),
                  pl.BlockSpec((FF_TILE, 1024), lambda i, j: (j, 0))],
        out_specs=pl.BlockSpec((tm, 1024), lambda i, j: (i, 0)),
        scratch_shapes=[pltpu.VMEM((tm, 1024), BF16), pltpu.VMEM((tm, 1024), F32)],
        compiler_params=_cparams(("parallel", "arbitrary"), vmem=VMEM_LIMIT_FFN),
        name="ffn",
    )(x, g, wg, wu, wd)


MOE_FF_TILE = 256


def _top2(h, rw_hi, rw_lo, rb, lane):
    h_hi = h.astype(BF16)
    h_lo = (h - h_hi.astype(F32)).astype(BF16)
    d = lambda a, b: jnp.dot(a, b, preferred_element_type=F32)
    logits = d(h_hi, rw_hi) + d(h_hi, rw_lo) + d(h_lo, rw_hi) + rb
    logits = jnp.where(lane < N_EXPERTS, logits, -jnp.inf)
    m1 = jnp.max(logits, axis=1, keepdims=True)
    i1 = jnp.min(jnp.where(logits == m1, lane, float(LANES)), axis=1, keepdims=True)
    rest = jnp.where(lane == i1, -jnp.inf, logits)
    m2 = jnp.max(rest, axis=1, keepdims=True)
    i2 = jnp.min(jnp.where(rest == m2, lane, float(LANES)), axis=1, keepdims=True)
    e2 = jnp.exp(m2 - m1)
    p1 = 1.0 / (1.0 + e2)
    return i1, i2, p1, e2 * p1


def _moe_kernel(x_ref, g_ref, rw_ref, rb_ref, wg_ref, wu_ref, wd_ref, o_ref, h_ref, acc_ref, comb_ref):
    e = pl.program_id(1)
    j = pl.program_id(2)
    tm = x_ref.shape[0]
    lane = lax.broadcasted_iota(jnp.int32, (tm, LANES), 1).astype(F32)

    @pl.when(jnp.logical_and(e == 0, j == 0))
    def _():
        h = _rms(x_ref[...], g_ref[...])
        h_ref[...] = h.astype(BF16)
        i1, i2, p1, p2 = _top2(h, rw_ref[0], rw_ref[1], rb_ref[...], lane)
        comb_ref[...] = jnp.where(lane == i1, p1, 0.0) + jnp.where(lane == i2, p2, 0.0)
        acc_ref[...] = jnp.zeros_like(acc_ref)

    ce = jnp.sum(jnp.where(lane == e.astype(F32), comb_ref[...], 0.0), axis=1, keepdims=True)
    h = h_ref[...]
    a = jnp.dot(h, wg_ref[0], preferred_element_type=F32)
    u = jnp.dot(h, wu_ref[0], preferred_element_type=F32)
    acc_ref[...] += ce * jnp.dot((_silu(a) * u).astype(BF16), wd_ref[0], preferred_element_type=F32)

    @pl.when(jnp.logical_and(e == pl.num_programs(1) - 1, j == pl.num_programs(2) - 1))
    def _():
        o_ref[...] = x_ref[...] + acc_ref[...]


def _moe(x, g, rw, rb, wg, wu, wd):
    rows = x.shape[0]
    tm = min(rows, 1024)
    tf = MOE_FF_TILE
    return pl.pallas_call(
        _moe_kernel,
        out_shape=jax.ShapeDtypeStruct((rows, 1024), F32),
        grid=(rows // tm, N_EXPERTS, D_FF // tf),
        in_specs=[pl.BlockSpec((tm, 1024), lambda i, e, j: (i, 0)), pl.BlockSpec((1, 1024), lambda i, e, j: (0, 0)),
                  pl.BlockSpec((2, 1024, LANES), lambda i, e, j: (0, 0, 0)),
                  pl.BlockSpec((1, LANES), lambda i, e, j: (0, 0)),
                  pl.BlockSpec((1, 1024, tf), lambda i, e, j: (e, 0, j)),
                  pl.BlockSpec((1, 1024, tf), lambda i, e, j: (e, 0, j)),
                  pl.BlockSpec((1, tf, 1024), lambda i, e, j: (e, j, 0))],
        out_specs=pl.BlockSpec((tm, 1024), lambda i, e, j: (i, 0)),
        scratch_shapes=[pltpu.VMEM((tm, 1024), BF16), pltpu.VMEM((tm, 1024), F32), pltpu.VMEM((tm, LANES), F32)],
        compiler_params=_cparams(("parallel", "arbitrary", "arbitrary")),
        name="moe",
    )(x, g, rw, rb, wg, wu, wd)


MOE_ROWS = 512
MOE_GROUP_FF = 1408
GATHER_UNROLL = 8


def _route_row(h, rw_hi, rw_lo, rb):
    lane = lax.broadcasted_iota(jnp.int32, (h.shape[0], LANES), 1).astype(F32)
    i1, i2, p1, p2 = _top2(h, rw_hi, rw_lo, rb, lane)
    return jnp.where(lane == 0.0, i1, jnp.where(lane == 1.0, i2, jnp.where(lane == 2.0, p1,
                     jnp.where(lane == 3.0, p2, 0.0))))


def _route_plan(route, tm):
    n = route.shape[0]
    n_tiles = (2 * n) // tm + N_EXPERTS
    e_flat = route[:, :2].astype(jnp.int32).reshape(-1)
    onehot = (e_flat[:, None] == jnp.arange(N_EXPERTS, dtype=jnp.int32)[None, :]).astype(jnp.int32)
    csum = jnp.cumsum(onehot, axis=0)
    counts = csum[-1]
    tiles_e = (counts + tm - 1) // tm
    tile_end = jnp.cumsum(tiles_e)
    row_start = (tile_end - tiles_e) * tm
    pos = jnp.sum((csum - onehot + row_start[None, :]) * onehot, axis=1).astype(jnp.int32)
    tile_expert = jnp.minimum(jnp.sum(jnp.arange(n_tiles, dtype=jnp.int32)[:, None] >= tile_end[None, :], axis=1),
                              N_EXPERTS - 1).astype(jnp.int32)
    n_used = tile_end[-1:].astype(jnp.int32)
    src = jnp.zeros((n_tiles * tm,), jnp.int32).at[pos].set(jnp.arange(2 * n, dtype=jnp.int32) // 2)
    return pos, src.reshape(n_tiles, 1, tm), tile_expert, n_used


def _moe_group_kernel(te_ref, nu_ref, src_ref, srcn_ref, x_hbm, g_ref, wg_ref, wu_ref, wd_ref,
                      y_ref, buf, sem, h_ref, acc_ref):
    i = pl.program_id(0)
    j = pl.program_id(1)
    tm = buf.shape[1]
    slot = i % 2
    active = i < nu_ref[0]

    def row_copy(idx_ref, s, r):
        return pltpu.make_async_copy(x_hbm.at[pl.ds(idx_ref[0, 0, r], 1), :], buf.at[s, pl.ds(r, 1), :], sem.at[s])

    def gather(idx_ref, s):
        def body(r, c):
            row_copy(idx_ref, s, r).start()
            return c
        lax.fori_loop(0, tm, body, 0, unroll=GATHER_UNROLL)

    @pl.when(jnp.logical_and(active, j == 0))
    def _():
        @pl.when(i == 0)
        def _():
            gather(src_ref, 0)

        pltpu.make_async_copy(x_hbm.at[pl.ds(0, tm), :], buf.at[slot], sem.at[slot]).wait()

        @pl.when(i + 1 < nu_ref[0])
        def _():
            gather(srcn_ref, 1 - slot)

        h_ref[...] = _rms(buf[slot], g_ref[...]).astype(BF16)
        acc_ref[...] = jnp.zeros_like(acc_ref)

    @pl.when(active)
    def _():
        h = h_ref[...]
        a = jnp.dot(h, wg_ref[0], preferred_element_type=F32)
        u = jnp.dot(h, wu_ref[0], preferred_element_type=F32)
        acc_ref[...] += jnp.dot((_silu(a) * u).astype(BF16), wd_ref[0], preferred_element_type=F32)

    @pl.when(j == pl.num_programs(1) - 1)
    def _():
        @pl.when(active)
        def _():
            y_ref[...] = acc_ref[...]

        @pl.when(jnp.logical_not(active))
        def _():
            y_ref[...] = jnp.zeros_like(y_ref)


def _moe_group(tile_expert, n_used, src, x, g, wg, wu, wd):
    n_tiles, _, tm = src.shape
    tf = MOE_GROUP_FF
    grid_spec = pltpu.PrefetchScalarGridSpec(
        num_scalar_prefetch=2,
        grid=(n_tiles, D_FF // tf),
        in_specs=[pl.BlockSpec((1, 1, tm), lambda i, j, te, nu: (i, 0, 0), memory_space=pltpu.SMEM),
                  pl.BlockSpec((1, 1, tm), lambda i, j, te, nu: (jnp.minimum(i + 1, n_tiles - 1), 0, 0),
                               memory_space=pltpu.SMEM),
                  pl.BlockSpec(memory_space=pl.ANY),
                  pl.BlockSpec((1, 1024), lambda i, j, te, nu: (0, 0)),
                  pl.BlockSpec((1, 1024, tf), lambda i, j, te, nu: (te[i], 0, j)),
                  pl.BlockSpec((1, 1024, tf), lambda i, j, te, nu: (te[i], 0, j)),
                  pl.BlockSpec((1, tf, 1024), lambda i, j, te, nu: (te[i], j, 0))],
        out_specs=pl.BlockSpec((tm, 1024), lambda i, j, te, nu: (i, 0)),
        scratch_shapes=[pltpu.VMEM((2, tm, 1024), F32), pltpu.SemaphoreType.DMA((2,)),
                        pltpu.VMEM((tm, 1024), BF16), pltpu.VMEM((tm, 1024), F32)])
    return pl.pallas_call(
        _moe_group_kernel,
        out_shape=jax.ShapeDtypeStruct((n_tiles * tm, 1024), F32),
        grid_spec=grid_spec,
        compiler_params=_cparams(("arbitrary", "arbitrary")),
        name="moe_group",
    )(tile_expert, n_used, src, src, x, g, wg, wu, wd)


def _moe_combine_kernel(pos_ref, posn_ref, x_ref, r_ref, y_hbm, p_ref, g_ref, wp_ref, wgt_ref, gf_ref,
                        o_ref, bufa, bufb, sem, *, final):
    i = pl.program_id(0)
    tm = x_ref.shape[0]
    slot = i % 2

    def gather(idx_ref, s):
        def body(t, c):
            pltpu.make_async_copy(y_hbm.at[pl.ds(idx_ref[0, 0, 2 * t], 1), :], bufa.at[s, pl.ds(t, 1), :],
                                  sem.at[0, s]).start()
            pltpu.make_async_copy(y_hbm.at[pl.ds(idx_ref[0, 0, 2 * t + 1], 1), :], bufb.at[s, pl.ds(t, 1), :],
                                  sem.at[1, s]).start()
            return c
        lax.fori_loop(0, tm, body, 0, unroll=GATHER_UNROLL)

    @pl.when(i == 0)
    def _():
        gather(pos_ref, 0)

    pltpu.make_async_copy(y_hbm.at[pl.ds(0, tm), :], bufa.at[slot], sem.at[0, slot]).wait()
    pltpu.make_async_copy(y_hbm.at[pl.ds(0, tm), :], bufb.at[slot], sem.at[1, slot]).wait()

    @pl.when(i + 1 < pl.num_programs(0))
    def _():
        gather(posn_ref, 1 - slot)

    r = r_ref[...]
    x = x_ref[...] + r[:, 2:3] * bufa[slot] + r[:, 3:4] * bufb[slot]
    o_ref[...] = _ple_math(x, p_ref[...], g_ref[...], wp_ref[...], wgt_ref[...], gf_ref[...], final)


def _moe_combine(pos, x, route, y, p, g, wp, wgt, gf, final):
    rows = x.shape[0]
    tm = MOE_ROWS
    n = rows // tm
    pos3 = pos.reshape(n, 1, 2 * tm)
    vec = pl.BlockSpec((1, 1024), lambda i: (0, 0))
    return pl.pallas_call(
        functools.partial(_moe_combine_kernel, final=final),
        out_shape=jax.ShapeDtypeStruct((rows, 1024), F32),
        grid=(n,),
        in_specs=[pl.BlockSpec((1, 1, 2 * tm), lambda i: (i, 0, 0), memory_space=pltpu.SMEM),
                  pl.BlockSpec((1, 1, 2 * tm), lambda i: (jnp.minimum(i + 1, n - 1), 0, 0), memory_space=pltpu.SMEM),
                  pl.BlockSpec((tm, 1024), lambda i: (i, 0)), pl.BlockSpec((tm, LANES), lambda i: (i, 0)),
                  pl.BlockSpec(memory_space=pl.ANY),
                  pl.BlockSpec((tm, PLE_DIM), lambda i: (i, 0)), vec,
                  pl.BlockSpec((PLE_DIM, 1024), lambda i: (0, 0)), pl.BlockSpec((1024, 1024), lambda i: (0, 0)), vec],
        out_specs=pl.BlockSpec((tm, 1024), lambda i: (i, 0)),
        scratch_shapes=[pltpu.VMEM((2, tm, 1024), F32), pltpu.VMEM((2, tm, 1024), F32),
                        pltpu.SemaphoreType.DMA((2, 2))],
        compiler_params=_cparams(("arbitrary",)),
        name="moe_combine",
    )(pos3, pos3, x, route, y, p, g, wp, wgt, gf)


def _moe_routed_ple(x, route, g, wg, wu, wd, p, g_ple, wp, wgt, gf, final):
    pos, src, tile_expert, n_used = _route_plan(route, MOE_ROWS)
    y = _moe_group(tile_expert, n_used, src, x, g, wg, wu, wd)
    return _moe_combine(pos, x, route, y, p, g_ple, wp, wgt, gf, final)


def _ple_math(x, p, g, wp, wgt, gf, final):
    emb = jnp.dot(p.astype(BF16), wp, preferred_element_type=F32)
    gate = _sigmoid(jnp.dot(_rms(x, g).astype(BF16), wgt, preferred_element_type=F32))
    y = x + emb * gate
    return _rms(y, gf) if final else y


def _ple_kernel(x_ref, p_ref, g_ref, wp_ref, wgt_ref, gf_ref, o_ref, *, final):
    o_ref[...] = _ple_math(x_ref[...], p_ref[...], g_ref[...], wp_ref[...], wgt_ref[...], gf_ref[...], final)


def _ple(x, p, g, wp, wgt, gf, final):
    rows = x.shape[0]
    tm = min(rows, 512)
    vec = pl.BlockSpec((1, 1024), lambda i: (0, 0))
    return pl.pallas_call(
        functools.partial(_ple_kernel, final=final),
        out_shape=jax.ShapeDtypeStruct((rows, 1024), F32),
        grid=(rows // tm,),
        in_specs=[pl.BlockSpec((tm, 1024), lambda i: (i, 0)), pl.BlockSpec((tm, PLE_DIM), lambda i: (i, 0)), vec,
                  pl.BlockSpec((PLE_DIM, 1024), lambda i: (0, 0)), pl.BlockSpec((1024, 1024), lambda i: (0, 0)), vec],
        out_specs=pl.BlockSpec((tm, 1024), lambda i: (i, 0)),
        compiler_params=_cparams(("parallel",)),
        name="ple",
    )(x, p, g, wp, wgt, gf)


REPACK_ROWS = 128


def _repack_w_in_kernel(w_ref, o_ref):
    offs = [0]
    for wd in IN_WIDTHS:
        offs.append(offs[-1] + wd)
    z, xbc, dt, qa, ka, va, qr, kr, vr, gr, gates = [(offs[i], offs[i + 1]) for i in range(len(IN_WIDTHS))]
    take = lambda lo, hi: w_ref[:, lo:hi].astype(BF16)
    q = take(*qa)
    heads = lambda h: q[:, h * ATT_HEAD_DIM:(h + 1) * ATT_HEAD_DIM]
    q_rmajor = jnp.concatenate([heads(ATT_REP * g + r) for r in range(ATT_REP) for g in range(ATT_KV_HEADS)], axis=1)
    parts = [take(*xbc), take(*z), q_rmajor] + [take(*p) for p in (qr, kr, vr, gr, gates, ka, va)]
    used = sum(p.shape[1] for p in parts) + (dt[1] - dt[0])
    parts.append(jnp.concatenate([take(*dt), jnp.zeros((o_ref.shape[0], N_PROJ - used), BF16)], axis=1))
    dst = 0
    for part in parts:
        o_ref[:, dst:dst + part.shape[1]] = part
        dst += part.shape[1]


def _repack_w_in(w_all, layer):
    rows = w_all.shape[1]
    return pl.pallas_call(
        _repack_w_in_kernel,
        out_shape=jax.ShapeDtypeStruct((rows, N_PROJ), BF16),
        grid=(rows // REPACK_ROWS,),
        in_specs=[pl.BlockSpec((None, REPACK_ROWS, w_all.shape[2]), lambda i: (layer, i, 0))],
        out_specs=pl.BlockSpec((REPACK_ROWS, N_PROJ), lambda i: (i, 0)),
        compiler_params=_cparams(("parallel",)),
        name="repack_w_in",
    )(w_all)


def _att_tables(pos):
    half = ROPE_DIM // 2
    inv = jnp.exp(-math.log(ROPE_THETA) * jnp.arange(half, dtype=F32) * (2.0 / ROPE_DIM))
    ang = pos.astype(F32)[:, None] * inv[None, :]
    cos, sin = jnp.cos(ang), jnp.sin(ang)
    n = pos.shape[0]
    one = jnp.ones((n, ATT_HEAD_DIM - ROPE_DIM), F32)
    zero8 = jnp.zeros((n, half), F32)
    zero = jnp.zeros((n, ATT_HEAD_DIM - ROPE_DIM), F32)
    c = jnp.concatenate([cos, cos, one], axis=1)
    s1 = jnp.concatenate([zero8, sin, zero], axis=1)
    s2 = jnp.concatenate([-sin, zero8, zero], axis=1)
    return tuple(jnp.concatenate([t, t], axis=1) for t in (c, s1, s2))


def _ret_tables(pos):
    half = RET_DK // 2
    inv = jnp.exp(-math.log(RET_THETA) * jnp.arange(half, dtype=F32) * (2.0 / RET_DK))
    ang = pos.astype(F32)[:, None] * inv[None, :]
    cos, sin = jnp.cos(ang), jnp.sin(ang)
    return jnp.concatenate([cos, cos], axis=1), jnp.concatenate([-sin, sin], axis=1)


def _rope_step_tables(inv_lane, seq):
    per = MIX_ROWS // CHUNK
    ang_i = jnp.arange(CHUNK, dtype=F32)[:, None] * inv_lane[None, :]
    within = jnp.concatenate([jnp.cos(ang_i), jnp.sin(ang_i)], axis=1)
    ang_b = (jnp.arange(seq // CHUNK, dtype=F32) * CHUNK)[:, None] * inv_lane[None, :]
    blk = jnp.concatenate([jnp.cos(ang_b), jnp.sin(ang_b)], axis=1).reshape(seq // MIX_ROWS, per, 2 * LANES)
    steps = jnp.concatenate([blk, jnp.zeros((seq // MIX_ROWS, 8 - per, 2 * LANES), F32)], axis=1)
    return within, steps


def _att_inv_lanes():
    inv = jnp.exp(-math.log(ROPE_THETA) * jnp.arange(ROPE_DIM // 2, dtype=F32) * (2.0 / ROPE_DIM))
    return jnp.tile(inv, LANES // (ROPE_DIM // 2))


def _ret_inv_lanes():
    inv = jnp.exp(-math.log(RET_THETA) * jnp.arange(RET_DK // 2, dtype=F32) * (2.0 / RET_DK))
    return jnp.tile(inv, 2)


def _pad_lanes(v, fill=0.0):
    return jnp.concatenate([v.astype(F32), jnp.full((LANES - v.shape[0],), fill, F32)])[None, :]


def kernel(x_prompt, x_sample, state_ssm, state_conv, cache_win_k, cache_win_v, state_ret, p_prompt, p_sample,
           w_in, conv_w, conv_b, dt_bias, a_log, d_skip, ssd_norm, attn_sinks, ret_norm, w_o_ssd, w_o_att, w_o_ret,
           w_out, norm_mix, norm_ffn, norm_ple, ffn_w_gate, ffn_w_up, ffn_w_down, router_w, router_b, moe_w_gate,
           moe_w_up, moe_w_down, w_ple, w_ple_gate, norm_final):
    seq = x_prompt.shape[1]
    nb = x_sample.shape[0]
    xp = x_prompt.reshape(seq, D_MODEL)
    xs = x_sample.reshape(nb, D_MODEL)
    pos_s = PAST_LEN + jnp.arange(1)
    att_tab_p, att_tab_s = _rope_step_tables(_att_inv_lanes(), seq), _att_tables(pos_s)
    ret_tab_p, ret_tab_s = _rope_step_tables(_ret_inv_lanes(), seq), _ret_tables(pos_s)
    log_gamma = jnp.log1p(-jnp.exp2(-5.0 - jnp.arange(RET_HEADS, dtype=F32)))
    gamma = jnp.exp(log_gamma)
    row = lambda v: v.astype(F32)[None, :]

    ssm_all = state_ssm.reshape(DEPTH, nb, SSD_HEADS * SSD_HEAD_DIM, SSD_STATE)
    ret_all = state_ret.reshape(DEPTH, nb, RET_HEADS * RET_DK, 128)
    kc_all = jnp.transpose(cache_win_k, (0, 1, 3, 4, 2))
    vc_all = jnp.transpose(cache_win_v, (0, 1, 3, 4, 2))
    ssm_s = ret_s = kv_s = None
    conv_s = []

    new_p = [[], [], [], [], []]
    for i in range(DEPTH):
        w_in_i = _repack_w_in(w_in, i)
        cw, cb = conv_w[i], row(conv_b[i])
        dtb, alog = _pad_lanes(dt_bias[i]), _pad_lanes(a_log[i])
        dsk = row(jnp.repeat(d_skip[i], SSD_HEAD_DIM))
        nrm_ssd, nrm_ret = row(ssd_norm[i]), row(ret_norm[i])
        sinks = attn_sinks[i].astype(F32)
        sink32 = jnp.zeros((ATT_KV_HEADS, 8), F32).at[:, :ATT_REP].set(sinks.reshape(ATT_KV_HEADS, ATT_REP))
        sink32 = jnp.broadcast_to(sink32.reshape(32, 1), (32, LANES))
        w1 = w_o_ssd[i].astype(BF16)
        w2 = w_o_att[i].reshape(ATT_KV_HEADS, ATT_REP, ATT_HEAD_DIM, D_MODEL).transpose(1, 0, 2, 3) \
            .reshape(ATT_Q_HEADS * ATT_HEAD_DIM, D_MODEL).astype(BF16)
        w3 = w_o_ret[i].astype(BF16)
        wo = w_out[i].astype(BF16)
        g_mix, g_ffn, g_ple = row(norm_mix[i]), row(norm_ffn[i]), row(norm_ple[i])
        wp, wpg = w_ple[i].astype(BF16), w_ple_gate[i].astype(BF16)
        gf = row(norm_final)
        j = i // 2
        if i % 2 == 0:
            ffw = (ffn_w_gate[j].astype(BF16), ffn_w_up[j].astype(BF16), ffn_w_down[j].astype(BF16))
        else:
            rw = jnp.concatenate([router_w[j], jnp.zeros((D_MODEL, LANES - N_EXPERTS), F32)], axis=1)
            rw_hi = rw.astype(BF16)
            rw = jnp.stack([rw_hi, (rw - rw_hi.astype(F32)).astype(BF16)])
            ffw = (rw, _pad_lanes(router_b[j]), moe_w_gate[j].astype(BF16), moe_w_up[j].astype(BF16),
                   moe_w_down[j].astype(BF16))
        final = i == DEPTH - 1

        proj, dt = _inproj(xp, g_mix, w_in_i, BF16)
        head_rows = lambda v: jnp.broadcast_to(v.astype(F32)[:, None], (SSD_HEADS, LANES))
        y_ssd, ssm_fin, conv_fin = _ssd_prompt(proj, dt, cw, cb, head_rows(dt_bias[i]), head_rows(a_log[i]),
                                               dsk, nrm_ssd)
        o_att, wk, wv = _swa_prompt(sinks, proj, att_tab_p)
        o_ret, ret_fin = _ret_prompt(log_gamma, proj, ret_tab_p, nrm_ret)
        ple_args = (p_prompt[i].reshape(seq, PLE_DIM), g_ple, wp, wpg, gf, final)
        if i % 2 == 0:
            xp = _merge(xp, y_ssd, o_att, o_ret, proj, w1, w2, w3, wo)
            xp = _ple(_ffn(xp, g_ffn, *ffw), *ple_args)
        else:
            rw, rb, *expert_w = ffw
            xp, route = _merge(xp, y_ssd, o_att, o_ret, proj, w1, w2, w3, wo, router=(g_ffn, rw, rb))
            xp = _moe_routed_ple(xp, route, g_ffn, *expert_w, *ple_args)
        new_p[0].append(ssm_fin.reshape(1, SSD_HEADS, SSD_HEAD_DIM, SSD_STATE))
        new_p[1].append(conv_fin[None])
        new_p[2].append(wk.reshape(1, WINDOW, ATT_KV_HEADS, ATT_HEAD_DIM))
        new_p[3].append(wv.reshape(1, WINDOW, ATT_KV_HEADS, ATT_HEAD_DIM))
        new_p[4].append(ret_fin[None])

        proj, dt = _inproj(xs, g_mix, w_in_i, F32)
        cst_t = jnp.transpose(state_conv[i], (1, 0, 2))
        cnew_t, xs_conv, bc, dec_t, xdt_t = _ssd_dec_pre(proj, dt, cst_t, cw, cb, dtb, alog)
        y_ssd, ssm_s = _ssd_dec_state(i, dec_t, xdt_t, bc, proj, xs_conv, ssm_all, dsk, nrm_ssd, ssm_s)
        qa_rot, ka_t, va_t = _swa_dec_pre(proj, att_tab_s)
        o_att, *kv_s = _swa_dec(i, qa_rot, ka_t, va_t, kc_all, vc_all, sink32, kv_s)
        qrot, kt = _ret_dec_pre(proj, ret_tab_s)
        o_ret, ret_s = _ret_dec_state(i, gamma, kt, qrot, proj, ret_all, nrm_ret, ret_s)
        xs = _merge(xs, y_ssd, o_att, o_ret, proj, w1, w2, w3, wo)
        xs = _ffn(xs, g_ffn, *ffw) if i % 2 == 0 else _moe(xs, g_ffn, *ffw)
        xs = _ple(xs, p_sample[i].reshape(nb, PLE_DIM), g_ple, wp, wpg, gf, final)
        conv_s.append(jnp.transpose(cnew_t, (1, 0, 2)))

    y_prompt = xp.reshape(1, seq, D_MODEL)
    y_sample = xs.reshape(nb, 1, D_MODEL)
    outs_p = [jnp.stack(l) for l in new_p]
    outs_s = [ssm_s.reshape(state_ssm.shape), jnp.stack(conv_s), jnp.transpose(kv_s[0], (0, 1, 4, 2, 3)),
              jnp.transpose(kv_s[1], (0, 1, 4, 2, 3)), ret_s.reshape(state_ret.shape)]
    return (y_prompt, y_sample, *outs_p, *outs_s)
```

```python
import functools
import math

import jax
import jax.numpy as jnp
from jax import lax
from jax.experimental import pallas as pl
from jax.experimental.pallas import tpu as pltpu

F32 = jnp.float32
BF16 = jnp.bfloat16

D_MODEL = 1024
DEPTH = 2
PAST_LEN = 16384
SSD_HEADS = 16
SSD_HEAD_DIM = 64
SSD_GROUPS = 4
SSD_STATE = 128
SSD_CONV = 4
SSD_CONV_DIM = 2048
ATT_HEAD_DIM = 64
ATT_Q_HEADS = 16
ATT_KV_HEADS = 4
ATT_REP = ATT_Q_HEADS // ATT_KV_HEADS
WINDOW = 128
ROPE_THETA = 500000.0
ROPE_DIM = 16
RET_HEADS = 8
RET_DK = 128
RET_THETA = 10000.0
CHUNK = 128
D_FF = 2816
N_EXPERTS = 8
PLE_DIM = 256
EPS = 1e-6

IN_WIDTHS = (1024, 2048, 16, 1024, 256, 256, 1024, 1024, 1024, 1024, 3072)
N_PROJ = 12288
COL_XBC = 0
COL_Z = 2
COL_QA = 3
COL_QR = 4
COL_KR = 5
COL_VR = 6
COL_GR = 7
COL_GATE = 8
COL_KA = 44
COL_VA = 45
COL_DT = 92

LANES = 128
VMEM_LIMIT = 48 * 1024 * 1024
VMEM_LIMIT_FFN = 56 * 1024 * 1024
MIX_ROWS = 4 * CHUNK


def _cparams(sem, vmem=VMEM_LIMIT):
    return pltpu.CompilerParams(dimension_semantics=sem, vmem_limit_bytes=vmem)


def _bdot(a, b):
    return jnp.dot(a.astype(BF16), b.astype(BF16), preferred_element_type=F32)


def _bdot_nt(a, b):
    return lax.dot_general(a.astype(BF16), b.astype(BF16), (((1,), (1,)), ((), ())),
                           preferred_element_type=F32)


def _split3(x):
    x0 = x.astype(BF16)
    r1 = x - x0.astype(F32)
    x1 = r1.astype(BF16)
    x2 = (r1 - x1.astype(F32)).astype(BF16)
    return x0, x1, x2


def _dot_exact_lhs01(m01, x):
    m = m01.astype(BF16)
    x0, x1, x2 = _split3(x)
    d = lambda b: jnp.dot(m, b, preferred_element_type=F32)
    return d(x0) + d(x1) + d(x2)


def _dot_exact_rhs01(x, m01):
    m = m01.astype(BF16)
    x0, x1, x2 = _split3(x)
    d = lambda a: jnp.dot(a, m, preferred_element_type=F32)
    return d(x0) + d(x1) + d(x2)


def _rms(x, g):
    return x * lax.rsqrt(jnp.mean(x * x, axis=-1, keepdims=True) + EPS) * g


def _sigmoid(x):
    return 1.0 / (1.0 + jnp.exp(-x))


def _silu(x):
    return x * _sigmoid(x)


def _softplus(x):
    return jnp.maximum(x, 0.0) + jnp.log1p(jnp.exp(-jnp.abs(x)))


def _rope_att(x, c, s1, s2):
    w = x.shape[1]
    return x * c + pltpu.roll(x, 8, axis=1) * s1 + pltpu.roll(x, w - 8, axis=1) * s2


def _tile_lanes(t, n):
    return jnp.concatenate([t] * n, axis=1) if n > 1 else t


INPROJ_TN = 1024
DT_TILE = (COL_DT * LANES) // INPROJ_TN
DT_OFF = COL_DT * LANES - DT_TILE * INPROJ_TN


def _inproj_kernel(x_ref, g_ref, w_ref, o_ref, dt_ref, h_ref):
    j = pl.program_id(1)

    @pl.when(j == 0)
    def _():
        h_ref[...] = _rms(x_ref[...], g_ref[...]).astype(BF16)

    acc = jnp.dot(h_ref[...], w_ref[...], preferred_element_type=F32)
    o_ref[...] = acc.astype(o_ref.dtype)

    @pl.when(j == DT_TILE)
    def _():
        dt_ref[...] = acc[:, DT_OFF:DT_OFF + LANES]


def _inproj(x, g, w, out_dtype):
    rows = x.shape[0]
    tm = min(rows, 2048)
    tn = INPROJ_TN
    return pl.pallas_call(
        _inproj_kernel,
        out_shape=(jax.ShapeDtypeStruct((rows, N_PROJ), out_dtype), jax.ShapeDtypeStruct((rows, LANES), F32)),
        grid=(rows // tm, N_PROJ // tn),
        in_specs=[pl.BlockSpec((tm, D_MODEL), lambda i, j: (i, 0)),
                  pl.BlockSpec((1, D_MODEL), lambda i, j: (0, 0)),
                  pl.BlockSpec((D_MODEL, tn), lambda i, j: (0, j))],
        out_specs=(pl.BlockSpec((tm, tn), lambda i, j: (i, j)), pl.BlockSpec((tm, LANES), lambda i, j: (i, 0))),
        scratch_shapes=[pltpu.VMEM((tm, D_MODEL), BF16)],
        compiler_params=_cparams(("parallel", "arbitrary")),
        name="inproj",
    )(x, g, w)


def _chunk_rows(ci):
    return pl.ds(pl.multiple_of(ci * CHUNK, CHUNK), CHUNK)


def _ssd_prompt_chunk(rows, xbc_ref, z_ref, dt_ref, cw_ref, cb_ref, dtb_ref, alog_ref, dsk_ref, nrm_ref,
                      y_ref, xpad_ref, s_ref):
    xbc = xbc_ref[rows, :].astype(F32)
    xpad_ref[8:8 + CHUNK, :] = xbc
    cw = cw_ref[...]
    acc = (xbc * cw[3:4, :] + xpad_ref[7:7 + CHUNK, :] * cw[2:3, :]
           + xpad_ref[6:6 + CHUNK, :] * cw[1:2, :] + xpad_ref[5:5 + CHUNK, :] * cw[0:1, :] + cb_ref[...])
    conv = _silu(acc)
    xpad_ref[0:8, :] = xbc[CHUNK - 8:CHUNK, :]

    xs = conv[:, :1024]
    dt_t = _softplus(dt_ref[rows, :].T[:SSD_HEADS, :] + dtb_ref[...])
    la_t = dt_t * (-jnp.exp(alog_ref[...]))
    row = lax.broadcasted_iota(jnp.int32, (CHUNK, CHUNK), 0)
    col = lax.broadcasted_iota(jnp.int32, (CHUNK, CHUNK), 1)
    causal = row >= col
    cum_t = _dot_exact_rhs01(la_t, (row <= col).astype(F32))
    cum = jnp.concatenate([cum_t, jnp.zeros((LANES - SSD_HEADS, CHUNK), F32)], axis=0).T
    cum_last = jnp.broadcast_to(cum_t[:, CHUNK - 1:CHUNK], (SSD_HEADS, CHUNK))
    w_t = jnp.exp(cum_last - cum_t) * dt_t
    dec_end = jnp.exp(cum_last)

    xs_t = xs.T
    ys = []
    for g in range(SSD_GROUPS):
        bg = conv[:, 1024 + 128 * g:1024 + 128 * (g + 1)]
        cg = conv[:, 1536 + 128 * g:1536 + 128 * (g + 1)]
        cb = _bdot_nt(cg, bg)
        s_g = s_ref[256 * g:256 * (g + 1), :]
        cs = _bdot_nt(cg, s_g)
        xw_parts = []
        dec_parts = []
        for r in range(4):
            h = 4 * g + r
            colb = jnp.broadcast_to(cum[:, h:h + 1], (CHUNK, CHUNK))
            rowb = jnp.broadcast_to(cum_t[h:h + 1, :], (CHUNK, CHUNK))
            dec = jnp.exp(jnp.where(causal, colb - rowb, -jnp.inf))
            m = cb * dec * jnp.broadcast_to(dt_t[h:h + 1, :], (CHUNK, CHUNK))
            xh = xs[:, 64 * h:64 * (h + 1)]
            yh = _bdot(m, xh) + cs[:, 64 * r:64 * (r + 1)] * jnp.exp(colb)[:, :64]
            ys.append(yh)
            xw_parts.append(xs_t[64 * h:64 * (h + 1), :] * jnp.broadcast_to(w_t[h:h + 1, :], (64, CHUNK)))
            dec_parts.append(jnp.broadcast_to(dec_end[h:h + 1, :], (64, SSD_STATE)))
        xw = jnp.concatenate(xw_parts, axis=0)
        s_ref[256 * g:256 * (g + 1), :] = s_g * jnp.concatenate(dec_parts, axis=0) + _bdot(xw, bg)

    y = jnp.concatenate(ys, axis=1) + dsk_ref[...] * xs
    y = y * _silu(z_ref[rows, :].astype(F32))
    y_ref[rows, :] = _rms(y, nrm_ref[...]).astype(y_ref.dtype)


def _ssd_prompt_kernel(xbc_ref, z_ref, dt_ref, cw_ref, cb_ref, dtb_ref, alog_ref, dsk_ref, nrm_ref,
                       y_ref, sfin_ref, cfin_ref, xpad_ref, s_ref):
    t = pl.program_id(0)

    @pl.when(t == 0)
    def _():
        xpad_ref[0:8, :] = jnp.zeros((8, SSD_CONV_DIM), F32)
        s_ref[...] = jnp.zeros_like(s_ref)

    def chunk(ci, carry):
        _ssd_prompt_chunk(_chunk_rows(ci), xbc_ref, z_ref, dt_ref, cw_ref, cb_ref, dtb_ref, alog_ref, dsk_ref,
                          nrm_ref, y_ref, xpad_ref, s_ref)
        return carry

    lax.fori_loop(0, xbc_ref.shape[0] // CHUNK, chunk, 0)

    @pl.when(t == pl.num_programs(0) - 1)
    def _():
        sfin_ref[...] = s_ref[...]
        cfin_ref[...] = xpad_ref[8 + CHUNK - (SSD_CONV - 1):8 + CHUNK, :]


def _ssd_prompt(proj, dt, cw, cb, dtb, alog, dsk, nrm):
    seq = proj.shape[0]
    const = lambda shape: pl.BlockSpec(shape, lambda t: (0,) * len(shape))
    return pl.pallas_call(
        _ssd_prompt_kernel,
        out_shape=(jax.ShapeDtypeStruct((seq, 1024), BF16),
                   jax.ShapeDtypeStruct((1024, SSD_STATE), F32),
                   jax.ShapeDtypeStruct((SSD_CONV - 1, SSD_CONV_DIM), F32)),
        grid=(seq // MIX_ROWS,),
        in_specs=[pl.BlockSpec((MIX_ROWS, 2048), lambda t: (t, COL_XBC)),
                  pl.BlockSpec((MIX_ROWS, 1024), lambda t: (t, COL_Z)),
                  pl.BlockSpec((MIX_ROWS, LANES), lambda t: (t, 0)),
                  const((SSD_CONV, 2048)), const((1, 2048)), const((SSD_HEADS, LANES)), const((SSD_HEADS, LANES)),
                  const((1, 1024)), const((1, 1024))],
        out_specs=(pl.BlockSpec((MIX_ROWS, 1024), lambda t: (t, 0)),
                   const((1024, SSD_STATE)), const((SSD_CONV - 1, SSD_CONV_DIM))),
        scratch_shapes=[pltpu.VMEM((8 + CHUNK, SSD_CONV_DIM), F32), pltpu.VMEM((1024, SSD_STATE), F32)],
        compiler_params=_cparams(("arbitrary",)),
        name="ssd_prompt",
    )(proj, proj, dt, cw, cb, dtb, alog, dsk, nrm)


def _ssd_dec_pre_kernel(xbc_ref, dt_ref, cst_ref, cw_ref, cb_ref, dtb_ref, alog_ref,
                        cnew_ref, xs_ref, bc_ref, dec_t_ref, xdt_t_ref):
    xbc = xbc_ref[...]
    cw = cw_ref[...]
    acc = (cst_ref[0] * cw[0:1, :] + cst_ref[1] * cw[1:2, :] + cst_ref[2] * cw[2:3, :]
           + xbc * cw[3:4, :] + cb_ref[...])
    conv = _silu(acc)
    cnew_ref[0] = cst_ref[1]
    cnew_ref[1] = cst_ref[2]
    cnew_ref[2] = xbc
    xs = conv[:, :1024]
    xs_ref[...] = xs
    bc_ref[...] = conv[:, 1024:]
    dt = _softplus(dt_ref[...] + dtb_ref[...])
    dec = jnp.exp(dt * (-jnp.exp(alog_ref[...])))
    hrow = lax.broadcasted_iota(jnp.int32, (LANES, 1024), 0)
    hcol = lax.broadcasted_iota(jnp.int32, (LANES, 1024), 1)
    expand = ((hcol >> 6) == hrow).astype(F32)
    dec_t_ref[...] = _dot_exact_rhs01(dec, expand).T
    xdt_t_ref[...] = (xs * _dot_exact_rhs01(dt, expand)).T


def _ssd_dec_pre(proj, dt, cst_t, cw, cb, dtb, alog):
    nb = proj.shape[0]
    const = lambda shape: pl.BlockSpec(shape, lambda t: (0,) * len(shape))
    return pl.pallas_call(
        _ssd_dec_pre_kernel,
        out_shape=(jax.ShapeDtypeStruct((3, nb, 2048), F32), jax.ShapeDtypeStruct((nb, 1024), F32),
                   jax.ShapeDtypeStruct((nb, 1024), F32), jax.ShapeDtypeStruct((1024, nb), F32),
                   jax.ShapeDtypeStruct((1024, nb), F32)),
        grid=(1,),
        in_specs=[pl.BlockSpec((nb, 2048), lambda t: (0, COL_XBC)),
                  pl.BlockSpec((nb, LANES), lambda t: (0, 0)),
                  const((3, nb, 2048)), const((SSD_CONV, 2048)), const((1, 2048)), const((1, 128)), const((1, 128))],
        out_specs=(const((3, nb, 2048)), const((nb, 1024)), const((nb, 1024)), const((1024, nb)), const((1024, nb))),
        compiler_params=_cparams(("arbitrary",)),
        name="ssd_dec_pre",
    )(proj, dt, cst_t, cw, cb, dtb, alog)


DEC_BLOCK = 8


def _ssd_dec_state_kernel(dec_t_ref, xdt_t_ref, bc_ref, z_ref, xs_ref, st_ref, dsk_ref, nrm_ref,
                          y_ref, stn_ref):
    i = pl.program_id(0)
    shift = (LANES - i * DEC_BLOCK) % LANES
    decr = pltpu.roll(dec_t_ref[...], shift, axis=1)
    xr = pltpu.roll(xdt_t_ref[...], shift, axis=1)
    bc = bc_ref[...]
    for j in range(DEC_BLOCK):
        for g in range(SSD_GROUPS):
            lo, hi = 256 * g, 256 * (g + 1)
            s_old = st_ref[j, lo:hi, :]
            dcol = jnp.broadcast_to(decr[lo:hi, j:j + 1], (256, SSD_STATE))
            xcol = jnp.broadcast_to(xr[lo:hi, j:j + 1], (256, SSD_STATE))
            s_new = s_old * dcol + xcol * bc[j:j + 1, 128 * g:128 * (g + 1)]
            stn_ref[j, lo:hi, :] = s_new
            cs = _bdot_nt(bc[:, 512 + 128 * g:512 + 128 * (g + 1)], s_new)
            y_ref[j:j + 1, lo:hi] = cs[j:j + 1, :]
    xs = xs_ref[...]
    y = y_ref[...] + dsk_ref[...] * xs
    y = y * _silu(z_ref[...])
    y_ref[...] = _rms(y, nrm_ref[...])


class _LayerCall:
    def __init__(self, layer, steps):
        self.layer, self.steps, self.first = layer, steps, layer == 0
        self.grid = (DEPTH * steps,) if self.first else (steps,)

    def _block(self, t):
        return jnp.minimum(t, self.steps - 1) if self.first else t

    def rows(self, width, col=0):
        return pl.BlockSpec((DEC_BLOCK, width), lambda t: (self._block(t), col))

    def state_in(self, tail):
        zeros = (0,) * len(tail)
        return pl.BlockSpec((None, DEC_BLOCK) + tail, lambda t: (self.layer, self._block(t)) + zeros)

    def state_out(self, tail):
        zeros = (0,) * len(tail)
        if self.first:
            return pl.BlockSpec((None, DEC_BLOCK) + tail, lambda t: (t // self.steps, t % self.steps) + zeros)
        return pl.BlockSpec((None, DEC_BLOCK) + tail, lambda t: (self.layer, t) + zeros)

    def kernel(self, body, n_in, n_carried, stacked_outs):
        def wrapped(*refs):
            refs = refs[:n_in] + refs[n_in + n_carried:]
            if not self.first:
                body(*refs)
                return
            t = pl.program_id(0)

            @pl.when(t < self.steps)
            def _():
                body(*refs)

            @pl.when(t >= self.steps)
            def _():
                for k in stacked_outs:
                    refs[n_in + k][...] = jnp.zeros_like(refs[n_in + k])
        return wrapped


def _ssd_dec_state(layer, dec_t, xdt_t, bc, proj, xs, st_all, dsk, nrm, carried):
    nb = xs.shape[0]
    lc = _LayerCall(layer, nb // DEC_BLOCK)
    const = lambda shape: pl.BlockSpec(shape, lambda t: (0,) * len(shape))
    tail = (1024, SSD_STATE)
    ins = [dec_t, xdt_t, bc, proj, xs, st_all, dsk, nrm]
    in_specs = [const((1024, nb)), const((1024, nb)), lc.rows(1024), lc.rows(1024, COL_Z), lc.rows(1024),
                lc.state_in(tail), const((1, 1024)), const((1, 1024))]
    n_in = len(ins)
    carried = [] if carried is None else [carried]
    return pl.pallas_call(
        lc.kernel(_ssd_dec_state_kernel, n_in, len(carried), (1,)),
        out_shape=(jax.ShapeDtypeStruct((nb, 1024), F32), jax.ShapeDtypeStruct(st_all.shape, F32)),
        grid=lc.grid,
        in_specs=in_specs + [pl.BlockSpec(memory_space=pl.ANY)] * len(carried),
        out_specs=(lc.rows(1024), lc.state_out(tail)),
        input_output_aliases={n_in + k: 1 + k for k in range(len(carried))},
        compiler_params=_cparams(("arbitrary",)),
        name="ssd_dec_state",
    )(*ins, *carried)


def _block_cos_sin(within_ref, step_ref, bi):
    row = step_ref[pl.ds(bi, 1), :]
    cb, sb = row[:, :LANES], row[:, LANES:]
    ci, si = within_ref[:, :LANES], within_ref[:, LANES:]
    return cb * ci - sb * si, sb * ci + cb * si


def _swa_prompt_block(bi, has_prev, sink_ref, q_ref, k_ref, v_ref, within_ref, step_ref, o_ref, kp_ref, vp_ref):
    rows = _chunk_rows(bi)
    cosp, sinp = _block_cos_sin(within_ref, step_ref, bi)
    l64 = lax.broadcasted_iota(jnp.int32, (1, LANES), 1) & (ATT_HEAD_DIM - 1)
    c = jnp.where(l64 < ROPE_DIM, cosp, 1.0)
    s1 = jnp.where(jnp.logical_and(l64 >= ROPE_DIM // 2, l64 < ROPE_DIM), sinp, 0.0)
    s2 = jnp.where(l64 < ROPE_DIM // 2, -sinp, 0.0)
    q = _rope_att(q_ref[rows, :].astype(F32), _tile_lanes(c, 8), _tile_lanes(s1, 8), _tile_lanes(s2, 8))
    k = _rope_att(k_ref[rows, :].astype(F32), _tile_lanes(c, 2), _tile_lanes(s1, 2), _tile_lanes(s2, 2))
    v = v_ref[rows, :].astype(F32)
    ghead = lax.broadcasted_iota(jnp.int32, (WINDOW, 256), 1) >> 6
    expand = lambda t: jnp.concatenate([jnp.where(ghead == g, t, 0.0) for g in range(ATT_KV_HEADS)],
                                       axis=0).astype(BF16)
    kbd, vbd = expand(k), expand(v)
    kbd_prev, vbd_prev = expand(kp_ref[...]), expand(vp_ref[...])
    qall = jnp.concatenate([q[:, 256 * r:256 * (r + 1)] for r in range(ATT_REP)], axis=0).astype(BF16)
    scale = ATT_HEAD_DIM ** -0.5
    nt_dims = (((1,), (1,)), ((), ()))
    sp_all = lax.dot_general(qall, kbd_prev, nt_dims, preferred_element_type=F32) * scale
    sc_all = lax.dot_general(qall, kbd, nt_dims, preferred_element_type=F32) * scale
    nq = ATT_REP * WINDOW
    qi = lax.broadcasted_iota(jnp.int32, (nq, WINDOW), 0) & (WINDOW - 1)
    kj = lax.broadcasted_iota(jnp.int32, (nq, WINDOW), 1)
    mask_prev = jnp.logical_and(kj > qi, has_prev)
    mask_cur = kj <= qi
    rep = lax.broadcasted_iota(jnp.int32, (nq, 1), 0) >> 7
    pp, pc = [], []
    for g in range(ATT_KV_HEADS):
        sp = jnp.where(mask_prev, sp_all[:, WINDOW * g:WINDOW * (g + 1)], -jnp.inf)
        sc = jnp.where(mask_cur, sc_all[:, WINDOW * g:WINDOW * (g + 1)], -jnp.inf)
        sink = jnp.where(rep == 0, sink_ref[4 * g],
                         jnp.where(rep == 1, sink_ref[4 * g + 1],
                                   jnp.where(rep == 2, sink_ref[4 * g + 2], sink_ref[4 * g + 3])))
        m = jnp.maximum(jnp.max(jnp.maximum(sp, sc), axis=1, keepdims=True), sink)
        ep = jnp.exp(sp - m)
        ec = jnp.exp(sc - m)
        inv = 1.0 / (jnp.sum(ep + ec, axis=1, keepdims=True) + jnp.exp(sink - m))
        pp.append((ep * inv).astype(BF16))
        pc.append((ec * inv).astype(BF16))
    o = (jnp.dot(jnp.concatenate(pp, axis=1), vbd_prev, preferred_element_type=F32)
         + jnp.dot(jnp.concatenate(pc, axis=1), vbd, preferred_element_type=F32))
    for r in range(ATT_REP):
        o_ref[rows, 256 * r:256 * (r + 1)] = o[WINDOW * r:WINDOW * (r + 1), :].astype(o_ref.dtype)
    kp_ref[...] = k
    vp_ref[...] = v


def _swa_prompt_kernel(sink_ref, q_ref, k_ref, v_ref, within_ref, step_ref,
                       o_ref, wk_ref, wv_ref, kp_ref, vp_ref):
    n = pl.program_id(0)
    blocks = q_ref.shape[0] // WINDOW

    @pl.when(n == 0)
    def _():
        kp_ref[...] = jnp.zeros_like(kp_ref)
        vp_ref[...] = jnp.zeros_like(vp_ref)

    def block(bi, carry):
        _swa_prompt_block(bi, n * blocks + bi > 0, sink_ref, q_ref, k_ref, v_ref,
                          within_ref, step_ref, o_ref, kp_ref, vp_ref)
        return carry

    lax.fori_loop(0, blocks, block, 0)

    @pl.when(n == pl.num_programs(0) - 1)
    def _():
        wk_ref[...] = kp_ref[...]
        wv_ref[...] = vp_ref[...]


def _swa_prompt(sinks, proj, tabs):
    seq = proj.shape[0]
    within, steps = tabs
    const = lambda shape: pl.BlockSpec(shape, lambda t: (0,) * len(shape))
    return pl.pallas_call(
        _swa_prompt_kernel,
        out_shape=(jax.ShapeDtypeStruct((seq, 1024), BF16),
                   jax.ShapeDtypeStruct((WINDOW, 256), F32), jax.ShapeDtypeStruct((WINDOW, 256), F32)),
        grid=(seq // MIX_ROWS,),
        in_specs=[pl.BlockSpec(memory_space=pltpu.SMEM),
                  pl.BlockSpec((MIX_ROWS, 1024), lambda t: (t, COL_QA)),
                  pl.BlockSpec((MIX_ROWS, 256), lambda t: (t, COL_KA)),
                  pl.BlockSpec((MIX_ROWS, 256), lambda t: (t, COL_VA)),
                  const((CHUNK, 2 * LANES)), pl.BlockSpec((None, 8, 2 * LANES), lambda t: (t, 0, 0))],
        out_specs=(pl.BlockSpec((MIX_ROWS, 1024), lambda t: (t, 0)), const((WINDOW, 256)), const((WINDOW, 256))),
        scratch_shapes=[pltpu.VMEM((WINDOW, 256), F32), pltpu.VMEM((WINDOW, 256), F32)],
        compiler_params=_cparams(("arbitrary",)),
        name="swa_prompt",
    )(sinks, proj, proj, proj, within, steps)


def _swa_dec_pre_kernel(q_ref, k_ref, v_ref, c_ref, s1_ref, s2_ref, qrot_ref, kt_ref, vt_ref):
    c, s1, s2 = c_ref[...], s1_ref[...], s2_ref[...]
    qrot_ref[...] = _rope_att(q_ref[...], _tile_lanes(c, 8), _tile_lanes(s1, 8), _tile_lanes(s2, 8))
    kt_ref[...] = _rope_att(k_ref[...], _tile_lanes(c, 2), _tile_lanes(s1, 2), _tile_lanes(s2, 2)).T
    vt_ref[...] = v_ref[...].T


def _swa_dec_pre(proj, tabs):
    nb = proj.shape[0]
    c, s1, s2 = tabs
    const = lambda shape: pl.BlockSpec(shape, lambda t: (0,) * len(shape))
    return pl.pallas_call(
        _swa_dec_pre_kernel,
        out_shape=(jax.ShapeDtypeStruct((nb, 1024), F32), jax.ShapeDtypeStruct((256, nb), F32),
                   jax.ShapeDtypeStruct((256, nb), F32)),
        grid=(1,),
        in_specs=[pl.BlockSpec((nb, 1024), lambda t: (0, COL_QA)), pl.BlockSpec((nb, 256), lambda t: (0, COL_KA)),
                  pl.BlockSpec((nb, 256), lambda t: (0, COL_VA)), const((1, LANES)), const((1, LANES)), const((1, LANES))],
        out_specs=(const((nb, 1024)), const((256, nb)), const((256, nb))),
        compiler_params=_cparams(("arbitrary",)),
        name="swa_dec_pre",
    )(proj, proj, proj, c, s1, s2)


def _swa_dec_kernel(q_ref, kt_ref, vt_ref, kc_ref, vc_ref, sink_ref, o_ref, kcn_ref, vcn_ref):
    i = pl.program_id(0)
    shift = (LANES - i * DEC_BLOCK) % LANES
    ktr = pltpu.roll(kt_ref[...], shift, axis=1)
    vtr = pltpu.roll(vt_ref[...], shift, axis=1)
    q = q_ref[...]
    lane = lax.broadcasted_iota(jnp.int32, (ATT_HEAD_DIM, WINDOW), 1)
    rowid = lax.broadcasted_iota(jnp.int32, (8, ATT_HEAD_DIM), 0)
    scale = ATT_HEAD_DIM ** -0.5
    append = lambda old, col: jnp.where(lane == WINDOW - 1, jnp.broadcast_to(col, (ATT_HEAD_DIM, WINDOW)),
                                        pltpu.roll(old, WINDOW - 1, axis=1))
    pairs = [(j, g) for j in range(DEC_BLOCK) for g in range(ATT_KV_HEADS)]
    scores, values = [], []
    for j, g in pairs:
        lo, hi = ATT_HEAD_DIM * g, ATT_HEAD_DIM * (g + 1)
        kt = append(kc_ref[j, g], ktr[lo:hi, j:j + 1])
        vt = append(vc_ref[j, g], vtr[lo:hi, j:j + 1])
        kcn_ref[j, g] = kt
        vcn_ref[j, g] = vt
        qg = jnp.zeros((8, ATT_HEAD_DIM), F32)
        for r in range(ATT_REP):
            src = 256 * r + lo
            qg = jnp.where(rowid == r, jnp.broadcast_to(q[j:j + 1, src:src + ATT_HEAD_DIM], (8, ATT_HEAD_DIM)), qg)
        scores.append(_bdot(qg, kt))
        values.append(vt.astype(BF16))
    s = jnp.concatenate(scores, axis=0) * scale
    sink = jnp.concatenate([sink_ref[...]] * DEC_BLOCK, axis=0)[:, 0:1]
    m = jnp.maximum(jnp.max(s, axis=1, keepdims=True), sink)
    e = jnp.exp(s - m)
    p = e * (1.0 / (jnp.sum(e, axis=1, keepdims=True) + jnp.exp(sink - m)))
    for idx, (j, g) in enumerate(pairs):
        o = lax.dot_general(p[8 * idx:8 * (idx + 1), :].astype(BF16), values[idx], (((1,), (1,)), ((), ())),
                            preferred_element_type=F32)
        for r in range(ATT_REP):
            dst = 256 * r + ATT_HEAD_DIM * g
            o_ref[j:j + 1, dst:dst + ATT_HEAD_DIM] = o[r:r + 1, :]


def _swa_dec(layer, qrot, kt, vt, kc_all, vc_all, sink32, carried):
    nb = qrot.shape[0]
    lc = _LayerCall(layer, nb // DEC_BLOCK)
    const = lambda shape: pl.BlockSpec(shape, lambda t: (0,) * len(shape))
    tail = (ATT_KV_HEADS, ATT_HEAD_DIM, WINDOW)
    ins = [qrot, kt, vt, kc_all, vc_all, sink32]
    in_specs = [lc.rows(1024), const((256, nb)), const((256, nb)), lc.state_in(tail), lc.state_in(tail),
                const((32, LANES))]
    n_in = len(ins)
    carried = [] if carried is None else list(carried)
    return pl.pallas_call(
        lc.kernel(_swa_dec_kernel, n_in, len(carried), (1, 2)),
        out_shape=(jax.ShapeDtypeStruct((nb, 1024), F32),
                   jax.ShapeDtypeStruct(kc_all.shape, F32), jax.ShapeDtypeStruct(vc_all.shape, F32)),
        grid=lc.grid,
        in_specs=in_specs + [pl.BlockSpec(memory_space=pl.ANY)] * len(carried),
        out_specs=(lc.rows(1024), lc.state_out(tail), lc.state_out(tail)),
        input_output_aliases={n_in + k: 1 + k for k in range(len(carried))},
        compiler_params=_cparams(("arbitrary",)),
        name="swa_dec",
    )(*ins, *carried)


def _ret_prompt_chunk(ci, lg_ref, q_ref, k_ref, v_ref, gr_ref, within_ref, step_ref, nrm_ref,
                      o_ref, st_ref, intra_ref, fs_ref, te_ref):
    rows = _chunk_rows(ci)
    c, sinp = _block_cos_sin(within_ref, step_ref, ci)
    s = jnp.where(lax.broadcasted_iota(jnp.int32, (1, LANES), 1) < RET_DK // 2, -sinp, sinp)
    q = q_ref[rows, :].astype(F32)
    k = k_ref[rows, :].astype(F32)
    v = v_ref[rows, :].astype(F32)
    gr = gr_ref[rows, :].astype(F32)
    nrm = nrm_ref[...]
    for h in range(RET_HEADS):
        sl = slice(128 * h, 128 * (h + 1))
        qh = q[:, sl]
        kh = k[:, sl]
        qh = qh * c + pltpu.roll(qh, 64, axis=1) * s
        kh = (kh * c + pltpu.roll(kh, 64, axis=1) * s) * (RET_DK ** -0.5)
        vh = v[:, sl]
        att = _bdot_nt(qh, kh) * intra_ref[h]
        s_old = st_ref[h]
        o = _bdot(att, vh) + _bdot(qh, s_old) * fs_ref[h]
        cd = jnp.exp(jnp.zeros((1, RET_DK), F32) + CHUNK * lg_ref[h])
        st_ref[h] = s_old * cd + _bdot((kh * te_ref[h]).T, vh)
        o = o * lax.rsqrt(jnp.mean(o * o, axis=-1, keepdims=True) + EPS)
        o_ref[rows, sl] = (o * nrm[:, sl] * _silu(gr[:, sl])).astype(o_ref.dtype)


def _ret_prompt_kernel(lg_ref, q_ref, k_ref, v_ref, gr_ref, within_ref, step_ref, nrm_ref,
                       o_ref, sfin_ref, st_ref, intra_ref, fs_ref, te_ref):
    t = pl.program_id(0)

    @pl.when(t == 0)
    def _():
        st_ref[...] = jnp.zeros_like(st_ref)
        ri = lax.broadcasted_iota(jnp.int32, (CHUNK, CHUNK), 0).astype(F32)
        ci = lax.broadcasted_iota(jnp.int32, (CHUNK, CHUNK), 1).astype(F32)
        rel = ri - ci
        for h in range(RET_HEADS):
            lg = lg_ref[h]
            intra_ref[h] = jnp.exp(jnp.where(rel >= 0, rel * lg, -jnp.inf))
            fs_ref[h] = jnp.exp((ri + 1.0) * lg)
            te_ref[h] = jnp.exp((CHUNK - 1.0 - ri) * lg)

    def chunk(ci, carry):
        _ret_prompt_chunk(ci, lg_ref, q_ref, k_ref, v_ref, gr_ref, within_ref, step_ref, nrm_ref,
                          o_ref, st_ref, intra_ref, fs_ref, te_ref)
        return carry

    lax.fori_loop(0, q_ref.shape[0] // CHUNK, chunk, 0)

    @pl.when(t == pl.num_programs(0) - 1)
    def _():
        sfin_ref[...] = st_ref[...]


def _ret_prompt(log_gamma, proj, tabs, nrm):
    seq = proj.shape[0]
    within, steps = tabs
    const = lambda shape: pl.BlockSpec(shape, lambda t: (0,) * len(shape))
    col = lambda cidx: pl.BlockSpec((MIX_ROWS, 1024), lambda t: (t, cidx))
    tbl = pltpu.VMEM((RET_HEADS, CHUNK, CHUNK), F32)
    return pl.pallas_call(
        _ret_prompt_kernel,
        out_shape=(jax.ShapeDtypeStruct((seq, 1024), BF16), jax.ShapeDtypeStruct((RET_HEADS, RET_DK, 128), F32)),
        grid=(seq // MIX_ROWS,),
        in_specs=[pl.BlockSpec(memory_space=pltpu.SMEM), col(COL_QR), col(COL_KR), col(COL_VR), col(COL_GR),
                  const((CHUNK, 2 * LANES)), pl.BlockSpec((None, 8, 2 * LANES), lambda t: (t, 0, 0)), const((1, 1024))],
        out_specs=(pl.BlockSpec((MIX_ROWS, 1024), lambda t: (t, 0)), const((RET_HEADS, RET_DK, 128))),
        scratch_shapes=[tbl, tbl, tbl, tbl],
        compiler_params=_cparams(("arbitrary",)),
        name="ret_prompt",
    )(log_gamma, proj, proj, proj, proj, within, steps, nrm)


def _ret_dec_pre_kernel(q_ref, k_ref, c_ref, s_ref, qrot_ref, kt_ref):
    c = c_ref[...]
    s = s_ref[...]
    q = q_ref[...]
    k = k_ref[...]
    ks = []
    for h in range(RET_HEADS):
        sl = slice(128 * h, 128 * (h + 1))
        qh = q[:, sl]
        kh = k[:, sl]
        qrot_ref[:, sl] = qh * c + pltpu.roll(qh, 64, axis=1) * s
        ks.append((kh * c + pltpu.roll(kh, 64, axis=1) * s) * (RET_DK ** -0.5))
    kt_ref[...] = jnp.concatenate(ks, axis=1).T


def _ret_dec_pre(proj, tabs):
    nb = proj.shape[0]
    c, s = tabs
    const = lambda shape: pl.BlockSpec(shape, lambda t: (0,) * len(shape))
    return pl.pallas_call(
        _ret_dec_pre_kernel,
        out_shape=(jax.ShapeDtypeStruct((nb, 1024), F32), jax.ShapeDtypeStruct((1024, nb), F32)),
        grid=(1,),
        in_specs=[pl.BlockSpec((nb, 1024), lambda t: (0, COL_QR)), pl.BlockSpec((nb, 1024), lambda t: (0, COL_KR)),
                  const((1, LANES)), const((1, LANES))],
        out_specs=(const((nb, 1024)), const((1024, nb))),
        compiler_params=_cparams(("arbitrary",)),
        name="ret_dec_pre",
    )(proj, proj, c, s)


def _ret_dec_state_kernel(gam_ref, kt_ref, q_ref, v_ref, gr_ref, st_ref, nrm_ref, o_ref, stn_ref):
    i = pl.program_id(0)
    shift = (LANES - i * DEC_BLOCK) % LANES
    kr = pltpu.roll(kt_ref[...], shift, axis=1)
    q = q_ref[...]
    v = v_ref[...]
    for j in range(DEC_BLOCK):
        for h in range(RET_HEADS):
            lo, hi = 128 * h, 128 * (h + 1)
            kcol = jnp.broadcast_to(kr[lo:hi, j:j + 1], (RET_DK, 128))
            s_new = st_ref[j, lo:hi, :] * gam_ref[h] + kcol * v[j:j + 1, lo:hi]
            stn_ref[j, lo:hi, :] = s_new
            qs = _bdot(q[:, lo:hi], s_new)
            o_ref[j:j + 1, lo:hi] = qs[j:j + 1, :]
    gr = gr_ref[...]
    nrm = nrm_ref[...]
    for h in range(RET_HEADS):
        sl = slice(128 * h, 128 * (h + 1))
        o = o_ref[:, sl]
        o = o * lax.rsqrt(jnp.mean(o * o, axis=-1, keepdims=True) + EPS)
        o_ref[:, sl] = o * nrm[:, sl] * _silu(gr[:, sl])


def _ret_dec_state(layer, gam, kt, qrot, proj, st_all, nrm, carried):
    nb = qrot.shape[0]
    lc = _LayerCall(layer, nb // DEC_BLOCK)
    const = lambda shape: pl.BlockSpec(shape, lambda t: (0,) * len(shape))
    tail = (1024, 128)
    ins = [gam, kt, qrot, proj, proj, st_all, nrm]
    in_specs = [pl.BlockSpec(memory_space=pltpu.SMEM), const((1024, nb)), lc.rows(1024),
                lc.rows(1024, COL_VR), lc.rows(1024, COL_GR), lc.state_in(tail), const((1, 1024))]
    n_in = len(ins)
    carried = [] if carried is None else [carried]
    return pl.pallas_call(
        lc.kernel(_ret_dec_state_kernel, n_in, len(carried), (1,)),
        out_shape=(jax.ShapeDtypeStruct((nb, 1024), F32), jax.ShapeDtypeStruct(st_all.shape, F32)),
        grid=lc.grid,
        in_specs=in_specs + [pl.BlockSpec(memory_space=pl.ANY)] * len(carried),
        out_specs=(lc.rows(1024), lc.state_out(tail)),
        input_output_aliases={n_in + k: 1 + k for k in range(len(carried))},
        compiler_params=_cparams(("arbitrary",)),
        name="ret_dec_state",
    )(*ins, *carried)


def _merge_math(x_ref, a_ref, b_ref, c_ref, g1_ref, g2_ref, g3_ref, w1_ref, w2_ref, w3_ref, wo_ref):
    gate = lambda ref: _sigmoid(ref[...].astype(F32))
    m = (gate(g1_ref) * jnp.dot(a_ref[...].astype(BF16), w1_ref[...], preferred_element_type=F32)
         + gate(g2_ref) * jnp.dot(b_ref[...].astype(BF16), w2_ref[...], preferred_element_type=F32)
         + gate(g3_ref) * jnp.dot(c_ref[...].astype(BF16), w3_ref[...], preferred_element_type=F32))
    return x_ref[...] + jnp.dot(m.astype(BF16), wo_ref[...], preferred_element_type=F32)


def _merge_kernel(*refs):
    *ins, o_ref = refs
    o_ref[...] = _merge_math(*ins)


def _merge_route_kernel(*refs):
    *ins, gn_ref, rw_ref, rb_ref, o_ref, route_ref = refs
    x = _merge_math(*ins)
    o_ref[...] = x
    route_ref[...] = _route_row(_rms(x, gn_ref[...]), rw_ref[0], rw_ref[1], rb_ref[...])


def _merge(x, a, b, c, proj, w1, w2, w3, wo, router=None):
    rows = x.shape[0]
    tm = min(rows, 512)
    rowb = pl.BlockSpec((tm, 1024), lambda i: (i, 0))
    gate = lambda k: pl.BlockSpec((tm, 1024), lambda i: (i, COL_GATE + k))
    wsp = pl.BlockSpec((1024, 1024), lambda i: (0, 0))
    ins = [x, a, b, c, proj, proj, proj, w1, w2, w3, wo]
    in_specs = [rowb, rowb, rowb, rowb, gate(0), gate(1), gate(2), wsp, wsp, wsp, wsp]
    out_shape = jax.ShapeDtypeStruct((rows, 1024), F32)
    if router is None:
        kern, out_specs = _merge_kernel, rowb
    else:
        kern = _merge_route_kernel
        ins += list(router)
        in_specs += [pl.BlockSpec((1, 1024), lambda i: (0, 0)), pl.BlockSpec((2, 1024, LANES), lambda i: (0, 0, 0)),
                     pl.BlockSpec((1, LANES), lambda i: (0, 0))]
        out_shape = (out_shape, jax.ShapeDtypeStruct((rows, LANES), F32))
        out_specs = (rowb, pl.BlockSpec((tm, LANES), lambda i: (i, 0)))
    return pl.pallas_call(
        kern,
        out_shape=out_shape,
        grid=(rows // tm,),
        in_specs=in_specs,
        out_specs=out_specs,
        compiler_params=_cparams(("parallel",)),
        name="merge",
    )(*ins)


FF_TILE = 1408
FF_SPLIT = 768


def _ffn_kernel(x_ref, g_ref, wg_ref, wu_ref, wd_ref, o_ref, h_ref, acc_ref):
    j = pl.program_id(1)

    @pl.when(j == 0)
    def _():
        h_ref[...] = _rms(x_ref[...], g_ref[...]).astype(BF16)
        acc_ref[...] = jnp.zeros_like(acc_ref)

    h = h_ref[...]
    for lo, hi in ((0, FF_SPLIT), (FF_SPLIT, FF_TILE)):
        a = jnp.dot(h, wg_ref[:, lo:hi], preferred_element_type=F32)
        u = jnp.dot(h, wu_ref[:, lo:hi], preferred_element_type=F32)
        acc_ref[...] += jnp.dot((_silu(a) * u).astype(BF16), wd_ref[lo:hi, :], preferred_element_type=F32)

    @pl.when(j == pl.num_programs(1) - 1)
    def _():
        o_ref[...] = x_ref[...] + acc_ref[...]


def _ffn(x, g, wg, wu, wd):
    rows = x.shape[0]
    tm = min(rows, 1024)
    return pl.pallas_call(
        _ffn_kernel,
        out_shape=jax.ShapeDtypeStruct((rows, 1024), F32),
        grid=(rows // tm, D_FF // FF_TILE),
        in_specs=[pl.BlockSpec((tm, 1024), lambda i, j: (i, 0)), pl.BlockSpec((1, 1024), lambda i, j: (0, 0)),
                  pl.BlockSpec((1024, FF_TILE), lambda i, j: (0, j)), pl.BlockSpec((1024, FF_TILE), lambda i, j: (0, j)),
                  pl.BlockSpec((FF_TILE, 1024), lambda i, j: (j, 0))],
        out_specs=pl.BlockSpec((tm, 1024), lambda i, j: (i, 0)),
        scratch_shapes=[pltpu.VMEM((tm, 1024), BF16), pltpu.VMEM((tm, 1024), F32)],
        compiler_params=_cparams(("parallel", "arbitrary"), vmem=VMEM_LIMIT_FFN),
        name="ffn",
    )(x, g, wg, wu, wd)


MOE_FF_TILE = 256


def _top2(h, rw_hi, rw_lo, rb, lane):
    h_hi = h.astype(BF16)
    h_lo = (h - h_hi.astype(F32)).astype(BF16)
    d = lambda a, b: jnp.dot(a, b, preferred_element_type=F32)
    logits = d(h_hi, rw_hi) + d(h_hi, rw_lo) + d(h_lo, rw_hi) + rb
    logits = jnp.where(lane < N_EXPERTS, logits, -jnp.inf)
    m1 = jnp.max(logits, axis=1, keepdims=True)
    i1 = jnp.min(jnp.where(logits == m1, lane, float(LANES)), axis=1, keepdims=True)
    rest = jnp.where(lane == i1, -jnp.inf, logits)
    m2 = jnp.max(rest, axis=1, keepdims=True)
    i2 = jnp.min(jnp.where(rest == m2, lane, float(LANES)), axis=1, keepdims=True)
    e2 = jnp.exp(m2 - m1)
    p1 = 1.0 / (1.0 + e2)
    return i1, i2, p1, e2 * p1


def _moe_kernel(x_ref, g_ref, rw_ref, rb_ref, wg_ref, wu_ref, wd_ref, o_ref, h_ref, acc_ref, comb_ref):
    e = pl.program_id(1)
    j = pl.program_id(2)
    tm = x_ref.shape[0]
    lane = lax.broadcasted_iota(jnp.int32, (tm, LANES), 1).astype(F32)

    @pl.when(jnp.logical_and(e == 0, j == 0))
    def _():
        h = _rms(x_ref[...], g_ref[...])
        h_ref[...] = h.astype(BF16)
        i1, i2, p1, p2 = _top2(h, rw_ref[0], rw_ref[1], rb_ref[...], lane)
        comb_ref[...] = jnp.where(lane == i1, p1, 0.0) + jnp.where(lane == i2, p2, 0.0)
        acc_ref[...] = jnp.zeros_like(acc_ref)

    ce = jnp.sum(jnp.where(lane == e.astype(F32), comb_ref[...], 0.0), axis=1, keepdims=True)
    h = h_ref[...]
    a = jnp.dot(h, wg_ref[0], preferred_element_type=F32)
    u = jnp.dot(h, wu_ref[0], preferred_element_type=F32)
    acc_ref[...] += ce * jnp.dot((_silu(a) * u).astype(BF16), wd_ref[0], preferred_element_type=F32)

    @pl.when(jnp.logical_and(e == pl.num_programs(1) - 1, j == pl.num_programs(2) - 1))
    def _():
        o_ref[...] = x_ref[...] + acc_ref[...]


def _moe(x, g, rw, rb, wg, wu, wd):
    rows = x.shape[0]
    tm = min(rows, 1024)
    tf = MOE_FF_TILE
    return pl.pallas_call(
        _moe_kernel,
        out_shape=jax.ShapeDtypeStruct((rows, 1024), F32),
        grid=(rows // tm, N_EXPERTS, D_FF // tf),
        in_specs=[pl.BlockSpec((tm, 1024), lambda i, e, j: (i, 0)), pl.BlockSpec((1, 1024), lambda i, e, j: (0, 0)),
                  pl.BlockSpec((2, 1024, LANES), lambda i, e, j: (0, 0, 0)),
                  pl.BlockSpec((1, LANES), lambda i, e, j: (0, 0)),
                  pl.BlockSpec((1, 1024, tf), lambda i, e, j: (e, 0, j)),
                  pl.BlockSpec((1, 1024, tf), lambda i, e, j: (e, 0, j)),
                  pl.BlockSpec((1, tf, 1024), lambda i, e, j: (e, j, 0))],
        out_specs=pl.BlockSpec((tm, 1024), lambda i, e, j: (i, 0)),
        scratch_shapes=[pltpu.VMEM((tm, 1024), BF16), pltpu.VMEM((tm, 1024), F32), pltpu.VMEM((tm, LANES), F32)],
        compiler_params=_cparams(("parallel", "arbitrary", "arbitrary")),
        name="moe",
    )(x, g, rw, rb, wg, wu, wd)


MOE_ROWS = 512
MOE_GROUP_FF = 1408
GATHER_UNROLL = 8


def _route_row(h, rw_hi, rw_lo, rb):
    lane = lax.broadcasted_iota(jnp.int32, (h.shape[0], LANES), 1).astype(F32)
    i1, i2, p1, p2 = _top2(h, rw_hi, rw_lo, rb, lane)
    return jnp.where(lane == 0.0, i1, jnp.where(lane == 1.0, i2, jnp.where(lane == 2.0, p1,
                     jnp.where(lane == 3.0, p2, 0.0))))


def _route_plan(route, tm):
    n = route.shape[0]
    n_tiles = (2 * n) // tm + N_EXPERTS
    e_flat = route[:, :2].astype(jnp.int32).reshape(-1)
    onehot = (e_flat[:, None] == jnp.arange(N_EXPERTS, dtype=jnp.int32)[None, :]).astype(jnp.int32)
    csum = jnp.cumsum(onehot, axis=0)
    counts = csum[-1]
    tiles_e = (counts + tm - 1) // tm
    tile_end = jnp.cumsum(tiles_e)
    row_start = (tile_end - tiles_e) * tm
    pos = jnp.sum((csum - onehot + row_start[None, :]) * onehot, axis=1).astype(jnp.int32)
    tile_expert = jnp.minimum(jnp.sum(jnp.arange(n_tiles, dtype=jnp.int32)[:, None] >= tile_end[None, :], axis=1),
                              N_EXPERTS - 1).astype(jnp.int32)
    n_used = tile_end[-1:].astype(jnp.int32)
    src = jnp.zeros((n_tiles * tm,), jnp.int32).at[pos].set(jnp.arange(2 * n, dtype=jnp.int32) // 2)
    return pos, src.reshape(n_tiles, 1, tm), tile_expert, n_used


def _moe_group_kernel(te_ref, nu_ref, src_ref, srcn_ref, x_hbm, g_ref, wg_ref, wu_ref, wd_ref,
                      y_ref, buf, sem, h_ref, acc_ref):
    i = pl.program_id(0)
    j = pl.program_id(1)
    tm = buf.shape[1]
    slot = i % 2
    active = i < nu_ref[0]

    def row_copy(idx_ref, s, r):
        return pltpu.make_async_copy(x_hbm.at[pl.ds(idx_ref[0, 0, r], 1), :], buf.at[s, pl.ds(r, 1), :], sem.at[s])

    def gather(idx_ref, s):
        def body(r, c):
            row_copy(idx_ref, s, r).start()
            return c
        lax.fori_loop(0, tm, body, 0, unroll=GATHER_UNROLL)

    @pl.when(jnp.logical_and(active, j == 0))
    def _():
        @pl.when(i == 0)
        def _():
            gather(src_ref, 0)

        pltpu.make_async_copy(x_hbm.at[pl.ds(0, tm), :], buf.at[slot], sem.at[slot]).wait()

        h_ref[...] = _rms(buf[slot], g_ref[...]).astype(BF16)
        acc_ref[...] = jnp.zeros_like(acc_ref)

    @pl.when(active)
    def _():
        share = tm // (D_FF // MOE_GROUP_FF)

        def start_copies(lo, hi):
            for r in range(lo, hi):
                row_copy(srcn_ref, 1 - slot, j * share + r).start()

        h = h_ref[...]
        start_copies(0, share // 3)
        a = jnp.dot(h, wg_ref[0], preferred_element_type=F32)
        start_copies(share // 3, 2 * share // 3)
        u = jnp.dot(h, wu_ref[0], preferred_element_type=F32)
        start_copies(2 * share // 3, share)
        acc_ref[...] += jnp.dot((_silu(a) * u).astype(BF16), wd_ref[0], preferred_element_type=F32)

    @pl.when(j == pl.num_programs(1) - 1)
    def _():
        @pl.when(active)
        def _():
            y_ref[...] = acc_ref[...]

            @pl.when(i == nu_ref[0] - 1)
            def _():
                pltpu.make_async_copy(x_hbm.at[pl.ds(0, tm), :], buf.at[1 - slot], sem.at[1 - slot]).wait()

        @pl.when(jnp.logical_not(active))
        def _():
            y_ref[...] = jnp.zeros_like(y_ref)


def _moe_group(tile_expert, n_used, src, x, g, wg, wu, wd):
    n_tiles, _, tm = src.shape
    tf = MOE_GROUP_FF
    grid_spec = pltpu.PrefetchScalarGridSpec(
        num_scalar_prefetch=2,
        grid=(n_tiles, D_FF // tf),
        in_specs=[pl.BlockSpec((1, 1, tm), lambda i, j, te, nu: (i, 0, 0), memory_space=pltpu.SMEM),
                  pl.BlockSpec((1, 1, tm), lambda i, j, te, nu: (jnp.minimum(i + 1, n_tiles - 1), 0, 0),
                               memory_space=pltpu.SMEM),
                  pl.BlockSpec(memory_space=pl.ANY),
                  pl.BlockSpec((1, 1024), lambda i, j, te, nu: (0, 0)),
                  pl.BlockSpec((1, 1024, tf), lambda i, j, te, nu: (te[i], 0, j)),
                  pl.BlockSpec((1, 1024, tf), lambda i, j, te, nu: (te[i], 0, j)),
                  pl.BlockSpec((1, tf, 1024), lambda i, j, te, nu: (te[i], j, 0))],
        out_specs=pl.BlockSpec((tm, 1024), lambda i, j, te, nu: (i, 0)),
        scratch_shapes=[pltpu.VMEM((2, tm, 1024), F32), pltpu.SemaphoreType.DMA((2,)),
                        pltpu.VMEM((tm, 1024), BF16), pltpu.VMEM((tm, 1024), F32)])
    return pl.pallas_call(
        _moe_group_kernel,
        out_shape=jax.ShapeDtypeStruct((n_tiles * tm, 1024), F32),
        grid_spec=grid_spec,
        compiler_params=_cparams(("arbitrary", "arbitrary")),
        name="moe_group",
    )(tile_expert, n_used, src, src, x, g, wg, wu, wd)


def _moe_combine_kernel(pos_ref, posn_ref, x_ref, r_ref, y_hbm, p_ref, g_ref, wp_ref, wgt_ref, gf_ref,
                        o_ref, bufa, bufb, sem, *, final):
    i = pl.program_id(0)
    tm = x_ref.shape[0]
    slot = i % 2

    def row_copies(idx_ref, s, t):
        pltpu.make_async_copy(y_hbm.at[pl.ds(idx_ref[0, 0, 2 * t], 1), :], bufa.at[s, pl.ds(t, 1), :],
                              sem.at[0, s]).start()
        pltpu.make_async_copy(y_hbm.at[pl.ds(idx_ref[0, 0, 2 * t + 1], 1), :], bufb.at[s, pl.ds(t, 1), :],
                              sem.at[1, s]).start()

    def gather(idx_ref, s):
        def body(t, c):
            row_copies(idx_ref, s, t)
            return c
        lax.fori_loop(0, tm, body, 0, unroll=GATHER_UNROLL)

    @pl.when(i == 0)
    def _():
        gather(pos_ref, 0)

    pltpu.make_async_copy(y_hbm.at[pl.ds(0, tm), :], bufa.at[slot], sem.at[0, slot]).wait()
    pltpu.make_async_copy(y_hbm.at[pl.ds(0, tm), :], bufb.at[slot], sem.at[1, slot]).wait()

    for t in range(tm):
        row_copies(posn_ref, 1 - slot, t)
    r = r_ref[...]
    x = x_ref[...] + r[:, 2:3] * bufa[slot] + r[:, 3:4] * bufb[slot]
    o_ref[...] = _ple_math(x, p_ref[...], g_ref[...], wp_ref[...], wgt_ref[...], gf_ref[...], final)

    @pl.when(i == pl.num_programs(0) - 1)
    def _():
        pltpu.make_async_copy(y_hbm.at[pl.ds(0, tm), :], bufa.at[1 - slot], sem.at[0, 1 - slot]).wait()
        pltpu.make_async_copy(y_hbm.at[pl.ds(0, tm), :], bufb.at[1 - slot], sem.at[1, 1 - slot]).wait()


def _moe_combine(pos, x, route, y, p, g, wp, wgt, gf, final):
    rows = x.shape[0]
    tm = MOE_ROWS
    n = rows // tm
    pos3 = pos.reshape(n, 1, 2 * tm)
    vec = pl.BlockSpec((1, 1024), lambda i: (0, 0))
    return pl.pallas_call(
        functools.partial(_moe_combine_kernel, final=final),
        out_shape=jax.ShapeDtypeStruct((rows, 1024), F32),
        grid=(n,),
        in_specs=[pl.BlockSpec((1, 1, 2 * tm), lambda i: (i, 0, 0), memory_space=pltpu.SMEM),
                  pl.BlockSpec((1, 1, 2 * tm), lambda i: (jnp.minimum(i + 1, n - 1), 0, 0), memory_space=pltpu.SMEM),
                  pl.BlockSpec((tm, 1024), lambda i: (i, 0)), pl.BlockSpec((tm, LANES), lambda i: (i, 0)),
                  pl.BlockSpec(memory_space=pl.ANY),
                  pl.BlockSpec((tm, PLE_DIM), lambda i: (i, 0)), vec,
                  pl.BlockSpec((PLE_DIM, 1024), lambda i: (0, 0)), pl.BlockSpec((1024, 1024), lambda i: (0, 0)), vec],
        out_specs=pl.BlockSpec((tm, 1024), lambda i: (i, 0)),
        scratch_shapes=[pltpu.VMEM((2, tm, 1024), F32), pltpu.VMEM((2, tm, 1024), F32),
                        pltpu.SemaphoreType.DMA((2, 2))],
        compiler_params=_cparams(("arbitrary",)),
        name="moe_combine",
    )(pos3, pos3, x, route, y, p, g, wp, wgt, gf)


def _moe_routed_ple(x, route, g, wg, wu, wd, p, g_ple, wp, wgt, gf, final):
    pos, src, tile_expert, n_used = _route_plan(route, MOE_ROWS)
    y = _moe_group(tile_expert, n_used, src, x, g, wg, wu, wd)
    return _moe_combine(pos, x, route, y, p, g_ple, wp, wgt, gf, final)


def _ple_math(x, p, g, wp, wgt, gf, final):
    emb = jnp.dot(p.astype(BF16), wp, preferred_element_type=F32)
    gate = _sigmoid(jnp.dot(_rms(x, g).astype(BF16), wgt, preferred_element_type=F32))
    y = x + emb * gate
    return _rms(y, gf) if final else y


def _ple_kernel(x_ref, p_ref, g_ref, wp_ref, wgt_ref, gf_ref, o_ref, *, final):
    o_ref[...] = _ple_math(x_ref[...], p_ref[...], g_ref[...], wp_ref[...], wgt_ref[...], gf_ref[...], final)


def _ple(x, p, g, wp, wgt, gf, final):
    rows = x.shape[0]
    tm = min(rows, 512)
    vec = pl.BlockSpec((1, 1024), lambda i: (0, 0))
    return pl.pallas_call(
        functools.partial(_ple_kernel, final=final),
        out_shape=jax.ShapeDtypeStruct((rows, 1024), F32),
        grid=(rows // tm,),
        in_specs=[pl.BlockSpec((tm, 1024), lambda i: (i, 0)), pl.BlockSpec((tm, PLE_DIM), lambda i: (i, 0)), vec,
                  pl.BlockSpec((PLE_DIM, 1024), lambda i: (0, 0)), pl.BlockSpec((1024, 1024), lambda i: (0, 0)), vec],
        out_specs=pl.BlockSpec((tm, 1024), lambda i: (i, 0)),
        compiler_params=_cparams(("parallel",)),
        name="ple",
    )(x, p, g, wp, wgt, gf)


REPACK_ROWS = 128


def _repack_w_in_kernel(w_ref, o_ref):
    offs = [0]
    for wd in IN_WIDTHS:
        offs.append(offs[-1] + wd)
    z, xbc, dt, qa, ka, va, qr, kr, vr, gr, gates = [(offs[i], offs[i + 1]) for i in range(len(IN_WIDTHS))]
    take = lambda lo, hi: w_ref[:, lo:hi].astype(BF16)
    q = take(*qa)
    heads = lambda h: q[:, h * ATT_HEAD_DIM:(h + 1) * ATT_HEAD_DIM]
    q_rmajor = jnp.concatenate([heads(ATT_REP * g + r) for r in range(ATT_REP) for g in range(ATT_KV_HEADS)], axis=1)
    parts = [take(*xbc), take(*z), q_rmajor] + [take(*p) for p in (qr, kr, vr, gr, gates, ka, va)]
    used = sum(p.shape[1] for p in parts) + (dt[1] - dt[0])
    parts.append(jnp.concatenate([take(*dt), jnp.zeros((o_ref.shape[0], N_PROJ - used), BF16)], axis=1))
    dst = 0
    for part in parts:
        o_ref[:, dst:dst + part.shape[1]] = part
        dst += part.shape[1]


def _repack_w_in(w_all, layer):
    rows = w_all.shape[1]
    return pl.pallas_call(
        _repack_w_in_kernel,
        out_shape=jax.ShapeDtypeStruct((rows, N_PROJ), BF16),
        grid=(rows // REPACK_ROWS,),
        in_specs=[pl.BlockSpec((None, REPACK_ROWS, w_all.shape[2]), lambda i: (layer, i, 0))],
        out_specs=pl.BlockSpec((REPACK_ROWS, N_PROJ), lambda i: (i, 0)),
        compiler_params=_cparams(("parallel",)),
        name="repack_w_in",
    )(w_all)


def _att_tables(pos):
    half = ROPE_DIM // 2
    inv = jnp.exp(-math.log(ROPE_THETA) * jnp.arange(half, dtype=F32) * (2.0 / ROPE_DIM))
    ang = pos.astype(F32)[:, None] * inv[None, :]
    cos, sin = jnp.cos(ang), jnp.sin(ang)
    n = pos.shape[0]
    one = jnp.ones((n, ATT_HEAD_DIM - ROPE_DIM), F32)
    zero8 = jnp.zeros((n, half), F32)
    zero = jnp.zeros((n, ATT_HEAD_DIM - ROPE_DIM), F32)
    c = jnp.concatenate([cos, cos, one], axis=1)
    s1 = jnp.concatenate([zero8, sin, zero], axis=1)
    s2 = jnp.concatenate([-sin, zero8, zero], axis=1)
    return tuple(jnp.concatenate([t, t], axis=1) for t in (c, s1, s2))


def _ret_tables(pos):
    half = RET_DK // 2
    inv = jnp.exp(-math.log(RET_THETA) * jnp.arange(half, dtype=F32) * (2.0 / RET_DK))
    ang = pos.astype(F32)[:, None] * inv[None, :]
    cos, sin = jnp.cos(ang), jnp.sin(ang)
    return jnp.concatenate([cos, cos], axis=1), jnp.concatenate([-sin, sin], axis=1)


def _rope_step_tables(inv_lane, seq):
    per = MIX_ROWS // CHUNK
    ang_i = jnp.arange(CHUNK, dtype=F32)[:, None] * inv_lane[None, :]
    within = jnp.concatenate([jnp.cos(ang_i), jnp.sin(ang_i)], axis=1)
    ang_b = (jnp.arange(seq // CHUNK, dtype=F32) * CHUNK)[:, None] * inv_lane[None, :]
    blk = jnp.concatenate([jnp.cos(ang_b), jnp.sin(ang_b)], axis=1).reshape(seq // MIX_ROWS, per, 2 * LANES)
    steps = jnp.concatenate([blk, jnp.zeros((seq // MIX_ROWS, 8 - per, 2 * LANES), F32)], axis=1)
    return within, steps


def _att_inv_lanes():
    inv = jnp.exp(-math.log(ROPE_THETA) * jnp.arange(ROPE_DIM // 2, dtype=F32) * (2.0 / ROPE_DIM))
    return jnp.tile(inv, LANES // (ROPE_DIM // 2))


def _ret_inv_lanes():
    inv = jnp.exp(-math.log(RET_THETA) * jnp.arange(RET_DK // 2, dtype=F32) * (2.0 / RET_DK))
    return jnp.tile(inv, 2)


def _pad_lanes(v, fill=0.0):
    return jnp.concatenate([v.astype(F32), jnp.full((LANES - v.shape[0],), fill, F32)])[None, :]


def kernel(x_prompt, x_sample, state_ssm, state_conv, cache_win_k, cache_win_v, state_ret, p_prompt, p_sample,
           w_in, conv_w, conv_b, dt_bias, a_log, d_skip, ssd_norm, attn_sinks, ret_norm, w_o_ssd, w_o_att, w_o_ret,
           w_out, norm_mix, norm_ffn, norm_ple, ffn_w_gate, ffn_w_up, ffn_w_down, router_w, router_b, moe_w_gate,
           moe_w_up, moe_w_down, w_ple, w_ple_gate, norm_final):
    seq = x_prompt.shape[1]
    nb = x_sample.shape[0]
    xp = x_prompt.reshape(seq, D_MODEL)
    xs = x_sample.reshape(nb, D_MODEL)
    pos_s = PAST_LEN + jnp.arange(1)
    att_tab_p, att_tab_s = _rope_step_tables(_att_inv_lanes(), seq), _att_tables(pos_s)
    ret_tab_p, ret_tab_s = _rope_step_tables(_ret_inv_lanes(), seq), _ret_tables(pos_s)
    log_gamma = jnp.log1p(-jnp.exp2(-5.0 - jnp.arange(RET_HEADS, dtype=F32)))
    gamma = jnp.exp(log_gamma)
    row = lambda v: v.astype(F32)[None, :]

    ssm_all = state_ssm.reshape(DEPTH, nb, SSD_HEADS * SSD_HEAD_DIM, SSD_STATE)
    ret_all = state_ret.reshape(DEPTH, nb, RET_HEADS * RET_DK, 128)
    kc_all = jnp.transpose(cache_win_k, (0, 1, 3, 4, 2))
    vc_all = jnp.transpose(cache_win_v, (0, 1, 3, 4, 2))
    ssm_s = ret_s = kv_s = None
    conv_s = []

    new_p = [[], [], [], [], []]
    for i in range(DEPTH):
        w_in_i = _repack_w_in(w_in, i)
        cw, cb = conv_w[i], row(conv_b[i])
        dtb, alog = _pad_lanes(dt_bias[i]), _pad_lanes(a_log[i])
        dsk = row(jnp.repeat(d_skip[i], SSD_HEAD_DIM))
        nrm_ssd, nrm_ret = row(ssd_norm[i]), row(ret_norm[i])
        sinks = attn_sinks[i].astype(F32)
        sink32 = jnp.zeros((ATT_KV_HEADS, 8), F32).at[:, :ATT_REP].set(sinks.reshape(ATT_KV_HEADS, ATT_REP))
        sink32 = jnp.broadcast_to(sink32.reshape(32, 1), (32, LANES))
        w1 = w_o_ssd[i].astype(BF16)
        w2 = w_o_att[i].reshape(ATT_KV_HEADS, ATT_REP, ATT_HEAD_DIM, D_MODEL).transpose(1, 0, 2, 3) \
            .reshape(ATT_Q_HEADS * ATT_HEAD_DIM, D_MODEL).astype(BF16)
        w3 = w_o_ret[i].astype(BF16)
        wo = w_out[i].astype(BF16)
        g_mix, g_ffn, g_ple = row(norm_mix[i]), row(norm_ffn[i]), row(norm_ple[i])
        wp, wpg = w_ple[i].astype(BF16), w_ple_gate[i].astype(BF16)
        gf = row(norm_final)
        j = i // 2
        if i % 2 == 0:
            ffw = (ffn_w_gate[j].astype(BF16), ffn_w_up[j].astype(BF16), ffn_w_down[j].astype(BF16))
        else:
            rw = jnp.concatenate([router_w[j], jnp.zeros((D_MODEL, LANES - N_EXPERTS), F32)], axis=1)
            rw_hi = rw.astype(BF16)
            rw = jnp.stack([rw_hi, (rw - rw_hi.astype(F32)).astype(BF16)])
            ffw = (rw, _pad_lanes(router_b[j]), moe_w_gate[j].astype(BF16), moe_w_up[j].astype(BF16),
                   moe_w_down[j].astype(BF16))
        final = i == DEPTH - 1

        proj, dt = _inproj(xp, g_mix, w_in_i, BF16)
        head_rows = lambda v: jnp.broadcast_to(v.astype(F32)[:, None], (SSD_HEADS, LANES))
        y_ssd, ssm_fin, conv_fin = _ssd_prompt(proj, dt, cw, cb, head_rows(dt_bias[i]), head_rows(a_log[i]),
                                               dsk, nrm_ssd)
        o_att, wk, wv = _swa_prompt(sinks, proj, att_tab_p)
        o_ret, ret_fin = _ret_prompt(log_gamma, proj, ret_tab_p, nrm_ret)
        ple_args = (p_prompt[i].reshape(seq, PLE_DIM), g_ple, wp, wpg, gf, final)
        if i % 2 == 0:
            xp = _merge(xp, y_ssd, o_att, o_ret, proj, w1, w2, w3, wo)
            xp = _ple(_ffn(xp, g_ffn, *ffw), *ple_args)
        else:
            rw, rb, *expert_w = ffw
            xp, route = _merge(xp, y_ssd, o_att, o_ret, proj, w1, w2, w3, wo, router=(g_ffn, rw, rb))
            xp = _moe_routed_ple(xp, route, g_ffn, *expert_w, *ple_args)
        new_p[0].append(ssm_fin.reshape(1, SSD_HEADS, SSD_HEAD_DIM, SSD_STATE))
        new_p[1].append(conv_fin[None])
        new_p[2].append(wk.reshape(1, WINDOW, ATT_KV_HEADS, ATT_HEAD_DIM))
        new_p[3].append(wv.reshape(1, WINDOW, ATT_KV_HEADS, ATT_HEAD_DIM))
        new_p[4].append(ret_fin[None])

        proj, dt = _inproj(xs, g_mix, w_in_i, F32)
        cst_t = jnp.transpose(state_conv[i], (1, 0, 2))
        cnew_t, xs_conv, bc, dec_t, xdt_t = _ssd_dec_pre(proj, dt, cst_t, cw, cb, dtb, alog)
        y_ssd, ssm_s = _ssd_dec_state(i, dec_t, xdt_t, bc, proj, xs_conv, ssm_all, dsk, nrm_ssd, ssm_s)
        qa_rot, ka_t, va_t = _swa_dec_pre(proj, att_tab_s)
        o_att, *kv_s = _swa_dec(i, qa_rot, ka_t, va_t, kc_all, vc_all, sink32, kv_s)
        qrot, kt = _ret_dec_pre(proj, ret_tab_s)
        o_ret, ret_s = _ret_dec_state(i, gamma, kt, qrot, proj, ret_all, nrm_ret, ret_s)
        xs = _merge(xs, y_ssd, o_att, o_ret, proj, w1, w2, w3, wo)
        xs = _ffn(xs, g_ffn, *ffw) if i % 2 == 0 else _moe(xs, g_ffn, *ffw)
        xs = _ple(xs, p_sample[i].reshape(nb, PLE_DIM), g_ple, wp, wpg, gf, final)
        conv_s.append(jnp.transpose(cnew_t, (1, 0, 2)))

    y_prompt = xp.reshape(1, seq, D_MODEL)
    y_sample = xs.reshape(nb, 1, D_MODEL)
    outs_p = [jnp.stack(l) for l in new_p]
    outs_s = [ssm_s.reshape(state_ssm.shape), jnp.stack(conv_s), jnp.transpose(kv_s[0], (0, 1, 4, 2, 3)),
              jnp.transpose(kv_s[1], (0, 1, 4, 2, 3)), ret_s.reshape(state_ret.shape)]
    return (y_prompt, y_sample, *outs_p, *outs_s)
```

```python
import functools
import math

import jax
import jax.numpy as jnp
from jax import lax
from jax.experimental import pallas as pl
from jax.experimental.pallas import tpu as pltpu

F32 = jnp.float32
BF16 = jnp.bfloat16

D_MODEL = 1024
DEPTH = 2
PAST_LEN = 16384
SSD_HEADS = 16
SSD_HEAD_DIM = 64
SSD_GROUPS = 4
SSD_STATE = 128
SSD_CONV = 4
SSD_CONV_DIM = 2048
ATT_HEAD_DIM = 64
ATT_Q_HEADS = 16
ATT_KV_HEADS = 4
ATT_REP = ATT_Q_HEADS // ATT_KV_HEADS
WINDOW = 128
ROPE_THETA = 500000.0
ROPE_DIM = 16
RET_HEADS = 8
RET_DK = 128
RET_THETA = 10000.0
CHUNK = 128
D_FF = 2816
N_EXPERTS = 8
PLE_DIM = 256
EPS = 1e-6

IN_WIDTHS = (1024, 2048, 16, 1024, 256, 256, 1024, 1024, 1024, 1024, 3072)
N_PROJ = 12288
COL_XBC = 0
COL_Z = 2
COL_QA = 3
COL_QR = 4
COL_KR = 5
COL_VR = 6
COL_GR = 7
COL_GATE = 8
COL_KA = 44
COL_VA = 45
COL_DT = 92

LANES = 128
VMEM_LIMIT = 48 * 1024 * 1024
VMEM_LIMIT_FFN = 56 * 1024 * 1024
MIX_ROWS = 4 * CHUNK


def _cparams(sem, vmem=VMEM_LIMIT):
    return pltpu.CompilerParams(dimension_semantics=sem, vmem_limit_bytes=vmem)


def _bdot(a, b):
    return jnp.dot(a.astype(BF16), b.astype(BF16), preferred_element_type=F32)


def _bdot_nt(a, b):
    return lax.dot_general(a.astype(BF16), b.astype(BF16), (((1,), (1,)), ((), ())),
                           preferred_element_type=F32)


def _split3(x):
    x0 = x.astype(BF16)
    r1 = x - x0.astype(F32)
    x1 = r1.astype(BF16)
    x2 = (r1 - x1.astype(F32)).astype(BF16)
    return x0, x1, x2


def _dot_exact_lhs01(m01, x):
    m = m01.astype(BF16)
    x0, x1, x2 = _split3(x)
    d = lambda b: jnp.dot(m, b, preferred_element_type=F32)
    return d(x0) + d(x1) + d(x2)


def _dot_exact_rhs01(x, m01):
    m = m01.astype(BF16)
    x0, x1, x2 = _split3(x)
    d = lambda a: jnp.dot(a, m, preferred_element_type=F32)
    return d(x0) + d(x1) + d(x2)


def _rms(x, g):
    return x * lax.rsqrt(jnp.mean(x * x, axis=-1, keepdims=True) + EPS) * g


def _sigmoid(x):
    return 1.0 / (1.0 + jnp.exp(-x))


def _silu(x):
    return x * _sigmoid(x)


def _softplus(x):
    return jnp.maximum(x, 0.0) + jnp.log1p(jnp.exp(-jnp.abs(x)))


def _rope_att(x, c, s1, s2):
    w = x.shape[1]
    return x * c + pltpu.roll(x, 8, axis=1) * s1 + pltpu.roll(x, w - 8, axis=1) * s2


def _tile_lanes(t, n):
    return jnp.concatenate([t] * n, axis=1) if n > 1 else t


INPROJ_TN = 1024
DT_TILE = (COL_DT * LANES) // INPROJ_TN
DT_OFF = COL_DT * LANES - DT_TILE * INPROJ_TN


def _inproj_kernel(x_ref, g_ref, w_ref, o_ref, dt_ref, h_ref):
    j = pl.program_id(1)

    @pl.when(j == 0)
    def _():
        h_ref[...] = _rms(x_ref[...], g_ref[...]).astype(BF16)

    acc = jnp.dot(h_ref[...], w_ref[...], preferred_element_type=F32)
    o_ref[...] = acc.astype(o_ref.dtype)

    @pl.when(j == DT_TILE)
    def _():
        dt_ref[...] = acc[:, DT_OFF:DT_OFF + LANES]


def _inproj(x, g, w, out_dtype):
    rows = x.shape[0]
    tm = min(rows, 2048)
    tn = INPROJ_TN
    return pl.pallas_call(
        _inproj_kernel,
        out_shape=(jax.ShapeDtypeStruct((rows, N_PROJ), out_dtype), jax.ShapeDtypeStruct((rows, LANES), F32)),
        grid=(rows // tm, N_PROJ // tn),
        in_specs=[pl.BlockSpec((tm, D_MODEL), lambda i, j: (i, 0)),
                  pl.BlockSpec((1, D_MODEL), lambda i, j: (0, 0)),
                  pl.BlockSpec((D_MODEL, tn), lambda i, j: (0, j))],
        out_specs=(pl.BlockSpec((tm, tn), lambda i, j: (i, j)), pl.BlockSpec((tm, LANES), lambda i, j: (i, 0))),
        scratch_shapes=[pltpu.VMEM((tm, D_MODEL), BF16)],
        compiler_params=_cparams(("parallel", "arbitrary")),
        name="inproj",
    )(x, g, w)


def _chunk_rows(ci):
    return pl.ds(pl.multiple_of(ci * CHUNK, CHUNK), CHUNK)


def _ssd_prompt_chunk(rows, xbc_ref, z_ref, dt_ref, cw_ref, cb_ref, dtb_ref, alog_ref, dsk_ref, nrm_ref,
                      y_ref, xpad_ref, s_ref):
    xbc = xbc_ref[rows, :].astype(F32)
    xpad_ref[8:8 + CHUNK, :] = xbc
    cw = cw_ref[...]
    acc = (xbc * cw[3:4, :] + xpad_ref[7:7 + CHUNK, :] * cw[2:3, :]
           + xpad_ref[6:6 + CHUNK, :] * cw[1:2, :] + xpad_ref[5:5 + CHUNK, :] * cw[0:1, :] + cb_ref[...])
    conv = _silu(acc)
    xpad_ref[0:8, :] = xbc[CHUNK - 8:CHUNK, :]

    xs = conv[:, :1024]
    dt_t = _softplus(dt_ref[rows, :].T[:SSD_HEADS, :] + dtb_ref[...])
    la_t = dt_t * (-jnp.exp(alog_ref[...]))
    row = lax.broadcasted_iota(jnp.int32, (CHUNK, CHUNK), 0)
    col = lax.broadcasted_iota(jnp.int32, (CHUNK, CHUNK), 1)
    causal = row >= col
    cum_t = _dot_exact_rhs01(la_t, (row <= col).astype(F32))
    cum = jnp.concatenate([cum_t, jnp.zeros((LANES - SSD_HEADS, CHUNK), F32)], axis=0).T
    cum_last = jnp.broadcast_to(cum_t[:, CHUNK - 1:CHUNK], (SSD_HEADS, CHUNK))
    w_t = jnp.exp(cum_last - cum_t) * dt_t
    dec_end = jnp.exp(cum_last)

    xs_t = xs.T
    ys = []
    for g in range(SSD_GROUPS):
        bg = conv[:, 1024 + 128 * g:1024 + 128 * (g + 1)]
        cg = conv[:, 1536 + 128 * g:1536 + 128 * (g + 1)]
        cb = _bdot_nt(cg, bg)
        s_g = s_ref[256 * g:256 * (g + 1), :]
        cs = _bdot_nt(cg, s_g)
        xw_parts = []
        dec_parts = []
        for r in range(4):
            h = 4 * g + r
            colb = jnp.broadcast_to(cum[:, h:h + 1], (CHUNK, CHUNK))
            rowb = jnp.broadcast_to(cum_t[h:h + 1, :], (CHUNK, CHUNK))
            dec = jnp.exp(jnp.where(causal, colb - rowb, -jnp.inf))
            m = cb * dec * jnp.broadcast_to(dt_t[h:h + 1, :], (CHUNK, CHUNK))
            xh = xs[:, 64 * h:64 * (h + 1)]
            yh = _bdot(m, xh) + cs[:, 64 * r:64 * (r + 1)] * jnp.exp(colb)[:, :64]
            ys.append(yh)
            xw_parts.append(xs_t[64 * h:64 * (h + 1), :] * jnp.broadcast_to(w_t[h:h + 1, :], (64, CHUNK)))
            dec_parts.append(jnp.broadcast_to(dec_end[h:h + 1, :], (64, SSD_STATE)))
        xw = jnp.concatenate(xw_parts, axis=0)
        s_ref[256 * g:256 * (g + 1), :] = s_g * jnp.concatenate(dec_parts, axis=0) + _bdot(xw, bg)

    y = jnp.concatenate(ys, axis=1) + dsk_ref[...] * xs
    y = y * _silu(z_ref[rows, :].astype(F32))
    y_ref[rows, :] = _rms(y, nrm_ref[...]).astype(y_ref.dtype)


def _ssd_prompt_kernel(xbc_ref, z_ref, dt_ref, cw_ref, cb_ref, dtb_ref, alog_ref, dsk_ref, nrm_ref,
                       y_ref, sfin_ref, cfin_ref, xpad_ref, s_ref):
    t = pl.program_id(0)

    @pl.when(t == 0)
    def _():
        xpad_ref[0:8, :] = jnp.zeros((8, SSD_CONV_DIM), F32)
        s_ref[...] = jnp.zeros_like(s_ref)

    def chunk(ci, carry):
        _ssd_prompt_chunk(_chunk_rows(ci), xbc_ref, z_ref, dt_ref, cw_ref, cb_ref, dtb_ref, alog_ref, dsk_ref,
                          nrm_ref, y_ref, xpad_ref, s_ref)
        return carry

    lax.fori_loop(0, xbc_ref.shape[0] // CHUNK, chunk, 0)

    @pl.when(t == pl.num_programs(0) - 1)
    def _():
        sfin_ref[...] = s_ref[...]
        cfin_ref[...] = xpad_ref[8 + CHUNK - (SSD_CONV - 1):8 + CHUNK, :]


def _ssd_prompt(proj, dt, cw, cb, dtb, alog, dsk, nrm):
    seq = proj.shape[0]
    const = lambda shape: pl.BlockSpec(shape, lambda t: (0,) * len(shape))
    return pl.pallas_call(
        _ssd_prompt_kernel,
        out_shape=(jax.ShapeDtypeStruct((seq, 1024), BF16),
                   jax.ShapeDtypeStruct((1024, SSD_STATE), F32),
                   jax.ShapeDtypeStruct((SSD_CONV - 1, SSD_CONV_DIM), F32)),
        grid=(seq // MIX_ROWS,),
        in_specs=[pl.BlockSpec((MIX_ROWS, 2048), lambda t: (t, COL_XBC)),
                  pl.BlockSpec((MIX_ROWS, 1024), lambda t: (t, COL_Z)),
                  pl.BlockSpec((MIX_ROWS, LANES), lambda t: (t, 0)),
                  const((SSD_CONV, 2048)), const((1, 2048)), const((SSD_HEADS, LANES)), const((SSD_HEADS, LANES)),
                  const((1, 1024)), const((1, 1024))],
        out_specs=(pl.BlockSpec((MIX_ROWS, 1024), lambda t: (t, 0)),
                   const((1024, SSD_STATE)), const((SSD_CONV - 1, SSD_CONV_DIM))),
        scratch_shapes=[pltpu.VMEM((8 + CHUNK, SSD_CONV_DIM), F32), pltpu.VMEM((1024, SSD_STATE), F32)],
        compiler_params=_cparams(("arbitrary",)),
        name="ssd_prompt",
    )(proj, proj, dt, cw, cb, dtb, alog, dsk, nrm)


def _ssd_dec_pre_kernel(xbc_ref, dt_ref, cst_ref, cw_ref, cb_ref, dtb_ref, alog_ref,
                        cnew_ref, xs_ref, bc_ref, dec_t_ref, xdt_t_ref):
    xbc = xbc_ref[...]
    cw = cw_ref[...]
    acc = (cst_ref[0] * cw[0:1, :] + cst_ref[1] * cw[1:2, :] + cst_ref[2] * cw[2:3, :]
           + xbc * cw[3:4, :] + cb_ref[...])
    conv = _silu(acc)
    cnew_ref[0] = cst_ref[1]
    cnew_ref[1] = cst_ref[2]
    cnew_ref[2] = xbc
    xs = conv[:, :1024]
    xs_ref[...] = xs
    bc_ref[...] = conv[:, 1024:]
    dt = _softplus(dt_ref[...] + dtb_ref[...])
    dec = jnp.exp(dt * (-jnp.exp(alog_ref[...])))
    hrow = lax.broadcasted_iota(jnp.int32, (LANES, 1024), 0)
    hcol = lax.broadcasted_iota(jnp.int32, (LANES, 1024), 1)
    expand = ((hcol >> 6) == hrow).astype(F32)
    dec_t_ref[...] = _dot_exact_rhs01(dec, expand).T
    xdt_t_ref[...] = (xs * _dot_exact_rhs01(dt, expand)).T


def _ssd_dec_pre(proj, dt, cst_t, cw, cb, dtb, alog):
    nb = proj.shape[0]
    const = lambda shape: pl.BlockSpec(shape, lambda t: (0,) * len(shape))
    return pl.pallas_call(
        _ssd_dec_pre_kernel,
        out_shape=(jax.ShapeDtypeStruct((3, nb, 2048), F32), jax.ShapeDtypeStruct((nb, 1024), F32),
                   jax.ShapeDtypeStruct((nb, 1024), F32), jax.ShapeDtypeStruct((1024, nb), F32),
                   jax.ShapeDtypeStruct((1024, nb), F32)),
        grid=(1,),
        in_specs=[pl.BlockSpec((nb, 2048), lambda t: (0, COL_XBC)),
                  pl.BlockSpec((nb, LANES), lambda t: (0, 0)),
                  const((3, nb, 2048)), const((SSD_CONV, 2048)), const((1, 2048)), const((1, 128)), const((1, 128))],
        out_specs=(const((3, nb, 2048)), const((nb, 1024)), const((nb, 1024)), const((1024, nb)), const((1024, nb))),
        compiler_params=_cparams(("arbitrary",)),
        name="ssd_dec_pre",
    )(proj, dt, cst_t, cw, cb, dtb, alog)


DEC_BLOCK = 8


def _ssd_dec_state_kernel(dec_t_ref, xdt_t_ref, bc_ref, z_ref, xs_ref, st_ref, dsk_ref, nrm_ref,
                          y_ref, stn_ref):
    i = pl.program_id(0)
    shift = (LANES - i * DEC_BLOCK) % LANES
    decr = pltpu.roll(dec_t_ref[...], shift, axis=1)
    xr = pltpu.roll(xdt_t_ref[...], shift, axis=1)
    bc = bc_ref[...]
    for j in range(DEC_BLOCK):
        for g in range(SSD_GROUPS):
            lo, hi = 256 * g, 256 * (g + 1)
            s_old = st_ref[j, lo:hi, :]
            dcol = jnp.broadcast_to(decr[lo:hi, j:j + 1], (256, SSD_STATE))
            xcol = jnp.broadcast_to(xr[lo:hi, j:j + 1], (256, SSD_STATE))
            s_new = s_old * dcol + xcol * bc[j:j + 1, 128 * g:128 * (g + 1)]
            stn_ref[j, lo:hi, :] = s_new
            cs = _bdot_nt(bc[:, 512 + 128 * g:512 + 128 * (g + 1)], s_new)
            y_ref[j:j + 1, lo:hi] = cs[j:j + 1, :]
    xs = xs_ref[...]
    y = y_ref[...] + dsk_ref[...] * xs
    y = y * _silu(z_ref[...])
    y_ref[...] = _rms(y, nrm_ref[...])


class _LayerCall:
    def __init__(self, layer, steps):
        self.layer, self.steps, self.first = layer, steps, layer == 0
        self.grid = (DEPTH * steps,) if self.first else (steps,)

    def _block(self, t):
        return jnp.minimum(t, self.steps - 1) if self.first else t

    def rows(self, width, col=0):
        return pl.BlockSpec((DEC_BLOCK, width), lambda t: (self._block(t), col))

    def state_in(self, tail):
        zeros = (0,) * len(tail)
        return pl.BlockSpec((None, DEC_BLOCK) + tail, lambda t: (self.layer, self._block(t)) + zeros)

    def state_out(self, tail):
        zeros = (0,) * len(tail)
        if self.first:
            return pl.BlockSpec((None, DEC_BLOCK) + tail, lambda t: (t // self.steps, t % self.steps) + zeros)
        return pl.BlockSpec((None, DEC_BLOCK) + tail, lambda t: (self.layer, t) + zeros)

    def kernel(self, body, n_in, n_carried, stacked_outs):
        def wrapped(*refs):
            refs = refs[:n_in] + refs[n_in + n_carried:]
            if not self.first:
                body(*refs)
                return
            t = pl.program_id(0)

            @pl.when(t < self.steps)
            def _():
                body(*refs)

            @pl.when(t >= self.steps)
            def _():
                for k in stacked_outs:
                    refs[n_in + k][...] = jnp.zeros_like(refs[n_in + k])
        return wrapped


def _ssd_dec_state(layer, dec_t, xdt_t, bc, proj, xs, st_all, dsk, nrm, carried):
    nb = xs.shape[0]
    lc = _LayerCall(layer, nb // DEC_BLOCK)
    const = lambda shape: pl.BlockSpec(shape, lambda t: (0,) * len(shape))
    tail = (1024, SSD_STATE)
    ins = [dec_t, xdt_t, bc, proj, xs, st_all, dsk, nrm]
    in_specs = [const((1024, nb)), const((1024, nb)), lc.rows(1024), lc.rows(1024, COL_Z), lc.rows(1024),
                lc.state_in(tail), const((1, 1024)), const((1, 1024))]
    n_in = len(ins)
    carried = [] if carried is None else [carried]
    return pl.pallas_call(
        lc.kernel(_ssd_dec_state_kernel, n_in, len(carried), (1,)),
        out_shape=(jax.ShapeDtypeStruct((nb, 1024), F32), jax.ShapeDtypeStruct(st_all.shape, F32)),
        grid=lc.grid,
        in_specs=in_specs + [pl.BlockSpec(memory_space=pl.ANY)] * len(carried),
        out_specs=(lc.rows(1024), lc.state_out(tail)),
        input_output_aliases={n_in + k: 1 + k for k in range(len(carried))},
        compiler_params=_cparams(("arbitrary",)),
        name="ssd_dec_state",
    )(*ins, *carried)


def _block_cos_sin(within_ref, step_ref, bi):
    row = step_ref[pl.ds(bi, 1), :]
    cb, sb = row[:, :LANES], row[:, LANES:]
    ci, si = within_ref[:, :LANES], within_ref[:, LANES:]
    return cb * ci - sb * si, sb * ci + cb * si


def _swa_prompt_block(bi, has_prev, sink_ref, q_ref, k_ref, v_ref, within_ref, step_ref, o_ref, kp_ref, vp_ref):
    rows = _chunk_rows(bi)
    cosp, sinp = _block_cos_sin(within_ref, step_ref, bi)
    l64 = lax.broadcasted_iota(jnp.int32, (1, LANES), 1) & (ATT_HEAD_DIM - 1)
    c = jnp.where(l64 < ROPE_DIM, cosp, 1.0)
    s1 = jnp.where(jnp.logical_and(l64 >= ROPE_DIM // 2, l64 < ROPE_DIM), sinp, 0.0)
    s2 = jnp.where(l64 < ROPE_DIM // 2, -sinp, 0.0)
    q = _rope_att(q_ref[rows, :].astype(F32), _tile_lanes(c, 8), _tile_lanes(s1, 8), _tile_lanes(s2, 8))
    k = _rope_att(k_ref[rows, :].astype(F32), _tile_lanes(c, 2), _tile_lanes(s1, 2), _tile_lanes(s2, 2))
    v = v_ref[rows, :].astype(F32)
    ghead = lax.broadcasted_iota(jnp.int32, (WINDOW, 256), 1) >> 6
    expand = lambda t: jnp.concatenate([jnp.where(ghead == g, t, 0.0) for g in range(ATT_KV_HEADS)],
                                       axis=0).astype(BF16)
    kbd, vbd = expand(k), expand(v)
    kbd_prev, vbd_prev = expand(kp_ref[...]), expand(vp_ref[...])
    qall = jnp.concatenate([q[:, 256 * r:256 * (r + 1)] for r in range(ATT_REP)], axis=0).astype(BF16)
    scale = ATT_HEAD_DIM ** -0.5
    nt_dims = (((1,), (1,)), ((), ()))
    sp_all = lax.dot_general(qall, kbd_prev, nt_dims, preferred_element_type=F32) * scale
    sc_all = lax.dot_general(qall, kbd, nt_dims, preferred_element_type=F32) * scale
    nq = ATT_REP * WINDOW
    qi = lax.broadcasted_iota(jnp.int32, (nq, WINDOW), 0) & (WINDOW - 1)
    kj = lax.broadcasted_iota(jnp.int32, (nq, WINDOW), 1)
    mask_prev = jnp.logical_and(kj > qi, has_prev)
    mask_cur = kj <= qi
    rep = lax.broadcasted_iota(jnp.int32, (nq, 1), 0) >> 7
    pp, pc = [], []
    for g in range(ATT_KV_HEADS):
        sp = jnp.where(mask_prev, sp_all[:, WINDOW * g:WINDOW * (g + 1)], -jnp.inf)
        sc = jnp.where(mask_cur, sc_all[:, WINDOW * g:WINDOW * (g + 1)], -jnp.inf)
        sink = jnp.where(rep == 0, sink_ref[4 * g],
                         jnp.where(rep == 1, sink_ref[4 * g + 1],
                                   jnp.where(rep == 2, sink_ref[4 * g + 2], sink_ref[4 * g + 3])))
        m = jnp.maximum(jnp.max(jnp.maximum(sp, sc), axis=1, keepdims=True), sink)
        ep = jnp.exp(sp - m)
        ec = jnp.exp(sc - m)
        inv = 1.0 / (jnp.sum(ep + ec, axis=1, keepdims=True) + jnp.exp(sink - m))
        pp.append((ep * inv).astype(BF16))
        pc.append((ec * inv).astype(BF16))
    o = (jnp.dot(jnp.concatenate(pp, axis=1), vbd_prev, preferred_element_type=F32)
         + jnp.dot(jnp.concatenate(pc, axis=1), vbd, preferred_element_type=F32))
    for r in range(ATT_REP):
        o_ref[rows, 256 * r:256 * (r + 1)] = o[WINDOW * r:WINDOW * (r + 1), :].astype(o_ref.dtype)
    kp_ref[...] = k
    vp_ref[...] = v


def _swa_prompt_kernel(sink_ref, q_ref, k_ref, v_ref, within_ref, step_ref,
                       o_ref, wk_ref, wv_ref, kp_ref, vp_ref):
    n = pl.program_id(0)
    blocks = q_ref.shape[0] // WINDOW

    @pl.when(n == 0)
    def _():
        kp_ref[...] = jnp.zeros_like(kp_ref)
        vp_ref[...] = jnp.zeros_like(vp_ref)

    def block(bi, carry):
        _swa_prompt_block(bi, n * blocks + bi > 0, sink_ref, q_ref, k_ref, v_ref,
                          within_ref, step_ref, o_ref, kp_ref, vp_ref)
        return carry

    lax.fori_loop(0, blocks, block, 0)

    @pl.when(n == pl.num_programs(0) - 1)
    def _():
        wk_ref[...] = kp_ref[...]
        wv_ref[...] = vp_ref[...]


def _swa_prompt(sinks, proj, tabs):
    seq = proj.shape[0]
    within, steps = tabs
    const = lambda shape: pl.BlockSpec(shape, lambda t: (0,) * len(shape))
    return pl.pallas_call(
        _swa_prompt_kernel,
        out_shape=(jax.ShapeDtypeStruct((seq, 1024), BF16),
                   jax.ShapeDtypeStruct((WINDOW, 256), F32), jax.ShapeDtypeStruct((WINDOW, 256), F32)),
        grid=(seq // MIX_ROWS,),
        in_specs=[pl.BlockSpec(memory_space=pltpu.SMEM),
                  pl.BlockSpec((MIX_ROWS, 1024), lambda t: (t, COL_QA)),
                  pl.BlockSpec((MIX_ROWS, 256), lambda t: (t, COL_KA)),
                  pl.BlockSpec((MIX_ROWS, 256), lambda t: (t, COL_VA)),
                  const((CHUNK, 2 * LANES)), pl.BlockSpec((None, 8, 2 * LANES), lambda t: (t, 0, 0))],
        out_specs=(pl.BlockSpec((MIX_ROWS, 1024), lambda t: (t, 0)), const((WINDOW, 256)), const((WINDOW, 256))),
        scratch_shapes=[pltpu.VMEM((WINDOW, 256), F32), pltpu.VMEM((WINDOW, 256), F32)],
        compiler_params=_cparams(("arbitrary",)),
        name="swa_prompt",
    )(sinks, proj, proj, proj, within, steps)


def _swa_dec_pre_kernel(q_ref, k_ref, v_ref, c_ref, s1_ref, s2_ref, qrot_ref, kt_ref, vt_ref):
    c, s1, s2 = c_ref[...], s1_ref[...], s2_ref[...]
    qrot_ref[...] = _rope_att(q_ref[...], _tile_lanes(c, 8), _tile_lanes(s1, 8), _tile_lanes(s2, 8))
    kt_ref[...] = _rope_att(k_ref[...], _tile_lanes(c, 2), _tile_lanes(s1, 2), _tile_lanes(s2, 2)).T
    vt_ref[...] = v_ref[...].T


def _swa_dec_pre(proj, tabs):
    nb = proj.shape[0]
    c, s1, s2 = tabs
    const = lambda shape: pl.BlockSpec(shape, lambda t: (0,) * len(shape))
    return pl.pallas_call(
        _swa_dec_pre_kernel,
        out_shape=(jax.ShapeDtypeStruct((nb, 1024), F32), jax.ShapeDtypeStruct((256, nb), F32),
                   jax.ShapeDtypeStruct((256, nb), F32)),
        grid=(1,),
        in_specs=[pl.BlockSpec((nb, 1024), lambda t: (0, COL_QA)), pl.BlockSpec((nb, 256), lambda t: (0, COL_KA)),
                  pl.BlockSpec((nb, 256), lambda t: (0, COL_VA)), const((1, LANES)), const((1, LANES)), const((1, LANES))],
        out_specs=(const((nb, 1024)), const((256, nb)), const((256, nb))),
        compiler_params=_cparams(("arbitrary",)),
        name="swa_dec_pre",
    )(proj, proj, proj, c, s1, s2)


def _swa_dec_kernel(q_ref, kt_ref, vt_ref, kc_ref, vc_ref, sink_ref, o_ref, kcn_ref, vcn_ref):
    i = pl.program_id(0)
    shift = (LANES - i * DEC_BLOCK) % LANES
    ktr = pltpu.roll(kt_ref[...], shift, axis=1)
    vtr = pltpu.roll(vt_ref[...], shift, axis=1)
    q = q_ref[...]
    lane = lax.broadcasted_iota(jnp.int32, (ATT_HEAD_DIM, WINDOW), 1)
    rowid = lax.broadcasted_iota(jnp.int32, (8, ATT_HEAD_DIM), 0)
    scale = ATT_HEAD_DIM ** -0.5
    append = lambda old, col: jnp.where(lane == WINDOW - 1, jnp.broadcast_to(col, (ATT_HEAD_DIM, WINDOW)),
                                        pltpu.roll(old, WINDOW - 1, axis=1))
    pairs = [(j, g) for j in range(DEC_BLOCK) for g in range(ATT_KV_HEADS)]
    scores, values = [], []
    for j, g in pairs:
        lo, hi = ATT_HEAD_DIM * g, ATT_HEAD_DIM * (g + 1)
        kt = append(kc_ref[j, g], ktr[lo:hi, j:j + 1])
        vt = append(vc_ref[j, g], vtr[lo:hi, j:j + 1])
        kcn_ref[j, g] = kt
        vcn_ref[j, g] = vt
        qg = jnp.zeros((8, ATT_HEAD_DIM), F32)
        for r in range(ATT_REP):
            src = 256 * r + lo
            qg = jnp.where(rowid == r, jnp.broadcast_to(q[j:j + 1, src:src + ATT_HEAD_DIM], (8, ATT_HEAD_DIM)), qg)
        scores.append(_bdot(qg, kt))
        values.append(vt.astype(BF16))
    s = jnp.concatenate(scores, axis=0) * scale
    sink = jnp.concatenate([sink_ref[...]] * DEC_BLOCK, axis=0)[:, 0:1]
    m = jnp.maximum(jnp.max(s, axis=1, keepdims=True), sink)
    e = jnp.exp(s - m)
    p = e * (1.0 / (jnp.sum(e, axis=1, keepdims=True) + jnp.exp(sink - m)))
    for idx, (j, g) in enumerate(pairs):
        o = lax.dot_general(p[8 * idx:8 * (idx + 1), :].astype(BF16), values[idx], (((1,), (1,)), ((), ())),
                            preferred_element_type=F32)
        for r in range(ATT_REP):
            dst = 256 * r + ATT_HEAD_DIM * g
            o_ref[j:j + 1, dst:dst + ATT_HEAD_DIM] = o[r:r + 1, :]


def _swa_dec(layer, qrot, kt, vt, kc_all, vc_all, sink32, carried):
    nb = qrot.shape[0]
    lc = _LayerCall(layer, nb // DEC_BLOCK)
    const = lambda shape: pl.BlockSpec(shape, lambda t: (0,) * len(shape))
    tail = (ATT_KV_HEADS, ATT_HEAD_DIM, WINDOW)
    ins = [qrot, kt, vt, kc_all, vc_all, sink32]
    in_specs = [lc.rows(1024), const((256, nb)), const((256, nb)), lc.state_in(tail), lc.state_in(tail),
                const((32, LANES))]
    n_in = len(ins)
    carried = [] if carried is None else list(carried)
    return pl.pallas_call(
        lc.kernel(_swa_dec_kernel, n_in, len(carried), (1, 2)),
        out_shape=(jax.ShapeDtypeStruct((nb, 1024), F32),
                   jax.ShapeDtypeStruct(kc_all.shape, F32), jax.ShapeDtypeStruct(vc_all.shape, F32)),
        grid=lc.grid,
        in_specs=in_specs + [pl.BlockSpec(memory_space=pl.ANY)] * len(carried),
        out_specs=(lc.rows(1024), lc.state_out(tail), lc.state_out(tail)),
        input_output_aliases={n_in + k: 1 + k for k in range(len(carried))},
        compiler_params=_cparams(("arbitrary",)),
        name="swa_dec",
    )(*ins, *carried)


def _ret_prompt_chunk(ci, lg_ref, q_ref, k_ref, v_ref, gr_ref, within_ref, step_ref, nrm_ref,
                      o_ref, st_ref, intra_ref, fs_ref, te_ref):
    rows = _chunk_rows(ci)
    c, sinp = _block_cos_sin(within_ref, step_ref, ci)
    s = jnp.where(lax.broadcasted_iota(jnp.int32, (1, LANES), 1) < RET_DK // 2, -sinp, sinp)
    q = q_ref[rows, :].astype(F32)
    k = k_ref[rows, :].astype(F32)
    v = v_ref[rows, :].astype(F32)
    gr = gr_ref[rows, :].astype(F32)
    nrm = nrm_ref[...]
    for h in range(RET_HEADS):
        sl = slice(128 * h, 128 * (h + 1))
        qh = q[:, sl]
        kh = k[:, sl]
        qh = qh * c + pltpu.roll(qh, 64, axis=1) * s
        kh = (kh * c + pltpu.roll(kh, 64, axis=1) * s) * (RET_DK ** -0.5)
        vh = v[:, sl]
        att = _bdot_nt(qh, kh) * intra_ref[h]
        s_old = st_ref[h]
        o = _bdot(att, vh) + _bdot(qh, s_old) * fs_ref[h]
        cd = jnp.exp(jnp.zeros((1, RET_DK), F32) + CHUNK * lg_ref[h])
        st_ref[h] = s_old * cd + _bdot((kh * te_ref[h]).T, vh)
        o = o * lax.rsqrt(jnp.mean(o * o, axis=-1, keepdims=True) + EPS)
        o_ref[rows, sl] = (o * nrm[:, sl] * _silu(gr[:, sl])).astype(o_ref.dtype)


def _ret_prompt_kernel(lg_ref, q_ref, k_ref, v_ref, gr_ref, within_ref, step_ref, nrm_ref,
                       o_ref, sfin_ref, st_ref, intra_ref, fs_ref, te_ref):
    t = pl.program_id(0)

    @pl.when(t == 0)
    def _():
        st_ref[...] = jnp.zeros_like(st_ref)
        ri = lax.broadcasted_iota(jnp.int32, (CHUNK, CHUNK), 0).astype(F32)
        ci = lax.broadcasted_iota(jnp.int32, (CHUNK, CHUNK), 1).astype(F32)
        rel = ri - ci
        for h in range(RET_HEADS):
            lg = lg_ref[h]
            intra_ref[h] = jnp.exp(jnp.where(rel >= 0, rel * lg, -jnp.inf))
            fs_ref[h] = jnp.exp((ri + 1.0) * lg)
            te_ref[h] = jnp.exp((CHUNK - 1.0 - ri) * lg)

    def chunk(ci, carry):
        _ret_prompt_chunk(ci, lg_ref, q_ref, k_ref, v_ref, gr_ref, within_ref, step_ref, nrm_ref,
                          o_ref, st_ref, intra_ref, fs_ref, te_ref)
        return carry

    lax.fori_loop(0, q_ref.shape[0] // CHUNK, chunk, 0)

    @pl.when(t == pl.num_programs(0) - 1)
    def _():
        sfin_ref[...] = st_ref[...]


def _ret_prompt(log_gamma, proj, tabs, nrm):
    seq = proj.shape[0]
    within, steps = tabs
    const = lambda shape: pl.BlockSpec(shape, lambda t: (0,) * len(shape))
    col = lambda cidx: pl.BlockSpec((MIX_ROWS, 1024), lambda t: (t, cidx))
    tbl = pltpu.VMEM((RET_HEADS, CHUNK, CHUNK), F32)
    return pl.pallas_call(
        _ret_prompt_kernel,
        out_shape=(jax.ShapeDtypeStruct((seq, 1024), BF16), jax.ShapeDtypeStruct((RET_HEADS, RET_DK, 128), F32)),
        grid=(seq // MIX_ROWS,),
        in_specs=[pl.BlockSpec(memory_space=pltpu.SMEM), col(COL_QR), col(COL_KR), col(COL_VR), col(COL_GR),
                  const((CHUNK, 2 * LANES)), pl.BlockSpec((None, 8, 2 * LANES), lambda t: (t, 0, 0)), const((1, 1024))],
        out_specs=(pl.BlockSpec((MIX_ROWS, 1024), lambda t: (t, 0)), const((RET_HEADS, RET_DK, 128))),
        scratch_shapes=[tbl, tbl, tbl, tbl],
        compiler_params=_cparams(("arbitrary",)),
        name="ret_prompt",
    )(log_gamma, proj, proj, proj, proj, within, steps, nrm)


def _ret_dec_pre_kernel(q_ref, k_ref, c_ref, s_ref, qrot_ref, kt_ref):
    c = c_ref[...]
    s = s_ref[...]
    q = q_ref[...]
    k = k_ref[...]
    ks = []
    for h in range(RET_HEADS):
        sl = slice(128 * h, 128 * (h + 1))
        qh = q[:, sl]
        kh = k[:, sl]
        qrot_ref[:, sl] = qh * c + pltpu.roll(qh, 64, axis=1) * s
        ks.append((kh * c + pltpu.roll(kh, 64, axis=1) * s) * (RET_DK ** -0.5))
    kt_ref[...] = jnp.concatenate(ks, axis=1).T


def _ret_dec_pre(proj, tabs):
    nb = proj.shape[0]
    c, s = tabs
    const = lambda shape: pl.BlockSpec(shape, lambda t: (0,) * len(shape))
    return pl.pallas_call(
        _ret_dec_pre_kernel,
        out_shape=(jax.ShapeDtypeStruct((nb, 1024), F32), jax.ShapeDtypeStruct((1024, nb), F32)),
        grid=(1,),
        in_specs=[pl.BlockSpec((nb, 1024), lambda t: (0, COL_QR)), pl.BlockSpec((nb, 1024), lambda t: (0, COL_KR)),
                  const((1, LANES)), const((1, LANES))],
        out_specs=(const((nb, 1024)), const((1024, nb))),
        compiler_params=_cparams(("arbitrary",)),
        name="ret_dec_pre",
    )(proj, proj, c, s)


def _ret_dec_state_kernel(gam_ref, kt_ref, q_ref, v_ref, gr_ref, st_ref, nrm_ref, o_ref, stn_ref):
    i = pl.program_id(0)
    shift = (LANES - i * DEC_BLOCK) % LANES
    kr = pltpu.roll(kt_ref[...], shift, axis=1)
    q = q_ref[...]
    v = v_ref[...]
    for j in range(DEC_BLOCK):
        for h in range(RET_HEADS):
            lo, hi = 128 * h, 128 * (h + 1)
            kcol = jnp.broadcast_to(kr[lo:hi, j:j + 1], (RET_DK, 128))
            s_new = st_ref[j, lo:hi, :] * gam_ref[h] + kcol * v[j:j + 1, lo:hi]
            stn_ref[j, lo:hi, :] = s_new
            qs = _bdot(q[:, lo:hi], s_new)
            o_ref[j:j + 1, lo:hi] = qs[j:j + 1, :]
    gr = gr_ref[...]
    nrm = nrm_ref[...]
    for h in range(RET_HEADS):
        sl = slice(128 * h, 128 * (h + 1))
        o = o_ref[:, sl]
        o = o * lax.rsqrt(jnp.mean(o * o, axis=-1, keepdims=True) + EPS)
        o_ref[:, sl] = o * nrm[:, sl] * _silu(gr[:, sl])


def _ret_dec_state(layer, gam, kt, qrot, proj, st_all, nrm, carried):
    nb = qrot.shape[0]
    lc = _LayerCall(layer, nb // DEC_BLOCK)
    const = lambda shape: pl.BlockSpec(shape, lambda t: (0,) * len(shape))
    tail = (1024, 128)
    ins = [gam, kt, qrot, proj, proj, st_all, nrm]
    in_specs = [pl.BlockSpec(memory_space=pltpu.SMEM), const((1024, nb)), lc.rows(1024),
                lc.rows(1024, COL_VR), lc.rows(1024, COL_GR), lc.state_in(tail), const((1, 1024))]
    n_in = len(ins)
    carried = [] if carried is None else [carried]
    return pl.pallas_call(
        lc.kernel(_ret_dec_state_kernel, n_in, len(carried), (1,)),
        out_shape=(jax.ShapeDtypeStruct((nb, 1024), F32), jax.ShapeDtypeStruct(st_all.shape, F32)),
        grid=lc.grid,
        in_specs=in_specs + [pl.BlockSpec(memory_space=pl.ANY)] * len(carried),
        out_specs=(lc.rows(1024), lc.state_out(tail)),
        input_output_aliases={n_in + k: 1 + k for k in range(len(carried))},
        compiler_params=_cparams(("arbitrary",)),
        name="ret_dec_state",
    )(*ins, *carried)


def _merge_math(x_ref, a_ref, b_ref, c_ref, g1_ref, g2_ref, g3_ref, w1_ref, w2_ref, w3_ref, wo_ref):
    gate = lambda ref: _sigmoid(ref[...].astype(F32))
    m = (gate(g1_ref) * jnp.dot(a_ref[...].astype(BF16), w1_ref[...], preferred_element_type=F32)
         + gate(g2_ref) * jnp.dot(b_ref[...].astype(BF16), w2_ref[...], preferred_element_type=F32)
         + gate(g3_ref) * jnp.dot(c_ref[...].astype(BF16), w3_ref[...], preferred_element_type=F32))
    return x_ref[...] + jnp.dot(m.astype(BF16), wo_ref[...], preferred_element_type=F32)


def _merge_kernel(*refs):
    *ins, o_ref = refs
    o_ref[...] = _merge_math(*ins)


def _merge_route_kernel(*refs):
    *ins, gn_ref, rw_ref, rb_ref, o_ref, route_ref = refs
    x = _merge_math(*ins)
    o_ref[...] = x
    route_ref[...] = _route_row(_rms(x, gn_ref[...]), rw_ref[0], rw_ref[1], rb_ref[...])


def _merge(x, a, b, c, proj, w1, w2, w3, wo, router=None):
    rows = x.shape[0]
    tm = min(rows, 512)
    rowb = pl.BlockSpec((tm, 1024), lambda i: (i, 0))
    gate = lambda k: pl.BlockSpec((tm, 1024), lambda i: (i, COL_GATE + k))
    wsp = pl.BlockSpec((1024, 1024), lambda i: (0, 0))
    ins = [x, a, b, c, proj, proj, proj, w1, w2, w3, wo]
    in_specs = [rowb, rowb, rowb, rowb, gate(0), gate(1), gate(2), wsp, wsp, wsp, wsp]
    out_shape = jax.ShapeDtypeStruct((rows, 1024), F32)
    if router is None:
        kern, out_specs = _merge_kernel, rowb
    else:
        kern = _merge_route_kernel
        ins += list(router)
        in_specs += [pl.BlockSpec((1, 1024), lambda i: (0, 0)), pl.BlockSpec((2, 1024, LANES), lambda i: (0, 0, 0)),
                     pl.BlockSpec((1, LANES), lambda i: (0, 0))]
        out_shape = (out_shape, jax.ShapeDtypeStruct((rows, LANES), F32))
        out_specs = (rowb, pl.BlockSpec((tm, LANES), lambda i: (i, 0)))
    return pl.pallas_call(
        kern,
        out_shape=out_shape,
        grid=(rows // tm,),
        in_specs=in_specs,
        out_specs=out_specs,
        compiler_params=_cparams(("parallel",)),
        name="merge",
    )(*ins)


FF_TILE = 1408
FF_SPLIT = 768


def _ffn_kernel(x_ref, g_ref, wg_ref, wu_ref, wd_ref, o_ref, h_ref, acc_ref):
    j = pl.program_id(1)

    @pl.when(j == 0)
    def _():
        h_ref[...] = _rms(x_ref[...], g_ref[...]).astype(BF16)
        acc_ref[...] = jnp.zeros_like(acc_ref)

    h = h_ref[...]
    for lo, hi in ((0, FF_SPLIT), (FF_SPLIT, FF_TILE)):
        a = jnp.dot(h, wg_ref[:, lo:hi], preferred_element_type=F32)
        u = jnp.dot(h, wu_ref[:, lo:hi], preferred_element_type=F32)
        acc_ref[...] += jnp.dot((_silu(a) * u).astype(BF16), wd_ref[lo:hi, :], preferred_element_type=F32)

    @pl.when(j == pl.num_programs(1) - 1)
    def _():
        o_ref[...] = x_ref[...] + acc_ref[...]


def _ffn(x, g, wg, wu, wd):
    rows = x.shape[0]
    tm = min(rows, 1024)
    return pl.pallas_call(
        _ffn_kernel,
        out_shape=jax.ShapeDtypeStruct((rows, 1024), F32),
        grid=(rows // tm, D_FF // FF_TILE),
        in_specs=[pl.BlockSpec((tm, 1024), lambda i, j: (i, 0)), pl.BlockSpec((1, 1024), lambda i, j: (0, 0)),
                  pl.BlockSpec((1024, FF_TILE), lambda i, j: (0, j)), pl.BlockSpec((1024, FF_TILE), lambda i, j: (0, j)),
                  pl.BlockSpec((FF_TILE, 1024), lambda i, j: (j, 0))],
        out_specs=pl.BlockSpec((tm, 1024), lambda i, j: (i, 0)),
        scratch_shapes=[pltpu.VMEM((tm, 1024), BF16), pltpu.VMEM((tm, 1024), F32)],
        compiler_params=_cparams(("parallel", "arbitrary"), vmem=VMEM_LIMIT_FFN),
        name="ffn",
    )(x, g, wg, wu, wd)


MOE_FF_TILE = 256


def _top2(h, rw_hi, rw_lo, rb, lane):
    h_hi = h.astype(BF16)
    h_lo = (h - h_hi.astype(F32)).astype(BF16)
    d = lambda a, b: jnp.dot(a, b, preferred_element_type=F32)
    logits = d(h_hi, rw_hi) + d(h_hi, rw_lo) + d(h_lo, rw_hi) + rb
    logits = jnp.where(lane < N_EXPERTS, logits, -jnp.inf)
    m1 = jnp.max(logits, axis=1, keepdims=True)
    i1 = jnp.min(jnp.where(logits == m1, lane, float(LANES)), axis=1, keepdims=True)
    rest = jnp.where(lane == i1, -jnp.inf, logits)
    m2 = jnp.max(rest, axis=1, keepdims=True)
    i2 = jnp.min(jnp.where(rest == m2, lane, float(LANES)), axis=1, keepdims=True)
    e2 = jnp.exp(m2 - m1)
    p1 = 1.0 / (1.0 + e2)
    return i1, i2, p1, e2 * p1


def _moe_kernel(x_ref, g_ref, rw_ref, rb_ref, wg_ref, wu_ref, wd_ref, o_ref, h_ref, acc_ref, comb_ref):
    e = pl.program_id(1)
    j = pl.program_id(2)
    tm = x_ref.shape[0]
    lane = lax.broadcasted_iota(jnp.int32, (tm, LANES), 1).astype(F32)

    @pl.when(jnp.logical_and(e == 0, j == 0))
    def _():
        h = _rms(x_ref[...], g_ref[...])
        h_ref[...] = h.astype(BF16)
        i1, i2, p1, p2 = _top2(h, rw_ref[0], rw_ref[1], rb_ref[...], lane)
        comb_ref[...] = jnp.where(lane == i1, p1, 0.0) + jnp.where(lane == i2, p2, 0.0)
        acc_ref[...] = jnp.zeros_like(acc_ref)

    ce = jnp.sum(jnp.where(lane == e.astype(F32), comb_ref[...], 0.0), axis=1, keepdims=True)
    h = h_ref[...]
    a = jnp.dot(h, wg_ref[0], preferred_element_type=F32)
    u = jnp.dot(h, wu_ref[0], preferred_element_type=F32)
    acc_ref[...] += ce * jnp.dot((_silu(a) * u).astype(BF16), wd_ref[0], preferred_element_type=F32)

    @pl.when(jnp.logical_and(e == pl.num_programs(1) - 1, j == pl.num_programs(2) - 1))
    def _():
        o_ref[...] = x_ref[...] + acc_ref[...]


def _moe(x, g, rw, rb, wg, wu, wd):
    rows = x.shape[0]
    tm = min(rows, 1024)
    tf = MOE_FF_TILE
    return pl.pallas_call(
        _moe_kernel,
        out_shape=jax.ShapeDtypeStruct((rows, 1024), F32),
        grid=(rows // tm, N_EXPERTS, D_FF // tf),
        in_specs=[pl.BlockSpec((tm, 1024), lambda i, e, j: (i, 0)), pl.BlockSpec((1, 1024), lambda i, e, j: (0, 0)),
                  pl.BlockSpec((2, 1024, LANES), lambda i, e, j: (0, 0, 0)),
                  pl.BlockSpec((1, LANES), lambda i, e, j: (0, 0)),
                  pl.BlockSpec((1, 1024, tf), lambda i, e, j: (e, 0, j)),
                  pl.BlockSpec((1, 1024, tf), lambda i, e, j: (e, 0, j)),
                  pl.BlockSpec((1, tf, 1024), lambda i, e, j: (e, j, 0))],
        out_specs=pl.BlockSpec((tm, 1024), lambda i, e, j: (i, 0)),
        scratch_shapes=[pltpu.VMEM((tm, 1024), BF16), pltpu.VMEM((tm, 1024), F32), pltpu.VMEM((tm, LANES), F32)],
        compiler_params=_cparams(("parallel", "arbitrary", "arbitrary")),
        name="moe",
    )(x, g, rw, rb, wg, wu, wd)


MOE_ROWS = 512
MOE_GROUP_FF = 1408
GATHER_UNROLL = 8


def _route_row(h, rw_hi, rw_lo, rb):
    lane = lax.broadcasted_iota(jnp.int32, (h.shape[0], LANES), 1).astype(F32)
    i1, i2, p1, p2 = _top2(h, rw_hi, rw_lo, rb, lane)
    return jnp.where(lane == 0.0, i1, jnp.where(lane == 1.0, i2, jnp.where(lane == 2.0, p1,
                     jnp.where(lane == 3.0, p2, 0.0))))


def _route_plan(route, tm):
    n = route.shape[0]
    n_tiles = (2 * n) // tm + N_EXPERTS
    e_flat = route[:, :2].astype(jnp.int32).reshape(-1)
    onehot = (e_flat[:, None] == jnp.arange(N_EXPERTS, dtype=jnp.int32)[None, :]).astype(jnp.int32)
    csum = jnp.cumsum(onehot, axis=0)
    counts = csum[-1]
    tiles_e = (counts + tm - 1) // tm
    tile_end = jnp.cumsum(tiles_e)
    row_start = (tile_end - tiles_e) * tm
    pos = jnp.sum((csum - onehot + row_start[None, :]) * onehot, axis=1).astype(jnp.int32)
    tile_expert = jnp.minimum(jnp.sum(jnp.arange(n_tiles, dtype=jnp.int32)[:, None] >= tile_end[None, :], axis=1),
                              N_EXPERTS - 1).astype(jnp.int32)
    n_used = tile_end[-1:].astype(jnp.int32)
    src = jnp.zeros((n_tiles * tm,), jnp.int32).at[pos].set(jnp.arange(2 * n, dtype=jnp.int32) // 2)
    return pos, src.reshape(n_tiles, 1, tm), tile_expert, n_used


def _moe_group_kernel(te_ref, nu_ref, src_ref, srcn_ref, x_hbm, g_ref, wg_ref, wu_ref, wd_ref,
                      y_ref, buf, sem, h_ref, acc_ref):
    i = pl.program_id(0)
    j = pl.program_id(1)
    tm = buf.shape[1]
    slot = i % 2
    active = i < nu_ref[0]

    def row_copy(idx_ref, s, r):
        return pltpu.make_async_copy(x_hbm.at[pl.ds(idx_ref[0, 0, r], 1), :], buf.at[s, pl.ds(r, 1), :], sem.at[s])

    def gather(idx_ref, s):
        def body(r, c):
            row_copy(idx_ref, s, r).start()
            return c
        lax.fori_loop(0, tm, body, 0, unroll=GATHER_UNROLL)

    @pl.when(jnp.logical_and(active, j == 0))
    def _():
        @pl.when(i == 0)
        def _():
            gather(src_ref, 0)

        pltpu.make_async_copy(x_hbm.at[pl.ds(0, tm), :], buf.at[slot], sem.at[slot]).wait()

        @pl.when(i + 1 < nu_ref[0])
        def _():
            gather(srcn_ref, 1 - slot)

        h_ref[...] = _rms(buf[slot], g_ref[...]).astype(BF16)
        acc_ref[...] = jnp.zeros_like(acc_ref)

    @pl.when(active)
    def _():
        h = h_ref[...]
        a = jnp.dot(h, wg_ref[0], preferred_element_type=F32)
        u = jnp.dot(h, wu_ref[0], preferred_element_type=F32)
        acc_ref[...] += jnp.dot((_silu(a) * u).astype(BF16), wd_ref[0], preferred_element_type=F32)

    @pl.when(j == pl.num_programs(1) - 1)
    def _():
        @pl.when(active)
        def _():
            y_ref[...] = acc_ref[...]

        @pl.when(jnp.logical_not(active))
        def _():
            y_ref[...] = jnp.zeros_like(y_ref)


def _moe_group(tile_expert, n_used, src, x, g, wg, wu, wd):
    n_tiles, _, tm = src.shape
    tf = MOE_GROUP_FF
    grid_spec = pltpu.PrefetchScalarGridSpec(
        num_scalar_prefetch=2,
        grid=(n_tiles, D_FF // tf),
        in_specs=[pl.BlockSpec((1, 1, tm), lambda i, j, te, nu: (i, 0, 0), memory_space=pltpu.SMEM),
                  pl.BlockSpec((1, 1, tm), lambda i, j, te, nu: (jnp.minimum(i + 1, n_tiles - 1), 0, 0),
                               memory_space=pltpu.SMEM),
                  pl.BlockSpec(memory_space=pl.ANY),
                  pl.BlockSpec((1, 1024), lambda i, j, te, nu: (0, 0)),
                  pl.BlockSpec((1, 1024, tf), lambda i, j, te, nu: (te[i], 0, j)),
                  pl.BlockSpec((1, 1024, tf), lambda i, j, te, nu: (te[i], 0, j)),
                  pl.BlockSpec((1, tf, 1024), lambda i, j, te, nu: (te[i], j, 0))],
        out_specs=pl.BlockSpec((tm, 1024), lambda i, j, te, nu: (i, 0)),
        scratch_shapes=[pltpu.VMEM((2, tm, 1024), F32), pltpu.SemaphoreType.DMA((2,)),
                        pltpu.VMEM((tm, 1024), BF16), pltpu.VMEM((tm, 1024), F32)])
    return pl.pallas_call(
        _moe_group_kernel,
        out_shape=jax.ShapeDtypeStruct((n_tiles * tm, 1024), F32),
        grid_spec=grid_spec,
        compiler_params=_cparams(("arbitrary", "arbitrary")),
        name="moe_group",
    )(tile_expert, n_used, src, src, x, g, wg, wu, wd)


def _moe_combine_kernel(pos_ref, posn_ref, x_ref, r_ref, y_hbm, p_ref, g_ref, wp_ref, wgt_ref, gf_ref,
                        o_ref, bufa, bufb, sem, *, final):
    i = pl.program_id(0)
    tm = x_ref.shape[0]
    slot = i % 2

    def row_copies(idx_ref, s, t):
        pltpu.make_async_copy(y_hbm.at[pl.ds(idx_ref[0, 0, 2 * t], 1), :], bufa.at[s, pl.ds(t, 1), :],
                              sem.at[0, s]).start()
        pltpu.make_async_copy(y_hbm.at[pl.ds(idx_ref[0, 0, 2 * t + 1], 1), :], bufb.at[s, pl.ds(t, 1), :],
                              sem.at[1, s]).start()

    def gather(idx_ref, s):
        def body(t, c):
            row_copies(idx_ref, s, t)
            return c
        lax.fori_loop(0, tm, body, 0, unroll=GATHER_UNROLL)

    @pl.when(i == 0)
    def _():
        gather(pos_ref, 0)

    pltpu.make_async_copy(y_hbm.at[pl.ds(0, tm), :], bufa.at[slot], sem.at[0, slot]).wait()
    pltpu.make_async_copy(y_hbm.at[pl.ds(0, tm), :], bufb.at[slot], sem.at[1, slot]).wait()

    for t in range(tm):
        row_copies(posn_ref, 1 - slot, t)
    r = r_ref[...]
    x = x_ref[...] + r[:, 2:3] * bufa[slot] + r[:, 3:4] * bufb[slot]
    o_ref[...] = _ple_math(x, p_ref[...], g_ref[...], wp_ref[...], wgt_ref[...], gf_ref[...], final)

    @pl.when(i == pl.num_programs(0) - 1)
    def _():
        pltpu.make_async_copy(y_hbm.at[pl.ds(0, tm), :], bufa.at[1 - slot], sem.at[0, 1 - slot]).wait()
        pltpu.make_async_copy(y_hbm.at[pl.ds(0, tm), :], bufb.at[1 - slot], sem.at[1, 1 - slot]).wait()


def _moe_combine(pos, x, route, y, p, g, wp, wgt, gf, final):
    rows = x.shape[0]
    tm = MOE_ROWS
    n = rows // tm
    pos3 = pos.reshape(n, 1, 2 * tm)
    vec = pl.BlockSpec((1, 1024), lambda i: (0, 0))
    return pl.pallas_call(
        functools.partial(_moe_combine_kernel, final=final),
        out_shape=jax.ShapeDtypeStruct((rows, 1024), F32),
        grid=(n,),
        in_specs=[pl.BlockSpec((1, 1, 2 * tm), lambda i: (i, 0, 0), memory_space=pltpu.SMEM),
                  pl.BlockSpec((1, 1, 2 * tm), lambda i: (jnp.minimum(i + 1, n - 1), 0, 0), memory_space=pltpu.SMEM),
                  pl.BlockSpec((tm, 1024), lambda i: (i, 0)), pl.BlockSpec((tm, LANES), lambda i: (i, 0)),
                  pl.BlockSpec(memory_space=pl.ANY),
                  pl.BlockSpec((tm, PLE_DIM), lambda i: (i, 0)), vec,
                  pl.BlockSpec((PLE_DIM, 1024), lambda i: (0, 0)), pl.BlockSpec((1024, 1024), lambda i: (0, 0)), vec],
        out_specs=pl.BlockSpec((tm, 1024), lambda i: (i, 0)),
        scratch_shapes=[pltpu.VMEM((2, tm, 1024), F32), pltpu.VMEM((2, tm, 1024), F32),
                        pltpu.SemaphoreType.DMA((2, 2))],
        compiler_params=_cparams(("arbitrary",)),
        name="moe_combine",
    )(pos3, pos3, x, route, y, p, g, wp, wgt, gf)


def _moe_routed_ple(x, route, g, wg, wu, wd, p, g_ple, wp, wgt, gf, final):
    pos, src, tile_expert, n_used = _route_plan(route, MOE_ROWS)
    y = _moe_group(tile_expert, n_used, src, x, g, wg, wu, wd)
    return _moe_combine(pos, x, route, y, p, g_ple, wp, wgt, gf, final)


def _ple_math(x, p, g, wp, wgt, gf, final):
    emb = jnp.dot(p.astype(BF16), wp, preferred_element_type=F32)
    gate = _sigmoid(jnp.dot(_rms(x, g).astype(BF16), wgt, preferred_element_type=F32))
    y = x + emb * gate
    return _rms(y, gf) if final else y


def _ple_kernel(x_ref, p_ref, g_ref, wp_ref, wgt_ref, gf_ref, o_ref, *, final):
    o_ref[...] = _ple_math(x_ref[...], p_ref[...], g_ref[...], wp_ref[...], wgt_ref[...], gf_ref[...], final)


def _ple(x, p, g, wp, wgt, gf, final):
    rows = x.shape[0]
    tm = min(rows, 512)
    vec = pl.BlockSpec((1, 1024), lambda i: (0, 0))
    return pl.pallas_call(
        functools.partial(_ple_kernel, final=final),
        out_shape=jax.ShapeDtypeStruct((rows, 1024), F32),
        grid=(rows // tm,),
        in_specs=[pl.BlockSpec((tm, 1024), lambda i: (i, 0)), pl.BlockSpec((tm, PLE_DIM), lambda i: (i, 0)), vec,
                  pl.BlockSpec((PLE_DIM, 1024), lambda i: (0, 0)), pl.BlockSpec((1024, 1024), lambda i: (0, 0)), vec],
        out_specs=pl.BlockSpec((tm, 1024), lambda i: (i, 0)),
        compiler_params=_cparams(("parallel",)),
        name="ple",
    )(x, p, g, wp, wgt, gf)


REPACK_ROWS = 128


def _repack_w_in_kernel(w_ref, o_ref):
    offs = [0]
    for wd in IN_WIDTHS:
        offs.append(offs[-1] + wd)
    z, xbc, dt, qa, ka, va, qr, kr, vr, gr, gates = [(offs[i], offs[i + 1]) for i in range(len(IN_WIDTHS))]
    take = lambda lo, hi: w_ref[:, lo:hi].astype(BF16)
    q = take(*qa)
    heads = lambda h: q[:, h * ATT_HEAD_DIM:(h + 1) * ATT_HEAD_DIM]
    q_rmajor = jnp.concatenate([heads(ATT_REP * g + r) for r in range(ATT_REP) for g in range(ATT_KV_HEADS)], axis=1)
    parts = [take(*xbc), take(*z), q_rmajor] + [take(*p) for p in (qr, kr, vr, gr, gates, ka, va)]
    used = sum(p.shape[1] for p in parts) + (dt[1] - dt[0])
    parts.append(jnp.concatenate([take(*dt), jnp.zeros((o_ref.shape[0], N_PROJ - used), BF16)], axis=1))
    dst = 0
    for part in parts:
        o_ref[:, dst:dst + part.shape[1]] = part
        dst += part.shape[1]


def _repack_w_in(w_all, layer):
    rows = w_all.shape[1]
    return pl.pallas_call(
        _repack_w_in_kernel,
        out_shape=jax.ShapeDtypeStruct((rows, N_PROJ), BF16),
        grid=(rows // REPACK_ROWS,),
        in_specs=[pl.BlockSpec((None, REPACK_ROWS, w_all.shape[2]), lambda i: (layer, i, 0))],
        out_specs=pl.BlockSpec((REPACK_ROWS, N_PROJ), lambda i: (i, 0)),
        compiler_params=_cparams(("parallel",)),
        name="repack_w_in",
    )(w_all)


def _att_tables(pos):
    half = ROPE_DIM // 2
    inv = jnp.exp(-math.log(ROPE_THETA) * jnp.arange(half, dtype=F32) * (2.0 / ROPE_DIM))
    ang = pos.astype(F32)[:, None] * inv[None, :]
    cos, sin = jnp.cos(ang), jnp.sin(ang)
    n = pos.shape[0]
    one = jnp.ones((n, ATT_HEAD_DIM - ROPE_DIM), F32)
    zero8 = jnp.zeros((n, half), F32)
    zero = jnp.zeros((n, ATT_HEAD_DIM - ROPE_DIM), F32)
    c = jnp.concatenate([cos, cos, one], axis=1)
    s1 = jnp.concatenate([zero8, sin, zero], axis=1)
    s2 = jnp.concatenate([-sin, zero8, zero], axis=1)
    return tuple(jnp.concatenate([t, t], axis=1) for t in (c, s1, s2))


def _ret_tables(pos):
    half = RET_DK // 2
    inv = jnp.exp(-math.log(RET_THETA) * jnp.arange(half, dtype=F32) * (2.0 / RET_DK))
    ang = pos.astype(F32)[:, None] * inv[None, :]
    cos, sin = jnp.cos(ang), jnp.sin(ang)
    return jnp.concatenate([cos, cos], axis=1), jnp.concatenate([-sin, sin], axis=1)


def _rope_step_tables(inv_lane, seq):
    per = MIX_ROWS // CHUNK
    ang_i = jnp.arange(CHUNK, dtype=F32)[:, None] * inv_lane[None, :]
    within = jnp.concatenate([jnp.cos(ang_i), jnp.sin(ang_i)], axis=1)
    ang_b = (jnp.arange(seq // CHUNK, dtype=F32) * CHUNK)[:, None] * inv_lane[None, :]
    blk = jnp.concatenate([jnp.cos(ang_b), jnp.sin(ang_b)], axis=1).reshape(seq // MIX_ROWS, per, 2 * LANES)
    steps = jnp.concatenate([blk, jnp.zeros((seq // MIX_ROWS, 8 - per, 2 * LANES), F32)], axis=1)
    return within, steps


def _att_inv_lanes():
    inv = jnp.exp(-math.log(ROPE_THETA) * jnp.arange(ROPE_DIM // 2, dtype=F32) * (2.0 / ROPE_DIM))
    return jnp.tile(inv, LANES // (ROPE_DIM // 2))


def _ret_inv_lanes():
    inv = jnp.exp(-math.log(RET_THETA) * jnp.arange(RET_DK // 2, dtype=F32) * (2.0 / RET_DK))
    return jnp.tile(inv, 2)


def _pad_lanes(v, fill=0.0):
    return jnp.concatenate([v.astype(F32), jnp.full((LANES - v.shape[0],), fill, F32)])[None, :]


def kernel(x_prompt, x_sample, state_ssm, state_conv, cache_win_k, cache_win_v, state_ret, p_prompt, p_sample,
           w_in, conv_w, conv_b, dt_bias, a_log, d_skip, ssd_norm, attn_sinks, ret_norm, w_o_ssd, w_o_att, w_o_ret,
           w_out, norm_mix, norm_ffn, norm_ple, ffn_w_gate, ffn_w_up, ffn_w_down, router_w, router_b, moe_w_gate,
           moe_w_up, moe_w_down, w_ple, w_ple_gate, norm_final):
    seq = x_prompt.shape[1]
    nb = x_sample.shape[0]
    xp = x_prompt.reshape(seq, D_MODEL)
    xs = x_sample.reshape(nb, D_MODEL)
    pos_s = PAST_LEN + jnp.arange(1)
    att_tab_p, att_tab_s = _rope_step_tables(_att_inv_lanes(), seq), _att_tables(pos_s)
    ret_tab_p, ret_tab_s = _rope_step_tables(_ret_inv_lanes(), seq), _ret_tables(pos_s)
    log_gamma = jnp.log1p(-jnp.exp2(-5.0 - jnp.arange(RET_HEADS, dtype=F32)))
    gamma = jnp.exp(log_gamma)
    row = lambda v: v.astype(F32)[None, :]

    ssm_all = state_ssm.reshape(DEPTH, nb, SSD_HEADS * SSD_HEAD_DIM, SSD_STATE)
    ret_all = state_ret.reshape(DEPTH, nb, RET_HEADS * RET_DK, 128)
    kc_all = jnp.transpose(cache_win_k, (0, 1, 3, 4, 2))
    vc_all = jnp.transpose(cache_win_v, (0, 1, 3, 4, 2))
    ssm_s = ret_s = kv_s = None
    conv_s = []

    new_p = [[], [], [], [], []]
    for i in range(DEPTH):
        w_in_i = _repack_w_in(w_in, i)
        cw, cb = conv_w[i], row(conv_b[i])
        dtb, alog = _pad_lanes(dt_bias[i]), _pad_lanes(a_log[i])
        dsk = row(jnp.repeat(d_skip[i], SSD_HEAD_DIM))
        nrm_ssd, nrm_ret = row(ssd_norm[i]), row(ret_norm[i])
        sinks = attn_sinks[i].astype(F32)
        sink32 = jnp.zeros((ATT_KV_HEADS, 8), F32).at[:, :ATT_REP].set(sinks.reshape(ATT_KV_HEADS, ATT_REP))
        sink32 = jnp.broadcast_to(sink32.reshape(32, 1), (32, LANES))
        w1 = w_o_ssd[i].astype(BF16)
        w2 = w_o_att[i].reshape(ATT_KV_HEADS, ATT_REP, ATT_HEAD_DIM, D_MODEL).transpose(1, 0, 2, 3) \
            .reshape(ATT_Q_HEADS * ATT_HEAD_DIM, D_MODEL).astype(BF16)
        w3 = w_o_ret[i].astype(BF16)
        wo = w_out[i].astype(BF16)
        g_mix, g_ffn, g_ple = row(norm_mix[i]), row(norm_ffn[i]), row(norm_ple[i])
        wp, wpg = w_ple[i].astype(BF16), w_ple_gate[i].astype(BF16)
        gf = row(norm_final)
        j = i // 2
        if i % 2 == 0:
            ffw = (ffn_w_gate[j].astype(BF16), ffn_w_up[j].astype(BF16), ffn_w_down[j].astype(BF16))
        else:
            rw = jnp.concatenate([router_w[j], jnp.zeros((D_MODEL, LANES - N_EXPERTS), F32)], axis=1)
            rw_hi = rw.astype(BF16)
            rw = jnp.stack([rw_hi, (rw - rw_hi.astype(F32)).astype(BF16)])
            ffw = (rw, _pad_lanes(router_b[j]), moe_w_gate[j].astype(BF16), moe_w_up[j].astype(BF16),
                   moe_w_down[j].astype(BF16))
        final = i == DEPTH - 1

        proj, dt = _inproj(xp, g_mix, w_in_i, BF16)
        head_rows = lambda v: jnp.broadcast_to(v.astype(F32)[:, None], (SSD_HEADS, LANES))
        y_ssd, ssm_fin, conv_fin = _ssd_prompt(proj, dt, cw, cb, head_rows(dt_bias[i]), head_rows(a_log[i]),
                                               dsk, nrm_ssd)
        o_att, wk, wv = _swa_prompt(sinks, proj, att_tab_p)
        o_ret, ret_fin = _ret_prompt(log_gamma, proj, ret_tab_p, nrm_ret)
        ple_args = (p_prompt[i].reshape(seq, PLE_DIM), g_ple, wp, wpg, gf, final)
        if i % 2 == 0:
            xp = _merge(xp, y_ssd, o_att, o_ret, proj, w1, w2, w3, wo)
            xp = _ple(_ffn(xp, g_ffn, *ffw), *ple_args)
        else:
            rw, rb, *expert_w = ffw
            xp, route = _merge(xp, y_ssd, o_att, o_ret, proj, w1, w2, w3, wo, router=(g_ffn, rw, rb))
            xp = _moe_routed_ple(xp, route, g_ffn, *expert_w, *ple_args)
        new_p[0].append(ssm_fin.reshape(1, SSD_HEADS, SSD_HEAD_DIM, SSD_STATE))
        new_p[1].append(conv_fin[None])
        new_p[2].append(wk.reshape(1, WINDOW, ATT_KV_HEADS, ATT_HEAD_DIM))
        new_p[3].append(wv.reshape(1, WINDOW, ATT_KV_HEADS, ATT_HEAD_DIM))
        new_p[4].append(ret_fin[None])

        proj, dt = _inproj(xs, g_mix, w_in_i, F32)
        cst_t = jnp.transpose(state_conv[i], (1, 0, 2))
        cnew_t, xs_conv, bc, dec_t, xdt_t = _ssd_dec_pre(proj, dt, cst_t, cw, cb, dtb, alog)
        y_ssd, ssm_s = _ssd_dec_state(i, dec_t, xdt_t, bc, proj, xs_conv, ssm_all, dsk, nrm_ssd, ssm_s)
        qa_rot, ka_t, va_t = _swa_dec_pre(proj, att_tab_s)
        o_att, *kv_s = _swa_dec(i, qa_rot, ka_t, va_t, kc_all, vc_all, sink32, kv_s)
        qrot, kt = _ret_dec_pre(proj, ret_tab_s)
        o_ret, ret_s = _ret_dec_state(i, gamma, kt, qrot, proj, ret_all, nrm_ret, ret_s)
        xs = _merge(xs, y_ssd, o_att, o_ret, proj, w1, w2, w3, wo)
        xs = _ffn(xs, g_ffn, *ffw) if i % 2 == 0 else _moe(xs, g_ffn, *ffw)
        xs = _ple(xs, p_sample[i].reshape(nb, PLE_DIM), g_ple, wp, wpg, gf, final)
        conv_s.append(jnp.transpose(cnew_t, (1, 0, 2)))

    y_prompt = xp.reshape(1, seq, D_MODEL)
    y_sample = xs.reshape(nb, 1, D_MODEL)
    outs_p = [jnp.stack(l) for l in new_p]
    outs_s = [ssm_s.reshape(state_ssm.shape), jnp.stack(conv_s), jnp.transpose(kv_s[0], (0, 1, 4, 2, 3)),
              jnp.transpose(kv_s[1], (0, 1, 4, 2, 3)), ret_s.reshape(state_ret.shape)]
    return (y_prompt, y_sample, *outs_p, *outs_s)
```

```python
import functools
import math

import jax
import jax.numpy as jnp
from jax import lax
from jax.experimental import pallas as pl
from jax.experimental.pallas import tpu as pltpu

F32 = jnp.float32
BF16 = jnp.bfloat16

D_MODEL = 1024
DEPTH = 2
PAST_LEN = 16384
SSD_HEADS = 16
SSD_HEAD_DIM = 64
SSD_GROUPS = 4
SSD_STATE = 128
SSD_CONV = 4
SSD_CONV_DIM = 2048
ATT_HEAD_DIM = 64
ATT_Q_HEADS = 16
ATT_KV_HEADS = 4
ATT_REP = ATT_Q_HEADS // ATT_KV_HEADS
WINDOW = 128
ROPE_THETA = 500000.0
ROPE_DIM = 16
RET_HEADS = 8
RET_DK = 128
RET_THETA = 10000.0
CHUNK = 128
D_FF = 2816
N_EXPERTS = 8
PLE_DIM = 256
EPS = 1e-6

IN_WIDTHS = (1024, 2048, 16, 1024, 256, 256, 1024, 1024, 1024, 1024, 3072)
N_PROJ = 12288
COL_XBC = 0
COL_Z = 2
COL_QA = 3
COL_QR = 4
COL_KR = 5
COL_VR = 6
COL_GR = 7
COL_GATE = 8
COL_KA = 44
COL_VA = 45
COL_DT = 92

LANES = 128
VMEM_LIMIT = 48 * 1024 * 1024
VMEM_LIMIT_FFN = 56 * 1024 * 1024
MIX_ROWS = 4 * CHUNK


def _cparams(sem, vmem=VMEM_LIMIT):
    return pltpu.CompilerParams(dimension_semantics=sem, vmem_limit_bytes=vmem)


def _bdot(a, b):
    return jnp.dot(a.astype(BF16), b.astype(BF16), preferred_element_type=F32)


def _bdot_nt(a, b):
    return lax.dot_general(a.astype(BF16), b.astype(BF16), (((1,), (1,)), ((), ())),
                           preferred_element_type=F32)


def _split3(x):
    x0 = x.astype(BF16)
    r1 = x - x0.astype(F32)
    x1 = r1.astype(BF16)
    x2 = (r1 - x1.astype(F32)).astype(BF16)
    return x0, x1, x2


def _dot_exact_lhs01(m01, x):
    m = m01.astype(BF16)
    x0, x1, x2 = _split3(x)
    d = lambda b: jnp.dot(m, b, preferred_element_type=F32)
    return d(x0) + d(x1) + d(x2)


def _dot_exact_rhs01(x, m01):
    m = m01.astype(BF16)
    x0, x1, x2 = _split3(x)
    d = lambda a: jnp.dot(a, m, preferred_element_type=F32)
    return d(x0) + d(x1) + d(x2)


def _rms(x, g):
    return x * lax.rsqrt(jnp.mean(x * x, axis=-1, keepdims=True) + EPS) * g


def _sigmoid(x):
    return 1.0 / (1.0 + jnp.exp(-x))


def _silu(x):
    return x * _sigmoid(x)


def _softplus(x):
    return jnp.maximum(x, 0.0) + jnp.log1p(jnp.exp(-jnp.abs(x)))


def _rope_att(x, c, s1, s2):
    w = x.shape[1]
    return x * c + pltpu.roll(x, 8, axis=1) * s1 + pltpu.roll(x, w - 8, axis=1) * s2


def _tile_lanes(t, n):
    return jnp.concatenate([t] * n, axis=1) if n > 1 else t


INPROJ_TN = 1024
DT_TILE = (COL_DT * LANES) // INPROJ_TN
DT_OFF = COL_DT * LANES - DT_TILE * INPROJ_TN


def _inproj_kernel(x_ref, g_ref, w_ref, o_ref, dt_ref, h_ref):
    j = pl.program_id(1)

    @pl.when(j == 0)
    def _():
        h_ref[...] = _rms(x_ref[...], g_ref[...]).astype(BF16)

    acc = jnp.dot(h_ref[...], w_ref[...], preferred_element_type=F32)
    o_ref[...] = acc.astype(o_ref.dtype)

    @pl.when(j == DT_TILE)
    def _():
        dt_ref[...] = acc[:, DT_OFF:DT_OFF + LANES]


def _inproj(x, g, w, out_dtype):
    rows = x.shape[0]
    tm = min(rows, 2048)
    tn = INPROJ_TN
    return pl.pallas_call(
        _inproj_kernel,
        out_shape=(jax.ShapeDtypeStruct((rows, N_PROJ), out_dtype), jax.ShapeDtypeStruct((rows, LANES), F32)),
        grid=(rows // tm, N_PROJ // tn),
        in_specs=[pl.BlockSpec((tm, D_MODEL), lambda i, j: (i, 0)),
                  pl.BlockSpec((1, D_MODEL), lambda i, j: (0, 0)),
                  pl.BlockSpec((D_MODEL, tn), lambda i, j: (0, j))],
        out_specs=(pl.BlockSpec((tm, tn), lambda i, j: (i, j)), pl.BlockSpec((tm, LANES), lambda i, j: (i, 0))),
        scratch_shapes=[pltpu.VMEM((tm, D_MODEL), BF16)],
        compiler_params=_cparams(("parallel", "arbitrary")),
        name="inproj",
    )(x, g, w)


def _chunk_rows(ci):
    return pl.ds(pl.multiple_of(ci * CHUNK, CHUNK), CHUNK)


def _ssd_prompt_chunk(rows, xbc_ref, z_ref, dt_ref, cw_ref, cb_ref, dtb_ref, alog_ref, dsk_ref, nrm_ref,
                      y_ref, xpad_ref, s_ref):
    xbc = xbc_ref[rows, :].astype(F32)
    xpad_ref[8:8 + CHUNK, :] = xbc
    cw = cw_ref[...]
    acc = (xbc * cw[3:4, :] + xpad_ref[7:7 + CHUNK, :] * cw[2:3, :]
           + xpad_ref[6:6 + CHUNK, :] * cw[1:2, :] + xpad_ref[5:5 + CHUNK, :] * cw[0:1, :] + cb_ref[...])
    conv = _silu(acc)
    xpad_ref[0:8, :] = xbc[CHUNK - 8:CHUNK, :]

    xs = conv[:, :1024]
    dt_t = _softplus(dt_ref[rows, :].T[:SSD_HEADS, :] + dtb_ref[...])
    la_t = dt_t * (-jnp.exp(alog_ref[...]))
    row = lax.broadcasted_iota(jnp.int32, (CHUNK, CHUNK), 0)
    col = lax.broadcasted_iota(jnp.int32, (CHUNK, CHUNK), 1)
    causal = row >= col
    cum_t = _dot_exact_rhs01(la_t, (row <= col).astype(F32))
    cum = jnp.concatenate([cum_t, jnp.zeros((LANES - SSD_HEADS, CHUNK), F32)], axis=0).T
    cum_last = jnp.broadcast_to(cum_t[:, CHUNK - 1:CHUNK], (SSD_HEADS, CHUNK))
    w_t = jnp.exp(cum_last - cum_t) * dt_t
    dec_end = jnp.exp(cum_last)

    xs_t = xs.T
    ys = []
    for g in range(SSD_GROUPS):
        bg = conv[:, 1024 + 128 * g:1024 + 128 * (g + 1)]
        cg = conv[:, 1536 + 128 * g:1536 + 128 * (g + 1)]
        cb = _bdot_nt(cg, bg)
        s_g = s_ref[256 * g:256 * (g + 1), :]
        cs = _bdot_nt(cg, s_g)
        xw_parts = []
        dec_parts = []
        for r in range(4):
            h = 4 * g + r
            colb = jnp.broadcast_to(cum[:, h:h + 1], (CHUNK, CHUNK))
            rowb = jnp.broadcast_to(cum_t[h:h + 1, :], (CHUNK, CHUNK))
            dec = jnp.exp(jnp.where(causal, colb - rowb, -jnp.inf))
            m = cb * dec * jnp.broadcast_to(dt_t[h:h + 1, :], (CHUNK, CHUNK))
            xh = xs[:, 64 * h:64 * (h + 1)]
            yh = _bdot(m, xh) + cs[:, 64 * r:64 * (r + 1)] * jnp.exp(colb)[:, :64]
            ys.append(yh)
            xw_parts.append(xs_t[64 * h:64 * (h + 1), :] * jnp.broadcast_to(w_t[h:h + 1, :], (64, CHUNK)))
            dec_parts.append(jnp.broadcast_to(dec_end[h:h + 1, :], (64, SSD_STATE)))
        xw = jnp.concatenate(xw_parts, axis=0)
        s_ref[256 * g:256 * (g + 1), :] = s_g * jnp.concatenate(dec_parts, axis=0) + _bdot(xw, bg)

    y = jnp.concatenate(ys, axis=1) + dsk_ref[...] * xs
    y = y * _silu(z_ref[rows, :].astype(F32))
    y_ref[rows, :] = _rms(y, nrm_ref[...]).astype(y_ref.dtype)


def _ssd_prompt_kernel(xbc_ref, z_ref, dt_ref, cw_ref, cb_ref, dtb_ref, alog_ref, dsk_ref, nrm_ref,
                       y_ref, sfin_ref, cfin_ref, xpad_ref, s_ref):
    t = pl.program_id(0)

    @pl.when(t == 0)
    def _():
        xpad_ref[0:8, :] = jnp.zeros((8, SSD_CONV_DIM), F32)
        s_ref[...] = jnp.zeros_like(s_ref)

    def chunk(ci, carry):
        _ssd_prompt_chunk(_chunk_rows(ci), xbc_ref, z_ref, dt_ref, cw_ref, cb_ref, dtb_ref, alog_ref, dsk_ref,
                          nrm_ref, y_ref, xpad_ref, s_ref)
        return carry

    lax.fori_loop(0, xbc_ref.shape[0] // CHUNK, chunk, 0)

    @pl.when(t == pl.num_programs(0) - 1)
    def _():
        sfin_ref[...] = s_ref[...]
        cfin_ref[...] = xpad_ref[8 + CHUNK - (SSD_CONV - 1):8 + CHUNK, :]


def _ssd_prompt(proj, dt, cw, cb, dtb, alog, dsk, nrm):
    seq = proj.shape[0]
    const = lambda shape: pl.BlockSpec(shape, lambda t: (0,) * len(shape))
    return pl.pallas_call(
        _ssd_prompt_kernel,
        out_shape=(jax.ShapeDtypeStruct((seq, 1024), BF16),
                   jax.ShapeDtypeStruct((1024, SSD_STATE), F32),
                   jax.ShapeDtypeStruct((SSD_CONV - 1, SSD_CONV_DIM), F32)),
        grid=(seq // MIX_ROWS,),
        in_specs=[pl.BlockSpec((MIX_ROWS, 2048), lambda t: (t, COL_XBC)),
                  pl.BlockSpec((MIX_ROWS, 1024), lambda t: (t, COL_Z)),
                  pl.BlockSpec((MIX_ROWS, LANES), lambda t: (t, 0)),
                  const((SSD_CONV, 2048)), const((1, 2048)), const((SSD_HEADS, LANES)), const((SSD_HEADS, LANES)),
                  const((1, 1024)), const((1, 1024))],
        out_specs=(pl.BlockSpec((MIX_ROWS, 1024), lambda t: (t, 0)),
                   const((1024, SSD_STATE)), const((SSD_CONV - 1, SSD_CONV_DIM))),
        scratch_shapes=[pltpu.VMEM((8 + CHUNK, SSD_CONV_DIM), F32), pltpu.VMEM((1024, SSD_STATE), F32)],
        compiler_params=_cparams(("arbitrary",)),
        name="ssd_prompt",
    )(proj, proj, dt, cw, cb, dtb, alog, dsk, nrm)


def _ssd_dec_pre_kernel(xbc_ref, dt_ref, cst_ref, cw_ref, cb_ref, dtb_ref, alog_ref,
                        cnew_ref, xs_ref, bc_ref, dec_t_ref, xdt_t_ref):
    xbc = xbc_ref[...]
    cw = cw_ref[...]
    acc = (cst_ref[0] * cw[0:1, :] + cst_ref[1] * cw[1:2, :] + cst_ref[2] * cw[2:3, :]
           + xbc * cw[3:4, :] + cb_ref[...])
    conv = _silu(acc)
    cnew_ref[0] = cst_ref[1]
    cnew_ref[1] = cst_ref[2]
    cnew_ref[2] = xbc
    xs = conv[:, :1024]
    xs_ref[...] = xs
    bc_ref[...] = conv[:, 1024:]
    dt = _softplus(dt_ref[...] + dtb_ref[...])
    dec = jnp.exp(dt * (-jnp.exp(alog_ref[...])))
    hrow = lax.broadcasted_iota(jnp.int32, (LANES, 1024), 0)
    hcol = lax.broadcasted_iota(jnp.int32, (LANES, 1024), 1)
    expand = ((hcol >> 6) == hrow).astype(F32)
    dec_t_ref[...] = _dot_exact_rhs01(dec, expand).T
    xdt_t_ref[...] = (xs * _dot_exact_rhs01(dt, expand)).T


def _ssd_dec_pre(proj, dt, cst_t, cw, cb, dtb, alog):
    nb = proj.shape[0]
    const = lambda shape: pl.BlockSpec(shape, lambda t: (0,) * len(shape))
    return pl.pallas_call(
        _ssd_dec_pre_kernel,
        out_shape=(jax.ShapeDtypeStruct((3, nb, 2048), F32), jax.ShapeDtypeStruct((nb, 1024), F32),
                   jax.ShapeDtypeStruct((nb, 1024), F32), jax.ShapeDtypeStruct((1024, nb), F32),
                   jax.ShapeDtypeStruct((1024, nb), F32)),
        grid=(1,),
        in_specs=[pl.BlockSpec((nb, 2048), lambda t: (0, COL_XBC)),
                  pl.BlockSpec((nb, LANES), lambda t: (0, 0)),
                  const((3, nb, 2048)), const((SSD_CONV, 2048)), const((1, 2048)), const((1, 128)), const((1, 128))],
        out_specs=(const((3, nb, 2048)), const((nb, 1024)), const((nb, 1024)), const((1024, nb)), const((1024, nb))),
        compiler_params=_cparams(("arbitrary",)),
        name="ssd_dec_pre",
    )(proj, dt, cst_t, cw, cb, dtb, alog)


DEC_BLOCK = 8


def _ssd_dec_state_kernel(dec_t_ref, xdt_t_ref, bc_ref, z_ref, xs_ref, st_ref, dsk_ref, nrm_ref,
                          y_ref, stn_ref):
    i = pl.program_id(0)
    shift = (LANES - i * DEC_BLOCK) % LANES
    decr = pltpu.roll(dec_t_ref[...], shift, axis=1)
    xr = pltpu.roll(xdt_t_ref[...], shift, axis=1)
    bc = bc_ref[...]
    for j in range(DEC_BLOCK):
        for g in range(SSD_GROUPS):
            lo, hi = 256 * g, 256 * (g + 1)
            s_old = st_ref[j, lo:hi, :]
            dcol = jnp.broadcast_to(decr[lo:hi, j:j + 1], (256, SSD_STATE))
            xcol = jnp.broadcast_to(xr[lo:hi, j:j + 1], (256, SSD_STATE))
            s_new = s_old * dcol + xcol * bc[j:j + 1, 128 * g:128 * (g + 1)]
            stn_ref[j, lo:hi, :] = s_new
            cs = _bdot_nt(bc[:, 512 + 128 * g:512 + 128 * (g + 1)], s_new)
            y_ref[j:j + 1, lo:hi] = cs[j:j + 1, :]
    xs = xs_ref[...]
    y = y_ref[...] + dsk_ref[...] * xs
    y = y * _silu(z_ref[...])
    y_ref[...] = _rms(y, nrm_ref[...])


class _LayerCall:
    def __init__(self, layer, steps):
        self.layer, self.steps, self.first = layer, steps, layer == 0
        self.grid = (DEPTH * steps,) if self.first else (steps,)

    def _block(self, t):
        return jnp.minimum(t, self.steps - 1) if self.first else t

    def rows(self, width, col=0):
        return pl.BlockSpec((DEC_BLOCK, width), lambda t: (self._block(t), col))

    def state_in(self, tail):
        zeros = (0,) * len(tail)
        return pl.BlockSpec((None, DEC_BLOCK) + tail, lambda t: (self.layer, self._block(t)) + zeros)

    def state_out(self, tail):
        zeros = (0,) * len(tail)
        if self.first:
            return pl.BlockSpec((None, DEC_BLOCK) + tail, lambda t: (t // self.steps, t % self.steps) + zeros)
        return pl.BlockSpec((None, DEC_BLOCK) + tail, lambda t: (self.layer, t) + zeros)

    def kernel(self, body, n_in, n_carried, stacked_outs):
        def wrapped(*refs):
            refs = refs[:n_in] + refs[n_in + n_carried:]
            if not self.first:
                body(*refs)
                return
            t = pl.program_id(0)

            @pl.when(t < self.steps)
            def _():
                body(*refs)

            @pl.when(t >= self.steps)
            def _():
                for k in stacked_outs:
                    refs[n_in + k][...] = jnp.zeros_like(refs[n_in + k])
        return wrapped


def _ssd_dec_state(layer, dec_t, xdt_t, bc, proj, xs, st_all, dsk, nrm, carried):
    nb = xs.shape[0]
    lc = _LayerCall(layer, nb // DEC_BLOCK)
    const = lambda shape: pl.BlockSpec(shape, lambda t: (0,) * len(shape))
    tail = (1024, SSD_STATE)
    ins = [dec_t, xdt_t, bc, proj, xs, st_all, dsk, nrm]
    in_specs = [const((1024, nb)), const((1024, nb)), lc.rows(1024), lc.rows(1024, COL_Z), lc.rows(1024),
                lc.state_in(tail), const((1, 1024)), const((1, 1024))]
    n_in = len(ins)
    carried = [] if carried is None else [carried]
    return pl.pallas_call(
        lc.kernel(_ssd_dec_state_kernel, n_in, len(carried), (1,)),
        out_shape=(jax.ShapeDtypeStruct((nb, 1024), F32), jax.ShapeDtypeStruct(st_all.shape, F32)),
        grid=lc.grid,
        in_specs=in_specs + [pl.BlockSpec(memory_space=pl.ANY)] * len(carried),
        out_specs=(lc.rows(1024), lc.state_out(tail)),
        input_output_aliases={n_in + k: 1 + k for k in range(len(carried))},
        compiler_params=_cparams(("arbitrary",)),
        name="ssd_dec_state",
    )(*ins, *carried)


def _block_cos_sin(within_ref, step_ref, bi):
    row = step_ref[pl.ds(bi, 1), :]
    cb, sb = row[:, :LANES], row[:, LANES:]
    ci, si = within_ref[:, :LANES], within_ref[:, LANES:]
    return cb * ci - sb * si, sb * ci + cb * si


def _swa_prompt_block(bi, has_prev, sink_ref, q_ref, k_ref, v_ref, within_ref, step_ref, o_ref, kp_ref, vp_ref):
    rows = _chunk_rows(bi)
    cosp, sinp = _block_cos_sin(within_ref, step_ref, bi)
    l64 = lax.broadcasted_iota(jnp.int32, (1, LANES), 1) & (ATT_HEAD_DIM - 1)
    c = jnp.where(l64 < ROPE_DIM, cosp, 1.0)
    s1 = jnp.where(jnp.logical_and(l64 >= ROPE_DIM // 2, l64 < ROPE_DIM), sinp, 0.0)
    s2 = jnp.where(l64 < ROPE_DIM // 2, -sinp, 0.0)
    q = _rope_att(q_ref[rows, :].astype(F32), _tile_lanes(c, 8), _tile_lanes(s1, 8), _tile_lanes(s2, 8))
    k = _rope_att(k_ref[rows, :].astype(F32), _tile_lanes(c, 2), _tile_lanes(s1, 2), _tile_lanes(s2, 2))
    v = v_ref[rows, :].astype(F32)
    ghead = lax.broadcasted_iota(jnp.int32, (WINDOW, 256), 1) >> 6
    expand = lambda t: jnp.concatenate([jnp.where(ghead == g, t, 0.0) for g in range(ATT_KV_HEADS)],
                                       axis=0).astype(BF16)
    kbd, vbd = expand(k), expand(v)
    kbd_prev, vbd_prev = expand(kp_ref[...]), expand(vp_ref[...])
    qall = jnp.concatenate([q[:, 256 * r:256 * (r + 1)] for r in range(ATT_REP)], axis=0).astype(BF16)
    scale = ATT_HEAD_DIM ** -0.5
    nt_dims = (((1,), (1,)), ((), ()))
    sp_all = lax.dot_general(qall, kbd_prev, nt_dims, preferred_element_type=F32) * scale
    sc_all = lax.dot_general(qall, kbd, nt_dims, preferred_element_type=F32) * scale
    nq = ATT_REP * WINDOW
    qi = lax.broadcasted_iota(jnp.int32, (nq, WINDOW), 0) & (WINDOW - 1)
    kj = lax.broadcasted_iota(jnp.int32, (nq, WINDOW), 1)
    mask_prev = jnp.logical_and(kj > qi, has_prev)
    mask_cur = kj <= qi
    rep = lax.broadcasted_iota(jnp.int32, (nq, 1), 0) >> 7
    pp, pc = [], []
    for g in range(ATT_KV_HEADS):
        sp = jnp.where(mask_prev, sp_all[:, WINDOW * g:WINDOW * (g + 1)], -jnp.inf)
        sc = jnp.where(mask_cur, sc_all[:, WINDOW * g:WINDOW * (g + 1)], -jnp.inf)
        sink = jnp.where(rep == 0, sink_ref[4 * g],
                         jnp.where(rep == 1, sink_ref[4 * g + 1],
                                   jnp.where(rep == 2, sink_ref[4 * g + 2], sink_ref[4 * g + 3])))
        m = jnp.maximum(jnp.max(jnp.maximum(sp, sc), axis=1, keepdims=True), sink)
        ep = jnp.exp(sp - m)
        ec = jnp.exp(sc - m)
        inv = 1.0 / (jnp.sum(ep + ec, axis=1, keepdims=True) + jnp.exp(sink - m))
        pp.append((ep * inv).astype(BF16))
        pc.append((ec * inv).astype(BF16))
    o = (jnp.dot(jnp.concatenate(pp, axis=1), vbd_prev, preferred_element_type=F32)
         + jnp.dot(jnp.concatenate(pc, axis=1), vbd, preferred_element_type=F32))
    for r in range(ATT_REP):
        o_ref[rows, 256 * r:256 * (r + 1)] = o[WINDOW * r:WINDOW * (r + 1), :].astype(o_ref.dtype)
    kp_ref[...] = k
    vp_ref[...] = v


def _swa_prompt_kernel(sink_ref, q_ref, k_ref, v_ref, within_ref, step_ref,
                       o_ref, wk_ref, wv_ref, kp_ref, vp_ref):
    n = pl.program_id(0)
    blocks = q_ref.shape[0] // WINDOW

    @pl.when(n == 0)
    def _():
        kp_ref[...] = jnp.zeros_like(kp_ref)
        vp_ref[...] = jnp.zeros_like(vp_ref)

    def block(bi, carry):
        _swa_prompt_block(bi, n * blocks + bi > 0, sink_ref, q_ref, k_ref, v_ref,
                          within_ref, step_ref, o_ref, kp_ref, vp_ref)
        return carry

    lax.fori_loop(0, blocks, block, 0)

    @pl.when(n == pl.num_programs(0) - 1)
    def _():
        wk_ref[...] = kp_ref[...]
        wv_ref[...] = vp_ref[...]


def _swa_prompt(sinks, proj, tabs):
    seq = proj.shape[0]
    within, steps = tabs
    const = lambda shape: pl.BlockSpec(shape, lambda t: (0,) * len(shape))
    return pl.pallas_call(
        _swa_prompt_kernel,
        out_shape=(jax.ShapeDtypeStruct((seq, 1024), BF16),
                   jax.ShapeDtypeStruct((WINDOW, 256), F32), jax.ShapeDtypeStruct((WINDOW, 256), F32)),
        grid=(seq // MIX_ROWS,),
        in_specs=[pl.BlockSpec(memory_space=pltpu.SMEM),
                  pl.BlockSpec((MIX_ROWS, 1024), lambda t: (t, COL_QA)),
                  pl.BlockSpec((MIX_ROWS, 256), lambda t: (t, COL_KA)),
                  pl.BlockSpec((MIX_ROWS, 256), lambda t: (t, COL_VA)),
                  const((CHUNK, 2 * LANES)), pl.BlockSpec((None, 8, 2 * LANES), lambda t: (t, 0, 0))],
        out_specs=(pl.BlockSpec((MIX_ROWS, 1024), lambda t: (t, 0)), const((WINDOW, 256)), const((WINDOW, 256))),
        scratch_shapes=[pltpu.VMEM((WINDOW, 256), F32), pltpu.VMEM((WINDOW, 256), F32)],
        compiler_params=_cparams(("arbitrary",)),
        name="swa_prompt",
    )(sinks, proj, proj, proj, within, steps)


def _swa_dec_pre_kernel(q_ref, k_ref, v_ref, c_ref, s1_ref, s2_ref, qrot_ref, kt_ref, vt_ref):
    c, s1, s2 = c_ref[...], s1_ref[...], s2_ref[...]
    qrot_ref[...] = _rope_att(q_ref[...], _tile_lanes(c, 8), _tile_lanes(s1, 8), _tile_lanes(s2, 8))
    kt_ref[...] = _rope_att(k_ref[...], _tile_lanes(c, 2), _tile_lanes(s1, 2), _tile_lanes(s2, 2)).T
    vt_ref[...] = v_ref[...].T


def _swa_dec_pre(proj, tabs):
    nb = proj.shape[0]
    c, s1, s2 = tabs
    const = lambda shape: pl.BlockSpec(shape, lambda t: (0,) * len(shape))
    return pl.pallas_call(
        _swa_dec_pre_kernel,
        out_shape=(jax.ShapeDtypeStruct((nb, 1024), F32), jax.ShapeDtypeStruct((256, nb), F32),
                   jax.ShapeDtypeStruct((256, nb), F32)),
        grid=(1,),
        in_specs=[pl.BlockSpec((nb, 1024), lambda t: (0, COL_QA)), pl.BlockSpec((nb, 256), lambda t: (0, COL_KA)),
                  pl.BlockSpec((nb, 256), lambda t: (0, COL_VA)), const((1, LANES)), const((1, LANES)), const((1, LANES))],
        out_specs=(const((nb, 1024)), const((256, nb)), const((256, nb))),
        compiler_params=_cparams(("arbitrary",)),
        name="swa_dec_pre",
    )(proj, proj, proj, c, s1, s2)


def _swa_dec_kernel(q_ref, kt_ref, vt_ref, kc_ref, vc_ref, sink_ref, o_ref, kcn_ref, vcn_ref):
    i = pl.program_id(0)
    shift = (LANES - i * DEC_BLOCK) % LANES
    ktr = pltpu.roll(kt_ref[...], shift, axis=1)
    vtr = pltpu.roll(vt_ref[...], shift, axis=1)
    q = q_ref[...]
    lane = lax.broadcasted_iota(jnp.int32, (ATT_HEAD_DIM, WINDOW), 1)
    rowid = lax.broadcasted_iota(jnp.int32, (8, ATT_HEAD_DIM), 0)
    scale = ATT_HEAD_DIM ** -0.5
    append = lambda old, col: jnp.where(lane == WINDOW - 1, jnp.broadcast_to(col, (ATT_HEAD_DIM, WINDOW)),
                                        pltpu.roll(old, WINDOW - 1, axis=1))
    pairs = [(j, g) for j in range(DEC_BLOCK) for g in range(ATT_KV_HEADS)]
    scores, values = [], []
    for j, g in pairs:
        lo, hi = ATT_HEAD_DIM * g, ATT_HEAD_DIM * (g + 1)
        kt = append(kc_ref[j, g], ktr[lo:hi, j:j + 1])
        vt = append(vc_ref[j, g], vtr[lo:hi, j:j + 1])
        kcn_ref[j, g] = kt
        vcn_ref[j, g] = vt
        qg = jnp.zeros((8, ATT_HEAD_DIM), F32)
        for r in range(ATT_REP):
            src = 256 * r + lo
            qg = jnp.where(rowid == r, jnp.broadcast_to(q[j:j + 1, src:src + ATT_HEAD_DIM], (8, ATT_HEAD_DIM)), qg)
        scores.append(_bdot(qg, kt))
        values.append(vt.astype(BF16))
    s = jnp.concatenate(scores, axis=0) * scale
    sink = jnp.concatenate([sink_ref[...]] * DEC_BLOCK, axis=0)[:, 0:1]
    m = jnp.maximum(jnp.max(s, axis=1, keepdims=True), sink)
    e = jnp.exp(s - m)
    p = e * (1.0 / (jnp.sum(e, axis=1, keepdims=True) + jnp.exp(sink - m)))
    for idx, (j, g) in enumerate(pairs):
        o = lax.dot_general(p[8 * idx:8 * (idx + 1), :].astype(BF16), values[idx], (((1,), (1,)), ((), ())),
                            preferred_element_type=F32)
        for r in range(ATT_REP):
            dst = 256 * r + ATT_HEAD_DIM * g
            o_ref[j:j + 1, dst:dst + ATT_HEAD_DIM] = o[r:r + 1, :]


def _swa_dec(layer, qrot, kt, vt, kc_all, vc_all, sink32, carried):
    nb = qrot.shape[0]
    lc = _LayerCall(layer, nb // DEC_BLOCK)
    const = lambda shape: pl.BlockSpec(shape, lambda t: (0,) * len(shape))
    tail = (ATT_KV_HEADS, ATT_HEAD_DIM, WINDOW)
    ins = [qrot, kt, vt, kc_all, vc_all, sink32]
    in_specs = [lc.rows(1024), const((256, nb)), const((256, nb)), lc.state_in(tail), lc.state_in(tail),
                const((32, LANES))]
    n_in = len(ins)
    carried = [] if carried is None else list(carried)
    return pl.pallas_call(
        lc.kernel(_swa_dec_kernel, n_in, len(carried), (1, 2)),
        out_shape=(jax.ShapeDtypeStruct((nb, 1024), F32),
                   jax.ShapeDtypeStruct(kc_all.shape, F32), jax.ShapeDtypeStruct(vc_all.shape, F32)),
        grid=lc.grid,
        in_specs=in_specs + [pl.BlockSpec(memory_space=pl.ANY)] * len(carried),
        out_specs=(lc.rows(1024), lc.state_out(tail), lc.state_out(tail)),
        input_output_aliases={n_in + k: 1 + k for k in range(len(carried))},
        compiler_params=_cparams(("arbitrary",)),
        name="swa_dec",
    )(*ins, *carried)


def _ret_prompt_chunk(ci, lg_ref, q_ref, k_ref, v_ref, gr_ref, within_ref, step_ref, nrm_ref,
                      o_ref, st_ref, intra_ref, fs_ref, te_ref):
    rows = _chunk_rows(ci)
    c, sinp = _block_cos_sin(within_ref, step_ref, ci)
    s = jnp.where(lax.broadcasted_iota(jnp.int32, (1, LANES), 1) < RET_DK // 2, -sinp, sinp)
    q = q_ref[rows, :].astype(F32)
    k = k_ref[rows, :].astype(F32)
    v = v_ref[rows, :].astype(F32)
    gr = gr_ref[rows, :].astype(F32)
    nrm = nrm_ref[...]
    for h in range(RET_HEADS):
        sl = slice(128 * h, 128 * (h + 1))
        qh = q[:, sl]
        kh = k[:, sl]
        qh = qh * c + pltpu.roll(qh, 64, axis=1) * s
        kh = (kh * c + pltpu.roll(kh, 64, axis=1) * s) * (RET_DK ** -0.5)
        vh = v[:, sl]
        att = _bdot_nt(qh, kh) * intra_ref[h]
        s_old = st_ref[h]
        o = _bdot(att, vh) + _bdot(qh, s_old) * fs_ref[h]
        cd = jnp.exp(jnp.zeros((1, RET_DK), F32) + CHUNK * lg_ref[h])
        st_ref[h] = s_old * cd + _bdot((kh * te_ref[h]).T, vh)
        o = o * lax.rsqrt(jnp.mean(o * o, axis=-1, keepdims=True) + EPS)
        o_ref[rows, sl] = (o * nrm[:, sl] * _silu(gr[:, sl])).astype(o_ref.dtype)


def _ret_prompt_kernel(lg_ref, q_ref, k_ref, v_ref, gr_ref, within_ref, step_ref, nrm_ref,
                       o_ref, sfin_ref, st_ref, intra_ref, fs_ref, te_ref):
    t = pl.program_id(0)

    @pl.when(t == 0)
    def _():
        st_ref[...] = jnp.zeros_like(st_ref)
        ri = lax.broadcasted_iota(jnp.int32, (CHUNK, CHUNK), 0).astype(F32)
        ci = lax.broadcasted_iota(jnp.int32, (CHUNK, CHUNK), 1).astype(F32)
        rel = ri - ci
        for h in range(RET_HEADS):
            lg = lg_ref[h]
            intra_ref[h] = jnp.exp(jnp.where(rel >= 0, rel * lg, -jnp.inf))
            fs_ref[h] = jnp.exp((ri + 1.0) * lg)
            te_ref[h] = jnp.exp((CHUNK - 1.0 - ri) * lg)

    def chunk(ci, carry):
        _ret_prompt_chunk(ci, lg_ref, q_ref, k_ref, v_ref, gr_ref, within_ref, step_ref, nrm_ref,
                          o_ref, st_ref, intra_ref, fs_ref, te_ref)
        return carry

    lax.fori_loop(0, q_ref.shape[0] // CHUNK, chunk, 0)

    @pl.when(t == pl.num_programs(0) - 1)
    def _():
        sfin_ref[...] = st_ref[...]


def _ret_prompt(log_gamma, proj, tabs, nrm):
    seq = proj.shape[0]
    within, steps = tabs
    const = lambda shape: pl.BlockSpec(shape, lambda t: (0,) * len(shape))
    col = lambda cidx: pl.BlockSpec((MIX_ROWS, 1024), lambda t: (t, cidx))
    tbl = pltpu.VMEM((RET_HEADS, CHUNK, CHUNK), F32)
    return pl.pallas_call(
        _ret_prompt_kernel,
        out_shape=(jax.ShapeDtypeStruct((seq, 1024), BF16), jax.ShapeDtypeStruct((RET_HEADS, RET_DK, 128), F32)),
        grid=(seq // MIX_ROWS,),
        in_specs=[pl.BlockSpec(memory_space=pltpu.SMEM), col(COL_QR), col(COL_KR), col(COL_VR), col(COL_GR),
                  const((CHUNK, 2 * LANES)), pl.BlockSpec((None, 8, 2 * LANES), lambda t: (t, 0, 0)), const((1, 1024))],
        out_specs=(pl.BlockSpec((MIX_ROWS, 1024), lambda t: (t, 0)), const((RET_HEADS, RET_DK, 128))),
        scratch_shapes=[tbl, tbl, tbl, tbl],
        compiler_params=_cparams(("arbitrary",)),
        name="ret_prompt",
    )(log_gamma, proj, proj, proj, proj, within, steps, nrm)


def _ret_dec_pre_kernel(q_ref, k_ref, c_ref, s_ref, qrot_ref, kt_ref):
    c = c_ref[...]
    s = s_ref[...]
    q = q_ref[...]
    k = k_ref[...]
    ks = []
    for h in range(RET_HEADS):
        sl = slice(128 * h, 128 * (h + 1))
        qh = q[:, sl]
        kh = k[:, sl]
        qrot_ref[:, sl] = qh * c + pltpu.roll(qh, 64, axis=1) * s
        ks.append((kh * c + pltpu.roll(kh, 64, axis=1) * s) * (RET_DK ** -0.5))
    kt_ref[...] = jnp.concatenate(ks, axis=1).T


def _ret_dec_pre(proj, tabs):
    nb = proj.shape[0]
    c, s = tabs
    const = lambda shape: pl.BlockSpec(shape, lambda t: (0,) * len(shape))
    return pl.pallas_call(
        _ret_dec_pre_kernel,
        out_shape=(jax.ShapeDtypeStruct((nb, 1024), F32), jax.ShapeDtypeStruct((1024, nb), F32)),
        grid=(1,),
        in_specs=[pl.BlockSpec((nb, 1024), lambda t: (0, COL_QR)), pl.BlockSpec((nb, 1024), lambda t: (0, COL_KR)),
                  const((1, LANES)), const((1, LANES))],
        out_specs=(const((nb, 1024)), const((1024, nb))),
        compiler_params=_cparams(("arbitrary",)),
        name="ret_dec_pre",
    )(proj, proj, c, s)


def _ret_dec_state_kernel(gam_ref, kt_ref, q_ref, v_ref, gr_ref, st_ref, nrm_ref, o_ref, stn_ref):
    i = pl.program_id(0)
    shift = (LANES - i * DEC_BLOCK) % LANES
    kr = pltpu.roll(kt_ref[...], shift, axis=1)
    q = q_ref[...]
    v = v_ref[...]
    for j in range(DEC_BLOCK):
        for h in range(RET_HEADS):
            lo, hi = 128 * h, 128 * (h + 1)
            kcol = jnp.broadcast_to(kr[lo:hi, j:j + 1], (RET_DK, 128))
            s_new = st_ref[j, lo:hi, :] * gam_ref[h] + kcol * v[j:j + 1, lo:hi]
            stn_ref[j, lo:hi, :] = s_new
            qs = _bdot(q[:, lo:hi], s_new)
            o_ref[j:j + 1, lo:hi] = qs[j:j + 1, :]
    gr = gr_ref[...]
    nrm = nrm_ref[...]
    for h in range(RET_HEADS):
        sl = slice(128 * h, 128 * (h + 1))
        o = o_ref[:, sl]
        o = o * lax.rsqrt(jnp.mean(o * o, axis=-1, keepdims=True) + EPS)
        o_ref[:, sl] = o * nrm[:, sl] * _silu(gr[:, sl])


def _ret_dec_state(layer, gam, kt, qrot, proj, st_all, nrm, carried):
    nb = qrot.shape[0]
    lc = _LayerCall(layer, nb // DEC_BLOCK)
    const = lambda shape: pl.BlockSpec(shape, lambda t: (0,) * len(shape))
    tail = (1024, 128)
    ins = [gam, kt, qrot, proj, proj, st_all, nrm]
    in_specs = [pl.BlockSpec(memory_space=pltpu.SMEM), const((1024, nb)), lc.rows(1024),
                lc.rows(1024, COL_VR), lc.rows(1024, COL_GR), lc.state_in(tail), const((1, 1024))]
    n_in = len(ins)
    carried = [] if carried is None else [carried]
    return pl.pallas_call(
        lc.kernel(_ret_dec_state_kernel, n_in, len(carried), (1,)),
        out_shape=(jax.ShapeDtypeStruct((nb, 1024), F32), jax.ShapeDtypeStruct(st_all.shape, F32)),
        grid=lc.grid,
        in_specs=in_specs + [pl.BlockSpec(memory_space=pl.ANY)] * len(carried),
        out_specs=(lc.rows(1024), lc.state_out(tail)),
        input_output_aliases={n_in + k: 1 + k for k in range(len(carried))},
        compiler_params=_cparams(("arbitrary",)),
        name="ret_dec_state",
    )(*ins, *carried)


def _merge_math(x_ref, a_ref, b_ref, c_ref, g1_ref, g2_ref, g3_ref, w1_ref, w2_ref, w3_ref, wo_ref):
    gate = lambda ref: _sigmoid(ref[...].astype(F32))
    m = (gate(g1_ref) * jnp.dot(a_ref[...].astype(BF16), w1_ref[...], preferred_element_type=F32)
         + gate(g2_ref) * jnp.dot(b_ref[...].astype(BF16), w2_ref[...], preferred_element_type=F32)
         + gate(g3_ref) * jnp.dot(c_ref[...].astype(BF16), w3_ref[...], preferred_element_type=F32))
    return x_ref[...] + jnp.dot(m.astype(BF16), wo_ref[...], preferred_element_type=F32)


def _merge_kernel(*refs):
    *ins, o_ref = refs
    o_ref[...] = _merge_math(*ins)


def _merge_route_kernel(*refs):
    *ins, gn_ref, rw_ref, rb_ref, o_ref, route_ref = refs
    x = _merge_math(*ins)
    o_ref[...] = x
    route_ref[...] = _route_row(_rms(x, gn_ref[...]), rw_ref[0], rw_ref[1], rb_ref[...])


def _merge(x, a, b, c, proj, w1, w2, w3, wo, router=None):
    rows = x.shape[0]
    tm = min(rows, 512)
    rowb = pl.BlockSpec((tm, 1024), lambda i: (i, 0))
    gate = lambda k: pl.BlockSpec((tm, 1024), lambda i: (i, COL_GATE + k))
    wsp = pl.BlockSpec((1024, 1024), lambda i: (0, 0))
    ins = [x, a, b, c, proj, proj, proj, w1, w2, w3, wo]
    in_specs = [rowb, rowb, rowb, rowb, gate(0), gate(1), gate(2), wsp, wsp, wsp, wsp]
    out_shape = jax.ShapeDtypeStruct((rows, 1024), F32)
    if router is None:
        kern, out_specs = _merge_kernel, rowb
    else:
        kern = _merge_route_kernel
        ins += list(router)
        in_specs += [pl.BlockSpec((1, 1024), lambda i: (0, 0)), pl.BlockSpec((2, 1024, LANES), lambda i: (0, 0, 0)),
                     pl.BlockSpec((1, LANES), lambda i: (0, 0))]
        out_shape = (out_shape, jax.ShapeDtypeStruct((rows, LANES), F32))
        out_specs = (rowb, pl.BlockSpec((tm, LANES), lambda i: (i, 0)))
    return pl.pallas_call(
        kern,
        out_shape=out_shape,
        grid=(rows // tm,),
        in_specs=in_specs,
        out_specs=out_specs,
        compiler_params=_cparams(("parallel",)),
        name="merge",
    )(*ins)


FF_TILE = 1408
FF_SPLIT = 768


def _ffn_kernel(x_ref, g_ref, wg_ref, wu_ref, wd_ref, o_ref, h_ref, acc_ref):
    j = pl.program_id(1)

    @pl.when(j == 0)
    def _():
        h_ref[...] = _rms(x_ref[...], g_ref[...]).astype(BF16)
        acc_ref[...] = jnp.zeros_like(acc_ref)

    h = h_ref[...]
    for lo, hi in ((0, FF_SPLIT), (FF_SPLIT, FF_TILE)):
        a = jnp.dot(h, wg_ref[:, lo:hi], preferred_element_type=F32)
        u = jnp.dot(h, wu_ref[:, lo:hi], preferred_element_type=F32)
        acc_ref[...] += jnp.dot((_silu(a) * u).astype(BF16), wd_ref[lo:hi, :], preferred_element_type=F32)

    @pl.when(j == pl.num_programs(1) - 1)
    def _():
        o_ref[...] = x_ref[...] + acc_ref[...]


def _ffn(x, g, wg, wu, wd):
    rows = x.shape[0]
    tm = min(rows, 1024)
    return pl.pallas_call(
        _ffn_kernel,
        out_shape=jax.ShapeDtypeStruct((rows, 1024), F32),
        grid=(rows // tm, D_FF // FF_TILE),
        in_specs=[pl.BlockSpec((tm, 1024), lambda i, j: (i, 0)), pl.BlockSpec((1, 1024), lambda i, j: (0, 0)),
                  pl.BlockSpec((1024, FF_TILE), lambda i, j: (0, j)), pl.BlockSpec((1024, FF_TILE), lambda i, j: (0, j)),
                  pl.BlockSpec((FF_TILE, 1024), lambda i, j: (j, 0))],
        out_specs=pl.BlockSpec((tm, 1024), lambda i, j: (i, 0)),
        scratch_shapes=[pltpu.VMEM((tm, 1024), BF16), pltpu.VMEM((tm, 1024), F32)],
        compiler_params=_cparams(("parallel", "arbitrary"), vmem=VMEM_LIMIT_FFN),
        name="ffn",
    )(x, g, wg, wu, wd)


MOE_FF_TILE = 1408


def _top2(h, rw_hi, rw_lo, rb, lane):
    h_hi = h.astype(BF16)
    h_lo = (h - h_hi.astype(F32)).astype(BF16)
    d = lambda a, b: jnp.dot(a, b, preferred_element_type=F32)
    logits = d(h_hi, rw_hi) + d(h_hi, rw_lo) + d(h_lo, rw_hi) + rb
    logits = jnp.where(lane < N_EXPERTS, logits, -jnp.inf)
    m1 = jnp.max(logits, axis=1, keepdims=True)
    i1 = jnp.min(jnp.where(logits == m1, lane, float(LANES)), axis=1, keepdims=True)
    rest = jnp.where(lane == i1, -jnp.inf, logits)
    m2 = jnp.max(rest, axis=1, keepdims=True)
    i2 = jnp.min(jnp.where(rest == m2, lane, float(LANES)), axis=1, keepdims=True)
    e2 = jnp.exp(m2 - m1)
    p1 = 1.0 / (1.0 + e2)
    return i1, i2, p1, e2 * p1


def _moe_kernel(x_ref, g_ref, rw_ref, rb_ref, wg_ref, wu_ref, wd_ref, o_ref, h_ref, acc_ref, comb_ref):
    e = pl.program_id(1)
    j = pl.program_id(2)
    tm = x_ref.shape[0]
    lane = lax.broadcasted_iota(jnp.int32, (tm, LANES), 1).astype(F32)

    @pl.when(jnp.logical_and(e == 0, j == 0))
    def _():
        h = _rms(x_ref[...], g_ref[...])
        h_ref[...] = h.astype(BF16)
        i1, i2, p1, p2 = _top2(h, rw_ref[0], rw_ref[1], rb_ref[...], lane)
        comb_ref[...] = jnp.where(lane == i1, p1, 0.0) + jnp.where(lane == i2, p2, 0.0)
        acc_ref[...] = jnp.zeros_like(acc_ref)

    ce = jnp.sum(jnp.where(lane == e.astype(F32), comb_ref[...], 0.0), axis=1, keepdims=True)
    h = h_ref[...]
    a = jnp.dot(h, wg_ref[0], preferred_element_type=F32)
    u = jnp.dot(h, wu_ref[0], preferred_element_type=F32)
    acc_ref[...] += ce * jnp.dot((_silu(a) * u).astype(BF16), wd_ref[0], preferred_element_type=F32)

    @pl.when(jnp.logical_and(e == pl.num_programs(1) - 1, j == pl.num_programs(2) - 1))
    def _():
        o_ref[...] = x_ref[...] + acc_ref[...]


def _moe(x, g, rw, rb, wg, wu, wd):
    rows = x.shape[0]
    tm = min(rows, 1024)
    tf = MOE_FF_TILE
    return pl.pallas_call(
        _moe_kernel,
        out_shape=jax.ShapeDtypeStruct((rows, 1024), F32),
        grid=(rows // tm, N_EXPERTS, D_FF // tf),
        in_specs=[pl.BlockSpec((tm, 1024), lambda i, e, j: (i, 0)), pl.BlockSpec((1, 1024), lambda i, e, j: (0, 0)),
                  pl.BlockSpec((2, 1024, LANES), lambda i, e, j: (0, 0, 0)),
                  pl.BlockSpec((1, LANES), lambda i, e, j: (0, 0)),
                  pl.BlockSpec((1, 1024, tf), lambda i, e, j: (e, 0, j)),
                  pl.BlockSpec((1, 1024, tf), lambda i, e, j: (e, 0, j)),
                  pl.BlockSpec((1, tf, 1024), lambda i, e, j: (e, j, 0))],
        out_specs=pl.BlockSpec((tm, 1024), lambda i, e, j: (i, 0)),
        scratch_shapes=[pltpu.VMEM((tm, 1024), BF16), pltpu.VMEM((tm, 1024), F32), pltpu.VMEM((tm, LANES), F32)],
        compiler_params=_cparams(("parallel", "arbitrary", "arbitrary")),
        name="moe",
    )(x, g, rw, rb, wg, wu, wd)


MOE_ROWS = 512
MOE_GROUP_FF = 1408
GATHER_UNROLL = 8


def _route_row(h, rw_hi, rw_lo, rb):
    lane = lax.broadcasted_iota(jnp.int32, (h.shape[0], LANES), 1).astype(F32)
    i1, i2, p1, p2 = _top2(h, rw_hi, rw_lo, rb, lane)
    return jnp.where(lane == 0.0, i1, jnp.where(lane == 1.0, i2, jnp.where(lane == 2.0, p1,
                     jnp.where(lane == 3.0, p2, 0.0))))


def _route_plan(route, tm):
    n = route.shape[0]
    n_tiles = (2 * n) // tm + N_EXPERTS
    e_flat = route[:, :2].astype(jnp.int32).reshape(-1)
    onehot = (e_flat[:, None] == jnp.arange(N_EXPERTS, dtype=jnp.int32)[None, :]).astype(jnp.int32)
    csum = jnp.cumsum(onehot, axis=0)
    counts = csum[-1]
    tiles_e = (counts + tm - 1) // tm
    tile_end = jnp.cumsum(tiles_e)
    row_start = (tile_end - tiles_e) * tm
    pos = jnp.sum((csum - onehot + row_start[None, :]) * onehot, axis=1).astype(jnp.int32)
    tile_expert = jnp.minimum(jnp.sum(jnp.arange(n_tiles, dtype=jnp.int32)[:, None] >= tile_end[None, :], axis=1),
                              N_EXPERTS - 1).astype(jnp.int32)
    n_used = tile_end[-1:].astype(jnp.int32)
    src = jnp.zeros((n_tiles * tm,), jnp.int32).at[pos].set(jnp.arange(2 * n, dtype=jnp.int32) // 2)
    return pos, src.reshape(n_tiles, 1, tm), tile_expert, n_used


def _moe_group_kernel(te_ref, nu_ref, src_ref, srcn_ref, x_hbm, g_ref, wg_ref, wu_ref, wd_ref,
                      y_ref, buf, sem, h_ref, acc_ref):
    i = pl.program_id(0)
    j = pl.program_id(1)
    tm = buf.shape[1]
    slot = i % 2
    active = i < nu_ref[0]

    def row_copy(idx_ref, s, r):
        return pltpu.make_async_copy(x_hbm.at[pl.ds(idx_ref[0, 0, r], 1), :], buf.at[s, pl.ds(r, 1), :], sem.at[s])

    def gather(idx_ref, s):
        def body(r, c):
            row_copy(idx_ref, s, r).start()
            return c
        lax.fori_loop(0, tm, body, 0, unroll=GATHER_UNROLL)

    @pl.when(jnp.logical_and(active, j == 0))
    def _():
        @pl.when(i == 0)
        def _():
            gather(src_ref, 0)

        pltpu.make_async_copy(x_hbm.at[pl.ds(0, tm), :], buf.at[slot], sem.at[slot]).wait()

        @pl.when(i + 1 < nu_ref[0])
        def _():
            gather(srcn_ref, 1 - slot)

        h_ref[...] = _rms(buf[slot], g_ref[...]).astype(BF16)
        acc_ref[...] = jnp.zeros_like(acc_ref)

    @pl.when(active)
    def _():
        h = h_ref[...]
        a = jnp.dot(h, wg_ref[0], preferred_element_type=F32)
        u = jnp.dot(h, wu_ref[0], preferred_element_type=F32)
        acc_ref[...] += jnp.dot((_silu(a) * u).astype(BF16), wd_ref[0], preferred_element_type=F32)

    @pl.when(j == pl.num_programs(1) - 1)
    def _():
        @pl.when(active)
        def _():
            y_ref[...] = acc_ref[...]

        @pl.when(jnp.logical_not(active))
        def _():
            y_ref[...] = jnp.zeros_like(y_ref)


def _moe_group(tile_expert, n_used, src, x, g, wg, wu, wd):
    n_tiles, _, tm = src.shape
    tf = MOE_GROUP_FF
    grid_spec = pltpu.PrefetchScalarGridSpec(
        num_scalar_prefetch=2,
        grid=(n_tiles, D_FF // tf),
        in_specs=[pl.BlockSpec((1, 1, tm), lambda i, j, te, nu: (i, 0, 0), memory_space=pltpu.SMEM),
                  pl.BlockSpec((1, 1, tm), lambda i, j, te, nu: (jnp.minimum(i + 1, n_tiles - 1), 0, 0),
                               memory_space=pltpu.SMEM),
                  pl.BlockSpec(memory_space=pl.ANY),
                  pl.BlockSpec((1, 1024), lambda i, j, te, nu: (0, 0)),
                  pl.BlockSpec((1, 1024, tf), lambda i, j, te, nu: (te[i], 0, j)),
                  pl.BlockSpec((1, 1024, tf), lambda i, j, te, nu: (te[i], 0, j)),
                  pl.BlockSpec((1, tf, 1024), lambda i, j, te, nu: (te[i], j, 0))],
        out_specs=pl.BlockSpec((tm, 1024), lambda i, j, te, nu: (i, 0)),
        scratch_shapes=[pltpu.VMEM((2, tm, 1024), F32), pltpu.SemaphoreType.DMA((2,)),
                        pltpu.VMEM((tm, 1024), BF16), pltpu.VMEM((tm, 1024), F32)])
    return pl.pallas_call(
        _moe_group_kernel,
        out_shape=jax.ShapeDtypeStruct((n_tiles * tm, 1024), F32),
        grid_spec=grid_spec,
        compiler_params=_cparams(("arbitrary", "arbitrary")),
        name="moe_group",
    )(tile_expert, n_used, src, src, x, g, wg, wu, wd)


def _moe_combine_kernel(pos_ref, posn_ref, x_ref, r_ref, y_hbm, p_ref, g_ref, wp_ref, wgt_ref, gf_ref,
                        o_ref, bufa, bufb, sem, *, final):
    i = pl.program_id(0)
    tm = x_ref.shape[0]
    slot = i % 2

    def row_copies(idx_ref, s, t):
        pltpu.make_async_copy(y_hbm.at[pl.ds(idx_ref[0, 0, 2 * t], 1), :], bufa.at[s, pl.ds(t, 1), :],
                              sem.at[0, s]).start()
        pltpu.make_async_copy(y_hbm.at[pl.ds(idx_ref[0, 0, 2 * t + 1], 1), :], bufb.at[s, pl.ds(t, 1), :],
                              sem.at[1, s]).start()

    def gather(idx_ref, s):
        def body(t, c):
            row_copies(idx_ref, s, t)
            return c
        lax.fori_loop(0, tm, body, 0, unroll=GATHER_UNROLL)

    @pl.when(i == 0)
    def _():
        gather(pos_ref, 0)

    pltpu.make_async_copy(y_hbm.at[pl.ds(0, tm), :], bufa.at[slot], sem.at[0, slot]).wait()
    pltpu.make_async_copy(y_hbm.at[pl.ds(0, tm), :], bufb.at[slot], sem.at[1, slot]).wait()

    for t in range(tm):
        row_copies(posn_ref, 1 - slot, t)
    r = r_ref[...]
    x = x_ref[...] + r[:, 2:3] * bufa[slot] + r[:, 3:4] * bufb[slot]
    o_ref[...] = _ple_math(x, p_ref[...], g_ref[...], wp_ref[...], wgt_ref[...], gf_ref[...], final)

    @pl.when(i == pl.num_programs(0) - 1)
    def _():
        pltpu.make_async_copy(y_hbm.at[pl.ds(0, tm), :], bufa.at[1 - slot], sem.at[0, 1 - slot]).wait()
        pltpu.make_async_copy(y_hbm.at[pl.ds(0, tm), :], bufb.at[1 - slot], sem.at[1, 1 - slot]).wait()


def _moe_combine(pos, x, route, y, p, g, wp, wgt, gf, final):
    rows = x.shape[0]
    tm = MOE_ROWS
    n = rows // tm
    pos3 = pos.reshape(n, 1, 2 * tm)
    vec = pl.BlockSpec((1, 1024), lambda i: (0, 0))
    return pl.pallas_call(
        functools.partial(_moe_combine_kernel, final=final),
        out_shape=jax.ShapeDtypeStruct((rows, 1024), F32),
        grid=(n,),
        in_specs=[pl.BlockSpec((1, 1, 2 * tm), lambda i: (i, 0, 0), memory_space=pltpu.SMEM),
                  pl.BlockSpec((1, 1, 2 * tm), lambda i: (jnp.minimum(i + 1, n - 1), 0, 0), memory_space=pltpu.SMEM),
                  pl.BlockSpec((tm, 1024), lambda i: (i, 0)), pl.BlockSpec((tm, LANES), lambda i: (i, 0)),
                  pl.BlockSpec(memory_space=pl.ANY),
                  pl.BlockSpec((tm, PLE_DIM), lambda i: (i, 0)), vec,
                  pl.BlockSpec((PLE_DIM, 1024), lambda i: (0, 0)), pl.BlockSpec((1024, 1024), lambda i: (0, 0)), vec],
        out_specs=pl.BlockSpec((tm, 1024), lambda i: (i, 0)),
        scratch_shapes=[pltpu.VMEM((2, tm, 1024), F32), pltpu.VMEM((2, tm, 1024), F32),
                        pltpu.SemaphoreType.DMA((2, 2))],
        compiler_params=_cparams(("arbitrary",)),
        name="moe_combine",
    )(pos3, pos3, x, route, y, p, g, wp, wgt, gf)


def _moe_routed_ple(x, route, g, wg, wu, wd, p, g_ple, wp, wgt, gf, final):
    pos, src, tile_expert, n_used = _route_plan(route, MOE_ROWS)
    y = _moe_group(tile_expert, n_used, src, x, g, wg, wu, wd)
    return _moe_combine(pos, x, route, y, p, g_ple, wp, wgt, gf, final)


def _ple_math(x, p, g, wp, wgt, gf, final):
    emb = jnp.dot(p.astype(BF16), wp, preferred_element_type=F32)
    gate = _sigmoid(jnp.dot(_rms(x, g).astype(BF16), wgt, preferred_element_type=F32))
    y = x + emb * gate
    return _rms(y, gf) if final else y


def _ple_kernel(x_ref, p_ref, g_ref, wp_ref, wgt_ref, gf_ref, o_ref, *, final):
    o_ref[...] = _ple_math(x_ref[...], p_ref[...], g_ref[...], wp_ref[...], wgt_ref[...], gf_ref[...], final)


def _ple(x, p, g, wp, wgt, gf, final):
    rows = x.shape[0]
    tm = min(rows, 512)
    vec = pl.BlockSpec((1, 1024), lambda i: (0, 0))
    return pl.pallas_call(
        functools.partial(_ple_kernel, final=final),
        out_shape=jax.ShapeDtypeStruct((rows, 1024), F32),
        grid=(rows // tm,),
        in_specs=[pl.BlockSpec((tm, 1024), lambda i: (i, 0)), pl.BlockSpec((tm, PLE_DIM), lambda i: (i, 0)), vec,
                  pl.BlockSpec((PLE_DIM, 1024), lambda i: (0, 0)), pl.BlockSpec((1024, 1024), lambda i: (0, 0)), vec],
        out_specs=pl.BlockSpec((tm, 1024), lambda i: (i, 0)),
        compiler_params=_cparams(("parallel",)),
        name="ple",
    )(x, p, g, wp, wgt, gf)


REPACK_ROWS = 128


def _repack_w_in_kernel(w_ref, o_ref):
    offs = [0]
    for wd in IN_WIDTHS:
        offs.append(offs[-1] + wd)
    z, xbc, dt, qa, ka, va, qr, kr, vr, gr, gates = [(offs[i], offs[i + 1]) for i in range(len(IN_WIDTHS))]
    take = lambda lo, hi: w_ref[:, lo:hi].astype(BF16)
    q = take(*qa)
    heads = lambda h: q[:, h * ATT_HEAD_DIM:(h + 1) * ATT_HEAD_DIM]
    q_rmajor = jnp.concatenate([heads(ATT_REP * g + r) for r in range(ATT_REP) for g in range(ATT_KV_HEADS)], axis=1)
    parts = [take(*xbc), take(*z), q_rmajor] + [take(*p) for p in (qr, kr, vr, gr, gates, ka, va)]
    used = sum(p.shape[1] for p in parts) + (dt[1] - dt[0])
    parts.append(jnp.concatenate([take(*dt), jnp.zeros((o_ref.shape[0], N_PROJ - used), BF16)], axis=1))
    dst = 0
    for part in parts:
        o_ref[:, dst:dst + part.shape[1]] = part
        dst += part.shape[1]


def _repack_w_in(w_all, layer):
    rows = w_all.shape[1]
    return pl.pallas_call(
        _repack_w_in_kernel,
        out_shape=jax.ShapeDtypeStruct((rows, N_PROJ), BF16),
        grid=(rows // REPACK_ROWS,),
        in_specs=[pl.BlockSpec((None, REPACK_ROWS, w_all.shape[2]), lambda i: (layer, i, 0))],
        out_specs=pl.BlockSpec((REPACK_ROWS, N_PROJ), lambda i: (i, 0)),
        compiler_params=_cparams(("parallel",)),
        name="repack_w_in",
    )(w_all)


def _att_tables(pos):
    half = ROPE_DIM // 2
    inv = jnp.exp(-math.log(ROPE_THETA) * jnp.arange(half, dtype=F32) * (2.0 / ROPE_DIM))
    ang = pos.astype(F32)[:, None] * inv[None, :]
    cos, sin = jnp.cos(ang), jnp.sin(ang)
    n = pos.shape[0]
    one = jnp.ones((n, ATT_HEAD_DIM - ROPE_DIM), F32)
    zero8 = jnp.zeros((n, half), F32)
    zero = jnp.zeros((n, ATT_HEAD_DIM - ROPE_DIM), F32)
    c = jnp.concatenate([cos, cos, one], axis=1)
    s1 = jnp.concatenate([zero8, sin, zero], axis=1)
    s2 = jnp.concatenate([-sin, zero8, zero], axis=1)
    return tuple(jnp.concatenate([t, t], axis=1) for t in (c, s1, s2))


def _ret_tables(pos):
    half = RET_DK // 2
    inv = jnp.exp(-math.log(RET_THETA) * jnp.arange(half, dtype=F32) * (2.0 / RET_DK))
    ang = pos.astype(F32)[:, None] * inv[None, :]
    cos, sin = jnp.cos(ang), jnp.sin(ang)
    return jnp.concatenate([cos, cos], axis=1), jnp.concatenate([-sin, sin], axis=1)


def _rope_step_tables(inv_lane, seq):
    per = MIX_ROWS // CHUNK
    ang_i = jnp.arange(CHUNK, dtype=F32)[:, None] * inv_lane[None, :]
    within = jnp.concatenate([jnp.cos(ang_i), jnp.sin(ang_i)], axis=1)
    ang_b = (jnp.arange(seq // CHUNK, dtype=F32) * CHUNK)[:, None] * inv_lane[None, :]
    blk = jnp.concatenate([jnp.cos(ang_b), jnp.sin(ang_b)], axis=1).reshape(seq // MIX_ROWS, per, 2 * LANES)
    steps = jnp.concatenate([blk, jnp.zeros((seq // MIX_ROWS, 8 - per, 2 * LANES), F32)], axis=1)
    return within, steps


def _att_inv_lanes():
    inv = jnp.exp(-math.log(ROPE_THETA) * jnp.arange(ROPE_DIM // 2, dtype=F32) * (2.0 / ROPE_DIM))
    return jnp.tile(inv, LANES // (ROPE_DIM // 2))


def _ret_inv_lanes():
    inv = jnp.exp(-math.log(RET_THETA) * jnp.arange(RET_DK // 2, dtype=F32) * (2.0 / RET_DK))
    return jnp.tile(inv, 2)


def _pad_lanes(v, fill=0.0):
    return jnp.concatenate([v.astype(F32), jnp.full((LANES - v.shape[0],), fill, F32)])[None, :]


def kernel(x_prompt, x_sample, state_ssm, state_conv, cache_win_k, cache_win_v, state_ret, p_prompt, p_sample,
           w_in, conv_w, conv_b, dt_bias, a_log, d_skip, ssd_norm, attn_sinks, ret_norm, w_o_ssd, w_o_att, w_o_ret,
           w_out, norm_mix, norm_ffn, norm_ple, ffn_w_gate, ffn_w_up, ffn_w_down, router_w, router_b, moe_w_gate,
           moe_w_up, moe_w_down, w_ple, w_ple_gate, norm_final):
    seq = x_prompt.shape[1]
    nb = x_sample.shape[0]
    xp = x_prompt.reshape(seq, D_MODEL)
    xs = x_sample.reshape(nb, D_MODEL)
    pos_s = PAST_LEN + jnp.arange(1)
    att_tab_p, att_tab_s = _rope_step_tables(_att_inv_lanes(), seq), _att_tables(pos_s)
    ret_tab_p, ret_tab_s = _rope_step_tables(_ret_inv_lanes(), seq), _ret_tables(pos_s)
    log_gamma = jnp.log1p(-jnp.exp2(-5.0 - jnp.arange(RET_HEADS, dtype=F32)))
    gamma = jnp.exp(log_gamma)
    row = lambda v: v.astype(F32)[None, :]

    ssm_all = state_ssm.reshape(DEPTH, nb, SSD_HEADS * SSD_HEAD_DIM, SSD_STATE)
    ret_all = state_ret.reshape(DEPTH, nb, RET_HEADS * RET_DK, 128)
    kc_all = jnp.transpose(cache_win_k, (0, 1, 3, 4, 2))
    vc_all = jnp.transpose(cache_win_v, (0, 1, 3, 4, 2))
    ssm_s = ret_s = kv_s = None
    conv_s = []

    new_p = [[], [], [], [], []]
    for i in range(DEPTH):
        w_in_i = _repack_w_in(w_in, i)
        cw, cb = conv_w[i], row(conv_b[i])
        dtb, alog = _pad_lanes(dt_bias[i]), _pad_lanes(a_log[i])
        dsk = row(jnp.repeat(d_skip[i], SSD_HEAD_DIM))
        nrm_ssd, nrm_ret = row(ssd_norm[i]), row(ret_norm[i])
        sinks = attn_sinks[i].astype(F32)
        sink32 = jnp.zeros((ATT_KV_HEADS, 8), F32).at[:, :ATT_REP].set(sinks.reshape(ATT_KV_HEADS, ATT_REP))
        sink32 = jnp.broadcast_to(sink32.reshape(32, 1), (32, LANES))
        w1 = w_o_ssd[i].astype(BF16)
        w2 = w_o_att[i].reshape(ATT_KV_HEADS, ATT_REP, ATT_HEAD_DIM, D_MODEL).transpose(1, 0, 2, 3) \
            .reshape(ATT_Q_HEADS * ATT_HEAD_DIM, D_MODEL).astype(BF16)
        w3 = w_o_ret[i].astype(BF16)
        wo = w_out[i].astype(BF16)
        g_mix, g_ffn, g_ple = row(norm_mix[i]), row(norm_ffn[i]), row(norm_ple[i])
        wp, wpg = w_ple[i].astype(BF16), w_ple_gate[i].astype(BF16)
        gf = row(norm_final)
        j = i // 2
        if i % 2 == 0:
            ffw = (ffn_w_gate[j].astype(BF16), ffn_w_up[j].astype(BF16), ffn_w_down[j].astype(BF16))
        else:
            rw = jnp.concatenate([router_w[j], jnp.zeros((D_MODEL, LANES - N_EXPERTS), F32)], axis=1)
            rw_hi = rw.astype(BF16)
            rw = jnp.stack([rw_hi, (rw - rw_hi.astype(F32)).astype(BF16)])
            ffw = (rw, _pad_lanes(router_b[j]), moe_w_gate[j].astype(BF16), moe_w_up[j].astype(BF16),
                   moe_w_down[j].astype(BF16))
        final = i == DEPTH - 1

        proj, dt = _inproj(xp, g_mix, w_in_i, BF16)
        head_rows = lambda v: jnp.broadcast_to(v.astype(F32)[:, None], (SSD_HEADS, LANES))
        y_ssd, ssm_fin, conv_fin = _ssd_prompt(proj, dt, cw, cb, head_rows(dt_bias[i]), head_rows(a_log[i]),
                                               dsk, nrm_ssd)
        o_att, wk, wv = _swa_prompt(sinks, proj, att_tab_p)
        o_ret, ret_fin = _ret_prompt(log_gamma, proj, ret_tab_p, nrm_ret)
        ple_args = (p_prompt[i].reshape(seq, PLE_DIM), g_ple, wp, wpg, gf, final)
        if i % 2 == 0:
            xp = _merge(xp, y_ssd, o_att, o_ret, proj, w1, w2, w3, wo)
            xp = _ple(_ffn(xp, g_ffn, *ffw), *ple_args)
        else:
            rw, rb, *expert_w = ffw
            xp, route = _merge(xp, y_ssd, o_att, o_ret, proj, w1, w2, w3, wo, router=(g_ffn, rw, rb))
            xp = _moe_routed_ple(xp, route, g_ffn, *expert_w, *ple_args)
        new_p[0].append(ssm_fin.reshape(1, SSD_HEADS, SSD_HEAD_DIM, SSD_STATE))
        new_p[1].append(conv_fin[None])
        new_p[2].append(wk.reshape(1, WINDOW, ATT_KV_HEADS, ATT_HEAD_DIM))
        new_p[3].append(wv.reshape(1, WINDOW, ATT_KV_HEADS, ATT_HEAD_DIM))
        new_p[4].append(ret_fin[None])

        proj, dt = _inproj(xs, g_mix, w_in_i, F32)
        cst_t = jnp.transpose(state_conv[i], (1, 0, 2))
        cnew_t, xs_conv, bc, dec_t, xdt_t = _ssd_dec_pre(proj, dt, cst_t, cw, cb, dtb, alog)
        y_ssd, ssm_s = _ssd_dec_state(i, dec_t, xdt_t, bc, proj, xs_conv, ssm_all, dsk, nrm_ssd, ssm_s)
        qa_rot, ka_t, va_t = _swa_dec_pre(proj, att_tab_s)
        o_att, *kv_s = _swa_dec(i, qa_rot, ka_t, va_t, kc_all, vc_all, sink32, kv_s)
        qrot, kt = _ret_dec_pre(proj, ret_tab_s)
        o_ret, ret_s = _ret_dec_state(i, gamma, kt, qrot, proj, ret_all, nrm_ret, ret_s)
        xs = _merge(xs, y_ssd, o_att, o_ret, proj, w1, w2, w3, wo)
        xs = _ffn(xs, g_ffn, *ffw) if i % 2 == 0 else _moe(xs, g_ffn, *ffw)
        xs = _ple(xs, p_sample[i].reshape(nb, PLE_DIM), g_ple, wp, wpg, gf, final)
        conv_s.append(jnp.transpose(cnew_t, (1, 0, 2)))

    y_prompt = xp.reshape(1, seq, D_MODEL)
    y_sample = xs.reshape(nb, 1, D_MODEL)
    outs_p = [jnp.stack(l) for l in new_p]
    outs_s = [ssm_s.reshape(state_ssm.shape), jnp.stack(conv_s), jnp.transpose(kv_s[0], (0, 1, 4, 2, 3)),
              jnp.transpose(kv_s[1], (0, 1, 4, 2, 3)), ret_s.reshape(state_ret.shape)]
    return (y_prompt, y_sample, *outs_p, *outs_s)
```
